```python
import math
import jax, jax.numpy as jnp
from jax import lax
import numpy as np

D_MODEL = 1024
BATCH = 16
SEQ = 2048
DEPTH = 2
DEC_BATCH = 8
DEC_SEQ = 64
PAST_LEN = 1024

CHUNK = 64
N_EVEN = (DEPTH + 1) // 2
N_ODD = DEPTH // 2

NH_A = 4
DK_A = 128
DV_A = 128
W_A = NH_A * DV_A
NH_B = 8
NKV_B = 2
G_B = NH_B // NKV_B
DH_B = 64
WINDOW = 128
W_B = NH_B * DH_B
NUM_BUCKETS = 32
MAX_DISTANCE = 256
MIX_EVEN = W_A + W_B
SIZES_EVEN = (NH_A * DK_A, NH_A * DK_A, NH_A * DV_A, NH_A * DV_A, NH_A, NH_A,
              NH_B * DH_B, NKV_B * DH_B, NKV_B * DH_B)
IN_EVEN = 2 * NH_A * DK_A + 2 * NH_A * DV_A + 2 * NH_A + NH_B * DH_B + 2 * NKV_B * DH_B
NH_C = 8
DK_C = 128
DV_C = 128
CONV_W = 4
QKV_C = NH_C * (2 * DK_C + DV_C)
MIX_ODD = NH_C * DV_C
SIZES_ODD = (QKV_C, NH_C * DV_C, NH_C, NH_C)
IN_ODD = QKV_C + NH_C * DV_C + 2 * NH_C
N_EXPERTS = 32
TOP_K = 4
D_FF = 1024
SWIGLU_LIMIT = 7.0
SWIGLU_ALPHA = 1.702
MOE_BLOCK = 128
ALPHA_DN = (2 * DEPTH) ** 0.25
BETA_DN = (8 * DEPTH) ** -0.25
LN_EPS = 1e-5
RMS_EPS = 1e-6

kernel_name = 'hybrid_stream_encoder_step'


def _split(x, sizes):
    out, o = [], 0
    for s in sizes:
        out.append(x[..., o:o + s])
        o += s
    return out


def _layer_norm(x, g, b):
    xf = x.astype(jnp.float32)
    mu = xf.mean(-1, keepdims=True)
    var = jnp.square(xf - mu).mean(-1, keepdims=True)
    return ((xf - mu) * lax.rsqrt(var + LN_EPS) * g + b).astype(x.dtype)


def _head_layer_norm(x, g):
    mu = x.mean(-1, keepdims=True)
    var = jnp.square(x - mu).mean(-1, keepdims=True)
    return (x - mu) * lax.rsqrt(var + LN_EPS) * g


def _head_rms_norm(x, g):
    return x * lax.rsqrt(jnp.square(x).mean(-1, keepdims=True) + RMS_EPS) * g


def _l2norm(x):
    x = x.astype(jnp.float32)
    return x * lax.rsqrt(jnp.square(x).sum(-1, keepdims=True) + RMS_EPS)


def _modulate(x, shift, scale):
    return x * (1.0 + scale[:, None, :]) + shift[:, None, :]


def _post_norm(x, y, gate, g, b):
    return _layer_norm(ALPHA_DN * x + (1.0 + gate[:, None, :]) * y, g, b)


def _to_chunks(t, nc, cs):
    t = t.reshape((t.shape[0], nc, cs) + t.shape[2:])
    return t.transpose((1, 0, 3, 2) + tuple(range(4, t.ndim)))


def _from_chunks(t):
    nc, b, h, cs, d = t.shape
    return t.transpose(1, 0, 3, 2, 4).reshape(b, nc * cs, h, d)


def _rel_bucket(rel):
    nb = NUM_BUCKETS // 2
    max_exact = nb // 2
    n = jnp.abs(rel)
    nf = jnp.maximum(n, 1).astype(jnp.float32)
    large = max_exact + (jnp.log(nf / max_exact) / math.log(MAX_DISTANCE / max_exact)
                         * (nb - max_exact)).astype(jnp.int32)
    large = jnp.minimum(large, nb - 1)
    return jnp.where(rel > 0, nb, 0) + jnp.where(n < max_exact, n, large)


def _mlstm(q, k, v, ig, lf, C0, n0, m0):
    f32 = jnp.float32
    L = q.shape[1]
    cs = min(CHUNK, L)
    nc = L // cs
    causal = jnp.tril(jnp.ones((cs, cs), dtype=bool))

    def step(carry, inp):
        C, n, m = carry
        qc, kc, vc, ic, fc = inp
        b = jnp.cumsum(fc, axis=-1)
        logw = jnp.where(causal, b[..., :, None] - b[..., None, :] + ic[..., None, :], -jnp.inf)
        m_inter = b + m[..., None]
        m_t = jnp.maximum(m_inter, logw.max(-1))
        w = jnp.exp(logw - m_t[..., None])
        dec = jnp.exp(m_inter - m_t)
        s = jnp.einsum('bhtd,bhsd->bhts', qc, kc) * w
        num = dec[..., None] * jnp.einsum('bhtd,bhde->bhte', qc, C) + jnp.einsum('bhts,bhse->bhte', s, vc)
        den = dec * jnp.einsum('bhtd,bhd->bht', qc, n) + s.sum(-1)
        h = num / jnp.maximum(jnp.abs(den), jnp.exp(-m_t))[..., None]
        m_new = m_t[..., -1]
        kw = kc * jnp.exp(b[..., -1:] - b + ic - m_new[..., None])[..., None]
        carry_dec = jnp.exp(b[..., -1] + m - m_new)
        C_new = carry_dec[..., None, None] * C + jnp.einsum('bhsd,bhse->bhde', kw, vc)
        n_new = carry_dec[..., None] * n + kw.sum(2)
        return (C_new, n_new, m_new), h

    qs = q.astype(f32) * DK_A ** -0.5
    xs = (_to_chunks(qs, nc, cs), _to_chunks(k.astype(f32), nc, cs), _to_chunks(v.astype(f32), nc, cs),
          _to_chunks(ig, nc, cs), _to_chunks(lf, nc, cs))
    (C, n, m), h = lax.scan(step, (C0.astype(f32), n0.astype(f32), m0.astype(f32)), xs)
    return _from_chunks(h), C, n, m


def _swa_sink(q, k, v, k_hist, v_hist, hist_valid, sink, rel_bias):
    f32 = jnp.float32
    B, L = q.shape[:2]
    hl = k_hist.shape[1]
    cs = min(CHUNK, L)
    nc = L // cs
    kw = hl + cs
    k_all = jnp.concatenate([k_hist.astype(k.dtype), k], axis=1)
    v_all = jnp.concatenate([v_hist.astype(v.dtype), v], axis=1)
    idx = jnp.arange(nc)[:, None] * cs + jnp.arange(kw)[None, :]
    kb = k_all[:, idx]
    vb = v_all[:, idx]
    qb = q.reshape(B, nc, cs, NKV_B, G_B, DH_B)
    s = jnp.einsum('bcqhgd,bckhd->bchgqk', qb, kb).astype(f32) * DH_B ** -0.5
    rel = jnp.arange(kw)[None, :] - hl - jnp.arange(cs)[:, None]
    bias = rel_bias.astype(f32)[_rel_bucket(rel)]
    bias = bias.transpose(2, 0, 1).reshape(NKV_B, G_B, cs, kw)
    valid = (idx >= hl) | hist_valid
    s = jnp.where(valid[None, :, None, None, None, :], s + bias[None, None], -jnp.inf)
    sk = sink.astype(f32).reshape(NKV_B, G_B)[None, None, :, :, None, None]
    mx = jnp.maximum(s.max(-1, keepdims=True), sk)
    p = jnp.exp(s - mx)
    p = p / (p.sum(-1, keepdims=True) + jnp.exp(sk - mx))
    o = jnp.einsum('bchgqk,bckhd->bcqhgd', p.astype(vb.dtype), vb)
    return o.reshape(B, L, W_B), k_all[:, -hl:], v_all[:, -hl:]


def _mixer_ab(h, C0, n0, m0, k_hist, v_hist, hist_valid, w_in, b_in, norm_a, sink, rel_bias, w_out):
    B, L, _ = h.shape
    f32 = jnp.float32
    qa, ka, va, oa, ia, fa, qb, kb, vb = _split(h @ w_in + b_in, SIZES_EVEN)
    ha, C, n, m = _mlstm(qa.reshape(B, L, NH_A, DK_A), ka.reshape(B, L, NH_A, DK_A),
                         va.reshape(B, L, NH_A, DV_A), ia.astype(f32),
                         jax.nn.log_sigmoid(fa.astype(f32)), C0, n0, m0)
    ha = _head_layer_norm(ha, norm_a.astype(f32).reshape(NH_A, DV_A)) * \
        jax.nn.sigmoid(oa.reshape(B, L, NH_A, DV_A).astype(f32))
    hb, k_new, v_new = _swa_sink(qb.reshape(B, L, NH_B, DH_B), kb.reshape(B, L, NKV_B, DH_B),
                                 vb.reshape(B, L, NKV_B, DH_B), k_hist, v_hist, hist_valid,
                                 sink, rel_bias)
    mix = jnp.concatenate([ha.reshape(B, L, W_A).astype(h.dtype), hb.astype(h.dtype)], axis=-1)
    return mix @ w_out, C, n, m, k_new, v_new


def _gated_delta(q, k, v, beta, g, S0):
    f32 = jnp.float32
    L = q.shape[1]
    cs = min(CHUNK, L)
    nc = L // cs
    incl = jnp.tril(jnp.ones((cs, cs), dtype=bool))
    strict = jnp.tril(jnp.ones((cs, cs), dtype=bool), -1)

    def step(S, inp):
        qc, kc, vc, bc, gc = inp
        G = jnp.cumsum(gc, axis=-1)
        dmat = jnp.exp(jnp.where(incl, G[..., :, None] - G[..., None, :], -jnp.inf))
        eg = jnp.exp(G)[..., None]
        a_mat = jnp.where(strict, bc[..., :, None] * dmat * jnp.einsum('bhtd,bhsd->bhts', kc, kc), 0.0)
        rhs = bc[..., None] * (vc - eg * jnp.einsum('bhtd,bhde->bhte', kc, S))
        delta = lax.linalg.triangular_solve(a_mat, rhs, left_side=True, lower=True, unit_diagonal=True)
        o = eg * jnp.einsum('bhtd,bhde->bhte', qc, S) + \
            jnp.einsum('bhts,bhse->bhte', jnp.einsum('bhtd,bhsd->bhts', qc, kc) * dmat, delta)
        GL = G[..., -1:]
        S_new = jnp.exp(GL)[..., None] * S + \
            jnp.einsum('bhsd,bhse->bhde', kc * jnp.exp(GL - G)[..., None], delta)
        return S_new, o

    xs = (_to_chunks(q * DK_C ** -0.5, nc, cs), _to_chunks(k, nc, cs), _to_chunks(v.astype(f32), nc, cs),
          _to_chunks(beta, nc, cs), _to_chunks(g, nc, cs))
    S, o = lax.scan(step, S0.astype(f32), xs)
    return _from_chunks(o), S


def _causal_conv(x, hist, w):
    xa = jnp.concatenate([hist.astype(x.dtype), x], axis=1)
    L = x.shape[1]
    y = sum(xa[:, j:j + L] * w[j] for j in range(CONV_W))
    return y, xa[:, -(CONV_W - 1):]


def _mixer_c(h, conv_hist, S0, w_in, conv_w, a_log, dt_bias, norm_c, w_out):
    B, L, _ = h.shape
    f32 = jnp.float32
    qkv, z, bt, a = _split(h @ w_in, SIZES_ODD)
    qkv, conv_new = _causal_conv(qkv, conv_hist, conv_w)
    q, k, v = _split(jax.nn.silu(qkv), (NH_C * DK_C, NH_C * DK_C, NH_C * DV_C))
    q = _l2norm(q.reshape(B, L, NH_C, DK_C))
    k = _l2norm(k.reshape(B, L, NH_C, DK_C))
    beta = jax.nn.sigmoid(bt.astype(f32))
    g = -jnp.exp(a_log.astype(f32)) * jax.nn.softplus(a.astype(f32) + dt_bias.astype(f32))
    o, S = _gated_delta(q, k, v.reshape(B, L, NH_C, DV_C), beta, g, S0)
    o = _head_rms_norm(o, norm_c.astype(f32)) * jax.nn.silu(z.reshape(B, L, NH_C, DV_C).astype(f32))
    return o.reshape(B, L, MIX_ODD).astype(h.dtype) @ w_out, conv_new, S


def _moe(h, w_r, b_r, w1, b1, w2, b2):
    f32 = jnp.float32
    shp = h.shape
    x = h.reshape(-1, shp[-1])
    T = x.shape[0]
    logits = (x @ w_r + b_r).astype(f32)
    top_v, top_i = lax.top_k(logits, TOP_K)
    gates = jax.nn.softmax(top_v, axis=-1)
    n_slots = T * TOP_K
    flat_e = top_i.reshape(-1)
    order = jnp.argsort(flat_e)
    sorted_e = flat_e[order]
    counts = jnp.bincount(flat_e, length=N_EXPERTS)
    padded = (counts + MOE_BLOCK - 1) // MOE_BLOCK * MOE_BLOCK
    pad_end = jnp.cumsum(padded)
    pad_start = pad_end - padded
    start = jnp.cumsum(counts) - counts
    dest = pad_start[sorted_e] + jnp.arange(n_slots) - start[sorted_e]
    tok = order // TOP_K
    n_blocks = -(-n_slots // MOE_BLOCK) + N_EXPERTS
    row_tok = jnp.zeros((n_blocks * MOE_BLOCK,), jnp.int32).at[dest].set(tok)
    block_e = jnp.minimum(jnp.searchsorted(pad_end, jnp.arange(n_blocks) * MOE_BLOCK, side='right'),
                          N_EXPERTS - 1)
    xb = x[row_tok].reshape(n_blocks, MOE_BLOCK, shp[-1])

    def expert_block(args):
        xe, e = args
        u = (xe @ w1[e] + b1[e]).astype(f32)
        glu, lin = jnp.split(u, 2, axis=-1)
        glu = jnp.minimum(glu, SWIGLU_LIMIT)
        lin = jnp.clip(lin, -SWIGLU_LIMIT, SWIGLU_LIMIT)
        act = glu * jax.nn.sigmoid(SWIGLU_ALPHA * glu) * (lin + 1.0)
        return act.astype(xe.dtype) @ w2[e] + b2[e]

    yb = lax.map(expert_block, (xb, block_e)).reshape(n_blocks * MOE_BLOCK, shp[-1])
    w_slot = gates.reshape(-1)[order]
    out = jnp.zeros((T, shp[-1]), f32).at[tok].add(yb[dest].astype(f32) * w_slot[:, None])
    return out.astype(h.dtype).reshape(shp)


def setup_inputs(seed: int = 0) -> dict:
    key = jax.random.key(seed)
    ks = iter(jax.random.split(key, 48))

    def nrm(shape, s=1.0):
        return s * jax.random.normal(next(ks), shape, jnp.float32)

    hl = min(WINDOW, PAST_LEN)
    f_off = sum(SIZES_EVEN[:5])
    b_in_even = nrm((N_EVEN, IN_EVEN), 0.01)
    b_in_even = b_in_even.at[:, f_off:f_off + NH_A].add(3.0 + nrm((N_EVEN, NH_A), 0.5))
    dt = jnp.exp(jax.random.uniform(next(ks), (N_ODD, NH_C), jnp.float32,
                                    math.log(1e-3), math.log(1e-1)))
    return {
        'x_prompt': nrm((BATCH, SEQ, D_MODEL)),
        'x_sample': nrm((DEC_BATCH, DEC_SEQ, D_MODEL)),
        'c_prompt': nrm((BATCH, D_MODEL)),
        'c_sample': nrm((DEC_BATCH, D_MODEL)),
        'state_a_C': nrm((N_EVEN, DEC_BATCH, NH_A, DK_A, DV_A), 0.5),
        'state_a_n': nrm((N_EVEN, DEC_BATCH, NH_A, DK_A), 0.5),
        'state_a_m': nrm((N_EVEN, DEC_BATCH, NH_A)),
        'cache_b_k': nrm((N_EVEN, DEC_BATCH, hl, NKV_B, DH_B)),
        'cache_b_v': nrm((N_EVEN, DEC_BATCH, hl, NKV_B, DH_B)),
        'state_c_S': nrm((N_ODD, DEC_BATCH, NH_C, DK_C, DV_C), 0.3),
        'state_c_conv': nrm((N_ODD, DEC_BATCH, CONV_W - 1, QKV_C)),
        'w_ada': nrm((DEPTH, D_MODEL, 6 * D_MODEL), 0.1 * D_MODEL ** -0.5),
        'b_ada': nrm((DEPTH, 6 * D_MODEL), 0.01),
        'ln_g': 1.0 + nrm((DEPTH, 2, D_MODEL), 0.01),
        'ln_b': nrm((DEPTH, 2, D_MODEL), 0.01),
        'w_in_even': nrm((N_EVEN, D_MODEL, IN_EVEN), D_MODEL ** -0.5),
        'b_in_even': b_in_even,
        'norm_a': 1.0 + nrm((N_EVEN, W_A), 0.01),
        'sink_b': nrm((N_EVEN, NH_B), 0.5),
        'rel_bias': nrm((NUM_BUCKETS, NH_B), 0.2),
        'w_out_even': nrm((N_EVEN, MIX_EVEN, D_MODEL), BETA_DN * MIX_EVEN ** -0.5),
        'w_in_odd': nrm((N_ODD, D_MODEL, IN_ODD), D_MODEL ** -0.5),
        'conv_c': nrm((N_ODD, CONV_W, QKV_C), CONV_W ** -0.5),
        'a_log_c': jnp.log(jax.random.uniform(next(ks), (N_ODD, NH_C), jnp.float32, 1.0, 16.0)),
        'dt_bias_c': dt + jnp.log(-jnp.expm1(-dt)),
        'norm_c': 1.0 + nrm((N_ODD, DV_C), 0.01),
        'w_out_odd': nrm((N_ODD, MIX_ODD, D_MODEL), BETA_DN * MIX_ODD ** -0.5),
        'w_router': nrm((DEPTH, D_MODEL, N_EXPERTS), D_MODEL ** -0.5),
        'b_router': nrm((DEPTH, N_EXPERTS), 0.01),
        'w_e1': nrm((DEPTH, N_EXPERTS, D_MODEL, 2 * D_FF), D_MODEL ** -0.5),
        'b_e1': nrm((DEPTH, N_EXPERTS, 2 * D_FF), 0.01),
        'w_e2': nrm((DEPTH, N_EXPERTS, D_FF, D_MODEL), BETA_DN * D_FF ** -0.5),
        'b_e2': nrm((DEPTH, N_EXPERTS, D_MODEL), 0.01),
    }


def reference(x_prompt, x_sample, c_prompt, c_sample, state_a_C, state_a_n, state_a_m,
              cache_b_k, cache_b_v, state_c_S, state_c_conv, w_ada, b_ada, ln_g, ln_b,
              w_in_even, b_in_even, norm_a, sink_b, rel_bias, w_out_even, w_in_odd, conv_c,
              a_log_c, dt_bias_c, norm_c, w_out_odd, w_router, b_router, w_e1, b_e1, w_e2, b_e2):
    f32 = jnp.float32
    xp, xs = x_prompt, x_sample
    bp = xp.shape[0]
    pa_C, pa_n, pa_m, pb_k, pb_v, pc_S, pc_conv = [], [], [], [], [], [], []
    sa_C, sa_n, sa_m, sb_k, sb_v, sc_S, sc_conv = [], [], [], [], [], [], []
    for l in range(DEPTH):
        mp = _split(jax.nn.silu(c_prompt) @ w_ada[l] + b_ada[l], (D_MODEL,) * 6)
        ms = _split(jax.nn.silu(c_sample) @ w_ada[l] + b_ada[l], (D_MODEL,) * 6)
        hp = _modulate(xp, mp[0], mp[1])
        hs = _modulate(xs, ms[0], ms[1])
        if l % 2 == 0:
            e = l // 2
            wts = (w_in_even[e], b_in_even[e], norm_a[e], sink_b[e], rel_bias, w_out_even[e])
            zk = jnp.zeros((bp, WINDOW, NKV_B, DH_B), xp.dtype)
            yp, C1, n1, m1, k1, v1 = _mixer_ab(
                hp, jnp.zeros((bp, NH_A, DK_A, DV_A), f32), jnp.zeros((bp, NH_A, DK_A), f32),
                jnp.zeros((bp, NH_A), f32), zk, zk, False, *wts)
            ys, C2, n2, m2, k2, v2 = _mixer_ab(
                hs, state_a_C[e], state_a_n[e], state_a_m[e], cache_b_k[e], cache_b_v[e], True, *wts)
            pa_C.append(C1); pa_n.append(n1); pa_m.append(m1); pb_k.append(k1); pb_v.append(v1)
            sa_C.append(C2); sa_n.append(n2); sa_m.append(m2); sb_k.append(k2); sb_v.append(v2)
        else:
            o = l // 2
            wts = (w_in_odd[o], conv_c[o], a_log_c[o], dt_bias_c[o], norm_c[o], w_out_odd[o])
            yp, cv1, S1 = _mixer_c(hp, jnp.zeros((bp, CONV_W - 1, QKV_C), xp.dtype),
                                   jnp.zeros((bp, NH_C, DK_C, DV_C), f32), *wts)
            ys, cv2, S2 = _mixer_c(hs, state_c_conv[o], state_c_S[o], *wts)
            pc_conv.append(cv1); pc_S.append(S1)
            sc_conv.append(cv2); sc_S.append(S2)
        xp = _post_norm(xp, yp, mp[2], ln_g[l, 0], ln_b[l, 0])
        xs = _post_norm(xs, ys, ms[2], ln_g[l, 0], ln_b[l, 0])
        moe_w = (w_router[l], b_router[l], w_e1[l], b_e1[l], w_e2[l], b_e2[l])
        xp = _post_norm(xp, _moe(_modulate(xp, mp[3], mp[4]), *moe_w), mp[5], ln_g[l, 1], ln_b[l, 1])
        xs = _post_norm(xs, _moe(_modulate(xs, ms[3], ms[4]), *moe_w), ms[5], ln_g[l, 1], ln_b[l, 1])
    y_prompt = xp
    y_sample = xs
    p_a_C = jnp.stack(pa_C); p_a_n = jnp.stack(pa_n); p_a_m = jnp.stack(pa_m)
    p_b_k = jnp.stack(pb_k); p_b_v = jnp.stack(pb_v)
    p_c_S = jnp.stack(pc_S); p_c_conv = jnp.stack(pc_conv)
    s_a_C = jnp.stack(sa_C); s_a_n = jnp.stack(sa_n); s_a_m = jnp.stack(sa_m)
    s_b_k = jnp.stack(sb_k); s_b_v = jnp.stack(sb_v)
    s_c_S = jnp.stack(sc_S); s_c_conv = jnp.stack(sc_conv)
    return (y_prompt, y_sample, p_a_C, p_a_n, p_a_m, p_b_k, p_b_v, p_c_S, p_c_conv,
            s_a_C, s_a_n, s_a_m, s_b_k, s_b_v, s_c_S, s_c_conv)
```

```python
import functools
import math

import jax
import jax.numpy as jnp
from jax import lax
from jax.experimental import pallas as pl
from jax.experimental.pallas import tpu as pltpu

F32 = jnp.float32
BF16 = jnp.bfloat16
I32 = jnp.int32

D_MODEL = 1024
CHUNK = 64
NH_A, DK_A, DV_A = 4, 128, 128
W_A = NH_A * DV_A
NH_B, NKV_B, DH_B, WINDOW = 8, 2, 64, 128
G_B = NH_B // NKV_B
W_B = NH_B * DH_B
NUM_BUCKETS, MAX_DISTANCE = 32, 256
NH_C, DK_C, DV_C, CONV_W = 8, 128, 128, 4
QKV_C = NH_C * (2 * DK_C + DV_C)
N_EXPERTS, TOP_K, D_FF = 32, 4, 1024
SWIGLU_LIMIT, SWIGLU_ALPHA = 7.0, 1.702
DEPTH = 2
ALPHA_DN = (2 * DEPTH) ** 0.25
LN_EPS = 1e-5
RMS_EPS = 1e-6

LANES = 128
E_QA, E_KA, E_VA, E_OA, E_QB, E_KB, E_VB, E_G = 0, 512, 1024, 1536, 2048, 2560, 2688, 2816
N_EVEN_COLS = 2944
O_QKV, O_Z, O_G = 0, 3072, 4096
N_ODD_COLS = 4224

VMEM_LIMIT = 56 * 1024 * 1024


def _cparams(sem):
    return pltpu.CompilerParams(dimension_semantics=sem, vmem_limit_bytes=VMEM_LIMIT)


def _softplus(x):
    return jnp.maximum(x, 0.0) + jnp.log(1.0 + jnp.exp(-jnp.abs(x)))


def _sigmoid(x):
    return 1.0 / (1.0 + jnp.exp(-x))


def _split_bf16(a):
    hi = a.astype(BF16)
    lo = (a - hi.astype(F32)).astype(BF16)
    return hi, lo


_NN = (((1,), (0,)), ((), ()))
_NT = (((1,), (1,)), ((), ()))
_TN = (((0,), (0,)), ((), ()))


def _dot(a, b, dims=_NN):
    return lax.dot_general(a, b, dims, preferred_element_type=F32)


def _dot_x3(a, b, dims=_NN):
    ah, al = _split_bf16(a)
    bh, bl = _split_bf16(b)
    return _dot(ah, bh, dims) + _dot(ah, bl, dims) + _dot(al, bh, dims)


def _dot_exact_lhs(a_bf16, b, dims=_NN):
    b0 = b.astype(BF16)
    r1 = b - b0.astype(F32)
    b1 = r1.astype(BF16)
    b2 = (r1 - b1.astype(F32)).astype(BF16)
    return _dot(a_bf16, b0, dims) + _dot(a_bf16, b1, dims) + _dot(a_bf16, b2, dims)


def _tr(x):
    r = x.shape[0]
    rp = -(-r // LANES) * LANES
    if rp != r:
        x = jnp.concatenate([x, jnp.zeros((rp - r, x.shape[1]), x.dtype)], axis=0)
    return x.T[:, :r]


def _tri(n, strict=False):
    r = lax.broadcasted_iota(I32, (n, n), 0)
    c = lax.broadcasted_iota(I32, (n, n), 1)
    return (r > c) if strict else (r >= c)


def _layer_norm_rows(v, g, b):
    mu = jnp.mean(v, axis=-1, keepdims=True)
    d = v - mu
    var = jnp.mean(d * d, axis=-1, keepdims=True)
    return d * lax.rsqrt(var + LN_EPS) * g + b


def _ada_kernel(c_ref, w_ref, b_ref, o_ref):
    c = c_ref[...]
    a = (c * _sigmoid(c)).astype(BF16)
    o_ref[...] = _dot(a, w_ref[...].astype(BF16)) + b_ref[...]


def _ada(c_all, w_ada, b_ada):
    nb = c_all.shape[0]
    tn = 1536
    return pl.pallas_call(
        _ada_kernel,
        out_shape=jax.ShapeDtypeStruct((DEPTH, nb, 6 * D_MODEL), F32),
        grid=(DEPTH, 6 * D_MODEL // tn),
        in_specs=[
            pl.BlockSpec((nb, D_MODEL), lambda l, j: (0, 0)),
            pl.BlockSpec((None, D_MODEL, tn), lambda l, j: (l, 0, j)),
            pl.BlockSpec((None, 1, tn), lambda l, j: (l, 0, j)),
        ],
        out_specs=pl.BlockSpec((None, nb, tn), lambda l, j: (l, 0, j)),
        compiler_params=_cparams(("arbitrary", "arbitrary")),
        name="ada",
    )(c_all, w_ada, b_ada.reshape(DEPTH, 1, 6 * D_MODEL))


def _inproj_kernel(x_ref, sc_ref, sh_ref, w_ref, b_ref, o_ref, *, n_cols, col_step):
    h = (x_ref[...] * (1.0 + sc_ref[...]) + sh_ref[...]).astype(BF16)
    for c0 in range(0, n_cols, col_step):
        c1 = min(c0 + col_step, n_cols)
        o_ref[:, c0:c1] = _dot(h, w_ref[:, c0:c1]) + b_ref[:, c0:c1]


def _inproj(x2d, scale, shift, w_bf16, bias, seq_len, tm):
    t = x2d.shape[0]
    n = w_bf16.shape[1]
    per = seq_len // tm
    return pl.pallas_call(
        functools.partial(_inproj_kernel, n_cols=n, col_step=1024),
        out_shape=jax.ShapeDtypeStruct((t, n), F32),
        grid=(t // tm,),
        in_specs=[
            pl.BlockSpec((tm, D_MODEL), lambda i: (i, 0)),
            pl.BlockSpec((None, 1, D_MODEL), lambda i: (i // per, 0, 0)),
            pl.BlockSpec((None, 1, D_MODEL), lambda i: (i // per, 0, 0)),
            pl.BlockSpec((D_MODEL, n), lambda i: (0, 0)),
            pl.BlockSpec((1, n), lambda i: (0, 0)),
        ],
        out_specs=pl.BlockSpec((tm, n), lambda i: (i, 0)),
        compiler_params=_cparams(("arbitrary",)),
        name="inproj",
    )(x2d, scale, shift, w_bf16, bias)


def _even_kernel(proj_ref, kh0_ref, vh0_ref, c0_ref, n0_ref, m0_ref, bias_ref, norma_ref, sink_ref,
                 mix_ref, cout_ref, nout_ref, mout_ref, kout_ref, vout_ref,
                 c_s, n_s, m_s, kh_s, vh_s, *, rows, hist_valid):
    R = rows
    KW = WINDOW + R
    i = pl.program_id(1)

    @pl.when(i == 0)
    def _():
        c_s[...] = c0_ref[...]
        n_s[...] = n0_ref[...]
        m_s[...] = m0_ref[...]
        kh_s[...] = kh0_ref[...]
        vh_s[...] = vh0_ref[...]

    g = proj_ref[:, E_G:E_G + LANES]
    lf = -_softplus(-g)
    causal = _tri(R)
    tri_b = jnp.where(causal, 1.0, 0.0).astype(BF16)
    b_all = _dot_exact_lhs(tri_b, lf)
    g_t = _tr(g)
    b_t = _tr(b_all)
    scale_a = DK_A ** -0.5
    for h in range(NH_A):
        q = proj_ref[:, E_QA + h * DK_A:E_QA + (h + 1) * DK_A] * scale_a
        k = proj_ref[:, E_KA + h * DK_A:E_KA + (h + 1) * DK_A]
        v = proj_ref[:, E_VA + h * DV_A:E_VA + (h + 1) * DV_A]
        og = proj_ref[:, E_OA + h * DV_A:E_OA + (h + 1) * DV_A]
        b_col = b_all[:, NH_A + h:NH_A + h + 1]
        b_row = b_t[NH_A + h:NH_A + h + 1, :]
        i_col = g[:, h:h + 1]
        i_row = g_t[h:h + 1, :]
        m_prev = m_s[h:h + 1, 0:1]
        n_prev = n_s[h:h + 1, :]
        c_prev = c_s[h]
        logw = jnp.where(causal, b_col - b_row + i_row, -jnp.inf)
        m_inter = b_col + m_prev
        m_t = jnp.maximum(m_inter, jnp.max(logw, axis=-1, keepdims=True))
        w = jnp.exp(logw - m_t)
        dec = jnp.exp(m_inter - m_t)
        qb = q.astype(BF16)
        kb = k.astype(BF16)
        vb = v.astype(BF16)
        s = _dot(qb, kb, _NT) * w
        num = dec * _dot(qb, c_prev.astype(BF16)) + _dot(s.astype(BF16), vb)
        den = dec * jnp.sum(q * n_prev, axis=-1, keepdims=True) + jnp.sum(s, axis=-1, keepdims=True)
        hh = num / jnp.maximum(jnp.abs(den), jnp.exp(-m_t))
        m_new = m_t[R - 1:R, :]
        b_last = b_col[R - 1:R, :]
        kw = k * jnp.exp(b_last - b_col + i_col - m_new)
        carry = jnp.exp(b_last + m_prev - m_new)
        c_s[h] = carry * c_prev + _dot(_tr(kw).astype(BF16), vb)
        n_s[h:h + 1, :] = carry * n_prev + jnp.sum(kw, axis=0, keepdims=True)
        m_s[h:h + 1, :] = jnp.broadcast_to(m_new, (1, LANES))
        mu = jnp.mean(hh, axis=-1, keepdims=True)
        dd = hh - mu
        var = jnp.mean(dd * dd, axis=-1, keepdims=True)
        ha = dd * lax.rsqrt(var + LN_EPS) * norma_ref[h:h + 1, :] * _sigmoid(og)
        mix_ref[:, h * DV_A:(h + 1) * DV_A] = ha.astype(mix_ref.dtype)

    cout_ref[...] = c_s[...]
    nout_ref[...] = n_s[...]
    mout_ref[...] = m_s[...]

    k_win = jnp.concatenate([kh_s[...], proj_ref[:, E_KB:E_KB + NKV_B * DH_B]], axis=0)
    v_win = jnp.concatenate([vh_s[...], proj_ref[:, E_VB:E_VB + NKV_B * DH_B]], axis=0)
    k_win_b = k_win.astype(BF16)
    v_win_b = v_win.astype(BF16)
    if not hist_valid:
        key_pos = lax.broadcasted_iota(I32, (R, KW), 1) + (i * R - WINDOW)
        key_ok = key_pos >= 0
    scale_b = DH_B ** -0.5
    for hd in range(NH_B):
        kv = hd // G_B
        qh = proj_ref[:, E_QB + hd * DH_B:E_QB + (hd + 1) * DH_B].astype(BF16)
        kg = k_win_b[:, kv * DH_B:(kv + 1) * DH_B]
        vg = v_win_b[:, kv * DH_B:(kv + 1) * DH_B]
        s = _dot(qh, kg, _NT) * scale_b + bias_ref[hd]
        if not hist_valid:
            s = jnp.where(key_ok, s, -jnp.inf)
        sk = sink_ref[hd]
        mx = jnp.maximum(jnp.max(s, axis=-1, keepdims=True), sk)
        p = jnp.exp(s - mx)
        p = p / (jnp.sum(p, axis=-1, keepdims=True) + jnp.exp(sk - mx))
        o = _dot(p.astype(BF16), vg)
        mix_ref[:, W_A + hd * DH_B:W_A + (hd + 1) * DH_B] = o.astype(mix_ref.dtype)

    kh_s[...] = k_win[R:, :]
    vh_s[...] = v_win[R:, :]
    kout_ref[...] = kh_s[...]
    vout_ref[...] = vh_s[...]


def _rel_bucket(rel):
    nb = NUM_BUCKETS // 2
    max_exact = nb // 2
    n = jnp.abs(rel)
    nf = jnp.maximum(n, 1).astype(F32)
    large = max_exact + (jnp.log(nf / max_exact) / math.log(MAX_DISTANCE / max_exact)
                         * (nb - max_exact)).astype(I32)
    large = jnp.minimum(large, nb - 1)
    return jnp.where(rel > 0, nb, 0) + jnp.where(n < max_exact, n, large)


def _swa_bias_table(rel_bias, rows):
    kw = WINDOW + rows
    qi = jnp.arange(rows)[:, None]
    kj = jnp.arange(kw)[None, :]
    bias = rel_bias.astype(F32)[_rel_bucket(kj - WINDOW - qi)]
    lo = (qi // CHUNK) * CHUNK
    ok = (kj >= lo) & (kj < lo + WINDOW + CHUNK)
    bias = jnp.where(ok[:, :, None], bias, -jnp.inf)
    return bias.transpose(2, 0, 1)


def _even_mixer(proj, k_hist, v_hist, c0, n0, m0, bias_tab, norm_a, sink, batch, seq_len, rows, hist_valid):
    nsteps = seq_len // rows
    kw = WINDOW + rows
    m0p = jnp.broadcast_to(jnp.pad(m0, ((0, 0), (0, 8 - NH_A)))[:, :, None], (batch, 8, LANES))
    n0p = jnp.pad(n0, ((0, 0), (0, 8 - NH_A), (0, 0)))
    full3 = lambda b, i: (b, 0, 0)
    outs = pl.pallas_call(
        functools.partial(_even_kernel, rows=rows, hist_valid=hist_valid),
        out_shape=(
            jax.ShapeDtypeStruct((batch * seq_len, D_MODEL), BF16),
            jax.ShapeDtypeStruct((batch, NH_A, DK_A, DV_A), F32),
            jax.ShapeDtypeStruct((batch, 8, DK_A), F32),
            jax.ShapeDtypeStruct((batch, 8, LANES), F32),
            jax.ShapeDtypeStruct((batch, WINDOW, NKV_B * DH_B), F32),
            jax.ShapeDtypeStruct((batch, WINDOW, NKV_B * DH_B), F32),
        ),
        grid=(batch, nsteps),
        in_specs=[
            pl.BlockSpec((rows, N_EVEN_COLS), lambda b, i: (b * nsteps + i, 0)),
            pl.BlockSpec((None, WINDOW, NKV_B * DH_B), full3),
            pl.BlockSpec((None, WINDOW, NKV_B * DH_B), full3),
            pl.BlockSpec((None, NH_A, DK_A, DV_A), lambda b, i: (b, 0, 0, 0)),
            pl.BlockSpec((None, 8, DK_A), full3),
            pl.BlockSpec((None, 8, LANES), full3),
            pl.BlockSpec((NH_B, rows, kw), lambda b, i: (0, 0, 0)),
            pl.BlockSpec((NH_A, DV_A), lambda b, i: (0, 0)),
            pl.BlockSpec(memory_space=pltpu.SMEM),
        ],
        out_specs=(
            pl.BlockSpec((rows, D_MODEL), lambda b, i: (b * nsteps + i, 0)),
            pl.BlockSpec((None, NH_A, DK_A, DV_A), lambda b, i: (b, 0, 0, 0)),
            pl.BlockSpec((None, 8, DK_A), full3),
            pl.BlockSpec((None, 8, LANES), full3),
            pl.BlockSpec((None, WINDOW, NKV_B * DH_B), full3),
            pl.BlockSpec((None, WINDOW, NKV_B * DH_B), full3),
        ),
        scratch_shapes=[
            pltpu.VMEM((NH_A, DK_A, DV_A), F32),
            pltpu.VMEM((8, DK_A), F32),
            pltpu.VMEM((8, LANES), F32),
            pltpu.VMEM((WINDOW, NKV_B * DH_B), F32),
            pltpu.VMEM((WINDOW, NKV_B * DH_B), F32),
        ],
        compiler_params=_cparams(("arbitrary", "arbitrary")),
        name="even_mixer",
    )(proj, k_hist.reshape(batch, WINDOW, NKV_B * DH_B), v_hist.reshape(batch, WINDOW, NKV_B * DH_B),
      c0, n0p, m0p, bias_tab, norm_a.reshape(NH_A, DV_A), sink)
    mix, c1, n1, m1, k1, v1 = outs
    return (mix, c1, n1[:, :NH_A, :], m1[:, :NH_A, 0],
            k1.reshape(batch, WINDOW, NKV_B, DH_B), v1.reshape(batch, WINDOW, NKV_B, DH_B))


def _odd_kernel(proj_ref, conv0_ref, s0_ref, convw_ref, hp_ref, normc_ref,
                mix_ref, sout_ref, convout_ref, s_s, xa_s, *, rows):
    R = rows
    i = pl.program_id(1)
    HB = 8

    @pl.when(i == 0)
    def _():
        s_s[...] = s0_ref[...]
        xa_s[HB - (CONV_W - 1):HB, :] = conv0_ref[...]

    xa_s[HB:HB + R, :] = proj_ref[:, O_QKV:O_QKV + QKV_C]
    convout_ref[...] = xa_s[HB + R - (CONV_W - 1):HB + R, :]

    gcols = proj_ref[:, O_G:O_G + LANES]
    beta_all = _sigmoid(gcols)
    g_all = -jnp.exp(hp_ref[0:1, :]) * _softplus(gcols + hp_ref[1:2, :])
    incl = _tri(R)
    strict = _tri(R, strict=True)
    tri_b = jnp.where(incl, 1.0, 0.0).astype(BF16)
    G_all = _dot_exact_lhs(tri_b, g_all)
    G_t = _tr(G_all)
    eye = jnp.where(lax.broadcasted_iota(I32, (R, R), 0) == lax.broadcasted_iota(I32, (R, R), 1), 1.0, 0.0)
    scale_c = DK_C ** -0.5

    def conv_silu(c0):
        y = xa_s[HB - 3:HB - 3 + R, c0:c0 + LANES] * convw_ref[0:1, c0:c0 + LANES]
        for j in range(1, CONV_W):
            y = y + xa_s[HB - 3 + j:HB - 3 + j + R, c0:c0 + LANES] * convw_ref[j:j + 1, c0:c0 + LANES]
        return y * _sigmoid(y)

    def l2n(x):
        return x * lax.rsqrt(jnp.sum(x * x, axis=-1, keepdims=True) + RMS_EPS)

    for h in range(NH_C):
        q = l2n(conv_silu(h * DK_C))
        k = l2n(conv_silu(NH_C * DK_C + h * DK_C))
        v = conv_silu(2 * NH_C * DK_C + h * DV_C)
        z = proj_ref[:, O_Z + h * DV_C:O_Z + (h + 1) * DV_C]
        beta = beta_all[:, h:h + 1]
        G_col = G_all[:, NH_C + h:NH_C + h + 1]
        G_row = G_t[NH_C + h:NH_C + h + 1, :]
        dmat = jnp.exp(jnp.where(incl, G_col - G_row, -jnp.inf))
        eg = jnp.exp(G_col)
        S = s_s[h]
        Sb = S.astype(BF16)
        kb = k.astype(BF16)
        qb = (q * scale_c).astype(BF16)
        a_mat = jnp.where(strict, beta * dmat * _dot(kb, kb, _NT), 0.0)
        rhs = beta * (v - eg * _dot(kb, Sb))
        pw = -a_mat
        tinv = eye + pw
        nsq = max(1, (R - 1).bit_length()) - 1
        for _ in range(nsq):
            pw = _dot_x3(pw, pw)
            tinv = tinv + _dot_x3(tinv, pw)
        delta = _dot_x3(tinv, rhs)
        db = delta.astype(BF16)
        o = eg * _dot(qb, Sb) + _dot((_dot(qb, kb, _NT) * dmat).astype(BF16), db)
        GL = G_col[R - 1:R, :]
        kdec = k * jnp.exp(GL - G_col)
        s_s[h] = jnp.exp(GL) * S + _dot(_tr(kdec).astype(BF16), db)
        o = o * lax.rsqrt(jnp.mean(o * o, axis=-1, keepdims=True) + RMS_EPS) * normc_ref[...]
        o = o * (z * _sigmoid(z))
        mix_ref[:, h * DV_C:(h + 1) * DV_C] = o.astype(mix_ref.dtype)

    xa_s[HB - (CONV_W - 1):HB, :] = xa_s[HB + R - (CONV_W - 1):HB + R, :]
    sout_ref[...] = s_s[...]


def _odd_mixer(proj, conv_hist, s0, conv_w, a_log, dt_bias, norm_c, batch, seq_len):
    rows = CHUNK
    nsteps = seq_len // rows
    hp = jnp.zeros((8, LANES), F32)
    hp = hp.at[0, NH_C:2 * NH_C].set(a_log.astype(F32)).at[1, NH_C:2 * NH_C].set(dt_bias.astype(F32))
    outs = pl.pallas_call(
        functools.partial(_odd_kernel, rows=rows),
        out_shape=(
            jax.ShapeDtypeStruct((batch * seq_len, D_MODEL), BF16),
            jax.ShapeDtypeStruct((batch, NH_C, DK_C, DV_C), F32),
            jax.ShapeDtypeStruct((batch, CONV_W - 1, QKV_C), F32),
        ),
        grid=(batch, nsteps),
        in_specs=[
            pl.BlockSpec((rows, N_ODD_COLS), lambda b, i: (b * nsteps + i, 0)),
            pl.BlockSpec((None, CONV_W - 1, QKV_C), lambda b, i: (b, 0, 0)),
            pl.BlockSpec((None, NH_C, DK_C, DV_C), lambda b, i: (b, 0, 0, 0)),
            pl.BlockSpec((CONV_W, QKV_C), lambda b, i: (0, 0)),
            pl.BlockSpec((8, LANES), lambda b, i: (0, 0)),
            pl.BlockSpec((1, DV_C), lambda b, i: (0, 0)),
        ],
        out_specs=(
            pl.BlockSpec((rows, D_MODEL), lambda b, i: (b * nsteps + i, 0)),
            pl.BlockSpec((None, NH_C, DK_C, DV_C), lambda b, i: (b, 0, 0, 0)),
            pl.BlockSpec((None, CONV_W - 1, QKV_C), lambda b, i: (b, 0, 0)),
        ),
        scratch_shapes=[
            pltpu.VMEM((NH_C, DK_C, DV_C), F32),
            pltpu.VMEM((8 + rows, QKV_C), F32),
        ],
        compiler_params=_cparams(("arbitrary", "arbitrary")),
        name="odd_mixer",
    )(proj, conv_hist, s0, conv_w, hp, norm_c.reshape(1, DV_C))
    return outs


def _post_kernel(mix_ref, x_ref, gate_ref, sc_ref, sh_ref, wout_ref, lng_ref, lnb_ref, wr_ref, br_ref,
                 x1_ref, h2_ref, ri_ref, rg_ref, cnt_ref, carry_s, *, tm):
    i = pl.program_id(0)

    @pl.when(i == 0)
    def _():
        carry_s[...] = jnp.zeros_like(carry_s)

    y = _dot(mix_ref[...], wout_ref[...])
    x1 = _layer_norm_rows(ALPHA_DN * x_ref[...] + (1.0 + gate_ref[...]) * y, lng_ref[...], lnb_ref[...])
    x1_ref[...] = x1
    h2 = x1 * (1.0 + sc_ref[...]) + sh_ref[...]
    h2_ref[...] = h2
    lt = _dot_x3(wr_ref[...], h2, _NT) + br_ref[:, 0:1]
    e_iota = lax.broadcasted_iota(I32, (N_EXPERTS, tm), 0).astype(F32)
    vals, idxs = [], []
    for _ in range(TOP_K):
        mx = jnp.max(lt, axis=0, keepdims=True)
        idx = jnp.min(jnp.where(lt == mx, e_iota, float(N_EXPERTS)), axis=0, keepdims=True)
        vals.append(mx)
        idxs.append(idx)
        lt = jnp.where(e_iota == idx, -jnp.inf, lt)
    ex = [jnp.exp(v - vals[0]) for v in vals]
    tot = ex[0] + ex[1] + ex[2] + ex[3]
    hot = [jnp.where(e_iota == idx, 1.0, 0.0) for idx in idxs]
    m_all = hot[0] + hot[1] + hot[2] + hot[3]
    r = lax.broadcasted_iota(I32, (tm, tm), 0)
    c = lax.broadcasted_iota(I32, (tm, tm), 1)
    upper = jnp.where(r < c, 1.0, 0.0).astype(BF16)
    base = carry_s[:, 0:1] + _dot(m_all.astype(BF16), upper)
    ranks = [jnp.sum(hk * base, axis=0, keepdims=True) for hk in hot]
    carry_s[...] = carry_s[...] + jnp.sum(m_all, axis=1, keepdims=True)
    cnt_ref[...] = carry_s[...]
    ri_ref[...] = jnp.concatenate(idxs + ranks, axis=0).astype(I32)
    rg_ref[...] = jnp.concatenate([e / tot for e in ex] + [jnp.zeros((4, tm), F32)], axis=0)


def _post(mix, x2d, gate, scale, shift, w_out_bf16, ln_g, ln_b, wr_t, br, seq_len, tm):
    t = x2d.shape[0]
    per = seq_len // tm
    vec = lambda i: (i // per, 0, 0)
    const2 = lambda i: (0, 0)
    return pl.pallas_call(
        functools.partial(_post_kernel, tm=tm),
        out_shape=(
            jax.ShapeDtypeStruct((t, D_MODEL), F32),
            jax.ShapeDtypeStruct((t, D_MODEL), F32),
            jax.ShapeDtypeStruct((t // tm, 8, tm), I32),
            jax.ShapeDtypeStruct((t // tm, 8, tm), F32),
            jax.ShapeDtypeStruct((N_EXPERTS, LANES), F32),
        ),
        grid=(t // tm,),
        in_specs=[
            pl.BlockSpec((tm, D_MODEL), lambda i: (i, 0)),
            pl.BlockSpec((tm, D_MODEL), lambda i: (i, 0)),
            pl.BlockSpec((None, 1, D_MODEL), vec),
            pl.BlockSpec((None, 1, D_MODEL), vec),
            pl.BlockSpec((None, 1, D_MODEL), vec),
            pl.BlockSpec((D_MODEL, D_MODEL), const2),
            pl.BlockSpec((1, D_MODEL), const2),
            pl.BlockSpec((1, D_MODEL), const2),
            pl.BlockSpec((N_EXPERTS, D_MODEL), const2),
            pl.BlockSpec((N_EXPERTS, LANES), const2),
        ],
        out_specs=(
            pl.BlockSpec((tm, D_MODEL), lambda i: (i, 0)),
            pl.BlockSpec((tm, D_MODEL), lambda i: (i, 0)),
            pl.BlockSpec((None, 8, tm), lambda i: (i, 0, 0)),
            pl.BlockSpec((None, 8, tm), lambda i: (i, 0, 0)),
            pl.BlockSpec((N_EXPERTS, LANES), const2),
        ),
        scratch_shapes=[pltpu.VMEM((N_EXPERTS, LANES), F32)],
        compiler_params=_cparams(("arbitrary",)),
        name="post_router",
    )(mix, x2d, gate, scale, shift, w_out_bf16, ln_g, ln_b, wr_t, br)


def _dispatch_kernel(zblk_ref, pos_ref, h_ref, xs_ref, zero_s, sem, zsem, *, tm, bm):
    i = pl.program_id(0)

    def zero_copy(e):
        return pltpu.make_async_copy(zero_s, xs_ref.at[pl.ds(zblk_ref[e] * bm, bm), :], zsem)

    @pl.when(i == 0)
    def _():
        zero_s[...] = jnp.zeros_like(zero_s)

        def zstart(e, c):
            @pl.when(zblk_ref[e] >= 0)
            def _():
                zero_copy(e).start()
            return c

        def zwait(e, c):
            @pl.when(zblk_ref[e] >= 0)
            def _():
                zero_copy(e).wait()
            return c

        lax.fori_loop(0, N_EXPERTS, zstart, 0)
        lax.fori_loop(0, N_EXPERTS, zwait, 0)

    def body(t, c):
        for k in range(TOP_K):
            p = pos_ref[k, t]
            pltpu.make_async_copy(h_ref.at[pl.ds(t, 1), :], xs_ref.at[pl.ds(p, 1), :], sem).start()
        return c

    lax.fori_loop(0, tm, body, 0)
    for k in range(TOP_K):
        pltpu.make_async_copy(h_ref, xs_ref.at[pl.ds(0, tm), :], sem).wait()


def _dispatch(h2, pos, zblk, n_rows, tm, bm):
    t = h2.shape[0]
    return pl.pallas_call(
        functools.partial(_dispatch_kernel, tm=tm, bm=bm),
        out_shape=jax.ShapeDtypeStruct((n_rows, D_MODEL), F32),
        grid_spec=pltpu.PrefetchScalarGridSpec(
            num_scalar_prefetch=1,
            grid=(t // tm,),
            in_specs=[
                pl.BlockSpec((None, TOP_K, tm), lambda i, z: (i, 0, 0), memory_space=pltpu.SMEM),
                pl.BlockSpec((tm, D_MODEL), lambda i, z: (i, 0)),
            ],
            out_specs=pl.BlockSpec(memory_space=pl.ANY),
            scratch_shapes=[
                pltpu.VMEM((bm, D_MODEL), F32),
                pltpu.SemaphoreType.DMA,
                pltpu.SemaphoreType.DMA,
            ],
        ),
        compiler_params=_cparams(("arbitrary",)),
        name="dispatch",
    )(zblk, _tile_major(pos, tm), h2)


def _ffn_kernel(be_ref, nv_ref, x_ref, w1_ref, b1_ref, w2_ref, b2_ref, y_ref):
    b = pl.program_id(0)

    @pl.when(b < nv_ref[0])
    def _():
        xb = x_ref[...].astype(BF16)
        glu = _dot(xb, w1_ref[:, :D_FF]) + b1_ref[:, :D_FF]
        lin = _dot(xb, w1_ref[:, D_FF:]) + b1_ref[:, D_FF:]
        glu = jnp.minimum(glu, SWIGLU_LIMIT)
        lin = jnp.clip(lin, -SWIGLU_LIMIT, SWIGLU_LIMIT)
        act = glu * _sigmoid(SWIGLU_ALPHA * glu) * (lin + 1.0)
        y_ref[...] = _dot(act.astype(BF16), w2_ref[...]) + b2_ref[...]


def _ffn(xs, block_e, n_valid, w1_bf16, b1, w2_bf16, b2, bm):
    nb = xs.shape[0] // bm
    row = lambda b, be, nv: (jnp.minimum(b, nv[0] - 1), 0)
    wsel = lambda b, be, nv: (be[b], 0, 0)
    return pl.pallas_call(
        _ffn_kernel,
        out_shape=jax.ShapeDtypeStruct(xs.shape, F32),
        grid_spec=pltpu.PrefetchScalarGridSpec(
            num_scalar_prefetch=2,
            grid=(nb,),
            in_specs=[
                pl.BlockSpec((bm, D_MODEL), row),
                pl.BlockSpec((None, D_MODEL, 2 * D_FF), wsel),
                pl.BlockSpec((None, 1, 2 * D_FF), wsel),
                pl.BlockSpec((None, D_FF, D_MODEL), wsel),
                pl.BlockSpec((None, 1, D_MODEL), wsel),
            ],
            out_specs=pl.BlockSpec((bm, D_MODEL), row),
        ),
        compiler_params=_cparams(("arbitrary",)),
        name="expert_ffn",
    )(block_e, n_valid, xs, w1_bf16, b1.reshape(N_EXPERTS, 1, 2 * D_FF), w2_bf16,
      b2.reshape(N_EXPERTS, 1, D_MODEL))


def _combine_kernel(pos_ref, x_ref, gate_ref, rg_ref, lng_ref, lnb_ref, ys_ref, o_ref, buf, sem, *, tm):
    def body(t, c):
        for k in range(TOP_K):
            p = pos_ref[k, t]
            pltpu.make_async_copy(ys_ref.at[pl.ds(p, 1), :], buf.at[k, pl.ds(t, 1), :], sem).start()
        return c

    lax.fori_loop(0, tm, body, 0)
    for k in range(TOP_K):
        pltpu.make_async_copy(ys_ref.at[pl.ds(0, tm), :], buf.at[k], sem).wait()
    g_t = _tr(jnp.concatenate([rg_ref[...], jnp.zeros((LANES - 8, tm), F32)], axis=0))
    moe = g_t[:, 0:1] * buf[0]
    for k in range(1, TOP_K):
        moe = moe + g_t[:, k:k + 1] * buf[k]
    o_ref[...] = _layer_norm_rows(ALPHA_DN * x_ref[...] + (1.0 + gate_ref[...]) * moe,
                                  lng_ref[...], lnb_ref[...])


def _combine(pos, x1, gate, route_g, ln_g, ln_b, ys, seq_len, tm):
    t = x1.shape[0]
    per = seq_len // tm
    return pl.pallas_call(
        functools.partial(_combine_kernel, tm=tm),
        out_shape=jax.ShapeDtypeStruct((t, D_MODEL), F32),
        grid=(t // tm,),
        in_specs=[
            pl.BlockSpec((None, TOP_K, tm), lambda i: (i, 0, 0), memory_space=pltpu.SMEM),
            pl.BlockSpec((tm, D_MODEL), lambda i: (i, 0)),
            pl.BlockSpec((None, 1, D_MODEL), lambda i: (i // per, 0, 0)),
            pl.BlockSpec((None, 8, tm), lambda i: (i, 0, 0)),
            pl.BlockSpec((1, D_MODEL), lambda i: (0, 0)),
            pl.BlockSpec((1, D_MODEL), lambda i: (0, 0)),
            pl.BlockSpec(memory_space=pl.ANY),
        ],
        out_specs=pl.BlockSpec((tm, D_MODEL), lambda i: (i, 0)),
        scratch_shapes=[pltpu.VMEM((TOP_K, tm, D_MODEL), F32), pltpu.SemaphoreType.DMA],
        compiler_params=_cparams(("arbitrary",)),
        name="combine",
    )(_tile_major(pos, tm), x1, gate, _tile_major(route_g, tm), ln_g, ln_b, ys)


def _tile_major(a, tm):
    k, t = a.shape
    return a.reshape(k, t // tm, tm).transpose(1, 0, 2)


def _token_major(a):
    n, k, tm = a.shape
    return a.transpose(1, 0, 2).reshape(k, n * tm)


def _moe_tables(route_i, counts, bm, n_blocks):
    cnt = counts[:, 0].astype(I32)
    nblk = (cnt + bm - 1) // bm
    blk_end = jnp.cumsum(nblk)
    blk_start = blk_end - nblk
    pos = blk_start[route_i[:TOP_K]] * bm + route_i[TOP_K:]
    n_valid = blk_end[-1]
    bidx = jnp.minimum(jnp.arange(n_blocks, dtype=I32), n_valid - 1)
    block_e = jnp.minimum(jnp.searchsorted(blk_end, bidx, side='right'), N_EXPERTS - 1).astype(I32)
    zblk = jnp.where((cnt % bm) != 0, blk_end - 1, -1).astype(I32)
    return pos.astype(I32), block_e, n_valid.reshape(1).astype(I32), zblk


def _moe_and_norm(h2, x1, route_i, route_g, counts, gate, ln_g, ln_b, w1, b1, w2, b2, seq_len, tm_d, tm_c, bm):
    t = h2.shape[0]
    route_i = _token_major(route_i)
    route_g = _token_major(route_g)
    n_blocks = t * TOP_K // bm + N_EXPERTS
    pos, block_e, n_valid, zblk = _moe_tables(route_i, counts, bm, n_blocks)
    xs = _dispatch(h2, pos, zblk, n_blocks * bm, tm_d, bm)
    ys = _ffn(xs, block_e, n_valid, w1, b1, w2, b2, bm)
    return _combine(pos, x1, gate, route_g, ln_g, ln_b, ys, seq_len, tm_c)


def _group_cfg(batch, seq_len):
    if seq_len >= 512:
        return dict(tm=512, rows=256, tm_c=256, bm=256)
    return dict(tm=seq_len, rows=seq_len, tm_c=seq_len, bm=128)


def kernel(x_prompt, x_sample, c_prompt, c_sample, state_a_C, state_a_n, state_a_m, cache_b_k, cache_b_v,
           state_c_S, state_c_conv, w_ada, b_ada, ln_g, ln_b, w_in_even, b_in_even, norm_a, sink_b, rel_bias,
           w_out_even, w_in_odd, conv_c, a_log_c, dt_bias_c, norm_c, w_out_odd, w_router, b_router,
           w_e1, b_e1, w_e2, b_e2):
    bp, lp, _ = x_prompt.shape
    bs, ls, _ = x_sample.shape
    groups = [dict(b=bp, l=lp, x=x_prompt.reshape(bp * lp, D_MODEL), **_group_cfg(bp, lp)),
              dict(b=bs, l=ls, x=x_sample.reshape(bs * ls, D_MODEL), **_group_cfg(bs, ls))]
    mod = _ada(jnp.concatenate([c_prompt, c_sample], axis=0), w_ada, b_ada)
    offs = [0, bp]
    states = [dict(), dict()]
    for l in range(DEPTH):
        e = l // 2
        if l % 2 == 0:
            w = w_in_even[e]
            sz = (512, 512, 512, 512, 4, 4, 512, 128, 128)
            o = [sum(sz[:j]) for j in range(len(sz) + 1)]
            pad = jnp.zeros((D_MODEL, LANES - 2 * NH_A), w.dtype)
            w_in = jnp.concatenate([w[:, o[0]:o[4]], w[:, o[6]:o[9]], w[:, o[4]:o[6]], pad], axis=1).astype(BF16)
            bb = b_in_even[e]
            b_in = jnp.concatenate([bb[o[0]:o[4]], bb[o[6]:o[9]], bb[o[4]:o[6]],
                                    jnp.zeros((LANES - 2 * NH_A,), bb.dtype)]).reshape(1, N_EVEN_COLS)
            w_out = w_out_even[e].astype(BF16)
        else:
            w = w_in_odd[e]
            pad = jnp.zeros((D_MODEL, LANES - 2 * NH_C), w.dtype)
            w_in = jnp.concatenate([w, pad], axis=1).astype(BF16)
            b_in = jnp.zeros((1, N_ODD_COLS), F32)
            w_out = w_out_odd[e].astype(BF16)
        wr_t = w_router[l].T
        br = jnp.broadcast_to(b_router[l][:, None], (N_EXPERTS, LANES))
        w1 = w_e1[l].astype(BF16)
        w2 = w_e2[l].astype(BF16)
        for gi, gr in enumerate(groups):
            nb, sl = gr['b'], gr['l']
            m = mod[l, offs[gi]:offs[gi] + nb].reshape(nb, 6, 1, D_MODEL)
            m6 = [m[:, j] for j in range(6)]
            proj = _inproj(gr['x'], m6[1], m6[0], w_in, b_in, sl, gr['tm'])
            st = states[gi]
            if l % 2 == 0:
                if gi == 0:
                    kh = jnp.zeros((nb, WINDOW, NKV_B, DH_B), F32)
                    vh = kh
                    c0 = jnp.zeros((nb, NH_A, DK_A, DV_A), F32)
                    n0 = jnp.zeros((nb, NH_A, DK_A), F32)
                    m0 = jnp.zeros((nb, NH_A), F32)
                else:
                    kh, vh, c0, n0, m0 = cache_b_k[e], cache_b_v[e], state_a_C[e], state_a_n[e], state_a_m[e]
                bias_tab = _swa_bias_table(rel_bias, gr['rows'])
                mix, c1, n1, m1, k1, v1 = _even_mixer(proj, kh, vh, c0, n0, m0, bias_tab, norm_a[e], sink_b[e],
                                                      nb, sl, gr['rows'], gi == 1)
                for name, val in (('a_C', c1), ('a_n', n1), ('a_m', m1), ('b_k', k1), ('b_v', v1)):
                    st.setdefault(name, []).append(val)
            else:
                if gi == 0:
                    cv0 = jnp.zeros((nb, CONV_W - 1, QKV_C), F32)
                    s0 = jnp.zeros((nb, NH_C, DK_C, DV_C), F32)
                else:
                    cv0, s0 = state_c_conv[e], state_c_S[e]
                mix, s1, cv1 = _odd_mixer(proj, cv0, s0, conv_c[e], a_log_c[e], dt_bias_c[e], norm_c[e], nb, sl)
                st.setdefault('c_S', []).append(s1)
                st.setdefault('c_conv', []).append(cv1)
            x1, h2, route_i, route_g, counts = _post(
                mix, gr['x'], m6[2], m6[4], m6[3], w_out, ln_g[l, 0].reshape(1, D_MODEL),
                ln_b[l, 0].reshape(1, D_MODEL), wr_t, br, sl, gr['tm'])
            gr['x'] = _moe_and_norm(h2, x1, route_i, route_g, counts, m6[5], ln_g[l, 1].reshape(1, D_MODEL),
                                    ln_b[l, 1].reshape(1, D_MODEL), w1, b_e1[l], w2, b_e2[l],
                                    sl, gr['tm'], gr['tm_c'], gr['bm'])
    outs = [groups[0]['x'].reshape(bp, lp, D_MODEL), groups[1]['x'].reshape(bs, ls, D_MODEL)]
    for gi in range(2):
        for name in ('a_C', 'a_n', 'a_m', 'b_k', 'b_v', 'c_S', 'c_conv'):
            outs.append(jnp.stack(states[gi][name]))
    return tuple(outs)
```

```python
import functools
import math

import jax
import jax.numpy as jnp
from jax import lax
from jax.experimental import pallas as pl
from jax.experimental.pallas import tpu as pltpu

F32 = jnp.float32
BF16 = jnp.bfloat16
I32 = jnp.int32

D_MODEL = 1024
CHUNK = 64
NH_A, DK_A, DV_A = 4, 128, 128
W_A = NH_A * DV_A
NH_B, NKV_B, DH_B, WINDOW = 8, 2, 64, 128
G_B = NH_B // NKV_B
W_B = NH_B * DH_B
NUM_BUCKETS, MAX_DISTANCE = 32, 256
NH_C, DK_C, DV_C, CONV_W = 8, 128, 128, 4
QKV_C = NH_C * (2 * DK_C + DV_C)
N_EXPERTS, TOP_K, D_FF = 32, 4, 1024
SWIGLU_LIMIT, SWIGLU_ALPHA = 7.0, 1.702
DEPTH = 2
ALPHA_DN = (2 * DEPTH) ** 0.25
LN_EPS = 1e-5
RMS_EPS = 1e-6

LANES = 128
E_QA, E_KA, E_VA, E_OA, E_QB, E_KB, E_VB, E_G = 0, 512, 1024, 1536, 2048, 2560, 2688, 2816
N_EVEN_COLS = 2944
O_QKV, O_Z, O_G = 0, 3072, 4096
N_ODD_COLS = 4224

VMEM_LIMIT = 56 * 1024 * 1024


def _cparams(sem):
    return pltpu.CompilerParams(dimension_semantics=sem, vmem_limit_bytes=VMEM_LIMIT)


def _softplus(x):
    return jnp.maximum(x, 0.0) + jnp.log(1.0 + jnp.exp(-jnp.abs(x)))


def _sigmoid(x):
    return 1.0 / (1.0 + jnp.exp(-x))


def _split_bf16(a):
    hi = a.astype(BF16)
    lo = (a - hi.astype(F32)).astype(BF16)
    return hi, lo


_NN = (((1,), (0,)), ((), ()))
_NT = (((1,), (1,)), ((), ()))
_TN = (((0,), (0,)), ((), ()))


def _dot(a, b, dims=_NN):
    return lax.dot_general(a, b, dims, preferred_element_type=F32)


def _dot_x3(a, b, dims=_NN):
    ah, al = _split_bf16(a)
    bh, bl = _split_bf16(b)
    return _dot(ah, bh, dims) + _dot(ah, bl, dims) + _dot(al, bh, dims)


def _dot_exact_lhs(a_bf16, b, dims=_NN):
    b0 = b.astype(BF16)
    r1 = b - b0.astype(F32)
    b1 = r1.astype(BF16)
    b2 = (r1 - b1.astype(F32)).astype(BF16)
    return _dot(a_bf16, b0, dims) + _dot(a_bf16, b1, dims) + _dot(a_bf16, b2, dims)


def _tr(x):
    r = x.shape[0]
    rp = -(-r // LANES) * LANES
    if rp != r:
        x = jnp.concatenate([x, jnp.zeros((rp - r, x.shape[1]), x.dtype)], axis=0)
    return x.T[:, :r]


def _tri(n, strict=False):
    r = lax.broadcasted_iota(I32, (n, n), 0)
    c = lax.broadcasted_iota(I32, (n, n), 1)
    return (r > c) if strict else (r >= c)


def _layer_norm_rows(v, g, b):
    mu = jnp.mean(v, axis=-1, keepdims=True)
    d = v - mu
    var = jnp.mean(d * d, axis=-1, keepdims=True)
    return d * lax.rsqrt(var + LN_EPS) * g + b


def _ada_kernel(c_ref, w_ref, b_ref, o_ref):
    c = c_ref[...]
    a = (c * _sigmoid(c)).astype(BF16)
    o_ref[...] = _dot(a, w_ref[...].astype(BF16)) + b_ref[...]


def _ada(c_all, w_ada, b_ada):
    nb = c_all.shape[0]
    tn = 1536
    return pl.pallas_call(
        _ada_kernel,
        out_shape=jax.ShapeDtypeStruct((DEPTH, nb, 6 * D_MODEL), F32),
        grid=(DEPTH, 6 * D_MODEL // tn),
        in_specs=[
            pl.BlockSpec((nb, D_MODEL), lambda l, j: (0, 0)),
            pl.BlockSpec((None, D_MODEL, tn), lambda l, j: (l, 0, j)),
            pl.BlockSpec((None, 1, tn), lambda l, j: (l, 0, j)),
        ],
        out_specs=pl.BlockSpec((None, nb, tn), lambda l, j: (l, 0, j)),
        compiler_params=_cparams(("arbitrary", "arbitrary")),
        name="ada",
    )(c_all, w_ada, b_ada.reshape(DEPTH, 1, 6 * D_MODEL))


def _inproj_kernel(x_ref, sc_ref, sh_ref, w_ref, b_ref, o_ref, *, n_cols, col_step):
    h = (x_ref[...] * (1.0 + sc_ref[...]) + sh_ref[...]).astype(BF16)
    for c0 in range(0, n_cols, col_step):
        c1 = min(c0 + col_step, n_cols)
        o_ref[:, c0:c1] = _dot(h, w_ref[:, c0:c1]) + b_ref[:, c0:c1]


def _inproj(x2d, scale, shift, w_bf16, bias, seq_len, tm):
    t = x2d.shape[0]
    n = w_bf16.shape[1]
    per = seq_len // tm
    return pl.pallas_call(
        functools.partial(_inproj_kernel, n_cols=n, col_step=1024),
        out_shape=jax.ShapeDtypeStruct((t, n), F32),
        grid=(t // tm,),
        in_specs=[
            pl.BlockSpec((tm, D_MODEL), lambda i: (i, 0)),
            pl.BlockSpec((None, 1, D_MODEL), lambda i: (i // per, 0, 0)),
            pl.BlockSpec((None, 1, D_MODEL), lambda i: (i // per, 0, 0)),
            pl.BlockSpec((D_MODEL, n), lambda i: (0, 0)),
            pl.BlockSpec((1, n), lambda i: (0, 0)),
        ],
        out_specs=pl.BlockSpec((tm, n), lambda i: (i, 0)),
        compiler_params=_cparams(("arbitrary",)),
        name="inproj",
    )(x2d, scale, shift, w_bf16, bias)


def _even_kernel(proj_ref, kh0_ref, vh0_ref, c0_ref, n0_ref, m0_ref, bias_ref, norma_ref, sink_ref,
                 mix_ref, cout_ref, nout_ref, mout_ref, kout_ref, vout_ref,
                 c_s, n_s, m_s, kh_s, vh_s, *, rows, hist_valid):
    R = rows
    KW = WINDOW + R
    i = pl.program_id(1)

    @pl.when(i == 0)
    def _():
        c_s[...] = c0_ref[...]
        n_s[...] = n0_ref[...]
        m_s[...] = m0_ref[...]
        kh_s[...] = kh0_ref[...]
        vh_s[...] = vh0_ref[...]

    CS = CHUNK
    g = proj_ref[:, E_G:E_G + LANES]
    lf = -_softplus(-g)
    rr = lax.broadcasted_iota(I32, (R, R), 0)
    cc = lax.broadcasted_iota(I32, (R, R), 1)
    tri_b = jnp.where((rr >= cc) & (rr // CS == cc // CS), 1.0, 0.0).astype(BF16)
    b_all = _dot_exact_lhs(tri_b, lf)
    g_t = _tr(g)
    b_t = _tr(b_all)
    causal = _tri(CS)
    scale_a = DK_A ** -0.5
    for h in range(NH_A):
        m_prev = m_s[h:h + 1, 0:1]
        n_prev = n_s[h:h + 1, :]
        c_prev = c_s[h]
        for c0 in range(0, R, CS):
            rs = slice(c0, c0 + CS)
            q = proj_ref[rs, E_QA + h * DK_A:E_QA + (h + 1) * DK_A] * scale_a
            k = proj_ref[rs, E_KA + h * DK_A:E_KA + (h + 1) * DK_A]
            v = proj_ref[rs, E_VA + h * DV_A:E_VA + (h + 1) * DV_A]
            og = proj_ref[rs, E_OA + h * DV_A:E_OA + (h + 1) * DV_A]
            b_col = b_all[rs, NH_A + h:NH_A + h + 1]
            b_row = b_t[NH_A + h:NH_A + h + 1, rs]
            i_col = g[rs, h:h + 1]
            i_row = g_t[h:h + 1, rs]
            logw = jnp.where(causal, b_col - b_row + i_row, -jnp.inf)
            m_inter = b_col + m_prev
            m_t = jnp.maximum(m_inter, jnp.max(logw, axis=-1, keepdims=True))
            w = jnp.exp(logw - m_t)
            dec = jnp.exp(m_inter - m_t)
            qb = q.astype(BF16)
            kb = k.astype(BF16)
            vb = v.astype(BF16)
            s = _dot(qb, kb, _NT) * w
            num = dec * _dot(qb, c_prev.astype(BF16)) + _dot(s.astype(BF16), vb)
            qn = jnp.sum(qb.astype(F32) * n_prev.astype(BF16).astype(F32), axis=-1, keepdims=True)
            den = dec * qn + jnp.sum(s, axis=-1, keepdims=True)
            hh = num / jnp.maximum(jnp.abs(den), jnp.exp(-m_t))
            m_new = m_t[CS - 1:CS, :]
            b_last = b_col[CS - 1:CS, :]
            kw = k * jnp.exp(b_last - b_col + i_col - m_new)
            carry = jnp.exp(b_last + m_prev - m_new)
            c_prev = carry * c_prev + _dot(_tr(kw).astype(BF16), vb)
            n_prev = carry * n_prev + jnp.sum(kw, axis=0, keepdims=True)
            m_prev = m_new
            mu = jnp.mean(hh, axis=-1, keepdims=True)
            dd = hh - mu
            var = jnp.mean(dd * dd, axis=-1, keepdims=True)
            ha = dd * lax.rsqrt(var + LN_EPS) * norma_ref[h:h + 1, :] * _sigmoid(og)
            mix_ref[rs, h * DV_A:(h + 1) * DV_A] = ha.astype(mix_ref.dtype)
        c_s[h] = c_prev
        n_s[h:h + 1, :] = n_prev
        m_s[h:h + 1, :] = jnp.broadcast_to(m_prev, (1, LANES))

    cout_ref[...] = c_s[...]
    nout_ref[...] = n_s[...]
    mout_ref[...] = m_s[...]

    k_win = jnp.concatenate([kh_s[...], proj_ref[:, E_KB:E_KB + NKV_B * DH_B]], axis=0)
    v_win = jnp.concatenate([vh_s[...], proj_ref[:, E_VB:E_VB + NKV_B * DH_B]], axis=0)
    k_win_b = k_win.astype(BF16)
    v_win_b = v_win.astype(BF16)
    if not hist_valid:
        key_pos = lax.broadcasted_iota(I32, (R, KW), 1) + (i * R - WINDOW)
        key_ok = key_pos >= 0
    scale_b = DH_B ** -0.5
    for hd in range(NH_B):
        kv = hd // G_B
        qh = proj_ref[:, E_QB + hd * DH_B:E_QB + (hd + 1) * DH_B].astype(BF16)
        kg = k_win_b[:, kv * DH_B:(kv + 1) * DH_B]
        vg = v_win_b[:, kv * DH_B:(kv + 1) * DH_B]
        s = _dot(qh, kg, _NT) * scale_b + bias_ref[hd]
        if not hist_valid:
            s = jnp.where(key_ok, s, -jnp.inf)
        sk = sink_ref[hd]
        mx = jnp.maximum(jnp.max(s, axis=-1, keepdims=True), sk)
        p = jnp.exp(s - mx)
        p = p / (jnp.sum(p, axis=-1, keepdims=True) + jnp.exp(sk - mx))
        o = _dot(p.astype(BF16), vg)
        mix_ref[:, W_A + hd * DH_B:W_A + (hd + 1) * DH_B] = o.astype(mix_ref.dtype)

    kh_s[...] = k_win[R:, :]
    vh_s[...] = v_win[R:, :]
    kout_ref[...] = kh_s[...]
    vout_ref[...] = vh_s[...]


def _rel_bucket(rel):
    nb = NUM_BUCKETS // 2
    max_exact = nb // 2
    n = jnp.abs(rel)
    nf = jnp.maximum(n, 1).astype(F32)
    large = max_exact + (jnp.log(nf / max_exact) / math.log(MAX_DISTANCE / max_exact)
                         * (nb - max_exact)).astype(I32)
    large = jnp.minimum(large, nb - 1)
    return jnp.where(rel > 0, nb, 0) + jnp.where(n < max_exact, n, large)


def _swa_bias_table(rel_bias, rows):
    kw = WINDOW + rows
    qi = jnp.arange(rows)[:, None]
    kj = jnp.arange(kw)[None, :]
    bucket = _rel_bucket(kj - WINDOW - qi)
    rb = rel_bias.astype(F32)
    bias = jnp.zeros((NH_B, rows, kw), F32)
    for b in range(NUM_BUCKETS):
        bias = jnp.where((bucket == b)[None], rb[b][:, None, None], bias)
    lo = (qi // CHUNK) * CHUNK
    ok = (kj >= lo) & (kj < lo + WINDOW + CHUNK)
    return jnp.where(ok[None], bias, -jnp.inf)


def _even_mixer(proj, k_hist, v_hist, c0, n0, m0, bias_tab, norm_a, sink, batch, seq_len, rows, hist_valid):
    nsteps = seq_len // rows
    kw = WINDOW + rows
    m0p = jnp.broadcast_to(jnp.pad(m0, ((0, 0), (0, 8 - NH_A)))[:, :, None], (batch, 8, LANES))
    n0p = jnp.pad(n0, ((0, 0), (0, 8 - NH_A), (0, 0)))
    full3 = lambda b, i: (b, 0, 0)
    outs = pl.pallas_call(
        functools.partial(_even_kernel, rows=rows, hist_valid=hist_valid),
        out_shape=(
            jax.ShapeDtypeStruct((batch * seq_len, D_MODEL), BF16),
            jax.ShapeDtypeStruct((batch, NH_A, DK_A, DV_A), F32),
            jax.ShapeDtypeStruct((batch, 8, DK_A), F32),
            jax.ShapeDtypeStruct((batch, 8, LANES), F32),
            jax.ShapeDtypeStruct((batch, WINDOW, NKV_B * DH_B), F32),
            jax.ShapeDtypeStruct((batch, WINDOW, NKV_B * DH_B), F32),
        ),
        grid=(batch, nsteps),
        in_specs=[
            pl.BlockSpec((rows, N_EVEN_COLS), lambda b, i: (b * nsteps + i, 0)),
            pl.BlockSpec((None, WINDOW, NKV_B * DH_B), full3),
            pl.BlockSpec((None, WINDOW, NKV_B * DH_B), full3),
            pl.BlockSpec((None, NH_A, DK_A, DV_A), lambda b, i: (b, 0, 0, 0)),
            pl.BlockSpec((None, 8, DK_A), full3),
            pl.BlockSpec((None, 8, LANES), full3),
            pl.BlockSpec((NH_B, rows, kw), lambda b, i: (0, 0, 0)),
            pl.BlockSpec((NH_A, DV_A), lambda b, i: (0, 0)),
            pl.BlockSpec(memory_space=pltpu.SMEM),
        ],
        out_specs=(
            pl.BlockSpec((rows, D_MODEL), lambda b, i: (b * nsteps + i, 0)),
            pl.BlockSpec((None, NH_A, DK_A, DV_A), lambda b, i: (b, 0, 0, 0)),
            pl.BlockSpec((None, 8, DK_A), full3),
            pl.BlockSpec((None, 8, LANES), full3),
            pl.BlockSpec((None, WINDOW, NKV_B * DH_B), full3),
            pl.BlockSpec((None, WINDOW, NKV_B * DH_B), full3),
        ),
        scratch_shapes=[
            pltpu.VMEM((NH_A, DK_A, DV_A), F32),
            pltpu.VMEM((8, DK_A), F32),
            pltpu.VMEM((8, LANES), F32),
            pltpu.VMEM((WINDOW, NKV_B * DH_B), F32),
            pltpu.VMEM((WINDOW, NKV_B * DH_B), F32),
        ],
        compiler_params=_cparams(("arbitrary", "arbitrary")),
        name="even_mixer",
    )(proj, k_hist.reshape(batch, WINDOW, NKV_B * DH_B), v_hist.reshape(batch, WINDOW, NKV_B * DH_B),
      c0, n0p, m0p, bias_tab, norm_a.reshape(NH_A, DV_A), sink)
    mix, c1, n1, m1, k1, v1 = outs
    return (mix, c1, n1[:, :NH_A, :], m1[:, :NH_A, 0],
            k1.reshape(batch, WINDOW, NKV_B, DH_B), v1.reshape(batch, WINDOW, NKV_B, DH_B))


def _odd_kernel(proj_ref, conv0_ref, s0_ref, convw_ref, hp_ref, normc_ref,
                mix_ref, sout_ref, convout_ref, s_s, xa_s, *, rows):
    R = rows
    i = pl.program_id(1)
    HB = 8

    @pl.when(i == 0)
    def _():
        s_s[...] = s0_ref[...]
        xa_s[HB - (CONV_W - 1):HB, :] = conv0_ref[...]

    xa_s[HB:HB + R, :] = proj_ref[:, O_QKV:O_QKV + QKV_C]
    convout_ref[...] = xa_s[HB + R - (CONV_W - 1):HB + R, :]

    gcols = proj_ref[:, O_G:O_G + LANES]
    beta_all = _sigmoid(gcols)
    g_all = -jnp.exp(hp_ref[0:1, :]) * _softplus(gcols + hp_ref[1:2, :])
    incl = _tri(R)
    strict = _tri(R, strict=True)
    tri_b = jnp.where(incl, 1.0, 0.0).astype(BF16)
    G_all = _dot_exact_lhs(tri_b, g_all)
    G_t = _tr(G_all)
    eye = jnp.where(lax.broadcasted_iota(I32, (R, R), 0) == lax.broadcasted_iota(I32, (R, R), 1), 1.0, 0.0)
    scale_c = DK_C ** -0.5

    def conv_silu(c0):
        y = xa_s[HB - 3:HB - 3 + R, c0:c0 + LANES] * convw_ref[0:1, c0:c0 + LANES]
        for j in range(1, CONV_W):
            y = y + xa_s[HB - 3 + j:HB - 3 + j + R, c0:c0 + LANES] * convw_ref[j:j + 1, c0:c0 + LANES]
        return y * _sigmoid(y)

    def l2n(x):
        return x * lax.rsqrt(jnp.sum(x * x, axis=-1, keepdims=True) + RMS_EPS)

    for h in range(NH_C):
        q = l2n(conv_silu(h * DK_C))
        k = l2n(conv_silu(NH_C * DK_C + h * DK_C))
        v = conv_silu(2 * NH_C * DK_C + h * DV_C)
        z = proj_ref[:, O_Z + h * DV_C:O_Z + (h + 1) * DV_C]
        beta = beta_all[:, h:h + 1]
        G_col = G_all[:, NH_C + h:NH_C + h + 1]
        G_row = G_t[NH_C + h:NH_C + h + 1, :]
        dmat = jnp.exp(jnp.where(incl, G_col - G_row, -jnp.inf))
        eg = jnp.exp(G_col)
        S = s_s[h]
        Sb = S.astype(BF16)
        kb = k.astype(BF16)
        qb = (q * scale_c).astype(BF16)
        a_mat = jnp.where(strict, beta * dmat * _dot(kb, kb, _NT), 0.0)
        rhs = beta * (v - eg * _dot(kb, Sb))
        pw = -a_mat
        tinv = eye + pw
        nsq = max(1, (R - 1).bit_length()) - 1
        for _ in range(nsq):
            pw = _dot_x3(pw, pw)
            tinv = tinv + _dot_x3(tinv, pw)
        delta = _dot_x3(tinv, rhs)
        db = delta.astype(BF16)
        o = eg * _dot(qb, Sb) + _dot((_dot(qb, kb, _NT) * dmat).astype(BF16), db)
        GL = G_col[R - 1:R, :]
        kdec = k * jnp.exp(GL - G_col)
        s_s[h] = jnp.exp(GL) * S + _dot(_tr(kdec).astype(BF16), db)
        o = o * lax.rsqrt(jnp.mean(o * o, axis=-1, keepdims=True) + RMS_EPS) * normc_ref[...]
        o = o * (z * _sigmoid(z))
        mix_ref[:, h * DV_C:(h + 1) * DV_C] = o.astype(mix_ref.dtype)

    xa_s[HB - (CONV_W - 1):HB, :] = xa_s[HB + R - (CONV_W - 1):HB + R, :]
    sout_ref[...] = s_s[...]


def _odd_mixer(proj, conv_hist, s0, conv_w, a_log, dt_bias, norm_c, batch, seq_len):
    rows = CHUNK
    nsteps = seq_len // rows
    hp = jnp.zeros((8, LANES), F32)
    hp = hp.at[0, NH_C:2 * NH_C].set(a_log.astype(F32)).at[1, NH_C:2 * NH_C].set(dt_bias.astype(F32))
    outs = pl.pallas_call(
        functools.partial(_odd_kernel, rows=rows),
        out_shape=(
            jax.ShapeDtypeStruct((batch * seq_len, D_MODEL), BF16),
            jax.ShapeDtypeStruct((batch, NH_C, DK_C, DV_C), F32),
            jax.ShapeDtypeStruct((batch, CONV_W - 1, QKV_C), F32),
        ),
        grid=(batch, nsteps),
        in_specs=[
            pl.BlockSpec((rows, N_ODD_COLS), lambda b, i: (b * nsteps + i, 0)),
            pl.BlockSpec((None, CONV_W - 1, QKV_C), lambda b, i: (b, 0, 0)),
            pl.BlockSpec((None, NH_C, DK_C, DV_C), lambda b, i: (b, 0, 0, 0)),
            pl.BlockSpec((CONV_W, QKV_C), lambda b, i: (0, 0)),
            pl.BlockSpec((8, LANES), lambda b, i: (0, 0)),
            pl.BlockSpec((1, DV_C), lambda b, i: (0, 0)),
        ],
        out_specs=(
            pl.BlockSpec((rows, D_MODEL), lambda b, i: (b * nsteps + i, 0)),
            pl.BlockSpec((None, NH_C, DK_C, DV_C), lambda b, i: (b, 0, 0, 0)),
            pl.BlockSpec((None, CONV_W - 1, QKV_C), lambda b, i: (b, 0, 0)),
        ),
        scratch_shapes=[
            pltpu.VMEM((NH_C, DK_C, DV_C), F32),
            pltpu.VMEM((8 + rows, QKV_C), F32),
        ],
        compiler_params=_cparams(("arbitrary", "arbitrary")),
        name="odd_mixer",
    )(proj, conv_hist, s0, conv_w, hp, norm_c.reshape(1, DV_C))
    return outs


def _post_kernel(mix_ref, x_ref, gate_ref, sc_ref, sh_ref, wout_ref, lng_ref, lnb_ref, wr_ref, br_ref,
                 x1_ref, h2_ref, ri_ref, rg_ref, cnt_ref, carry_s, *, tm):
    i = pl.program_id(0)

    @pl.when(i == 0)
    def _():
        carry_s[...] = jnp.zeros_like(carry_s)

    y = _dot(mix_ref[...], wout_ref[...])
    x1 = _layer_norm_rows(ALPHA_DN * x_ref[...] + (1.0 + gate_ref[...]) * y, lng_ref[...], lnb_ref[...])
    x1_ref[...] = x1
    h2 = x1 * (1.0 + sc_ref[...]) + sh_ref[...]
    h2_ref[...] = h2
    lt = _dot(wr_ref[...].astype(BF16), h2.astype(BF16), _NT) + br_ref[:, 0:1]
    e_iota = lax.broadcasted_iota(I32, (N_EXPERTS, tm), 0).astype(F32)
    vals, idxs = [], []
    for _ in range(TOP_K):
        mx = jnp.max(lt, axis=0, keepdims=True)
        idx = jnp.min(jnp.where(lt == mx, e_iota, float(N_EXPERTS)), axis=0, keepdims=True)
        vals.append(mx)
        idxs.append(idx)
        lt = jnp.where(e_iota == idx, -jnp.inf, lt)
    ex = [jnp.exp(v - vals[0]) for v in vals]
    tot = ex[0] + ex[1] + ex[2] + ex[3]
    hot = [jnp.where(e_iota == idx, 1.0, 0.0) for idx in idxs]
    m_all = hot[0] + hot[1] + hot[2] + hot[3]
    r = lax.broadcasted_iota(I32, (tm, tm), 0)
    c = lax.broadcasted_iota(I32, (tm, tm), 1)
    upper = jnp.where(r < c, 1.0, 0.0).astype(BF16)
    base = carry_s[:, 0:1] + _dot(m_all.astype(BF16), upper)
    ranks = [jnp.sum(hk * base, axis=0, keepdims=True) for hk in hot]
    carry_s[...] = carry_s[...] + jnp.sum(m_all, axis=1, keepdims=True)
    cnt_ref[...] = carry_s[...]
    ri_ref[...] = jnp.concatenate(idxs + ranks, axis=0).astype(I32)
    rg_ref[...] = jnp.concatenate([e / tot for e in ex] + [jnp.zeros((4, tm), F32)], axis=0)


def _post(mix, x2d, gate, scale, shift, w_out_bf16, ln_g, ln_b, wr_t, br, seq_len, tm):
    t = x2d.shape[0]
    per = seq_len // tm
    vec = lambda i: (i // per, 0, 0)
    const2 = lambda i: (0, 0)
    return pl.pallas_call(
        functools.partial(_post_kernel, tm=tm),
        out_shape=(
            jax.ShapeDtypeStruct((t, D_MODEL), F32),
            jax.ShapeDtypeStruct((t, D_MODEL), F32),
            jax.ShapeDtypeStruct((t // tm, 8, tm), I32),
            jax.ShapeDtypeStruct((t // tm, 8, tm), F32),
            jax.ShapeDtypeStruct((N_EXPERTS, LANES), F32),
        ),
        grid=(t // tm,),
        in_specs=[
            pl.BlockSpec((tm, D_MODEL), lambda i: (i, 0)),
            pl.BlockSpec((tm, D_MODEL), lambda i: (i, 0)),
            pl.BlockSpec((None, 1, D_MODEL), vec),
            pl.BlockSpec((None, 1, D_MODEL), vec),
            pl.BlockSpec((None, 1, D_MODEL), vec),
            pl.BlockSpec((D_MODEL, D_MODEL), const2),
            pl.BlockSpec((1, D_MODEL), const2),
            pl.BlockSpec((1, D_MODEL), const2),
            pl.BlockSpec((N_EXPERTS, D_MODEL), const2),
            pl.BlockSpec((N_EXPERTS, LANES), const2),
        ],
        out_specs=(
            pl.BlockSpec((tm, D_MODEL), lambda i: (i, 0)),
            pl.BlockSpec((tm, D_MODEL), lambda i: (i, 0)),
            pl.BlockSpec((None, 8, tm), lambda i: (i, 0, 0)),
            pl.BlockSpec((None, 8, tm), lambda i: (i, 0, 0)),
            pl.BlockSpec((N_EXPERTS, LANES), const2),
        ),
        scratch_shapes=[pltpu.VMEM((N_EXPERTS, LANES), F32)],
        compiler_params=_cparams(("arbitrary",)),
        name="post_router",
    )(mix, x2d, gate, scale, shift, w_out_bf16, ln_g, ln_b, wr_t, br)


def _dispatch_kernel(zblk_ref, rstart_ref, ri_ref, h_ref, xs_ref, zero_s, sem, zsem, *, tm, bm):
    i = pl.program_id(0)

    def zero_copy(e):
        return pltpu.make_async_copy(zero_s, xs_ref.at[pl.ds(zblk_ref[e] * bm, bm), :], zsem)

    @pl.when(i == 0)
    def _():
        zero_s[...] = jnp.zeros_like(zero_s)

        def zstart(e, c):
            @pl.when(zblk_ref[e] >= 0)
            def _():
                zero_copy(e).start()
            return c

        def zwait(e, c):
            @pl.when(zblk_ref[e] >= 0)
            def _():
                zero_copy(e).wait()
            return c

        lax.fori_loop(0, N_EXPERTS, zstart, 0)
        lax.fori_loop(0, N_EXPERTS, zwait, 0)

    def body(t, c):
        for k in range(TOP_K):
            p = rstart_ref[ri_ref[k, t]] + ri_ref[TOP_K + k, t]
            pltpu.make_async_copy(h_ref.at[pl.ds(t, 1), :], xs_ref.at[pl.ds(p, 1), :], sem).start()
        return c

    lax.fori_loop(0, tm, body, 0)
    for k in range(TOP_K):
        pltpu.make_async_copy(h_ref, xs_ref.at[pl.ds(0, tm), :], sem).wait()


def _dispatch(h2, route_i, row_start, zblk, n_rows, tm, bm):
    t = h2.shape[0]
    return pl.pallas_call(
        functools.partial(_dispatch_kernel, tm=tm, bm=bm),
        out_shape=jax.ShapeDtypeStruct((n_rows, D_MODEL), F32),
        grid_spec=pltpu.PrefetchScalarGridSpec(
            num_scalar_prefetch=2,
            grid=(t // tm,),
            in_specs=[
                pl.BlockSpec((None, 2 * TOP_K, tm), lambda i, z, r: (i, 0, 0), memory_space=pltpu.SMEM),
                pl.BlockSpec((tm, D_MODEL), lambda i, z, r: (i, 0)),
            ],
            out_specs=pl.BlockSpec(memory_space=pl.ANY),
            scratch_shapes=[
                pltpu.VMEM((bm, D_MODEL), F32),
                pltpu.SemaphoreType.DMA,
                pltpu.SemaphoreType.DMA,
            ],
        ),
        compiler_params=_cparams(("arbitrary",)),
        name="dispatch",
    )(zblk, row_start, route_i, h2)


def _ffn_kernel(be_ref, nv_ref, x_ref, w1_ref, b1_ref, w2_ref, b2_ref, y_ref, w1_s, w2_s):
    b = pl.program_id(0)

    @pl.when(b < nv_ref[0])
    def _():
        @pl.when((b == 0) | (be_ref[b] != be_ref[jnp.maximum(b - 1, 0)]))
        def _():
            step = 256
            for r0 in range(0, D_MODEL, step):
                w1_s[r0:r0 + step, :] = w1_ref[r0:r0 + step, :].astype(BF16)
            for r0 in range(0, D_FF, step):
                w2_s[r0:r0 + step, :] = w2_ref[r0:r0 + step, :].astype(BF16)

        xb = x_ref[...].astype(BF16)
        glu = _dot(xb, w1_s[:, :D_FF]) + b1_ref[:, :D_FF]
        lin = _dot(xb, w1_s[:, D_FF:]) + b1_ref[:, D_FF:]
        glu = jnp.minimum(glu, SWIGLU_LIMIT)
        lin = jnp.clip(lin, -SWIGLU_LIMIT, SWIGLU_LIMIT)
        act = glu * _sigmoid(SWIGLU_ALPHA * glu) * (lin + 1.0)
        y_ref[...] = _dot(act.astype(BF16), w2_s[...]) + b2_ref[...]


def _ffn(xs, block_e, n_valid, layer, w1, b1, w2, b2, bm):
    nb = xs.shape[0] // bm
    row = lambda b, be, nv: (jnp.minimum(b, nv[0] - 1), 0)
    wsel = lambda b, be, nv: (layer, be[b], 0, 0)
    return pl.pallas_call(
        _ffn_kernel,
        out_shape=jax.ShapeDtypeStruct(xs.shape, F32),
        grid_spec=pltpu.PrefetchScalarGridSpec(
            num_scalar_prefetch=2,
            grid=(nb,),
            in_specs=[
                pl.BlockSpec((bm, D_MODEL), row),
                pl.BlockSpec((None, None, D_MODEL, 2 * D_FF), wsel),
                pl.BlockSpec((None, None, 1, 2 * D_FF), wsel),
                pl.BlockSpec((None, None, D_FF, D_MODEL), wsel),
                pl.BlockSpec((None, None, 1, D_MODEL), wsel),
            ],
            out_specs=pl.BlockSpec((bm, D_MODEL), row),
            scratch_shapes=[pltpu.VMEM((D_MODEL, 2 * D_FF), BF16), pltpu.VMEM((D_FF, D_MODEL), BF16)],
        ),
        compiler_params=_cparams(("arbitrary",)),
        name="expert_ffn",
    )(block_e, n_valid, xs, w1, b1, w2, b2)


def _combine_kernel(rstart_ref, ri_ref, x_ref, gate_ref, rg_ref, lng_ref, lnb_ref, ys_ref, o_ref, buf, sem, *, tm):
    def body(t, c):
        for k in range(TOP_K):
            p = rstart_ref[ri_ref[k, t]] + ri_ref[TOP_K + k, t]
            pltpu.make_async_copy(ys_ref.at[pl.ds(p, 1), :], buf.at[k, pl.ds(t, 1), :], sem).start()
        return c

    lax.fori_loop(0, tm, body, 0)
    for k in range(TOP_K):
        pltpu.make_async_copy(ys_ref.at[pl.ds(0, tm), :], buf.at[k], sem).wait()
    g_t = _tr(jnp.concatenate([rg_ref[...], jnp.zeros((LANES - 8, tm), F32)], axis=0))
    moe = g_t[:, 0:1] * buf[0]
    for k in range(1, TOP_K):
        moe = moe + g_t[:, k:k + 1] * buf[k]
    o_ref[...] = _layer_norm_rows(ALPHA_DN * x_ref[...] + (1.0 + gate_ref[...]) * moe,
                                  lng_ref[...], lnb_ref[...])


def _combine(row_start, route_i, x1, gate, route_g, ln_g, ln_b, ys, seq_len, tm):
    t = x1.shape[0]
    per = seq_len // tm
    return pl.pallas_call(
        functools.partial(_combine_kernel, tm=tm),
        out_shape=jax.ShapeDtypeStruct((t, D_MODEL), F32),
        grid_spec=pltpu.PrefetchScalarGridSpec(
            num_scalar_prefetch=1,
            grid=(t // tm,),
            in_specs=[
                pl.BlockSpec((None, 2 * TOP_K, tm), lambda i, r: (i, 0, 0), memory_space=pltpu.SMEM),
                pl.BlockSpec((tm, D_MODEL), lambda i, r: (i, 0)),
                pl.BlockSpec((None, 1, D_MODEL), lambda i, r: (i // per, 0, 0)),
                pl.BlockSpec((None, 8, tm), lambda i, r: (i, 0, 0)),
                pl.BlockSpec((1, D_MODEL), lambda i, r: (0, 0)),
                pl.BlockSpec((1, D_MODEL), lambda i, r: (0, 0)),
                pl.BlockSpec(memory_space=pl.ANY),
            ],
            out_specs=pl.BlockSpec((tm, D_MODEL), lambda i, r: (i, 0)),
            scratch_shapes=[pltpu.VMEM((TOP_K, tm, D_MODEL), F32), pltpu.SemaphoreType.DMA],
        ),
        compiler_params=_cparams(("arbitrary",)),
        name="combine",
    )(row_start, route_i, x1, gate, route_g, ln_g, ln_b, ys)


def _moe_tables(counts, bm, n_blocks):
    cnt = counts[:, 0].astype(I32)
    nblk = (cnt + bm - 1) // bm
    blk_end = jnp.cumsum(nblk)
    row_start = (blk_end - nblk) * bm
    n_valid = blk_end[-1]
    bidx = jnp.minimum(jnp.arange(n_blocks, dtype=I32), n_valid - 1)
    block_e = jnp.sum((blk_end[None, :] <= bidx[:, None]).astype(I32), axis=1)
    block_e = jnp.minimum(block_e, N_EXPERTS - 1)
    zblk = jnp.where((cnt % bm) != 0, blk_end - 1, -1).astype(I32)
    return row_start.astype(I32), block_e, n_valid.reshape(1).astype(I32), zblk


def _moe_and_norm(h2, x1, route_i, route_g, counts, gate, ln_g, ln_b, layer, w1, b1, w2, b2, seq_len, tm, bm):
    t = h2.shape[0]
    n_blocks = t * TOP_K // bm + N_EXPERTS
    row_start, block_e, n_valid, zblk = _moe_tables(counts, bm, n_blocks)
    xs = _dispatch(h2, route_i, row_start, zblk, n_blocks * bm, tm, bm)
    ys = _ffn(xs, block_e, n_valid, layer, w1, b1, w2, b2, bm)
    return _combine(row_start, route_i, x1, gate, route_g, ln_g, ln_b, ys, seq_len, tm)


def _group_cfg(batch, seq_len):
    if seq_len >= 512:
        return dict(tm=512, rows=256, bm=256)
    return dict(tm=seq_len, rows=seq_len, bm=128)


def kernel(x_prompt, x_sample, c_prompt, c_sample, state_a_C, state_a_n, state_a_m, cache_b_k, cache_b_v,
           state_c_S, state_c_conv, w_ada, b_ada, ln_g, ln_b, w_in_even, b_in_even, norm_a, sink_b, rel_bias,
           w_out_even, w_in_odd, conv_c, a_log_c, dt_bias_c, norm_c, w_out_odd, w_router, b_router,
           w_e1, b_e1, w_e2, b_e2):
    bp, lp, _ = x_prompt.shape
    bs, ls, _ = x_sample.shape
    groups = [dict(b=bp, l=lp, x=x_prompt.reshape(bp * lp, D_MODEL), **_group_cfg(bp, lp)),
              dict(b=bs, l=ls, x=x_sample.reshape(bs * ls, D_MODEL), **_group_cfg(bs, ls))]
    mod = _ada(jnp.concatenate([c_prompt, c_sample], axis=0), w_ada, b_ada)
    offs = [0, bp]
    states = [dict(), dict()]
    b1_all = b_e1.reshape(DEPTH, N_EXPERTS, 1, 2 * D_FF)
    b2_all = b_e2.reshape(DEPTH, N_EXPERTS, 1, D_MODEL)
    for l in range(DEPTH):
        e = l // 2
        if l % 2 == 0:
            w = w_in_even[e]
            sz = (512, 512, 512, 512, 4, 4, 512, 128, 128)
            o = [sum(sz[:j]) for j in range(len(sz) + 1)]
            pad = jnp.zeros((D_MODEL, LANES - 2 * NH_A), w.dtype)
            w_in = jnp.concatenate([w[:, o[0]:o[4]], w[:, o[6]:o[9]], w[:, o[4]:o[6]], pad], axis=1).astype(BF16)
            bb = b_in_even[e]
            b_in = jnp.concatenate([bb[o[0]:o[4]], bb[o[6]:o[9]], bb[o[4]:o[6]],
                                    jnp.zeros((LANES - 2 * NH_A,), bb.dtype)]).reshape(1, N_EVEN_COLS)
            w_out = w_out_even[e].astype(BF16)
        else:
            w = w_in_odd[e]
            pad = jnp.zeros((D_MODEL, LANES - 2 * NH_C), w.dtype)
            w_in = jnp.concatenate([w, pad], axis=1).astype(BF16)
            b_in = jnp.zeros((1, N_ODD_COLS), F32)
            w_out = w_out_odd[e].astype(BF16)
        wr_t = w_router[l].T
        br = jnp.broadcast_to(b_router[l][:, None], (N_EXPERTS, LANES))
        for gi, gr in enumerate(groups):
            nb, sl = gr['b'], gr['l']
            m = mod[l, offs[gi]:offs[gi] + nb].reshape(nb, 6, 1, D_MODEL)
            m6 = [m[:, j] for j in range(6)]
            proj = _inproj(gr['x'], m6[1], m6[0], w_in, b_in, sl, gr['tm'])
            st = states[gi]
            if l % 2 == 0:
                if gi == 0:
                    kh = jnp.zeros((nb, WINDOW, NKV_B, DH_B), F32)
                    vh = kh
                    c0 = jnp.zeros((nb, NH_A, DK_A, DV_A), F32)
                    n0 = jnp.zeros((nb, NH_A, DK_A), F32)
                    m0 = jnp.zeros((nb, NH_A), F32)
                else:
                    kh, vh, c0, n0, m0 = cache_b_k[e], cache_b_v[e], state_a_C[e], state_a_n[e], state_a_m[e]
                bias_tab = _swa_bias_table(rel_bias, gr['rows'])
                mix, c1, n1, m1, k1, v1 = _even_mixer(proj, kh, vh, c0, n0, m0, bias_tab, norm_a[e], sink_b[e],
                                                      nb, sl, gr['rows'], gi == 1)
                for name, val in (('a_C', c1), ('a_n', n1), ('a_m', m1), ('b_k', k1), ('b_v', v1)):
                    st.setdefault(name, []).append(val)
            else:
                if gi == 0:
                    cv0 = jnp.zeros((nb, CONV_W - 1, QKV_C), F32)
                    s0 = jnp.zeros((nb, NH_C, DK_C, DV_C), F32)
                else:
                    cv0, s0 = state_c_conv[e], state_c_S[e]
                mix, s1, cv1 = _odd_mixer(proj, cv0, s0, conv_c[e], a_log_c[e], dt_bias_c[e], norm_c[e], nb, sl)
                st.setdefault('c_S', []).append(s1)
                st.setdefault('c_conv', []).append(cv1)
            x1, h2, route_i, route_g, counts = _post(
                mix, gr['x'], m6[2], m6[4], m6[3], w_out, ln_g[l, 0].reshape(1, D_MODEL),
                ln_b[l, 0].reshape(1, D_MODEL), wr_t, br, sl, gr['tm'])
            gr['x'] = _moe_and_norm(h2, x1, route_i, route_g, counts, m6[5], ln_g[l, 1].reshape(1, D_MODEL),
                                    ln_b[l, 1].reshape(1, D_MODEL), l, w_e1, b1_all, w_e2, b2_all,
                                    sl, gr['tm'], gr['bm'])
    outs = [groups[0]['x'].reshape(bp, lp, D_MODEL), groups[1]['x'].reshape(bs, ls, D_MODEL)]
    for gi in range(2):
        for name in ('a_C', 'a_n', 'a_m', 'b_k', 'b_v', 'c_S', 'c_conv'):
            outs.append(jnp.stack(states[gi][name]))
    return tuple(outs)
```

```python
import functools
import math

import jax
import jax.numpy as jnp
from jax import lax
from jax.experimental import pallas as pl
from jax.experimental.pallas import tpu as pltpu

F32 = jnp.float32
BF16 = jnp.bfloat16
I32 = jnp.int32

D_MODEL = 1024
CHUNK = 64
NH_A, DK_A, DV_A = 4, 128, 128
W_A = NH_A * DV_A
NH_B, NKV_B, DH_B, WINDOW = 8, 2, 64, 128
G_B = NH_B // NKV_B
W_B = NH_B * DH_B
NUM_BUCKETS, MAX_DISTANCE = 32, 256
NH_C, DK_C, DV_C, CONV_W = 8, 128, 128, 4
QKV_C = NH_C * (2 * DK_C + DV_C)
N_EXPERTS, TOP_K, D_FF = 32, 4, 1024
SWIGLU_LIMIT, SWIGLU_ALPHA = 7.0, 1.702
DEPTH = 2
ALPHA_DN = (2 * DEPTH) ** 0.25
LN_EPS = 1e-5
RMS_EPS = 1e-6

LANES = 128
E_QA, E_KA, E_VA, E_OA, E_QB, E_KB, E_VB, E_G = 0, 512, 1024, 1536, 2048, 2560, 2688, 2816
N_EVEN_COLS = 2944
O_QKV, O_Z, O_G = 0, 3072, 4096
N_ODD_COLS = 4224

VMEM_LIMIT = 56 * 1024 * 1024


def _cparams(sem):
    return pltpu.CompilerParams(dimension_semantics=sem, vmem_limit_bytes=VMEM_LIMIT)


def _softplus(x):
    return jnp.maximum(x, 0.0) + jnp.log(1.0 + jnp.exp(-jnp.abs(x)))


def _sigmoid(x):
    return 1.0 / (1.0 + jnp.exp(-x))


def _split_bf16(a):
    hi = a.astype(BF16)
    lo = (a - hi.astype(F32)).astype(BF16)
    return hi, lo


_NN = (((1,), (0,)), ((), ()))
_NT = (((1,), (1,)), ((), ()))
_TN = (((0,), (0,)), ((), ()))


def _dot(a, b, dims=_NN):
    return lax.dot_general(a, b, dims, preferred_element_type=F32)


def _dot_x3(a, b, dims=_NN):
    ah, al = _split_bf16(a)
    bh, bl = _split_bf16(b)
    return _dot(ah, bh, dims) + _dot(ah, bl, dims) + _dot(al, bh, dims)


def _dot_exact_lhs(a_bf16, b, dims=_NN):
    b0 = b.astype(BF16)
    r1 = b - b0.astype(F32)
    b1 = r1.astype(BF16)
    b2 = (r1 - b1.astype(F32)).astype(BF16)
    return _dot(a_bf16, b0, dims) + _dot(a_bf16, b1, dims) + _dot(a_bf16, b2, dims)


def _tr(x):
    r = x.shape[0]
    rp = -(-r // LANES) * LANES
    if rp != r:
        x = jnp.concatenate([x, jnp.zeros((rp - r, x.shape[1]), x.dtype)], axis=0)
    return x.T[:, :r]


def _tri(n, strict=False):
    r = lax.broadcasted_iota(I32, (n, n), 0)
    c = lax.broadcasted_iota(I32, (n, n), 1)
    return (r > c) if strict else (r >= c)


def _layer_norm_rows(v, g, b):
    mu = jnp.mean(v, axis=-1, keepdims=True)
    d = v - mu
    var = jnp.mean(d * d, axis=-1, keepdims=True)
    return d * lax.rsqrt(var + LN_EPS) * g + b


def _ada_kernel(c_ref, w_ref, b_ref, o_ref):
    c = c_ref[...]
    a = (c * _sigmoid(c)).astype(BF16)
    o_ref[...] = _dot(a, w_ref[...].astype(BF16)) + b_ref[...]


def _ada(c_all, w_ada, b_ada):
    nb = c_all.shape[0]
    tn = 1536
    return pl.pallas_call(
        _ada_kernel,
        out_shape=jax.ShapeDtypeStruct((DEPTH, nb, 6 * D_MODEL), F32),
        grid=(DEPTH, 6 * D_MODEL // tn),
        in_specs=[
            pl.BlockSpec((nb, D_MODEL), lambda l, j: (0, 0)),
            pl.BlockSpec((None, D_MODEL, tn), lambda l, j: (l, 0, j)),
            pl.BlockSpec((None, 1, tn), lambda l, j: (l, 0, j)),
        ],
        out_specs=pl.BlockSpec((None, nb, tn), lambda l, j: (l, 0, j)),
        compiler_params=_cparams(("arbitrary", "arbitrary")),
        name="ada",
    )(c_all, w_ada, b_ada.reshape(DEPTH, 1, 6 * D_MODEL))


def _inproj_kernel(x_ref, sc_ref, sh_ref, w_ref, b_ref, o_ref, *, n_cols, col_step):
    h = (x_ref[...] * (1.0 + sc_ref[...]) + sh_ref[...]).astype(BF16)
    for c0 in range(0, n_cols, col_step):
        c1 = min(c0 + col_step, n_cols)
        o_ref[:, c0:c1] = _dot(h, w_ref[:, c0:c1]) + b_ref[:, c0:c1]


def _inproj(x2d, scale, shift, w_bf16, bias, seq_len, tm):
    t = x2d.shape[0]
    n = w_bf16.shape[1]
    per = seq_len // tm
    return pl.pallas_call(
        functools.partial(_inproj_kernel, n_cols=n, col_step=1024),
        out_shape=jax.ShapeDtypeStruct((t, n), F32),
        grid=(t // tm,),
        in_specs=[
            pl.BlockSpec((tm, D_MODEL), lambda i: (i, 0)),
            pl.BlockSpec((None, 1, D_MODEL), lambda i: (i // per, 0, 0)),
            pl.BlockSpec((None, 1, D_MODEL), lambda i: (i // per, 0, 0)),
            pl.BlockSpec((D_MODEL, n), lambda i: (0, 0)),
            pl.BlockSpec((1, n), lambda i: (0, 0)),
        ],
        out_specs=pl.BlockSpec((tm, n), lambda i: (i, 0)),
        compiler_params=_cparams(("arbitrary",)),
        name="inproj",
    )(x2d, scale, shift, w_bf16, bias)


def _even_kernel(proj_ref, kh0_ref, vh0_ref, c0_ref, n0_ref, m0_ref, bias_ref, norma_ref, sink_ref,
                 mix_ref, cout_ref, nout_ref, mout_ref, kout_ref, vout_ref,
                 c_s, n_s, m_s, kh_s, vh_s, *, rows, hist_valid):
    R = rows
    KW = WINDOW + R
    i = pl.program_id(1)

    @pl.when(i == 0)
    def _():
        c_s[...] = c0_ref[...]
        n_s[...] = n0_ref[...]
        m_s[...] = m0_ref[...]
        kh_s[...] = kh0_ref[...]
        vh_s[...] = vh0_ref[...]

    CS = CHUNK
    g = proj_ref[:, E_G:E_G + LANES]
    lf = -_softplus(-g)
    rr = lax.broadcasted_iota(I32, (R, R), 0)
    cc = lax.broadcasted_iota(I32, (R, R), 1)
    tri_b = jnp.where((rr >= cc) & (rr // CS == cc // CS), 1.0, 0.0).astype(BF16)
    b_all = _dot_exact_lhs(tri_b, lf)
    g_t = _tr(g)
    b_t = _tr(b_all)
    causal = _tri(CS)
    scale_a = DK_A ** -0.5
    chunks = list(range(0, R, CS))
    pb = {}
    for c0 in chunks:
        rs = slice(c0, c0 + CS)
        for h in range(NH_A):
            b_col = b_all[rs, NH_A + h:NH_A + h + 1]
            b_row = b_t[NH_A + h:NH_A + h + 1, rs]
            logw = jnp.where(causal, b_col - b_row + g_t[h:h + 1, rs], -jnp.inf)
            pb[c0, h] = dict(b_col=b_col, logw=logw, lmax=jnp.max(logw, axis=-1, keepdims=True))
    for h in range(NH_A):
        m_prev = m_s[h:h + 1, 0:1]
        for c0 in chunks:
            d = pb[c0, h]
            d['m_prev'] = m_prev
            d['m_inter'] = d['b_col'] + m_prev
            d['m_t'] = jnp.maximum(d['m_inter'], d['lmax'])
            m_prev = d['m_t'][CS - 1:CS, :]
            d['m_new'] = m_prev
        m_s[h:h + 1, :] = jnp.broadcast_to(m_prev, (1, LANES))
    for c0 in chunks:
        rs = slice(c0, c0 + CS)
        for h in range(NH_A):
            d = pb[c0, h]
            q = proj_ref[rs, E_QA + h * DK_A:E_QA + (h + 1) * DK_A] * scale_a
            k = proj_ref[rs, E_KA + h * DK_A:E_KA + (h + 1) * DK_A]
            qb = q.astype(BF16)
            vb = proj_ref[rs, E_VA + h * DV_A:E_VA + (h + 1) * DV_A].astype(BF16)
            s = _dot(qb, k.astype(BF16), _NT) * jnp.exp(d['logw'] - d['m_t'])
            b_last = d['b_col'][CS - 1:CS, :]
            kw = k * jnp.exp(b_last - d['b_col'] + g[rs, h:h + 1] - d['m_new'])
            d.update(qb=qb, dec=jnp.exp(d['m_inter'] - d['m_t']), sv=_dot(s.astype(BF16), vb),
                     ssum=jnp.sum(s, axis=-1, keepdims=True), kv=_dot(_tr(kw).astype(BF16), vb),
                     ksum=jnp.sum(kw, axis=0, keepdims=True), carry=jnp.exp(b_last + d['m_prev'] - d['m_new']))
    for h in range(NH_A):
        n_prev = n_s[h:h + 1, :]
        c_prev = c_s[h]
        for c0 in chunks:
            d = pb[c0, h]
            d['c_prev'], d['n_prev'] = c_prev, n_prev
            c_prev = d['carry'] * c_prev + d['kv']
            n_prev = d['carry'] * n_prev + d['ksum']
        c_s[h] = c_prev
        n_s[h:h + 1, :] = n_prev
    for c0 in chunks:
        rs = slice(c0, c0 + CS)
        for h in range(NH_A):
            d = pb[c0, h]
            qb = d['qb']
            num = d['dec'] * _dot(qb, d['c_prev'].astype(BF16)) + d['sv']
            qn = jnp.sum(qb.astype(F32) * d['n_prev'].astype(BF16).astype(F32), axis=-1, keepdims=True)
            den = d['dec'] * qn + d['ssum']
            hh = num / jnp.maximum(jnp.abs(den), jnp.exp(-d['m_t']))
            og = proj_ref[rs, E_OA + h * DV_A:E_OA + (h + 1) * DV_A]
            mu = jnp.mean(hh, axis=-1, keepdims=True)
            dd = hh - mu
            var = jnp.mean(dd * dd, axis=-1, keepdims=True)
            ha = dd * lax.rsqrt(var + LN_EPS) * norma_ref[h:h + 1, :] * _sigmoid(og)
            mix_ref[rs, h * DV_A:(h + 1) * DV_A] = ha.astype(mix_ref.dtype)

    cout_ref[...] = c_s[...]
    nout_ref[...] = n_s[...]
    mout_ref[...] = m_s[...]

    k_win = jnp.concatenate([kh_s[...], proj_ref[:, E_KB:E_KB + NKV_B * DH_B]], axis=0)
    v_win = jnp.concatenate([vh_s[...], proj_ref[:, E_VB:E_VB + NKV_B * DH_B]], axis=0)
    k_win_b = k_win.astype(BF16)
    v_win_b = v_win.astype(BF16)
    if not hist_valid:
        key_pos = lax.broadcasted_iota(I32, (R, KW), 1) + (i * R - WINDOW)
        key_ok = key_pos >= 0
    scale_b = DH_B ** -0.5
    kgs = [k_win_b[:, kv * DH_B:(kv + 1) * DH_B] for kv in range(NKV_B)]
    vgs = [v_win_b[:, kv * DH_B:(kv + 1) * DH_B] for kv in range(NKV_B)]
    scores = []
    for hd in range(NH_B):
        qh = proj_ref[:, E_QB + hd * DH_B:E_QB + (hd + 1) * DH_B].astype(BF16)
        s = _dot(qh, kgs[hd // G_B], _NT) * scale_b + bias_ref[hd]
        if not hist_valid:
            s = jnp.where(key_ok, s, -jnp.inf)
        scores.append(s)
    probs_b = []
    for hd, s in enumerate(scores):
        sk = sink_ref[hd]
        mx = jnp.maximum(jnp.max(s, axis=-1, keepdims=True), sk)
        p = jnp.exp(s - mx)
        p = p / (jnp.sum(p, axis=-1, keepdims=True) + jnp.exp(sk - mx))
        probs_b.append(p.astype(BF16))
    for hd, p in enumerate(probs_b):
        o = _dot(p, vgs[hd // G_B])
        mix_ref[:, W_A + hd * DH_B:W_A + (hd + 1) * DH_B] = o.astype(mix_ref.dtype)

    kh_s[...] = k_win[R:, :]
    vh_s[...] = v_win[R:, :]
    kout_ref[...] = kh_s[...]
    vout_ref[...] = vh_s[...]


def _rel_bucket(rel):
    nb = NUM_BUCKETS // 2
    max_exact = nb // 2
    n = jnp.abs(rel)
    nf = jnp.maximum(n, 1).astype(F32)
    large = max_exact + (jnp.log(nf / max_exact) / math.log(MAX_DISTANCE / max_exact)
                         * (nb - max_exact)).astype(I32)
    large = jnp.minimum(large, nb - 1)
    return jnp.where(rel > 0, nb, 0) + jnp.where(n < max_exact, n, large)


def _swa_bias_table(rel_bias, rows):
    kw = WINDOW + rows
    qi = jnp.arange(rows)[:, None]
    kj = jnp.arange(kw)[None, :]
    bucket = _rel_bucket(kj - WINDOW - qi)
    rb = rel_bias.astype(F32)
    bias = jnp.zeros((NH_B, rows, kw), F32)
    for b in range(NUM_BUCKETS):
        bias = jnp.where((bucket == b)[None], rb[b][:, None, None], bias)
    lo = (qi // CHUNK) * CHUNK
    ok = (kj >= lo) & (kj < lo + WINDOW + CHUNK)
    return jnp.where(ok[None], bias, -jnp.inf)


def _even_mixer(proj, k_hist, v_hist, c0, n0, m0, bias_tab, norm_a, sink, batch, seq_len, rows, hist_valid):
    nsteps = seq_len // rows
    kw = WINDOW + rows
    m0p = jnp.broadcast_to(jnp.pad(m0, ((0, 0), (0, 8 - NH_A)))[:, :, None], (batch, 8, LANES))
    n0p = jnp.pad(n0, ((0, 0), (0, 8 - NH_A), (0, 0)))
    full3 = lambda b, i: (b, 0, 0)
    outs = pl.pallas_call(
        functools.partial(_even_kernel, rows=rows, hist_valid=hist_valid),
        out_shape=(
            jax.ShapeDtypeStruct((batch * seq_len, D_MODEL), BF16),
            jax.ShapeDtypeStruct((batch, NH_A, DK_A, DV_A), F32),
            jax.ShapeDtypeStruct((batch, 8, DK_A), F32),
            jax.ShapeDtypeStruct((batch, 8, LANES), F32),
            jax.ShapeDtypeStruct((batch, WINDOW, NKV_B * DH_B), F32),
            jax.ShapeDtypeStruct((batch, WINDOW, NKV_B * DH_B), F32),
        ),
        grid=(batch, nsteps),
        in_specs=[
            pl.BlockSpec((rows, N_EVEN_COLS), lambda b, i: (b * nsteps + i, 0)),
            pl.BlockSpec((None, WINDOW, NKV_B * DH_B), full3),
            pl.BlockSpec((None, WINDOW, NKV_B * DH_B), full3),
            pl.BlockSpec((None, NH_A, DK_A, DV_A), lambda b, i: (b, 0, 0, 0)),
            pl.BlockSpec((None, 8, DK_A), full3),
            pl.BlockSpec((None, 8, LANES), full3),
            pl.BlockSpec((NH_B, rows, kw), lambda b, i: (0, 0, 0)),
            pl.BlockSpec((NH_A, DV_A), lambda b, i: (0, 0)),
            pl.BlockSpec(memory_space=pltpu.SMEM),
        ],
        out_specs=(
            pl.BlockSpec((rows, D_MODEL), lambda b, i: (b * nsteps + i, 0)),
            pl.BlockSpec((None, NH_A, DK_A, DV_A), lambda b, i: (b, 0, 0, 0)),
            pl.BlockSpec((None, 8, DK_A), full3),
            pl.BlockSpec((None, 8, LANES), full3),
            pl.BlockSpec((None, WINDOW, NKV_B * DH_B), full3),
            pl.BlockSpec((None, WINDOW, NKV_B * DH_B), full3),
        ),
        scratch_shapes=[
            pltpu.VMEM((NH_A, DK_A, DV_A), F32),
            pltpu.VMEM((8, DK_A), F32),
            pltpu.VMEM((8, LANES), F32),
            pltpu.VMEM((WINDOW, NKV_B * DH_B), F32),
            pltpu.VMEM((WINDOW, NKV_B * DH_B), F32),
        ],
        compiler_params=_cparams(("arbitrary", "arbitrary")),
        name="even_mixer",
    )(proj, k_hist.reshape(batch, WINDOW, NKV_B * DH_B), v_hist.reshape(batch, WINDOW, NKV_B * DH_B),
      c0, n0p, m0p, bias_tab, norm_a.reshape(NH_A, DV_A), sink)
    mix, c1, n1, m1, k1, v1 = outs
    return (mix, c1, n1[:, :NH_A, :], m1[:, :NH_A, 0],
            k1.reshape(batch, WINDOW, NKV_B, DH_B), v1.reshape(batch, WINDOW, NKV_B, DH_B))


def _odd_kernel(proj_ref, conv0_ref, s0_ref, convw_ref, hp_ref, normc_ref,
                mix_ref, sout_ref, convout_ref, s_s, xa_s, *, rows):
    R = rows
    i = pl.program_id(1)
    HB = 8

    @pl.when(i == 0)
    def _():
        s_s[...] = s0_ref[...]
        xa_s[HB - (CONV_W - 1):HB, :] = conv0_ref[...]

    xa_s[HB:HB + R, :] = proj_ref[:, O_QKV:O_QKV + QKV_C]
    convout_ref[...] = xa_s[HB + R - (CONV_W - 1):HB + R, :]

    CS = CHUNK
    P2 = 2 * CS
    gcols = proj_ref[:, O_G:O_G + LANES]
    beta_all = _sigmoid(gcols)
    g_all = -jnp.exp(hp_ref[0:1, :]) * _softplus(gcols + hp_ref[1:2, :])
    rr = lax.broadcasted_iota(I32, (R, R), 0)
    cc = lax.broadcasted_iota(I32, (R, R), 1)
    tri_b = jnp.where((rr >= cc) & (rr // CS == cc // CS), 1.0, 0.0).astype(BF16)
    G_all = _dot_exact_lhs(tri_b, g_all)
    G_t = _tr(G_all)
    eG_all = jnp.exp(G_all)
    r2 = lax.broadcasted_iota(I32, (P2, P2), 0)
    c2 = lax.broadcasted_iota(I32, (P2, P2), 1)
    same_head = (r2 // CS) == (c2 // CS)
    incl2 = same_head & (r2 >= c2)
    strict2 = same_head & (r2 > c2)
    eye2 = jnp.where(r2 == c2, 1.0, 0.0)
    lane_lo = lax.broadcasted_iota(I32, (DK_C, P2), 1) < CS
    scale_c = DK_C ** -0.5

    def conv_silu(rs0, c0):
        y = xa_s[HB - 3 + rs0:HB - 3 + rs0 + CS, c0:c0 + LANES] * convw_ref[0:1, c0:c0 + LANES]
        for j in range(1, CONV_W):
            y = y + xa_s[HB - 3 + j + rs0:HB - 3 + j + rs0 + CS, c0:c0 + LANES] * convw_ref[j:j + 1, c0:c0 + LANES]
        return y * _sigmoid(y)

    def l2n(x):
        return x * lax.rsqrt(jnp.sum(x * x, axis=-1, keepdims=True) + RMS_EPS)

    def stack(a, b):
        return jnp.concatenate([a, b], axis=0)

    def inv_unit_lower_all(a_list):
        ts = [eye2 - a for a in a_list]
        ps = [_dot_x3(a, a) for a in a_list]
        nlev = (CS - 1).bit_length() - 1
        for lvl in range(nlev):
            nt, npw = [], []
            for t, p in zip(ts, ps):
                th, tl = _split_bf16(t)
                ph, pl_ = _split_bf16(p)
                if lvl < nlev - 1:
                    lh, ll = stack(th, ph), stack(tl, pl_)
                    tp = _dot(lh, ph) + _dot(ll, ph) + _dot(lh, pl_)
                    nt.append(t + tp[:P2])
                    npw.append(tp[P2:])
                else:
                    nt.append(t + (_dot(th, ph) + _dot(tl, ph) + _dot(th, pl_)))
            ts, ps = nt, npw
        return ts

    probs = [(c0, pr) for c0 in range(0, R, CS) for pr in range(NH_C // 2)]
    pre = []
    for c0, pr in probs:
        rs = slice(c0, c0 + CS)
        ha, hb = 2 * pr, 2 * pr + 1
        ks = [l2n(conv_silu(c0, NH_C * DK_C + h * DK_C)) for h in (ha, hb)]
        qs = [l2n(conv_silu(c0, h * DK_C)) * scale_c for h in (ha, hb)]
        vs = [conv_silu(c0, 2 * NH_C * DK_C + h * DV_C) for h in (ha, hb)]
        k2 = stack(*ks)
        k2b = k2.astype(BF16)
        q2b = stack(*qs).astype(BF16)
        v2 = stack(*vs)
        G_col = stack(G_all[rs, NH_C + ha:NH_C + ha + 1], G_all[rs, NH_C + hb:NH_C + hb + 1])
        G_row = jnp.concatenate([G_t[NH_C + ha:NH_C + ha + 1, rs], G_t[NH_C + hb:NH_C + hb + 1, rs]], axis=1)
        beta = stack(beta_all[rs, ha:ha + 1], beta_all[rs, hb:hb + 1])
        eg = stack(eG_all[rs, NH_C + ha:NH_C + ha + 1], eG_all[rs, NH_C + hb:NH_C + hb + 1])
        GL = [G_all[c0 + CS - 1:c0 + CS, NH_C + h:NH_C + h + 1] for h in (ha, hb)]
        GL2 = stack(jnp.broadcast_to(GL[0], (CS, 1)), jnp.broadcast_to(GL[1], (CS, 1)))
        dmat = jnp.exp(jnp.where(incl2, G_col - G_row, -jnp.inf))
        pre.append(dict(
            a_mat=jnp.where(strict2, beta * dmat * _dot(k2b, k2b, _NT), 0.0),
            rhs=jnp.concatenate([beta * v2, (beta * eg) * k2], axis=1).astype(BF16),
            attn=(_dot(q2b, k2b, _NT) * dmat).astype(BF16),
            kdec_t=_tr(k2 * jnp.exp(GL2 - G_col)).astype(BF16),
            q2b=q2b, eg=eg, egl=[jnp.exp(GL[0]), jnp.exp(GL[1])]))
    tinvs = inv_unit_lower_all([d['a_mat'] for d in pre])
    for d, tinv in zip(pre, tinvs):
        d['uw'] = _dot(tinv.astype(BF16), d['rhs'])

    S_cur = [s_s[h] for h in range(NH_C)]
    npair = NH_C // 2
    for ci, c0 in enumerate(range(0, R, CS)):
        rs = slice(c0, c0 + CS)
        ds = pre[ci * npair:(ci + 1) * npair]
        wqs = []
        for pr, d in enumerate(ds):
            uw, q2b = d['uw'], d['q2b']
            wqs.append([_dot(stack(uw[j * CS:(j + 1) * CS, DV_C:].astype(BF16), q2b[j * CS:(j + 1) * CS]),
                             S_cur[2 * pr + j].astype(BF16)) for j in range(2)])
        dbs = [(d['uw'][:, :DV_C] - stack(wq[0][:CS], wq[1][:CS])).astype(BF16) for d, wq in zip(ds, wqs)]
        for pr, (d, wq, db) in enumerate(zip(ds, wqs, dbs)):
            ha, hb = 2 * pr, 2 * pr + 1
            kdec_t = d['kdec_t']
            zero = jnp.zeros_like(kdec_t)
            S_cur[ha] = d['egl'][0] * S_cur[ha] + _dot(jnp.where(lane_lo, kdec_t, zero), db)
            S_cur[hb] = d['egl'][1] * S_cur[hb] + _dot(jnp.where(lane_lo, zero, kdec_t), db)
        for pr, (d, wq, db) in enumerate(zip(ds, wqs, dbs)):
            o2 = d['eg'] * stack(wq[0][CS:], wq[1][CS:]) + _dot(d['attn'], db)
            for j in range(2):
                h = 2 * pr + j
                o = o2[j * CS:(j + 1) * CS]
                z = proj_ref[rs, O_Z + h * DV_C:O_Z + (h + 1) * DV_C]
                o = o * lax.rsqrt(jnp.mean(o * o, axis=-1, keepdims=True) + RMS_EPS) * normc_ref[...]
                o = o * (z * _sigmoid(z))
                mix_ref[rs, h * DV_C:(h + 1) * DV_C] = o.astype(mix_ref.dtype)
    for h in range(NH_C):
        s_s[h] = S_cur[h]

    xa_s[HB - (CONV_W - 1):HB, :] = xa_s[HB + R - (CONV_W - 1):HB + R, :]
    sout_ref[...] = s_s[...]


def _odd_mixer(proj, conv_hist, s0, conv_w, a_log, dt_bias, norm_c, batch, seq_len, rows):
    nsteps = seq_len // rows
    hp = jnp.zeros((8, LANES), F32)
    hp = hp.at[0, NH_C:2 * NH_C].set(a_log.astype(F32)).at[1, NH_C:2 * NH_C].set(dt_bias.astype(F32))
    outs = pl.pallas_call(
        functools.partial(_odd_kernel, rows=rows),
        out_shape=(
            jax.ShapeDtypeStruct((batch * seq_len, D_MODEL), BF16),
            jax.ShapeDtypeStruct((batch, NH_C, DK_C, DV_C), F32),
            jax.ShapeDtypeStruct((batch, CONV_W - 1, QKV_C), F32),
        ),
        grid=(batch, nsteps),
        in_specs=[
            pl.BlockSpec((rows, N_ODD_COLS), lambda b, i: (b * nsteps + i, 0)),
            pl.BlockSpec((None, CONV_W - 1, QKV_C), lambda b, i: (b, 0, 0)),
            pl.BlockSpec((None, NH_C, DK_C, DV_C), lambda b, i: (b, 0, 0, 0)),
            pl.BlockSpec((CONV_W, QKV_C), lambda b, i: (0, 0)),
            pl.BlockSpec((8, LANES), lambda b, i: (0, 0)),
            pl.BlockSpec((1, DV_C), lambda b, i: (0, 0)),
        ],
        out_specs=(
            pl.BlockSpec((rows, D_MODEL), lambda b, i: (b * nsteps + i, 0)),
            pl.BlockSpec((None, NH_C, DK_C, DV_C), lambda b, i: (b, 0, 0, 0)),
            pl.BlockSpec((None, CONV_W - 1, QKV_C), lambda b, i: (b, 0, 0)),
        ),
        scratch_shapes=[
            pltpu.VMEM((NH_C, DK_C, DV_C), F32),
            pltpu.VMEM((8 + rows, QKV_C), F32),
        ],
        compiler_params=_cparams(("arbitrary", "arbitrary")),
        name="odd_mixer",
    )(proj, conv_hist, s0, conv_w, hp, norm_c.reshape(1, DV_C))
    return outs


def _post_kernel(mix_ref, x_ref, gate_ref, sc_ref, sh_ref, wout_ref, lng_ref, lnb_ref, wr_ref, br_ref,
                 x1_ref, h2_ref, ri_ref, rg_ref, cnt_ref, carry_s, *, tm):
    i = pl.program_id(0)

    @pl.when(i == 0)
    def _():
        carry_s[...] = jnp.zeros_like(carry_s)

    y = _dot(mix_ref[...], wout_ref[...])
    x1 = _layer_norm_rows(ALPHA_DN * x_ref[...] + (1.0 + gate_ref[...]) * y, lng_ref[...], lnb_ref[...])
    x1_ref[...] = x1
    h2 = x1 * (1.0 + sc_ref[...]) + sh_ref[...]
    h2_ref[...] = h2
    lt = _dot(wr_ref[...].astype(BF16), h2.astype(BF16), _NT) + br_ref[:, 0:1]
    e_iota = lax.broadcasted_iota(I32, (N_EXPERTS, tm), 0).astype(F32)
    vals, idxs = [], []
    for _ in range(TOP_K):
        mx = jnp.max(lt, axis=0, keepdims=True)
        idx = jnp.min(jnp.where(lt == mx, e_iota, float(N_EXPERTS)), axis=0, keepdims=True)
        vals.append(mx)
        idxs.append(idx)
        lt = jnp.where(e_iota == idx, -jnp.inf, lt)
    ex = [jnp.exp(v - vals[0]) for v in vals]
    tot = ex[0] + ex[1] + ex[2] + ex[3]
    hot = [jnp.where(e_iota == idx, 1.0, 0.0) for idx in idxs]
    m_all = hot[0] + hot[1] + hot[2] + hot[3]
    r = lax.broadcasted_iota(I32, (tm, tm), 0)
    c = lax.broadcasted_iota(I32, (tm, tm), 1)
    upper = jnp.where(r < c, 1.0, 0.0).astype(BF16)
    base = carry_s[:, 0:1] + _dot(m_all.astype(BF16), upper)
    ranks = [jnp.sum(hk * base, axis=0, keepdims=True) for hk in hot]
    carry_s[...] = carry_s[...] + jnp.sum(m_all, axis=1, keepdims=True)
    cnt_ref[...] = carry_s[...]
    ri_ref[...] = jnp.concatenate(idxs + ranks, axis=0).astype(I32)
    rg_ref[...] = jnp.concatenate([e / tot for e in ex] + [jnp.zeros((4, tm), F32)], axis=0)


def _post(mix, x2d, gate, scale, shift, w_out_bf16, ln_g, ln_b, wr_t, br, seq_len, tm):
    t = x2d.shape[0]
    per = seq_len // tm
    vec = lambda i: (i // per, 0, 0)
    const2 = lambda i: (0, 0)
    return pl.pallas_call(
        functools.partial(_post_kernel, tm=tm),
        out_shape=(
            jax.ShapeDtypeStruct((t, D_MODEL), F32),
            jax.ShapeDtypeStruct((t, D_MODEL), F32),
            jax.ShapeDtypeStruct((t // tm, 8, tm), I32),
            jax.ShapeDtypeStruct((t // tm, 8, tm), F32),
            jax.ShapeDtypeStruct((N_EXPERTS, LANES), F32),
        ),
        grid=(t // tm,),
        in_specs=[
            pl.BlockSpec((tm, D_MODEL), lambda i: (i, 0)),
            pl.BlockSpec((tm, D_MODEL), lambda i: (i, 0)),
            pl.BlockSpec((None, 1, D_MODEL), vec),
            pl.BlockSpec((None, 1, D_MODEL), vec),
            pl.BlockSpec((None, 1, D_MODEL), vec),
            pl.BlockSpec((D_MODEL, D_MODEL), const2),
            pl.BlockSpec((1, D_MODEL), const2),
            pl.BlockSpec((1, D_MODEL), const2),
            pl.BlockSpec((N_EXPERTS, D_MODEL), const2),
            pl.BlockSpec((N_EXPERTS, LANES), const2),
        ],
        out_specs=(
            pl.BlockSpec((tm, D_MODEL), lambda i: (i, 0)),
            pl.BlockSpec((tm, D_MODEL), lambda i: (i, 0)),
            pl.BlockSpec((None, 8, tm), lambda i: (i, 0, 0)),
            pl.BlockSpec((None, 8, tm), lambda i: (i, 0, 0)),
            pl.BlockSpec((N_EXPERTS, LANES), const2),
        ),
        scratch_shapes=[pltpu.VMEM((N_EXPERTS, LANES), F32)],
        compiler_params=_cparams(("arbitrary",)),
        name="post_router",
    )(mix, x2d, gate, scale, shift, w_out_bf16, ln_g, ln_b, wr_t, br)


def _dispatch_kernel(zblk_ref, rstart_ref, ri_ref, h_ref, xs_ref, zero_s, sem, zsem, *, tm, bm):
    i = pl.program_id(0)

    def zero_copy(e):
        return pltpu.make_async_copy(zero_s, xs_ref.at[pl.ds(zblk_ref[e] * bm, bm), :], zsem)

    @pl.when(i == 0)
    def _():
        zero_s[...] = jnp.zeros_like(zero_s)

        def zstart(e, c):
            @pl.when(zblk_ref[e] >= 0)
            def _():
                zero_copy(e).start()
            return c

        def zwait(e, c):
            @pl.when(zblk_ref[e] >= 0)
            def _():
                zero_copy(e).wait()
            return c

        lax.fori_loop(0, N_EXPERTS, zstart, 0)
        lax.fori_loop(0, N_EXPERTS, zwait, 0)

    def body(t, c):
        for k in range(TOP_K):
            p = rstart_ref[ri_ref[k, t]] + ri_ref[TOP_K + k, t]
            pltpu.make_async_copy(h_ref.at[pl.ds(t, 1), :], xs_ref.at[pl.ds(p, 1), :], sem).start()
        return c

    lax.fori_loop(0, tm, body, 0)
    for k in range(TOP_K):
        pltpu.make_async_copy(h_ref, xs_ref.at[pl.ds(0, tm), :], sem).wait()


def _dispatch(h2, route_i, row_start, zblk, n_rows, tm, bm):
    t = h2.shape[0]
    return pl.pallas_call(
        functools.partial(_dispatch_kernel, tm=tm, bm=bm),
        out_shape=jax.ShapeDtypeStruct((n_rows, D_MODEL), F32),
        grid_spec=pltpu.PrefetchScalarGridSpec(
            num_scalar_prefetch=2,
            grid=(t // tm,),
            in_specs=[
                pl.BlockSpec((None, 2 * TOP_K, tm), lambda i, z, r: (i, 0, 0), memory_space=pltpu.SMEM),
                pl.BlockSpec((tm, D_MODEL), lambda i, z, r: (i, 0)),
            ],
            out_specs=pl.BlockSpec(memory_space=pl.ANY),
            scratch_shapes=[
                pltpu.VMEM((bm, D_MODEL), F32),
                pltpu.SemaphoreType.DMA,
                pltpu.SemaphoreType.DMA,
            ],
        ),
        compiler_params=_cparams(("arbitrary",)),
        name="dispatch",
    )(zblk, row_start, route_i, h2)


def _ffn_kernel(be_ref, nv_ref, x_ref, w1_ref, b1_ref, w2_ref, b2_ref, y_ref, w1_s, w2_s):
    b = pl.program_id(0)

    @pl.when(b < nv_ref[0])
    def _():
        @pl.when((b == 0) | (be_ref[b] != be_ref[jnp.maximum(b - 1, 0)]))
        def _():
            step = 256
            for r0 in range(0, D_MODEL, step):
                w1_s[r0:r0 + step, :] = w1_ref[r0:r0 + step, :].astype(BF16)
            for r0 in range(0, D_FF, step):
                w2_s[r0:r0 + step, :] = w2_ref[r0:r0 + step, :].astype(BF16)

        xb = x_ref[...].astype(BF16)
        glu = _dot(xb, w1_s[:, :D_FF]) + b1_ref[:, :D_FF]
        lin = _dot(xb, w1_s[:, D_FF:]) + b1_ref[:, D_FF:]
        glu = jnp.minimum(glu, SWIGLU_LIMIT)
        lin = jnp.clip(lin, -SWIGLU_LIMIT, SWIGLU_LIMIT)
        act = glu * _sigmoid(SWIGLU_ALPHA * glu) * (lin + 1.0)
        y_ref[...] = _dot(act.astype(BF16), w2_s[...]) + b2_ref[...]


def _ffn(xs, block_e, n_valid, layer, w1, b1, w2, b2, bm):
    nb = xs.shape[0] // bm
    row = lambda b, be, nv: (jnp.minimum(b, nv[0] - 1), 0)
    wsel = lambda b, be, nv: (layer, be[b], 0, 0)
    return pl.pallas_call(
        _ffn_kernel,
        out_shape=jax.ShapeDtypeStruct(xs.shape, F32),
        grid_spec=pltpu.PrefetchScalarGridSpec(
            num_scalar_prefetch=2,
            grid=(nb,),
            in_specs=[
                pl.BlockSpec((bm, D_MODEL), row),
                pl.BlockSpec((None, None, D_MODEL, 2 * D_FF), wsel),
                pl.BlockSpec((None, None, 1, 2 * D_FF), wsel),
                pl.BlockSpec((None, None, D_FF, D_MODEL), wsel),
                pl.BlockSpec((None, None, 1, D_MODEL), wsel),
            ],
            out_specs=pl.BlockSpec((bm, D_MODEL), row),
            scratch_shapes=[pltpu.VMEM((D_MODEL, 2 * D_FF), BF16), pltpu.VMEM((D_FF, D_MODEL), BF16)],
        ),
        compiler_params=_cparams(("arbitrary",)),
        name="expert_ffn",
    )(block_e, n_valid, xs, w1, b1, w2, b2)


def _combine_kernel(rstart_ref, ri_ref, x_ref, gate_ref, rg_ref, lng_ref, lnb_ref, ys_ref, o_ref, buf, sem, *, tm):
    def body(t, c):
        for k in range(TOP_K):
            p = rstart_ref[ri_ref[k, t]] + ri_ref[TOP_K + k, t]
            pltpu.make_async_copy(ys_ref.at[pl.ds(p, 1), :], buf.at[k, pl.ds(t, 1), :], sem).start()
        return c

    lax.fori_loop(0, tm, body, 0)
    for k in range(TOP_K):
        pltpu.make_async_copy(ys_ref.at[pl.ds(0, tm), :], buf.at[k], sem).wait()
    g_t = _tr(jnp.concatenate([rg_ref[...], jnp.zeros((LANES - 8, tm), F32)], axis=0))
    moe = g_t[:, 0:1] * buf[0]
    for k in range(1, TOP_K):
        moe = moe + g_t[:, k:k + 1] * buf[k]
    o_ref[...] = _layer_norm_rows(ALPHA_DN * x_ref[...] + (1.0 + gate_ref[...]) * moe,
                                  lng_ref[...], lnb_ref[...])


def _combine(row_start, route_i, x1, gate, route_g, ln_g, ln_b, ys, seq_len, tm):
    t = x1.shape[0]
    per = seq_len // tm
    return pl.pallas_call(
        functools.partial(_combine_kernel, tm=tm),
        out_shape=jax.ShapeDtypeStruct((t, D_MODEL), F32),
        grid_spec=pltpu.PrefetchScalarGridSpec(
            num_scalar_prefetch=1,
            grid=(t // tm,),
            in_specs=[
                pl.BlockSpec((None, 2 * TOP_K, tm), lambda i, r: (i, 0, 0), memory_space=pltpu.SMEM),
                pl.BlockSpec((tm, D_MODEL), lambda i, r: (i, 0)),
                pl.BlockSpec((None, 1, D_MODEL), lambda i, r: (i // per, 0, 0)),
                pl.BlockSpec((None, 8, tm), lambda i, r: (i, 0, 0)),
                pl.BlockSpec((1, D_MODEL), lambda i, r: (0, 0)),
                pl.BlockSpec((1, D_MODEL), lambda i, r: (0, 0)),
                pl.BlockSpec(memory_space=pl.ANY),
            ],
            out_specs=pl.BlockSpec((tm, D_MODEL), lambda i, r: (i, 0)),
            scratch_shapes=[pltpu.VMEM((TOP_K, tm, D_MODEL), F32), pltpu.SemaphoreType.DMA],
        ),
        compiler_params=_cparams(("arbitrary",)),
        name="combine",
    )(row_start, route_i, x1, gate, route_g, ln_g, ln_b, ys)


def _moe_tables(counts, bm, n_blocks):
    cnt = counts[:, 0].astype(I32)
    nblk = (cnt + bm - 1) // bm
    blk_end = jnp.cumsum(nblk)
    row_start = (blk_end - nblk) * bm
    n_valid = blk_end[-1]
    bidx = jnp.minimum(jnp.arange(n_blocks, dtype=I32), n_valid - 1)
    block_e = jnp.sum((blk_end[None, :] <= bidx[:, None]).astype(I32), axis=1)
    block_e = jnp.minimum(block_e, N_EXPERTS - 1)
    zblk = jnp.where((cnt % bm) != 0, blk_end - 1, -1).astype(I32)
    return row_start.astype(I32), block_e, n_valid.reshape(1).astype(I32), zblk


def _moe_and_norm(h2, x1, route_i, route_g, counts, gate, ln_g, ln_b, layer, w1, b1, w2, b2, seq_len, tm, bm):
    t = h2.shape[0]
    n_blocks = t * TOP_K // bm + N_EXPERTS
    row_start, block_e, n_valid, zblk = _moe_tables(counts, bm, n_blocks)
    xs = _dispatch(h2, route_i, row_start, zblk, n_blocks * bm, tm, bm)
    ys = _ffn(xs, block_e, n_valid, layer, w1, b1, w2, b2, bm)
    return _combine(row_start, route_i, x1, gate, route_g, ln_g, ln_b, ys, seq_len, tm)


def _group_cfg(batch, seq_len):
    if seq_len >= 512:
        return dict(tm=512, rows=256, bm=256)
    return dict(tm=seq_len, rows=seq_len, bm=128)


def kernel(x_prompt, x_sample, c_prompt, c_sample, state_a_C, state_a_n, state_a_m, cache_b_k, cache_b_v,
           state_c_S, state_c_conv, w_ada, b_ada, ln_g, ln_b, w_in_even, b_in_even, norm_a, sink_b, rel_bias,
           w_out_even, w_in_odd, conv_c, a_log_c, dt_bias_c, norm_c, w_out_odd, w_router, b_router,
           w_e1, b_e1, w_e2, b_e2):
    bp, lp, _ = x_prompt.shape
    bs, ls, _ = x_sample.shape
    groups = [dict(b=bp, l=lp, x=x_prompt.reshape(bp * lp, D_MODEL), **_group_cfg(bp, lp)),
              dict(b=bs, l=ls, x=x_sample.reshape(bs * ls, D_MODEL), **_group_cfg(bs, ls))]
    mod = _ada(jnp.concatenate([c_prompt, c_sample], axis=0), w_ada, b_ada)
    offs = [0, bp]
    states = [dict(), dict()]
    b1_all = b_e1.reshape(DEPTH, N_EXPERTS, 1, 2 * D_FF)
    b2_all = b_e2.reshape(DEPTH, N_EXPERTS, 1, D_MODEL)
    for l in range(DEPTH):
        e = l // 2
        if l % 2 == 0:
            w = w_in_even[e]
            sz = (512, 512, 512, 512, 4, 4, 512, 128, 128)
            o = [sum(sz[:j]) for j in range(len(sz) + 1)]
            pad = jnp.zeros((D_MODEL, LANES - 2 * NH_A), w.dtype)
            w_in = jnp.concatenate([w[:, o[0]:o[4]], w[:, o[6]:o[9]], w[:, o[4]:o[6]], pad], axis=1).astype(BF16)
            bb = b_in_even[e]
            b_in = jnp.concatenate([bb[o[0]:o[4]], bb[o[6]:o[9]], bb[o[4]:o[6]],
                                    jnp.zeros((LANES - 2 * NH_A,), bb.dtype)]).reshape(1, N_EVEN_COLS)
            w_out = w_out_even[e].astype(BF16)
        else:
            w = w_in_odd[e]
            pad = jnp.zeros((D_MODEL, LANES - 2 * NH_C), w.dtype)
            w_in = jnp.concatenate([w, pad], axis=1).astype(BF16)
            b_in = jnp.zeros((1, N_ODD_COLS), F32)
            w_out = w_out_odd[e].astype(BF16)
        wr_t = w_router[l].T
        br = jnp.broadcast_to(b_router[l][:, None], (N_EXPERTS, LANES))
        for gi, gr in enumerate(groups):
            nb, sl = gr['b'], gr['l']
            m = mod[l, offs[gi]:offs[gi] + nb].reshape(nb, 6, 1, D_MODEL)
            m6 = [m[:, j] for j in range(6)]
            proj = _inproj(gr['x'], m6[1], m6[0], w_in, b_in, sl, gr['tm'])
            st = states[gi]
            if l % 2 == 0:
                if gi == 0:
                    kh = jnp.zeros((nb, WINDOW, NKV_B, DH_B), F32)
                    vh = kh
                    c0 = jnp.zeros((nb, NH_A, DK_A, DV_A), F32)
                    n0 = jnp.zeros((nb, NH_A, DK_A), F32)
                    m0 = jnp.zeros((nb, NH_A), F32)
                else:
                    kh, vh, c0, n0, m0 = cache_b_k[e], cache_b_v[e], state_a_C[e], state_a_n[e], state_a_m[e]
                bias_tab = _swa_bias_table(rel_bias, gr['rows'])
                mix, c1, n1, m1, k1, v1 = _even_mixer(proj, kh, vh, c0, n0, m0, bias_tab, norm_a[e], sink_b[e],
                                                      nb, sl, gr['rows'], gi == 1)
                for name, val in (('a_C', c1), ('a_n', n1), ('a_m', m1), ('b_k', k1), ('b_v', v1)):
                    st.setdefault(name, []).append(val)
            else:
                if gi == 0:
                    cv0 = jnp.zeros((nb, CONV_W - 1, QKV_C), F32)
                    s0 = jnp.zeros((nb, NH_C, DK_C, DV_C), F32)
                else:
                    cv0, s0 = state_c_conv[e], state_c_S[e]
                mix, s1, cv1 = _odd_mixer(proj, cv0, s0, conv_c[e], a_log_c[e], dt_bias_c[e], norm_c[e], nb, sl,
                                          gr['rows'])
                st.setdefault('c_S', []).append(s1)
                st.setdefault('c_conv', []).append(cv1)
            x1, h2, route_i, route_g, counts = _post(
                mix, gr['x'], m6[2], m6[4], m6[3], w_out, ln_g[l, 0].reshape(1, D_MODEL),
                ln_b[l, 0].reshape(1, D_MODEL), wr_t, br, sl, gr['tm'])
            gr['x'] = _moe_and_norm(h2, x1, route_i, route_g, counts, m6[5], ln_g[l, 1].reshape(1, D_MODEL),
                                    ln_b[l, 1].reshape(1, D_MODEL), l, w_e1, b1_all, w_e2, b2_all,
                                    sl, gr['tm'], gr['bm'])
    outs = [groups[0]['x'].reshape(bp, lp, D_MODEL), groups[1]['x'].reshape(bs, ls, D_MODEL)]
    for gi in range(2):
        for name in ('a_C', 'a_n', 'a_m', 'b_k', 'b_v', 'c_S', 'c_conv'):
            outs.append(jnp.stack(states[gi][name]))
    return tuple(outs)
```

```python
import functools
import math

import jax
import jax.numpy as jnp
from jax import lax
from jax.experimental import pallas as pl
from jax.experimental.pallas import tpu as pltpu

F32 = jnp.float32
BF16 = jnp.bfloat16
I32 = jnp.int32

D_MODEL = 1024
CHUNK = 64
NH_A, DK_A, DV_A = 4, 128, 128
W_A = NH_A * DV_A
NH_B, NKV_B, DH_B, WINDOW = 8, 2, 64, 128
G_B = NH_B // NKV_B
W_B = NH_B * DH_B
NUM_BUCKETS, MAX_DISTANCE = 32, 256
NH_C, DK_C, DV_C, CONV_W = 8, 128, 128, 4
QKV_C = NH_C * (2 * DK_C + DV_C)
N_EXPERTS, TOP_K, D_FF = 32, 4, 1024
SWIGLU_LIMIT, SWIGLU_ALPHA = 7.0, 1.702
DEPTH = 2
ALPHA_DN = (2 * DEPTH) ** 0.25
LN_EPS = 1e-5
RMS_EPS = 1e-6

LANES = 128
E_QA, E_KA, E_VA, E_OA, E_QB, E_KB, E_VB, E_G = 0, 512, 1024, 1536, 2048, 2560, 2688, 2816
N_EVEN_COLS = 2944
O_QKV, O_Z, O_G = 0, 3072, 4096
N_ODD_COLS = 4224

VMEM_LIMIT = 56 * 1024 * 1024


def _cparams(sem):
    return pltpu.CompilerParams(dimension_semantics=sem, vmem_limit_bytes=VMEM_LIMIT)


def _softplus(x):
    return jnp.maximum(x, 0.0) + jnp.log(1.0 + jnp.exp(-jnp.abs(x)))


def _sigmoid(x):
    return 1.0 / (1.0 + jnp.exp(-x))


def _split_bf16(a):
    hi = a.astype(BF16)
    lo = (a - hi.astype(F32)).astype(BF16)
    return hi, lo


_NN = (((1,), (0,)), ((), ()))
_NT = (((1,), (1,)), ((), ()))
_TN = (((0,), (0,)), ((), ()))


def _dot(a, b, dims=_NN):
    return lax.dot_general(a, b, dims, preferred_element_type=F32)


def _dot_x3(a, b, dims=_NN):
    ah, al = _split_bf16(a)
    bh, bl = _split_bf16(b)
    return _dot(ah, bh, dims) + _dot(ah, bl, dims) + _dot(al, bh, dims)


def _dot_exact_lhs(a_bf16, b, dims=_NN):
    b0 = b.astype(BF16)
    r1 = b - b0.astype(F32)
    b1 = r1.astype(BF16)
    b2 = (r1 - b1.astype(F32)).astype(BF16)
    return _dot(a_bf16, b0, dims) + _dot(a_bf16, b1, dims) + _dot(a_bf16, b2, dims)


def _tr(x):
    r = x.shape[0]
    rp = -(-r // LANES) * LANES
    if rp != r:
        x = jnp.concatenate([x, jnp.zeros((rp - r, x.shape[1]), x.dtype)], axis=0)
    return x.T[:, :r]


def _tri(n, strict=False):
    r = lax.broadcasted_iota(I32, (n, n), 0)
    c = lax.broadcasted_iota(I32, (n, n), 1)
    return (r > c) if strict else (r >= c)


def _layer_norm_rows(v, g, b):
    mu = jnp.mean(v, axis=-1, keepdims=True)
    d = v - mu
    var = jnp.mean(d * d, axis=-1, keepdims=True)
    return d * lax.rsqrt(var + LN_EPS) * g + b


def _ada_kernel(c_ref, w_ref, b_ref, o_ref):
    c = c_ref[...]
    a = (c * _sigmoid(c)).astype(BF16)
    o_ref[...] = _dot(a, w_ref[...].astype(BF16)) + b_ref[...]


def _ada(c_all, w_ada, b_ada):
    nb = c_all.shape[0]
    tn = 1536
    return pl.pallas_call(
        _ada_kernel,
        out_shape=jax.ShapeDtypeStruct((DEPTH, nb, 6 * D_MODEL), F32),
        grid=(DEPTH, 6 * D_MODEL // tn),
        in_specs=[
            pl.BlockSpec((nb, D_MODEL), lambda l, j: (0, 0)),
            pl.BlockSpec((None, D_MODEL, tn), lambda l, j: (l, 0, j)),
            pl.BlockSpec((None, 1, tn), lambda l, j: (l, 0, j)),
        ],
        out_specs=pl.BlockSpec((None, nb, tn), lambda l, j: (l, 0, j)),
        compiler_params=_cparams(("arbitrary", "arbitrary")),
        name="ada",
    )(c_all, w_ada, b_ada.reshape(DEPTH, 1, 6 * D_MODEL))


def _inproj_kernel(x_ref, sc_ref, sh_ref, w_ref, b_ref, o_ref, *, n_cols, col_step):
    h = (x_ref[...] * (1.0 + sc_ref[...]) + sh_ref[...]).astype(BF16)
    for c0 in range(0, n_cols, col_step):
        c1 = min(c0 + col_step, n_cols)
        o_ref[:, c0:c1] = _dot(h, w_ref[:, c0:c1]) + b_ref[:, c0:c1]


def _inproj(x2d, scale, shift, w_bf16, bias, seq_len, tm):
    t = x2d.shape[0]
    n = w_bf16.shape[1]
    per = seq_len // tm
    return pl.pallas_call(
        functools.partial(_inproj_kernel, n_cols=n, col_step=1024),
        out_shape=jax.ShapeDtypeStruct((t, n), F32),
        grid=(t // tm,),
        in_specs=[
            pl.BlockSpec((tm, D_MODEL), lambda i: (i, 0)),
            pl.BlockSpec((None, 1, D_MODEL), lambda i: (i // per, 0, 0)),
            pl.BlockSpec((None, 1, D_MODEL), lambda i: (i // per, 0, 0)),
            pl.BlockSpec((D_MODEL, n), lambda i: (0, 0)),
            pl.BlockSpec((1, n), lambda i: (0, 0)),
        ],
        out_specs=pl.BlockSpec((tm, n), lambda i: (i, 0)),
        compiler_params=_cparams(("arbitrary",)),
        name="inproj",
    )(x2d, scale, shift, w_bf16, bias)


def _even_kernel(proj_ref, kh0_ref, vh0_ref, c0_ref, n0_ref, m0_ref, bias_ref, norma_ref, sink_ref,
                 mix_ref, cout_ref, nout_ref, mout_ref, kout_ref, vout_ref,
                 c_s, n_s, m_s, kh_s, vh_s, *, rows, hist_valid):
    R = rows
    KW = WINDOW + R
    i = pl.program_id(1)

    @pl.when(i == 0)
    def _():
        c_s[...] = c0_ref[...]
        n_s[...] = n0_ref[...]
        m_s[...] = m0_ref[...]
        kh_s[...] = kh0_ref[...]
        vh_s[...] = vh0_ref[...]

    CS = CHUNK
    g = proj_ref[:, E_G:E_G + LANES]
    lf = -_softplus(-g)
    rr = lax.broadcasted_iota(I32, (R, R), 0)
    cc = lax.broadcasted_iota(I32, (R, R), 1)
    tri_b = jnp.where((rr >= cc) & (rr // CS == cc // CS), 1.0, 0.0).astype(BF16)
    b_all = _dot_exact_lhs(tri_b, lf)
    g_t = _tr(g)
    b_t = _tr(b_all)
    causal = _tri(CS)
    scale_a = DK_A ** -0.5
    chunks = list(range(0, R, CS))
    pb = {}
    for c0 in chunks:
        rs = slice(c0, c0 + CS)
        for h in range(NH_A):
            b_col = b_all[rs, NH_A + h:NH_A + h + 1]
            b_row = b_t[NH_A + h:NH_A + h + 1, rs]
            logw = jnp.where(causal, b_col - b_row + g_t[h:h + 1, rs], -jnp.inf)
            pb[c0, h] = dict(b_col=b_col, logw=logw, lmax=jnp.max(logw, axis=-1, keepdims=True))
    for h in range(NH_A):
        m_prev = m_s[h:h + 1, 0:1]
        for c0 in chunks:
            d = pb[c0, h]
            d['m_prev'] = m_prev
            d['m_inter'] = d['b_col'] + m_prev
            d['m_t'] = jnp.maximum(d['m_inter'], d['lmax'])
            m_prev = d['m_t'][CS - 1:CS, :]
            d['m_new'] = m_prev
        m_s[h:h + 1, :] = jnp.broadcast_to(m_prev, (1, LANES))
    for c0 in chunks:
        rs = slice(c0, c0 + CS)
        for h in range(NH_A):
            d = pb[c0, h]
            q = proj_ref[rs, E_QA + h * DK_A:E_QA + (h + 1) * DK_A] * scale_a
            k = proj_ref[rs, E_KA + h * DK_A:E_KA + (h + 1) * DK_A]
            qb = q.astype(BF16)
            vb = proj_ref[rs, E_VA + h * DV_A:E_VA + (h + 1) * DV_A].astype(BF16)
            s = _dot(qb, k.astype(BF16), _NT) * jnp.exp(d['logw'] - d['m_t'])
            b_last = d['b_col'][CS - 1:CS, :]
            kw = k * jnp.exp(b_last - d['b_col'] + g[rs, h:h + 1] - d['m_new'])
            d.update(qb=qb, dec=jnp.exp(d['m_inter'] - d['m_t']), sv=_dot(s.astype(BF16), vb),
                     ssum=jnp.sum(s, axis=-1, keepdims=True), kv=_dot(_tr(kw).astype(BF16), vb),
                     ksum=jnp.sum(kw, axis=0, keepdims=True), carry=jnp.exp(b_last + d['m_prev'] - d['m_new']))
    for h in range(NH_A):
        n_prev = n_s[h:h + 1, :]
        c_prev = c_s[h]
        for c0 in chunks:
            d = pb[c0, h]
            d['c_prev'], d['n_prev'] = c_prev, n_prev
            c_prev = d['carry'] * c_prev + d['kv']
            n_prev = d['carry'] * n_prev + d['ksum']
        c_s[h] = c_prev
        n_s[h:h + 1, :] = n_prev
    for c0 in chunks:
        rs = slice(c0, c0 + CS)
        for h in range(NH_A):
            d = pb[c0, h]
            qb = d['qb']
            num = d['dec'] * _dot(qb, d['c_prev'].astype(BF16)) + d['sv']
            qn = jnp.sum(qb.astype(F32) * d['n_prev'].astype(BF16).astype(F32), axis=-1, keepdims=True)
            den = d['dec'] * qn + d['ssum']
            hh = num / jnp.maximum(jnp.abs(den), jnp.exp(-d['m_t']))
            og = proj_ref[rs, E_OA + h * DV_A:E_OA + (h + 1) * DV_A]
            mu = jnp.mean(hh, axis=-1, keepdims=True)
            dd = hh - mu
            var = jnp.mean(dd * dd, axis=-1, keepdims=True)
            ha = dd * lax.rsqrt(var + LN_EPS) * norma_ref[h:h + 1, :] * _sigmoid(og)
            mix_ref[rs, h * DV_A:(h + 1) * DV_A] = ha.astype(mix_ref.dtype)

    cout_ref[...] = c_s[...]
    nout_ref[...] = n_s[...]
    mout_ref[...] = m_s[...]

    k_win = jnp.concatenate([kh_s[...], proj_ref[:, E_KB:E_KB + NKV_B * DH_B]], axis=0)
    v_win = jnp.concatenate([vh_s[...], proj_ref[:, E_VB:E_VB + NKV_B * DH_B]], axis=0)
    k_win_b = k_win.astype(BF16)
    v_win_b = v_win.astype(BF16)
    if not hist_valid:
        key_pos = lax.broadcasted_iota(I32, (R, KW), 1) + (i * R - WINDOW)
        key_ok = key_pos >= 0
    scale_b = DH_B ** -0.5
    kgs = [k_win_b[:, kv * DH_B:(kv + 1) * DH_B] for kv in range(NKV_B)]
    vgs = [v_win_b[:, kv * DH_B:(kv + 1) * DH_B] for kv in range(NKV_B)]
    scores = []
    for hd in range(NH_B):
        qh = proj_ref[:, E_QB + hd * DH_B:E_QB + (hd + 1) * DH_B].astype(BF16)
        s = _dot(qh, kgs[hd // G_B], _NT) * scale_b + bias_ref[hd]
        if not hist_valid:
            s = jnp.where(key_ok, s, -jnp.inf)
        scores.append(s)
    probs_b = []
    for hd, s in enumerate(scores):
        sk = sink_ref[hd]
        mx = jnp.maximum(jnp.max(s, axis=-1, keepdims=True), sk)
        p = jnp.exp(s - mx)
        p = p / (jnp.sum(p, axis=-1, keepdims=True) + jnp.exp(sk - mx))
        probs_b.append(p.astype(BF16))
    for hd, p in enumerate(probs_b):
        o = _dot(p, vgs[hd // G_B])
        mix_ref[:, W_A + hd * DH_B:W_A + (hd + 1) * DH_B] = o.astype(mix_ref.dtype)

    kh_s[...] = k_win[R:, :]
    vh_s[...] = v_win[R:, :]
    kout_ref[...] = kh_s[...]
    vout_ref[...] = vh_s[...]


def _rel_bucket(rel):
    nb = NUM_BUCKETS // 2
    max_exact = nb // 2
    n = jnp.abs(rel)
    nf = jnp.maximum(n, 1).astype(F32)
    large = max_exact + (jnp.log(nf / max_exact) / math.log(MAX_DISTANCE / max_exact)
                         * (nb - max_exact)).astype(I32)
    large = jnp.minimum(large, nb - 1)
    return jnp.where(rel > 0, nb, 0) + jnp.where(n < max_exact, n, large)


def _swa_bias_table(rel_bias, rows):
    kw = WINDOW + rows
    qi = jnp.arange(rows)[:, None]
    kj = jnp.arange(kw)[None, :]
    bucket = _rel_bucket(kj - WINDOW - qi)
    rb = rel_bias.astype(F32)
    bias = jnp.zeros((NH_B, rows, kw), F32)
    for b in range(NUM_BUCKETS):
        bias = jnp.where((bucket == b)[None], rb[b][:, None, None], bias)
    lo = (qi // CHUNK) * CHUNK
    ok = (kj >= lo) & (kj < lo + WINDOW + CHUNK)
    return jnp.where(ok[None], bias, -jnp.inf)


def _even_mixer(proj, k_hist, v_hist, c0, n0, m0, bias_tab, norm_a, sink, batch, seq_len, rows, hist_valid):
    nsteps = seq_len // rows
    kw = WINDOW + rows
    m0p = jnp.broadcast_to(jnp.pad(m0, ((0, 0), (0, 8 - NH_A)))[:, :, None], (batch, 8, LANES))
    n0p = jnp.pad(n0, ((0, 0), (0, 8 - NH_A), (0, 0)))
    full3 = lambda b, i: (b, 0, 0)
    outs = pl.pallas_call(
        functools.partial(_even_kernel, rows=rows, hist_valid=hist_valid),
        out_shape=(
            jax.ShapeDtypeStruct((batch * seq_len, D_MODEL), BF16),
            jax.ShapeDtypeStruct((batch, NH_A, DK_A, DV_A), F32),
            jax.ShapeDtypeStruct((batch, 8, DK_A), F32),
            jax.ShapeDtypeStruct((batch, 8, LANES), F32),
            jax.ShapeDtypeStruct((batch, WINDOW, NKV_B * DH_B), F32),
            jax.ShapeDtypeStruct((batch, WINDOW, NKV_B * DH_B), F32),
        ),
        grid=(batch, nsteps),
        in_specs=[
            pl.BlockSpec((rows, N_EVEN_COLS), lambda b, i: (b * nsteps + i, 0)),
            pl.BlockSpec((None, WINDOW, NKV_B * DH_B), full3),
            pl.BlockSpec((None, WINDOW, NKV_B * DH_B), full3),
            pl.BlockSpec((None, NH_A, DK_A, DV_A), lambda b, i: (b, 0, 0, 0)),
            pl.BlockSpec((None, 8, DK_A), full3),
            pl.BlockSpec((None, 8, LANES), full3),
            pl.BlockSpec((NH_B, rows, kw), lambda b, i: (0, 0, 0)),
            pl.BlockSpec((NH_A, DV_A), lambda b, i: (0, 0)),
            pl.BlockSpec(memory_space=pltpu.SMEM),
        ],
        out_specs=(
            pl.BlockSpec((rows, D_MODEL), lambda b, i: (b * nsteps + i, 0)),
            pl.BlockSpec((None, NH_A, DK_A, DV_A), lambda b, i: (b, 0, 0, 0)),
            pl.BlockSpec((None, 8, DK_A), full3),
            pl.BlockSpec((None, 8, LANES), full3),
            pl.BlockSpec((None, WINDOW, NKV_B * DH_B), full3),
            pl.BlockSpec((None, WINDOW, NKV_B * DH_B), full3),
        ),
        scratch_shapes=[
            pltpu.VMEM((NH_A, DK_A, DV_A), F32),
            pltpu.VMEM((8, DK_A), F32),
            pltpu.VMEM((8, LANES), F32),
            pltpu.VMEM((WINDOW, NKV_B * DH_B), F32),
            pltpu.VMEM((WINDOW, NKV_B * DH_B), F32),
        ],
        compiler_params=_cparams(("arbitrary", "arbitrary")),
        name="even_mixer",
    )(proj, k_hist.reshape(batch, WINDOW, NKV_B * DH_B), v_hist.reshape(batch, WINDOW, NKV_B * DH_B),
      c0, n0p, m0p, bias_tab, norm_a.reshape(NH_A, DV_A), sink)
    mix, c1, n1, m1, k1, v1 = outs
    return (mix, c1, n1[:, :NH_A, :], m1[:, :NH_A, 0],
            k1.reshape(batch, WINDOW, NKV_B, DH_B), v1.reshape(batch, WINDOW, NKV_B, DH_B))


def _odd_kernel(proj_ref, conv0_ref, s0_ref, convw_ref, hp_ref, normc_ref,
                mix_ref, sout_ref, convout_ref, s_s, xa_s, *, rows):
    R = rows
    i = pl.program_id(1)
    HB = 8

    @pl.when(i == 0)
    def _():
        s_s[...] = s0_ref[...]
        xa_s[HB - (CONV_W - 1):HB, :] = conv0_ref[...]

    xa_s[HB:HB + R, :] = proj_ref[:, O_QKV:O_QKV + QKV_C]
    convout_ref[...] = xa_s[HB + R - (CONV_W - 1):HB + R, :]

    CS = CHUNK
    P2 = 2 * CS
    gcols = proj_ref[:, O_G:O_G + LANES]
    beta_all = _sigmoid(gcols)
    g_all = -jnp.exp(hp_ref[0:1, :]) * _softplus(gcols + hp_ref[1:2, :])
    rr = lax.broadcasted_iota(I32, (R, R), 0)
    cc = lax.broadcasted_iota(I32, (R, R), 1)
    tri_b = jnp.where((rr >= cc) & (rr // CS == cc // CS), 1.0, 0.0).astype(BF16)
    G_all = _dot_exact_lhs(tri_b, g_all)
    G_t = _tr(G_all)
    eG_all = jnp.exp(G_all)
    r2 = lax.broadcasted_iota(I32, (P2, P2), 0)
    c2 = lax.broadcasted_iota(I32, (P2, P2), 1)
    same_head = (r2 // CS) == (c2 // CS)
    incl2 = same_head & (r2 >= c2)
    strict2 = same_head & (r2 > c2)
    eye2 = jnp.where(r2 == c2, 1.0, 0.0)
    lane_lo = lax.broadcasted_iota(I32, (DK_C, P2), 1) < CS
    scale_c = DK_C ** -0.5

    def conv_silu(rs0, c0):
        y = xa_s[HB - 3 + rs0:HB - 3 + rs0 + CS, c0:c0 + LANES] * convw_ref[0:1, c0:c0 + LANES]
        for j in range(1, CONV_W):
            y = y + xa_s[HB - 3 + j + rs0:HB - 3 + j + rs0 + CS, c0:c0 + LANES] * convw_ref[j:j + 1, c0:c0 + LANES]
        return y * _sigmoid(y)

    def l2n(x):
        return x * lax.rsqrt(jnp.sum(x * x, axis=-1, keepdims=True) + RMS_EPS)

    def stack(a, b):
        return jnp.concatenate([a, b], axis=0)

    def inv_unit_lower_all(a_list):
        ts = [eye2 - a for a in a_list]
        ps = [_dot_x3(a, a) for a in a_list]
        nlev = (CS - 1).bit_length() - 1
        for lvl in range(nlev):
            nt, npw = [], []
            for t, p in zip(ts, ps):
                th, tl = _split_bf16(t)
                ph, pl_ = _split_bf16(p)
                if lvl < nlev - 1:
                    lh, ll = stack(th, ph), stack(tl, pl_)
                    tp = _dot(lh, ph) + _dot(ll, ph) + _dot(lh, pl_)
                    nt.append(t + tp[:P2])
                    npw.append(tp[P2:])
                else:
                    nt.append(t + (_dot(th, ph) + _dot(tl, ph) + _dot(th, pl_)))
            ts, ps = nt, npw
        return ts

    probs = [(c0, pr) for c0 in range(0, R, CS) for pr in range(NH_C // 2)]
    pre = []
    for c0, pr in probs:
        rs = slice(c0, c0 + CS)
        ha, hb = 2 * pr, 2 * pr + 1
        ks = [l2n(conv_silu(c0, NH_C * DK_C + h * DK_C)) for h in (ha, hb)]
        qs = [l2n(conv_silu(c0, h * DK_C)) * scale_c for h in (ha, hb)]
        vs = [conv_silu(c0, 2 * NH_C * DK_C + h * DV_C) for h in (ha, hb)]
        k2 = stack(*ks)
        k2b = k2.astype(BF16)
        q2b = stack(*qs).astype(BF16)
        v2 = stack(*vs)
        G_col = stack(G_all[rs, NH_C + ha:NH_C + ha + 1], G_all[rs, NH_C + hb:NH_C + hb + 1])
        G_row = jnp.concatenate([G_t[NH_C + ha:NH_C + ha + 1, rs], G_t[NH_C + hb:NH_C + hb + 1, rs]], axis=1)
        beta = stack(beta_all[rs, ha:ha + 1], beta_all[rs, hb:hb + 1])
        eg = stack(eG_all[rs, NH_C + ha:NH_C + ha + 1], eG_all[rs, NH_C + hb:NH_C + hb + 1])
        GL = [G_all[c0 + CS - 1:c0 + CS, NH_C + h:NH_C + h + 1] for h in (ha, hb)]
        GL2 = stack(jnp.broadcast_to(GL[0], (CS, 1)), jnp.broadcast_to(GL[1], (CS, 1)))
        dmat = jnp.exp(jnp.where(incl2, G_col - G_row, -jnp.inf))
        pre.append(dict(
            a_mat=jnp.where(strict2, beta * dmat * _dot(k2b, k2b, _NT), 0.0),
            rhs=jnp.concatenate([beta * v2, (beta * eg) * k2], axis=1).astype(BF16),
            attn=(_dot(q2b, k2b, _NT) * dmat).astype(BF16),
            kdec_t=_tr(k2 * jnp.exp(GL2 - G_col)).astype(BF16),
            q2b=q2b, eg=eg, egl=[jnp.exp(GL[0]), jnp.exp(GL[1])]))
    tinvs = inv_unit_lower_all([d['a_mat'] for d in pre])
    for d, tinv in zip(pre, tinvs):
        d['uw'] = _dot(tinv.astype(BF16), d['rhs'])

    S_cur = [s_s[h] for h in range(NH_C)]
    npair = NH_C // 2
    for ci, c0 in enumerate(range(0, R, CS)):
        rs = slice(c0, c0 + CS)
        ds = pre[ci * npair:(ci + 1) * npair]
        wqs = []
        for pr, d in enumerate(ds):
            uw, q2b = d['uw'], d['q2b']
            wqs.append([_dot(stack(uw[j * CS:(j + 1) * CS, DV_C:].astype(BF16), q2b[j * CS:(j + 1) * CS]),
                             S_cur[2 * pr + j].astype(BF16)) for j in range(2)])
        dbs = [(d['uw'][:, :DV_C] - stack(wq[0][:CS], wq[1][:CS])).astype(BF16) for d, wq in zip(ds, wqs)]
        for pr, (d, wq, db) in enumerate(zip(ds, wqs, dbs)):
            ha, hb = 2 * pr, 2 * pr + 1
            kdec_t = d['kdec_t']
            zero = jnp.zeros_like(kdec_t)
            S_cur[ha] = d['egl'][0] * S_cur[ha] + _dot(jnp.where(lane_lo, kdec_t, zero), db)
            S_cur[hb] = d['egl'][1] * S_cur[hb] + _dot(jnp.where(lane_lo, zero, kdec_t), db)
        for pr, (d, wq, db) in enumerate(zip(ds, wqs, dbs)):
            o2 = d['eg'] * stack(wq[0][CS:], wq[1][CS:]) + _dot(d['attn'], db)
            for j in range(2):
                h = 2 * pr + j
                o = o2[j * CS:(j + 1) * CS]
                z = proj_ref[rs, O_Z + h * DV_C:O_Z + (h + 1) * DV_C]
                o = o * lax.rsqrt(jnp.mean(o * o, axis=-1, keepdims=True) + RMS_EPS) * normc_ref[...]
                o = o * (z * _sigmoid(z))
                mix_ref[rs, h * DV_C:(h + 1) * DV_C] = o.astype(mix_ref.dtype)
    for h in range(NH_C):
        s_s[h] = S_cur[h]

    xa_s[HB - (CONV_W - 1):HB, :] = xa_s[HB + R - (CONV_W - 1):HB + R, :]
    sout_ref[...] = s_s[...]


def _odd_mixer(proj, conv_hist, s0, conv_w, a_log, dt_bias, norm_c, batch, seq_len, rows):
    nsteps = seq_len // rows
    hp = jnp.zeros((8, LANES), F32)
    hp = hp.at[0, NH_C:2 * NH_C].set(a_log.astype(F32)).at[1, NH_C:2 * NH_C].set(dt_bias.astype(F32))
    outs = pl.pallas_call(
        functools.partial(_odd_kernel, rows=rows),
        out_shape=(
            jax.ShapeDtypeStruct((batch * seq_len, D_MODEL), BF16),
            jax.ShapeDtypeStruct((batch, NH_C, DK_C, DV_C), F32),
            jax.ShapeDtypeStruct((batch, CONV_W - 1, QKV_C), F32),
        ),
        grid=(batch, nsteps),
        in_specs=[
            pl.BlockSpec((rows, N_ODD_COLS), lambda b, i: (b * nsteps + i, 0)),
            pl.BlockSpec((None, CONV_W - 1, QKV_C), lambda b, i: (b, 0, 0)),
            pl.BlockSpec((None, NH_C, DK_C, DV_C), lambda b, i: (b, 0, 0, 0)),
            pl.BlockSpec((CONV_W, QKV_C), lambda b, i: (0, 0)),
            pl.BlockSpec((8, LANES), lambda b, i: (0, 0)),
            pl.BlockSpec((1, DV_C), lambda b, i: (0, 0)),
        ],
        out_specs=(
            pl.BlockSpec((rows, D_MODEL), lambda b, i: (b * nsteps + i, 0)),
            pl.BlockSpec((None, NH_C, DK_C, DV_C), lambda b, i: (b, 0, 0, 0)),
            pl.BlockSpec((None, CONV_W - 1, QKV_C), lambda b, i: (b, 0, 0)),
        ),
        scratch_shapes=[
            pltpu.VMEM((NH_C, DK_C, DV_C), F32),
            pltpu.VMEM((8 + rows, QKV_C), F32),
        ],
        compiler_params=_cparams(("arbitrary", "arbitrary")),
        name="odd_mixer",
    )(proj, conv_hist, s0, conv_w, hp, norm_c.reshape(1, DV_C))
    return outs


def _post_kernel(mix_ref, x_ref, gate_ref, sc_ref, sh_ref, wout_ref, lng_ref, lnb_ref, wr_ref, br_ref,
                 x1_ref, h2_ref, ri_ref, rg_ref, seg_ref, cnt_ref, carry_s, *, tm):
    i = pl.program_id(0)

    @pl.when(i == 0)
    def _():
        carry_s[...] = jnp.zeros_like(carry_s)

    y = _dot(mix_ref[...], wout_ref[...])
    x1 = _layer_norm_rows(ALPHA_DN * x_ref[...] + (1.0 + gate_ref[...]) * y, lng_ref[...], lnb_ref[...])
    x1_ref[...] = x1
    h2 = x1 * (1.0 + sc_ref[...]) + sh_ref[...]
    h2_ref[...] = h2
    lt = _dot(wr_ref[...].astype(BF16), h2.astype(BF16), _NT) + br_ref[:, 0:1]
    e_iota = lax.broadcasted_iota(I32, (N_EXPERTS, tm), 0).astype(F32)
    vals, idxs = [], []
    for _ in range(TOP_K):
        mx = jnp.max(lt, axis=0, keepdims=True)
        idx = jnp.min(jnp.where(lt == mx, e_iota, float(N_EXPERTS)), axis=0, keepdims=True)
        vals.append(mx)
        idxs.append(idx)
        lt = jnp.where(e_iota == idx, -jnp.inf, lt)
    ex = [jnp.exp(v - vals[0]) for v in vals]
    tot = ex[0] + ex[1] + ex[2] + ex[3]
    hot = [jnp.where(e_iota == idx, 1.0, 0.0) for idx in idxs]
    m_all = hot[0] + hot[1] + hot[2] + hot[3]
    m_all_b = m_all.astype(BF16)
    r = lax.broadcasted_iota(I32, (tm, tm), 0)
    c = lax.broadcasted_iota(I32, (tm, tm), 1)
    upper = jnp.where(r < c, 1.0, 0.0).astype(BF16)
    prefix = _dot(m_all_b, upper)
    re = lax.broadcasted_iota(I32, (N_EXPERTS, N_EXPERTS), 0)
    ce = lax.broadcasted_iota(I32, (N_EXPERTS, N_EXPERTS), 1)
    cnt_col = jnp.sum(m_all, axis=1, keepdims=True)
    m8_col = jnp.floor((cnt_col + 7.0) * 0.125)
    lower_e = jnp.where(re > ce, 1.0, 0.0).astype(BF16)
    off8_col = 8.0 * _dot(lower_e, jnp.broadcast_to(m8_col, (N_EXPERTS, LANES)).astype(BF16))[:, 0:1]
    base = off8_col + prefix
    dests = [jnp.sum(hk * base, axis=0, keepdims=True) for hk in hot]
    cnt_row = _dot(jnp.ones((8, tm), BF16), m_all_b, _NT)
    m8_row = jnp.floor((cnt_row + 7.0) * 0.125)
    upper_e = jnp.where(re < ce, 1.0, 0.0).astype(BF16)
    off8_row = 8.0 * _dot(m8_row.astype(BF16), upper_e)
    gc = carry_s[:, 0:N_EXPERTS]
    srow = lax.broadcasted_iota(I32, (8, N_EXPERTS), 0)
    seg = jnp.where(srow == 0, 8.0 * m8_row, jnp.where(srow == 1, off8_row, jnp.where(srow == 2, gc, 0.0)))
    seg_ref[...] = jnp.concatenate([seg, jnp.zeros((8, LANES - N_EXPERTS), F32)], axis=1).astype(I32)
    carry_s[:, 0:N_EXPERTS] = gc + 8.0 * m8_row
    cnt_ref[...] = carry_s[...]
    ri_ref[...] = jnp.concatenate(dests + [jnp.zeros((4, tm), F32)], axis=0).astype(I32)
    rg_ref[...] = jnp.concatenate([e / tot for e in ex] + [jnp.zeros((4, tm), F32)], axis=0)


def _post(mix, x2d, gate, scale, shift, w_out_bf16, ln_g, ln_b, wr_t, br, seq_len, tm):
    t = x2d.shape[0]
    per = seq_len // tm
    vec = lambda i: (i // per, 0, 0)
    const2 = lambda i: (0, 0)
    return pl.pallas_call(
        functools.partial(_post_kernel, tm=tm),
        out_shape=(
            jax.ShapeDtypeStruct((t, D_MODEL), F32),
            jax.ShapeDtypeStruct((t, D_MODEL), F32),
            jax.ShapeDtypeStruct((t // tm, 8, tm), I32),
            jax.ShapeDtypeStruct((t // tm, 8, tm), F32),
            jax.ShapeDtypeStruct((t // tm, 8, LANES), I32),
            jax.ShapeDtypeStruct((8, LANES), F32),
        ),
        grid=(t // tm,),
        in_specs=[
            pl.BlockSpec((tm, D_MODEL), lambda i: (i, 0)),
            pl.BlockSpec((tm, D_MODEL), lambda i: (i, 0)),
            pl.BlockSpec((None, 1, D_MODEL), vec),
            pl.BlockSpec((None, 1, D_MODEL), vec),
            pl.BlockSpec((None, 1, D_MODEL), vec),
            pl.BlockSpec((D_MODEL, D_MODEL), const2),
            pl.BlockSpec((1, D_MODEL), const2),
            pl.BlockSpec((1, D_MODEL), const2),
            pl.BlockSpec((N_EXPERTS, D_MODEL), const2),
            pl.BlockSpec((N_EXPERTS, LANES), const2),
        ],
        out_specs=(
            pl.BlockSpec((tm, D_MODEL), lambda i: (i, 0)),
            pl.BlockSpec((tm, D_MODEL), lambda i: (i, 0)),
            pl.BlockSpec((None, 8, tm), lambda i: (i, 0, 0)),
            pl.BlockSpec((None, 8, tm), lambda i: (i, 0, 0)),
            pl.BlockSpec((None, 8, LANES), lambda i: (i, 0, 0)),
            pl.BlockSpec((8, LANES), const2),
        ),
        scratch_shapes=[pltpu.VMEM((8, LANES), F32)],
        compiler_params=_cparams(("arbitrary",)),
        name="post_router",
    )(mix, x2d, gate, scale, shift, w_out_bf16, ln_g, ln_b, wr_t, br)


SEG_FIELDS = 3


def _run_sizes(tm):
    top = 1 << (tm - 1).bit_length()
    return [s for s in (512, 256, 128, 64, 32, 16, 8) if s <= max(top, 8)]


def _for_each_run_piece(seg_ref, rstart_ref, tile, tm, fn):
    base = tile * (SEG_FIELDS * N_EXPERTS)

    def body(e, c):
        n8 = seg_ref[base + e]
        src = seg_ref[base + N_EXPERTS + e]
        dst = rstart_ref[e] + seg_ref[base + 2 * N_EXPERTS + e]
        done = jnp.int32(0)
        for size in _run_sizes(tm):
            @pl.when((n8 & size) != 0)
            def _(done=done, size=size):
                fn(pl.multiple_of(src + done, 8), pl.multiple_of(dst + done, 8), size)
            done = done + (n8 & size)
        return c

    lax.fori_loop(0, N_EXPERTS, body, 0)


def _dispatch_kernel(zblk_ref, rstart_ref, seg_ref, ri_ref, h_ref, xs_ref, zero_s, cbuf, sem, zsem,
                     *, tm, bm, cb, ntiles):
    i = pl.program_id(0)
    slot = i % 2

    def run_copies(tile, sl, wait):
        def piece(src, dst, size):
            cp = pltpu.make_async_copy(cbuf.at[sl, pl.ds(src, size), :], xs_ref.at[pl.ds(dst, size), :], sem.at[sl])
            if wait:
                cp.wait()
            else:
                cp.start()
        _for_each_run_piece(seg_ref, rstart_ref, tile, tm, piece)

    def zero_copy(e):
        return pltpu.make_async_copy(zero_s, xs_ref.at[pl.ds(zblk_ref[e] * bm, bm), :], zsem)

    @pl.when(i == 0)
    def _():
        zero_s[...] = jnp.zeros_like(zero_s)

        def zstart(e, c):
            @pl.when(zblk_ref[e] >= 0)
            def _():
                zero_copy(e).start()
            return c

        def zwait(e, c):
            @pl.when(zblk_ref[e] >= 0)
            def _():
                zero_copy(e).wait()
            return c

        lax.fori_loop(0, N_EXPERTS, zstart, 0)
        lax.fori_loop(0, N_EXPERTS, zwait, 0)

    @pl.when(i >= 2)
    def _():
        run_copies(i - 2, slot, True)

    rows = lax.broadcasted_iota(I32, (cb, tm), 0)
    hit = rows == ri_ref[0:1, :]
    for k in range(1, TOP_K):
        hit = hit | (rows == ri_ref[k:k + 1, :])
    perm = jnp.where(hit, 1.0, 0.0).astype(BF16)
    cbuf[slot] = _dot(perm, h_ref[...].astype(BF16))
    run_copies(i, slot, False)

    @pl.when(i == ntiles - 1)
    def _():
        if ntiles > 1:
            run_copies(i - 1, 1 - slot, True)
        run_copies(i, slot, True)


def _dispatch(h2, route_i, seg_flat, row_start, zblk, n_rows, tm, bm):
    t = h2.shape[0]
    ntiles = t // tm
    cb = TOP_K * tm + 8 * N_EXPERTS
    return pl.pallas_call(
        functools.partial(_dispatch_kernel, tm=tm, bm=bm, cb=cb, ntiles=ntiles),
        out_shape=jax.ShapeDtypeStruct((n_rows, D_MODEL), F32),
        grid_spec=pltpu.PrefetchScalarGridSpec(
            num_scalar_prefetch=3,
            grid=(ntiles,),
            in_specs=[
                pl.BlockSpec((None, 2 * TOP_K, tm), lambda i, z, r, s: (i, 0, 0)),
                pl.BlockSpec((tm, D_MODEL), lambda i, z, r, s: (i, 0)),
            ],
            out_specs=pl.BlockSpec(memory_space=pl.ANY),
            scratch_shapes=[
                pltpu.VMEM((bm, D_MODEL), F32),
                pltpu.VMEM((2, cb, D_MODEL), F32),
                pltpu.SemaphoreType.DMA((2,)),
                pltpu.SemaphoreType.DMA,
            ],
        ),
        compiler_params=_cparams(("arbitrary",)),
        name="dispatch",
    )(zblk, row_start, seg_flat, route_i, h2)


def _ffn_kernel(be_ref, nv_ref, x_ref, w1_ref, b1_ref, w2_ref, b2_ref, y_ref, w1_s, w2_s):
    b = pl.program_id(0)

    @pl.when(b < nv_ref[0])
    def _():
        @pl.when((b == 0) | (be_ref[b] != be_ref[jnp.maximum(b - 1, 0)]))
        def _():
            step = 256
            for r0 in range(0, D_MODEL, step):
                w1_s[r0:r0 + step, :] = w1_ref[r0:r0 + step, :].astype(BF16)
            for r0 in range(0, D_FF, step):
                w2_s[r0:r0 + step, :] = w2_ref[r0:r0 + step, :].astype(BF16)

        xb = x_ref[...].astype(BF16)
        glu = _dot(xb, w1_s[:, :D_FF]) + b1_ref[:, :D_FF]
        lin = _dot(xb, w1_s[:, D_FF:]) + b1_ref[:, D_FF:]
        glu = jnp.minimum(glu, SWIGLU_LIMIT)
        lin = jnp.clip(lin, -SWIGLU_LIMIT, SWIGLU_LIMIT)
        act = glu * _sigmoid(SWIGLU_ALPHA * glu) * (lin + 1.0)
        y_ref[...] = _dot(act.astype(BF16), w2_s[...]) + b2_ref[...]


def _ffn(xs, block_e, n_valid, layer, w1, b1, w2, b2, bm):
    nb = xs.shape[0] // bm
    row = lambda b, be, nv: (jnp.minimum(b, nv[0] - 1), 0)
    wsel = lambda b, be, nv: (layer, be[b], 0, 0)
    return pl.pallas_call(
        _ffn_kernel,
        out_shape=jax.ShapeDtypeStruct(xs.shape, F32),
        grid_spec=pltpu.PrefetchScalarGridSpec(
            num_scalar_prefetch=2,
            grid=(nb,),
            in_specs=[
                pl.BlockSpec((bm, D_MODEL), row),
                pl.BlockSpec((None, None, D_MODEL, 2 * D_FF), wsel),
                pl.BlockSpec((None, None, 1, 2 * D_FF), wsel),
                pl.BlockSpec((None, None, D_FF, D_MODEL), wsel),
                pl.BlockSpec((None, None, 1, D_MODEL), wsel),
            ],
            out_specs=pl.BlockSpec((bm, D_MODEL), row),
            scratch_shapes=[pltpu.VMEM((D_MODEL, 2 * D_FF), BF16), pltpu.VMEM((D_FF, D_MODEL), BF16)],
        ),
        compiler_params=_cparams(("arbitrary",)),
        name="expert_ffn",
    )(block_e, n_valid, xs, w1, b1, w2, b2)


def _combine_kernel(rstart_ref, seg_ref, ri_ref, x_ref, gate_ref, rg_ref, lng_ref, lnb_ref, ys_ref, o_ref,
                    ybuf, yb_s, wt_s, sem, *, tm, cb, ntiles):
    i = pl.program_id(0)
    slot = i % 2

    def run_copies(tile, sl, wait):
        def piece(loc, glob, size):
            cp = pltpu.make_async_copy(ys_ref.at[pl.ds(glob, size), :], ybuf.at[sl, pl.ds(loc, size), :], sem.at[sl])
            if wait:
                cp.wait()
            else:
                cp.start()
        _for_each_run_piece(seg_ref, rstart_ref, tile, tm, piece)

    @pl.when(i == 0)
    def _():
        ybuf[...] = jnp.zeros_like(ybuf)
        run_copies(0, 0, False)

    @pl.when(i + 1 < ntiles)
    def _():
        run_copies(i + 1, 1 - slot, False)

    dest_t = _tr(jnp.concatenate([ri_ref[...].astype(F32), jnp.zeros((LANES - 8, tm), F32)], axis=0))
    g_t = _tr(jnp.concatenate([rg_ref[...], jnp.zeros((LANES - 8, tm), F32)], axis=0))
    lane_step = 256
    for c0 in range(0, cb, lane_step):
        cols = (lax.broadcasted_iota(I32, (tm, lane_step), 1) + c0).astype(F32)
        w = jnp.where(cols == dest_t[:, 0:1], g_t[:, 0:1], 0.0)
        for k in range(1, TOP_K):
            w = w + jnp.where(cols == dest_t[:, k:k + 1], g_t[:, k:k + 1], 0.0)
        wt_s[:, c0:c0 + lane_step] = w.astype(BF16)

    run_copies(i, slot, True)
    row_step = 256
    for r0 in range(0, cb, row_step):
        yb_s[r0:r0 + row_step, :] = ybuf[slot, r0:r0 + row_step, :].astype(BF16)
    moe = _dot(wt_s[...], yb_s[...])
    o_ref[...] = _layer_norm_rows(ALPHA_DN * x_ref[...] + (1.0 + gate_ref[...]) * moe,
                                  lng_ref[...], lnb_ref[...])


def _combine(row_start, seg_flat, route_i, x1, gate, route_g, ln_g, ln_b, ys, seq_len, tm):
    t = x1.shape[0]
    per = seq_len // tm
    ntiles = t // tm
    cb = TOP_K * tm + 8 * N_EXPERTS
    return pl.pallas_call(
        functools.partial(_combine_kernel, tm=tm, cb=cb, ntiles=ntiles),
        out_shape=jax.ShapeDtypeStruct((t, D_MODEL), F32),
        grid_spec=pltpu.PrefetchScalarGridSpec(
            num_scalar_prefetch=2,
            grid=(ntiles,),
            in_specs=[
                pl.BlockSpec((None, 2 * TOP_K, tm), lambda i, r, s: (i, 0, 0)),
                pl.BlockSpec((tm, D_MODEL), lambda i, r, s: (i, 0)),
                pl.BlockSpec((None, 1, D_MODEL), lambda i, r, s: (i // per, 0, 0)),
                pl.BlockSpec((None, 8, tm), lambda i, r, s: (i, 0, 0)),
                pl.BlockSpec((1, D_MODEL), lambda i, r, s: (0, 0)),
                pl.BlockSpec((1, D_MODEL), lambda i, r, s: (0, 0)),
                pl.BlockSpec(memory_space=pl.ANY),
            ],
            out_specs=pl.BlockSpec((tm, D_MODEL), lambda i, r, s: (i, 0)),
            scratch_shapes=[
                pltpu.VMEM((2, cb, D_MODEL), F32),
                pltpu.VMEM((cb, D_MODEL), BF16),
                pltpu.VMEM((tm, cb), BF16),
                pltpu.SemaphoreType.DMA((2,)),
            ],
        ),
        compiler_params=_cparams(("arbitrary",)),
        name="combine",
    )(row_start, seg_flat, route_i, x1, gate, route_g, ln_g, ln_b, ys)


def _moe_tables(counts, bm, n_blocks):
    cnt = counts[0, :N_EXPERTS].astype(I32)
    nblk = (cnt + bm - 1) // bm
    blk_end = jnp.cumsum(nblk)
    row_start = (blk_end - nblk) * bm
    n_valid = blk_end[-1]
    bidx = jnp.minimum(jnp.arange(n_blocks, dtype=I32), n_valid - 1)
    block_e = jnp.sum((blk_end[None, :] <= bidx[:, None]).astype(I32), axis=1)
    block_e = jnp.minimum(block_e, N_EXPERTS - 1)
    zblk = jnp.where((cnt % bm) != 0, blk_end - 1, -1).astype(I32)
    return row_start.astype(I32), block_e, n_valid.reshape(1).astype(I32), zblk


def _moe_and_norm(h2, x1, route_i, route_g, seg, counts, gate, ln_g, ln_b, layer, w1, b1, w2, b2, seq_len, tm, bm):
    t = h2.shape[0]
    ntiles = t // tm
    n_blocks = -(-(t * TOP_K + 7 * N_EXPERTS * ntiles) // bm) + N_EXPERTS
    row_start, block_e, n_valid, zblk = _moe_tables(counts, bm, n_blocks)
    seg_flat = seg[:, :SEG_FIELDS, :N_EXPERTS].reshape(ntiles * SEG_FIELDS * N_EXPERTS)
    xs = _dispatch(h2, route_i, seg_flat, row_start, zblk, n_blocks * bm, tm, bm)
    ys = _ffn(xs, block_e, n_valid, layer, w1, b1, w2, b2, bm)
    return _combine(row_start, seg_flat, route_i, x1, gate, route_g, ln_g, ln_b, ys, seq_len, tm)


def _group_cfg(batch, seq_len):
    if seq_len >= 512:
        return dict(tm=512, rows=256, bm=256)
    return dict(tm=seq_len, rows=seq_len, bm=128)


def kernel(x_prompt, x_sample, c_prompt, c_sample, state_a_C, state_a_n, state_a_m, cache_b_k, cache_b_v,
           state_c_S, state_c_conv, w_ada, b_ada, ln_g, ln_b, w_in_even, b_in_even, norm_a, sink_b, rel_bias,
           w_out_even, w_in_odd, conv_c, a_log_c, dt_bias_c, norm_c, w_out_odd, w_router, b_router,
           w_e1, b_e1, w_e2, b_e2):
    bp, lp, _ = x_prompt.shape
    bs, ls, _ = x_sample.shape
    groups = [dict(b=bp, l=lp, x=x_prompt.reshape(bp * lp, D_MODEL), **_group_cfg(bp, lp)),
              dict(b=bs, l=ls, x=x_sample.reshape(bs * ls, D_MODEL), **_group_cfg(bs, ls))]
    mod = _ada(jnp.concatenate([c_prompt, c_sample], axis=0), w_ada, b_ada)
    offs = [0, bp]
    states = [dict(), dict()]
    b1_all = b_e1.reshape(DEPTH, N_EXPERTS, 1, 2 * D_FF)
    b2_all = b_e2.reshape(DEPTH, N_EXPERTS, 1, D_MODEL)
    for l in range(DEPTH):
        e = l // 2
        if l % 2 == 0:
            w = w_in_even[e]
            sz = (512, 512, 512, 512, 4, 4, 512, 128, 128)
            o = [sum(sz[:j]) for j in range(len(sz) + 1)]
            pad = jnp.zeros((D_MODEL, LANES - 2 * NH_A), w.dtype)
            w_in = jnp.concatenate([w[:, o[0]:o[4]], w[:, o[6]:o[9]], w[:, o[4]:o[6]], pad], axis=1).astype(BF16)
            bb = b_in_even[e]
            b_in = jnp.concatenate([bb[o[0]:o[4]], bb[o[6]:o[9]], bb[o[4]:o[6]],
                                    jnp.zeros((LANES - 2 * NH_A,), bb.dtype)]).reshape(1, N_EVEN_COLS)
            w_out = w_out_even[e].astype(BF16)
        else:
            w = w_in_odd[e]
            pad = jnp.zeros((D_MODEL, LANES - 2 * NH_C), w.dtype)
            w_in = jnp.concatenate([w, pad], axis=1).astype(BF16)
            b_in = jnp.zeros((1, N_ODD_COLS), F32)
            w_out = w_out_odd[e].astype(BF16)
        wr_t = w_router[l].T
        br = jnp.broadcast_to(b_router[l][:, None], (N_EXPERTS, LANES))
        for gi, gr in enumerate(groups):
            nb, sl = gr['b'], gr['l']
            m = mod[l, offs[gi]:offs[gi] + nb].reshape(nb, 6, 1, D_MODEL)
            m6 = [m[:, j] for j in range(6)]
            proj = _inproj(gr['x'], m6[1], m6[0], w_in, b_in, sl, gr['tm'])
            st = states[gi]
            if l % 2 == 0:
                if gi == 0:
                    kh = jnp.zeros((nb, WINDOW, NKV_B, DH_B), F32)
                    vh = kh
                    c0 = jnp.zeros((nb, NH_A, DK_A, DV_A), F32)
                    n0 = jnp.zeros((nb, NH_A, DK_A), F32)
                    m0 = jnp.zeros((nb, NH_A), F32)
                else:
                    kh, vh, c0, n0, m0 = cache_b_k[e], cache_b_v[e], state_a_C[e], state_a_n[e], state_a_m[e]
                bias_tab = _swa_bias_table(rel_bias, gr['rows'])
                mix, c1, n1, m1, k1, v1 = _even_mixer(proj, kh, vh, c0, n0, m0, bias_tab, norm_a[e], sink_b[e],
                                                      nb, sl, gr['rows'], gi == 1)
                for name, val in (('a_C', c1), ('a_n', n1), ('a_m', m1), ('b_k', k1), ('b_v', v1)):
                    st.setdefault(name, []).append(val)
            else:
                if gi == 0:
                    cv0 = jnp.zeros((nb, CONV_W - 1, QKV_C), F32)
                    s0 = jnp.zeros((nb, NH_C, DK_C, DV_C), F32)
                else:
                    cv0, s0 = state_c_conv[e], state_c_S[e]
                mix, s1, cv1 = _odd_mixer(proj, cv0, s0, conv_c[e], a_log_c[e], dt_bias_c[e], norm_c[e], nb, sl,
                                          gr['rows'])
                st.setdefault('c_S', []).append(s1)
                st.setdefault('c_conv', []).append(cv1)
            x1, h2, route_i, route_g, seg, counts = _post(
                mix, gr['x'], m6[2], m6[4], m6[3], w_out, ln_g[l, 0].reshape(1, D_MODEL),
                ln_b[l, 0].reshape(1, D_MODEL), wr_t, br, sl, gr['tm'])
            gr['x'] = _moe_and_norm(h2, x1, route_i, route_g, seg, counts, m6[5], ln_g[l, 1].reshape(1, D_MODEL),
                                    ln_b[l, 1].reshape(1, D_MODEL), l, w_e1, b1_all, w_e2, b2_all,
                                    sl, gr['tm'], gr['bm'])
    outs = [groups[0]['x'].reshape(bp, lp, D_MODEL), groups[1]['x'].reshape(bs, ls, D_MODEL)]
    for gi in range(2):
        for name in ('a_C', 'a_n', 'a_m', 'b_k', 'b_v', 'c_S', 'c_conv'):
            outs.append(jnp.stack(states[gi][name]))
    return tuple(outs)
```

```python
import functools
import math

import jax
import jax.numpy as jnp
from jax import lax
from jax.experimental import pallas as pl
from jax.experimental.pallas import tpu as pltpu

F32 = jnp.float32
BF16 = jnp.bfloat16
I32 = jnp.int32

D_MODEL = 1024
CHUNK = 64
NH_A, DK_A, DV_A = 4, 128, 128
W_A = NH_A * DV_A
NH_B, NKV_B, DH_B, WINDOW = 8, 2, 64, 128
G_B = NH_B // NKV_B
W_B = NH_B * DH_B
NUM_BUCKETS, MAX_DISTANCE = 32, 256
NH_C, DK_C, DV_C, CONV_W = 8, 128, 128, 4
QKV_C = NH_C * (2 * DK_C + DV_C)
N_EXPERTS, TOP_K, D_FF = 32, 4, 1024
SWIGLU_LIMIT, SWIGLU_ALPHA = 7.0, 1.702
DEPTH = 2
ALPHA_DN = (2 * DEPTH) ** 0.25
LN_EPS = 1e-5
RMS_EPS = 1e-6

LANES = 128
E_QA, E_KA, E_VA, E_OA, E_QB, E_KB, E_VB, E_G = 0, 512, 1024, 1536, 2048, 2560, 2688, 2816
N_EVEN_COLS = 2944
O_QKV, O_Z, O_G = 0, 3072, 4096
N_ODD_COLS = 4224

VMEM_LIMIT = 56 * 1024 * 1024


def _cparams(sem):
    return pltpu.CompilerParams(dimension_semantics=sem, vmem_limit_bytes=VMEM_LIMIT)


def _softplus(x):
    return jnp.maximum(x, 0.0) + jnp.log(1.0 + jnp.exp(-jnp.abs(x)))


def _sigmoid(x):
    return 1.0 / (1.0 + jnp.exp(-x))


def _split_bf16(a):
    hi = a.astype(BF16)
    lo = (a - hi.astype(F32)).astype(BF16)
    return hi, lo


_NN = (((1,), (0,)), ((), ()))
_NT = (((1,), (1,)), ((), ()))
_TN = (((0,), (0,)), ((), ()))


def _dot(a, b, dims=_NN):
    return lax.dot_general(a, b, dims, preferred_element_type=F32)


def _dot_x3(a, b, dims=_NN):
    ah, al = _split_bf16(a)
    bh, bl = _split_bf16(b)
    return _dot(ah, bh, dims) + _dot(ah, bl, dims) + _dot(al, bh, dims)


def _dot_exact_lhs(a_bf16, b, dims=_NN):
    b0 = b.astype(BF16)
    r1 = b - b0.astype(F32)
    b1 = r1.astype(BF16)
    b2 = (r1 - b1.astype(F32)).astype(BF16)
    return _dot(a_bf16, b0, dims) + _dot(a_bf16, b1, dims) + _dot(a_bf16, b2, dims)


def _tr(x):
    r = x.shape[0]
    rp = -(-r // LANES) * LANES
    if rp != r:
        x = jnp.concatenate([x, jnp.zeros((rp - r, x.shape[1]), x.dtype)], axis=0)
    return x.T[:, :r]


def _tri(n, strict=False):
    r = lax.broadcasted_iota(I32, (n, n), 0)
    c = lax.broadcasted_iota(I32, (n, n), 1)
    return (r > c) if strict else (r >= c)


def _layer_norm_rows(v, g, b):
    mu = jnp.mean(v, axis=-1, keepdims=True)
    d = v - mu
    var = jnp.mean(d * d, axis=-1, keepdims=True)
    return d * lax.rsqrt(var + LN_EPS) * g + b


def _ada_kernel(c_ref, w_ref, b_ref, o_ref):
    c = c_ref[...]
    a = (c * _sigmoid(c)).astype(BF16)
    o_ref[...] = _dot(a, w_ref[...].astype(BF16)) + b_ref[...]


def _ada(c_all, w_ada, b_ada):
    nb = c_all.shape[0]
    tn = 1536
    return pl.pallas_call(
        _ada_kernel,
        out_shape=jax.ShapeDtypeStruct((DEPTH, nb, 6 * D_MODEL), F32),
        grid=(DEPTH, 6 * D_MODEL // tn),
        in_specs=[
            pl.BlockSpec((nb, D_MODEL), lambda l, j: (0, 0)),
            pl.BlockSpec((None, D_MODEL, tn), lambda l, j: (l, 0, j)),
            pl.BlockSpec((None, 1, tn), lambda l, j: (l, 0, j)),
        ],
        out_specs=pl.BlockSpec((None, nb, tn), lambda l, j: (l, 0, j)),
        compiler_params=_cparams(("arbitrary", "arbitrary")),
        name="ada",
    )(c_all, w_ada, b_ada.reshape(DEPTH, 1, 6 * D_MODEL))


def _inproj_kernel(x_ref, sc_ref, sh_ref, w_ref, b_ref, o_ref, *, n_cols, col_step):
    h = (x_ref[...] * (1.0 + sc_ref[...]) + sh_ref[...]).astype(BF16)
    for c0 in range(0, n_cols, col_step):
        c1 = min(c0 + col_step, n_cols)
        o_ref[:, c0:c1] = _dot(h, w_ref[:, c0:c1]) + b_ref[:, c0:c1]


def _inproj(x2d, scale, shift, w_bf16, bias, seq_len, tm):
    t = x2d.shape[0]
    n = w_bf16.shape[1]
    per = seq_len // tm
    return pl.pallas_call(
        functools.partial(_inproj_kernel, n_cols=n, col_step=1024),
        out_shape=jax.ShapeDtypeStruct((t, n), F32),
        grid=(t // tm,),
        in_specs=[
            pl.BlockSpec((tm, D_MODEL), lambda i: (i, 0)),
            pl.BlockSpec((None, 1, D_MODEL), lambda i: (i // per, 0, 0)),
            pl.BlockSpec((None, 1, D_MODEL), lambda i: (i // per, 0, 0)),
            pl.BlockSpec((D_MODEL, n), lambda i: (0, 0)),
            pl.BlockSpec((1, n), lambda i: (0, 0)),
        ],
        out_specs=pl.BlockSpec((tm, n), lambda i: (i, 0)),
        compiler_params=_cparams(("arbitrary",)),
        name="inproj",
    )(x2d, scale, shift, w_bf16, bias)


def _even_kernel(proj_ref, kh0_ref, vh0_ref, c0_ref, n0_ref, m0_ref, bias_ref, norma_ref, sink_ref,
                 mix_ref, cout_ref, nout_ref, mout_ref, kout_ref, vout_ref,
                 c_s, n_s, m_s, kh_s, vh_s, *, rows, hist_valid):
    R = rows
    KW = WINDOW + R
    i = pl.program_id(1)

    @pl.when(i == 0)
    def _():
        c_s[...] = c0_ref[...]
        n_s[...] = n0_ref[...]
        m_s[...] = m0_ref[...]
        kh_s[...] = kh0_ref[...]
        vh_s[...] = vh0_ref[...]

    CS = CHUNK
    g = proj_ref[:, E_G:E_G + LANES]
    lf = -_softplus(-g)
    rr = lax.broadcasted_iota(I32, (R, R), 0)
    cc = lax.broadcasted_iota(I32, (R, R), 1)
    tri_b = jnp.where((rr >= cc) & (rr // CS == cc // CS), 1.0, 0.0).astype(BF16)
    b_all = _dot_exact_lhs(tri_b, lf)
    g_t = _tr(g)
    b_t = _tr(b_all)
    causal = _tri(CS)
    scale_a = DK_A ** -0.5
    chunks = list(range(0, R, CS))
    pb = {}
    for c0 in chunks:
        rs = slice(c0, c0 + CS)
        for h in range(NH_A):
            b_col = b_all[rs, NH_A + h:NH_A + h + 1]
            b_row = b_t[NH_A + h:NH_A + h + 1, rs]
            logw = jnp.where(causal, b_col - b_row + g_t[h:h + 1, rs], -jnp.inf)
            pb[c0, h] = dict(b_col=b_col, logw=logw, lmax=jnp.max(logw, axis=-1, keepdims=True))
    for h in range(NH_A):
        m_prev = m_s[h:h + 1, 0:1]
        for c0 in chunks:
            d = pb[c0, h]
            d['m_prev'] = m_prev
            d['m_inter'] = d['b_col'] + m_prev
            d['m_t'] = jnp.maximum(d['m_inter'], d['lmax'])
            m_prev = d['m_t'][CS - 1:CS, :]
            d['m_new'] = m_prev
        m_s[h:h + 1, :] = jnp.broadcast_to(m_prev, (1, LANES))
    for c0 in chunks:
        rs = slice(c0, c0 + CS)
        for h in range(NH_A):
            d = pb[c0, h]
            q = proj_ref[rs, E_QA + h * DK_A:E_QA + (h + 1) * DK_A] * scale_a
            k = proj_ref[rs, E_KA + h * DK_A:E_KA + (h + 1) * DK_A]
            qb = q.astype(BF16)
            vb = proj_ref[rs, E_VA + h * DV_A:E_VA + (h + 1) * DV_A].astype(BF16)
            s = _dot(qb, k.astype(BF16), _NT) * jnp.exp(d['logw'] - d['m_t'])
            b_last = d['b_col'][CS - 1:CS, :]
            kw = k * jnp.exp(b_last - d['b_col'] + g[rs, h:h + 1] - d['m_new'])
            d.update(qb=qb, dec=jnp.exp(d['m_inter'] - d['m_t']), sv=_dot(s.astype(BF16), vb),
                     ssum=jnp.sum(s, axis=-1, keepdims=True), kv=_dot(_tr(kw).astype(BF16), vb),
                     ksum=jnp.sum(kw, axis=0, keepdims=True), carry=jnp.exp(b_last + d['m_prev'] - d['m_new']))
    for h in range(NH_A):
        n_prev = n_s[h:h + 1, :]
        c_prev = c_s[h]
        for c0 in chunks:
            d = pb[c0, h]
            d['c_prev'], d['n_prev'] = c_prev, n_prev
            c_prev = d['carry'] * c_prev + d['kv']
            n_prev = d['carry'] * n_prev + d['ksum']
        c_s[h] = c_prev
        n_s[h:h + 1, :] = n_prev
    for c0 in chunks:
        rs = slice(c0, c0 + CS)
        for h in range(NH_A):
            d = pb[c0, h]
            qb = d['qb']
            num = d['dec'] * _dot(qb, d['c_prev'].astype(BF16)) + d['sv']
            qn = jnp.sum(qb.astype(F32) * d['n_prev'].astype(BF16).astype(F32), axis=-1, keepdims=True)
            den = d['dec'] * qn + d['ssum']
            hh = num / jnp.maximum(jnp.abs(den), jnp.exp(-d['m_t']))
            og = proj_ref[rs, E_OA + h * DV_A:E_OA + (h + 1) * DV_A]
            mu = jnp.mean(hh, axis=-1, keepdims=True)
            dd = hh - mu
            var = jnp.mean(dd * dd, axis=-1, keepdims=True)
            ha = dd * lax.rsqrt(var + LN_EPS) * norma_ref[h:h + 1, :] * _sigmoid(og)
            mix_ref[rs, h * DV_A:(h + 1) * DV_A] = ha.astype(mix_ref.dtype)

    cout_ref[...] = c_s[...]
    nout_ref[...] = n_s[...]
    mout_ref[...] = m_s[...]

    k_win = jnp.concatenate([kh_s[...], proj_ref[:, E_KB:E_KB + NKV_B * DH_B]], axis=0)
    v_win = jnp.concatenate([vh_s[...], proj_ref[:, E_VB:E_VB + NKV_B * DH_B]], axis=0)
    k_win_b = k_win.astype(BF16)
    v_win_b = v_win.astype(BF16)
    if not hist_valid:
        key_pos = lax.broadcasted_iota(I32, (R, KW), 1) + (i * R - WINDOW)
        key_ok = key_pos >= 0
    scale_b = DH_B ** -0.5
    kgs = [k_win_b[:, kv * DH_B:(kv + 1) * DH_B] for kv in range(NKV_B)]
    vgs = [v_win_b[:, kv * DH_B:(kv + 1) * DH_B] for kv in range(NKV_B)]
    scores = []
    for hd in range(NH_B):
        qh = proj_ref[:, E_QB + hd * DH_B:E_QB + (hd + 1) * DH_B].astype(BF16)
        s = _dot(qh, kgs[hd // G_B], _NT) * scale_b + bias_ref[hd]
        if not hist_valid:
            s = jnp.where(key_ok, s, -jnp.inf)
        scores.append(s)
    probs_b = []
    for hd, s in enumerate(scores):
        sk = sink_ref[hd]
        mx = jnp.maximum(jnp.max(s, axis=-1, keepdims=True), sk)
        p = jnp.exp(s - mx)
        p = p / (jnp.sum(p, axis=-1, keepdims=True) + jnp.exp(sk - mx))
        probs_b.append(p.astype(BF16))
    for hd, p in enumerate(probs_b):
        o = _dot(p, vgs[hd // G_B])
        mix_ref[:, W_A + hd * DH_B:W_A + (hd + 1) * DH_B] = o.astype(mix_ref.dtype)

    kh_s[...] = k_win[R:, :]
    vh_s[...] = v_win[R:, :]
    kout_ref[...] = kh_s[...]
    vout_ref[...] = vh_s[...]


def _rel_bucket(rel):
    nb = NUM_BUCKETS // 2
    max_exact = nb // 2
    n = jnp.abs(rel)
    nf = jnp.maximum(n, 1).astype(F32)
    large = max_exact + (jnp.log(nf / max_exact) / math.log(MAX_DISTANCE / max_exact)
                         * (nb - max_exact)).astype(I32)
    large = jnp.minimum(large, nb - 1)
    return jnp.where(rel > 0, nb, 0) + jnp.where(n < max_exact, n, large)


def _swa_bias_table(rel_bias, rows):
    kw = WINDOW + rows
    qi = jnp.arange(rows)[:, None]
    kj = jnp.arange(kw)[None, :]
    bucket = _rel_bucket(kj - WINDOW - qi)
    rb = rel_bias.astype(F32)
    bias = jnp.zeros((NH_B, rows, kw), F32)
    for b in range(NUM_BUCKETS):
        bias = jnp.where((bucket == b)[None], rb[b][:, None, None], bias)
    lo = (qi // CHUNK) * CHUNK
    ok = (kj >= lo) & (kj < lo + WINDOW + CHUNK)
    return jnp.where(ok[None], bias, -jnp.inf)


def _even_mixer(proj, k_hist, v_hist, c0, n0, m0, bias_tab, norm_a, sink, batch, seq_len, rows, hist_valid):
    nsteps = seq_len // rows
    kw = WINDOW + rows
    m0p = jnp.broadcast_to(jnp.pad(m0, ((0, 0), (0, 8 - NH_A)))[:, :, None], (batch, 8, LANES))
    n0p = jnp.pad(n0, ((0, 0), (0, 8 - NH_A), (0, 0)))
    full3 = lambda b, i: (b, 0, 0)
    outs = pl.pallas_call(
        functools.partial(_even_kernel, rows=rows, hist_valid=hist_valid),
        out_shape=(
            jax.ShapeDtypeStruct((batch * seq_len, D_MODEL), BF16),
            jax.ShapeDtypeStruct((batch, NH_A, DK_A, DV_A), F32),
            jax.ShapeDtypeStruct((batch, 8, DK_A), F32),
            jax.ShapeDtypeStruct((batch, 8, LANES), F32),
            jax.ShapeDtypeStruct((batch, WINDOW, NKV_B * DH_B), F32),
            jax.ShapeDtypeStruct((batch, WINDOW, NKV_B * DH_B), F32),
        ),
        grid=(batch, nsteps),
        in_specs=[
            pl.BlockSpec((rows, N_EVEN_COLS), lambda b, i: (b * nsteps + i, 0)),
            pl.BlockSpec((None, WINDOW, NKV_B * DH_B), full3),
            pl.BlockSpec((None, WINDOW, NKV_B * DH_B), full3),
            pl.BlockSpec((None, NH_A, DK_A, DV_A), lambda b, i: (b, 0, 0, 0)),
            pl.BlockSpec((None, 8, DK_A), full3),
            pl.BlockSpec((None, 8, LANES), full3),
            pl.BlockSpec((NH_B, rows, kw), lambda b, i: (0, 0, 0)),
            pl.BlockSpec((NH_A, DV_A), lambda b, i: (0, 0)),
            pl.BlockSpec(memory_space=pltpu.SMEM),
        ],
        out_specs=(
            pl.BlockSpec((rows, D_MODEL), lambda b, i: (b * nsteps + i, 0)),
            pl.BlockSpec((None, NH_A, DK_A, DV_A), lambda b, i: (b, 0, 0, 0)),
            pl.BlockSpec((None, 8, DK_A), full3),
            pl.BlockSpec((None, 8, LANES), full3),
            pl.BlockSpec((None, WINDOW, NKV_B * DH_B), full3),
            pl.BlockSpec((None, WINDOW, NKV_B * DH_B), full3),
        ),
        scratch_shapes=[
            pltpu.VMEM((NH_A, DK_A, DV_A), F32),
            pltpu.VMEM((8, DK_A), F32),
            pltpu.VMEM((8, LANES), F32),
            pltpu.VMEM((WINDOW, NKV_B * DH_B), F32),
            pltpu.VMEM((WINDOW, NKV_B * DH_B), F32),
        ],
        compiler_params=_cparams(("arbitrary", "arbitrary")),
        name="even_mixer",
    )(proj, k_hist.reshape(batch, WINDOW, NKV_B * DH_B), v_hist.reshape(batch, WINDOW, NKV_B * DH_B),
      c0, n0p, m0p, bias_tab, norm_a.reshape(NH_A, DV_A), sink)
    mix, c1, n1, m1, k1, v1 = outs
    return (mix, c1, n1[:, :NH_A, :], m1[:, :NH_A, 0],
            k1.reshape(batch, WINDOW, NKV_B, DH_B), v1.reshape(batch, WINDOW, NKV_B, DH_B))


def _odd_kernel(proj_ref, conv0_ref, s0_ref, convw_ref, hp_ref, normc_ref,
                mix_ref, sout_ref, convout_ref, s_s, xa_s, *, rows):
    R = rows
    i = pl.program_id(1)
    HB = 8

    @pl.when(i == 0)
    def _():
        s_s[...] = s0_ref[...]
        xa_s[HB - (CONV_W - 1):HB, :] = conv0_ref[...]

    xa_s[HB:HB + R, :] = proj_ref[:, O_QKV:O_QKV + QKV_C]
    convout_ref[...] = xa_s[HB + R - (CONV_W - 1):HB + R, :]

    CS = CHUNK
    P2 = 2 * CS
    gcols = proj_ref[:, O_G:O_G + LANES]
    beta_all = _sigmoid(gcols)
    g_all = -jnp.exp(hp_ref[0:1, :]) * _softplus(gcols + hp_ref[1:2, :])
    rr = lax.broadcasted_iota(I32, (R, R), 0)
    cc = lax.broadcasted_iota(I32, (R, R), 1)
    tri_b = jnp.where((rr >= cc) & (rr // CS == cc // CS), 1.0, 0.0).astype(BF16)
    G_all = _dot_exact_lhs(tri_b, g_all)
    G_t = _tr(G_all)
    eG_all = jnp.exp(G_all)
    r2 = lax.broadcasted_iota(I32, (P2, P2), 0)
    c2 = lax.broadcasted_iota(I32, (P2, P2), 1)
    same_head = (r2 // CS) == (c2 // CS)
    incl2 = same_head & (r2 >= c2)
    strict2 = same_head & (r2 > c2)
    eye2 = jnp.where(r2 == c2, 1.0, 0.0)
    lane_lo = lax.broadcasted_iota(I32, (DK_C, P2), 1) < CS
    scale_c = DK_C ** -0.5

    def conv_silu(rs0, c0):
        y = xa_s[HB - 3 + rs0:HB - 3 + rs0 + CS, c0:c0 + LANES] * convw_ref[0:1, c0:c0 + LANES]
        for j in range(1, CONV_W):
            y = y + xa_s[HB - 3 + j + rs0:HB - 3 + j + rs0 + CS, c0:c0 + LANES] * convw_ref[j:j + 1, c0:c0 + LANES]
        return y * _sigmoid(y)

    def l2n(x):
        return x * lax.rsqrt(jnp.sum(x * x, axis=-1, keepdims=True) + RMS_EPS)

    def stack(a, b):
        return jnp.concatenate([a, b], axis=0)

    def inv_unit_lower_all(a_list):
        ts = [eye2 - a for a in a_list]
        abs_ = [a.astype(BF16) for a in a_list]
        ps = [_dot(ab, ab) for ab in abs_]
        nlev = (CS - 1).bit_length() - 1
        for lvl in range(nlev):
            nt, npw = [], []
            for t, p in zip(ts, ps):
                th = t.astype(BF16)
                ph = p.astype(BF16)
                if lvl < nlev - 1:
                    tp = _dot(stack(th, ph), ph)
                    nt.append(t + tp[:P2])
                    npw.append(tp[P2:])
                else:
                    nt.append(t + _dot(th, ph))
            ts, ps = nt, npw
        return ts

    probs = [(c0, pr) for c0 in range(0, R, CS) for pr in range(NH_C // 2)]
    pre = []
    for c0, pr in probs:
        rs = slice(c0, c0 + CS)
        ha, hb = 2 * pr, 2 * pr + 1
        ks = [l2n(conv_silu(c0, NH_C * DK_C + h * DK_C)) for h in (ha, hb)]
        qs = [l2n(conv_silu(c0, h * DK_C)) * scale_c for h in (ha, hb)]
        vs = [conv_silu(c0, 2 * NH_C * DK_C + h * DV_C) for h in (ha, hb)]
        k2 = stack(*ks)
        k2b = k2.astype(BF16)
        q2b = stack(*qs).astype(BF16)
        v2 = stack(*vs)
        G_col = stack(G_all[rs, NH_C + ha:NH_C + ha + 1], G_all[rs, NH_C + hb:NH_C + hb + 1])
        G_row = jnp.concatenate([G_t[NH_C + ha:NH_C + ha + 1, rs], G_t[NH_C + hb:NH_C + hb + 1, rs]], axis=1)
        beta = stack(beta_all[rs, ha:ha + 1], beta_all[rs, hb:hb + 1])
        eg = stack(eG_all[rs, NH_C + ha:NH_C + ha + 1], eG_all[rs, NH_C + hb:NH_C + hb + 1])
        GL = [G_all[c0 + CS - 1:c0 + CS, NH_C + h:NH_C + h + 1] for h in (ha, hb)]
        GL2 = stack(jnp.broadcast_to(GL[0], (CS, 1)), jnp.broadcast_to(GL[1], (CS, 1)))
        dmat = jnp.exp(jnp.where(incl2, G_col - G_row, -jnp.inf))
        pre.append(dict(
            a_mat=jnp.where(strict2, beta * dmat * _dot(k2b, k2b, _NT), 0.0),
            rhs=jnp.concatenate([beta * v2, (beta * eg) * k2], axis=1).astype(BF16),
            attn=(_dot(q2b, k2b, _NT) * dmat).astype(BF16),
            kdec_t=_tr(k2 * jnp.exp(GL2 - G_col)).astype(BF16),
            q2b=q2b, eg=eg, egl=[jnp.exp(GL[0]), jnp.exp(GL[1])]))
    tinvs = inv_unit_lower_all([d['a_mat'] for d in pre])
    for d, tinv in zip(pre, tinvs):
        d['uw'] = _dot(tinv.astype(BF16), d['rhs'])

    S_cur = [s_s[h] for h in range(NH_C)]
    npair = NH_C // 2
    for ci, c0 in enumerate(range(0, R, CS)):
        rs = slice(c0, c0 + CS)
        ds = pre[ci * npair:(ci + 1) * npair]
        wqs = []
        for pr, d in enumerate(ds):
            uw, q2b = d['uw'], d['q2b']
            wqs.append([_dot(stack(uw[j * CS:(j + 1) * CS, DV_C:].astype(BF16), q2b[j * CS:(j + 1) * CS]),
                             S_cur[2 * pr + j].astype(BF16)) for j in range(2)])
        dbs = [(d['uw'][:, :DV_C] - stack(wq[0][:CS], wq[1][:CS])).astype(BF16) for d, wq in zip(ds, wqs)]
        for pr, (d, wq, db) in enumerate(zip(ds, wqs, dbs)):
            ha, hb = 2 * pr, 2 * pr + 1
            kdec_t = d['kdec_t']
            zero = jnp.zeros_like(kdec_t)
            S_cur[ha] = d['egl'][0] * S_cur[ha] + _dot(jnp.where(lane_lo, kdec_t, zero), db)
            S_cur[hb] = d['egl'][1] * S_cur[hb] + _dot(jnp.where(lane_lo, zero, kdec_t), db)
        for pr, (d, wq, db) in enumerate(zip(ds, wqs, dbs)):
            o2 = d['eg'] * stack(wq[0][CS:], wq[1][CS:]) + _dot(d['attn'], db)
            for j in range(2):
                h = 2 * pr + j
                o = o2[j * CS:(j + 1) * CS]
                z = proj_ref[rs, O_Z + h * DV_C:O_Z + (h + 1) * DV_C]
                o = o * lax.rsqrt(jnp.mean(o * o, axis=-1, keepdims=True) + RMS_EPS) * normc_ref[...]
                o = o * (z * _sigmoid(z))
                mix_ref[rs, h * DV_C:(h + 1) * DV_C] = o.astype(mix_ref.dtype)
    for h in range(NH_C):
        s_s[h] = S_cur[h]

    xa_s[HB - (CONV_W - 1):HB, :] = xa_s[HB + R - (CONV_W - 1):HB + R, :]
    sout_ref[...] = s_s[...]


def _odd_mixer(proj, conv_hist, s0, conv_w, a_log, dt_bias, norm_c, batch, seq_len, rows):
    nsteps = seq_len // rows
    hp = jnp.zeros((8, LANES), F32)
    hp = hp.at[0, NH_C:2 * NH_C].set(a_log.astype(F32)).at[1, NH_C:2 * NH_C].set(dt_bias.astype(F32))
    outs = pl.pallas_call(
        functools.partial(_odd_kernel, rows=rows),
        out_shape=(
            jax.ShapeDtypeStruct((batch * seq_len, D_MODEL), BF16),
            jax.ShapeDtypeStruct((batch, NH_C, DK_C, DV_C), F32),
            jax.ShapeDtypeStruct((batch, CONV_W - 1, QKV_C), F32),
        ),
        grid=(batch, nsteps),
        in_specs=[
            pl.BlockSpec((rows, N_ODD_COLS), lambda b, i: (b * nsteps + i, 0)),
            pl.BlockSpec((None, CONV_W - 1, QKV_C), lambda b, i: (b, 0, 0)),
            pl.BlockSpec((None, NH_C, DK_C, DV_C), lambda b, i: (b, 0, 0, 0)),
            pl.BlockSpec((CONV_W, QKV_C), lambda b, i: (0, 0)),
            pl.BlockSpec((8, LANES), lambda b, i: (0, 0)),
            pl.BlockSpec((1, DV_C), lambda b, i: (0, 0)),
        ],
        out_specs=(
            pl.BlockSpec((rows, D_MODEL), lambda b, i: (b * nsteps + i, 0)),
            pl.BlockSpec((None, NH_C, DK_C, DV_C), lambda b, i: (b, 0, 0, 0)),
            pl.BlockSpec((None, CONV_W - 1, QKV_C), lambda b, i: (b, 0, 0)),
        ),
        scratch_shapes=[
            pltpu.VMEM((NH_C, DK_C, DV_C), F32),
            pltpu.VMEM((8 + rows, QKV_C), F32),
        ],
        compiler_params=_cparams(("arbitrary", "arbitrary")),
        name="odd_mixer",
    )(proj, conv_hist, s0, conv_w, hp, norm_c.reshape(1, DV_C))
    return outs


def _post_kernel(mix_ref, x_ref, gate_ref, sc_ref, sh_ref, wout_ref, lng_ref, lnb_ref, wr_ref, br_ref,
                 x1_ref, h2_ref, ri_ref, rg_ref, seg_ref, cnt_ref, carry_s, *, tm):
    i = pl.program_id(0)

    @pl.when(i == 0)
    def _():
        carry_s[...] = jnp.zeros_like(carry_s)

    y = _dot(mix_ref[...], wout_ref[...])
    x1 = _layer_norm_rows(ALPHA_DN * x_ref[...] + (1.0 + gate_ref[...]) * y, lng_ref[...], lnb_ref[...])
    x1_ref[...] = x1
    h2 = x1 * (1.0 + sc_ref[...]) + sh_ref[...]
    h2b = h2.astype(BF16)
    h2_ref[...] = h2b
    lt = _dot(wr_ref[...].astype(BF16), h2b, _NT) + br_ref[:, 0:1]
    e_iota = lax.broadcasted_iota(I32, (N_EXPERTS, tm), 0).astype(F32)
    vals, idxs = [], []
    for _ in range(TOP_K):
        mx = jnp.max(lt, axis=0, keepdims=True)
        idx = jnp.min(jnp.where(lt == mx, e_iota, float(N_EXPERTS)), axis=0, keepdims=True)
        vals.append(mx)
        idxs.append(idx)
        lt = jnp.where(e_iota == idx, -jnp.inf, lt)
    ex = [jnp.exp(v - vals[0]) for v in vals]
    tot = ex[0] + ex[1] + ex[2] + ex[3]
    hot = [jnp.where(e_iota == idx, 1.0, 0.0) for idx in idxs]
    m_all = hot[0] + hot[1] + hot[2] + hot[3]
    m_all_b = m_all.astype(BF16)
    r = lax.broadcasted_iota(I32, (tm, tm), 0)
    c = lax.broadcasted_iota(I32, (tm, tm), 1)
    upper = jnp.where(r < c, 1.0, 0.0).astype(BF16)
    prefix = _dot(m_all_b, upper)
    re = lax.broadcasted_iota(I32, (N_EXPERTS, N_EXPERTS), 0)
    ce = lax.broadcasted_iota(I32, (N_EXPERTS, N_EXPERTS), 1)
    cnt_col = jnp.sum(m_all, axis=1, keepdims=True)
    m8_col = jnp.floor((cnt_col + 7.0) * 0.125)
    lower_e = jnp.where(re > ce, 1.0, 0.0).astype(BF16)
    off8_col = 8.0 * _dot(lower_e, jnp.broadcast_to(m8_col, (N_EXPERTS, LANES)).astype(BF16))[:, 0:1]
    base = off8_col + prefix
    dests = [jnp.sum(hk * base, axis=0, keepdims=True) for hk in hot]
    cnt_row = _dot(jnp.ones((8, tm), BF16), m_all_b, _NT)
    m8_row = jnp.floor((cnt_row + 7.0) * 0.125)
    upper_e = jnp.where(re < ce, 1.0, 0.0).astype(BF16)
    off8_row = 8.0 * _dot(m8_row.astype(BF16), upper_e)
    gc = carry_s[:, 0:N_EXPERTS]
    srow = lax.broadcasted_iota(I32, (8, N_EXPERTS), 0)
    seg = jnp.where(srow == 0, 8.0 * m8_row, jnp.where(srow == 1, off8_row, jnp.where(srow == 2, gc, 0.0)))
    seg_ref[...] = jnp.concatenate([seg, jnp.zeros((8, LANES - N_EXPERTS), F32)], axis=1).astype(I32)
    carry_s[:, 0:N_EXPERTS] = gc + 8.0 * m8_row
    cnt_ref[...] = carry_s[...]
    ri_ref[...] = jnp.concatenate(dests + [jnp.zeros((4, tm), F32)], axis=0).astype(I32)
    rg_ref[...] = jnp.concatenate([e / tot for e in ex] + [jnp.zeros((4, tm), F32)], axis=0)


def _post(mix, x2d, gate, scale, shift, w_out_bf16, ln_g, ln_b, wr_t, br, seq_len, tm):
    t = x2d.shape[0]
    per = seq_len // tm
    vec = lambda i: (i // per, 0, 0)
    const2 = lambda i: (0, 0)
    return pl.pallas_call(
        functools.partial(_post_kernel, tm=tm),
        out_shape=(
            jax.ShapeDtypeStruct((t, D_MODEL), F32),
            jax.ShapeDtypeStruct((t, D_MODEL), BF16),
            jax.ShapeDtypeStruct((t // tm, 8, tm), I32),
            jax.ShapeDtypeStruct((t // tm, 8, tm), F32),
            jax.ShapeDtypeStruct((t // tm, 8, LANES), I32),
            jax.ShapeDtypeStruct((8, LANES), F32),
        ),
        grid=(t // tm,),
        in_specs=[
            pl.BlockSpec((tm, D_MODEL), lambda i: (i, 0)),
            pl.BlockSpec((tm, D_MODEL), lambda i: (i, 0)),
            pl.BlockSpec((None, 1, D_MODEL), vec),
            pl.BlockSpec((None, 1, D_MODEL), vec),
            pl.BlockSpec((None, 1, D_MODEL), vec),
            pl.BlockSpec((D_MODEL, D_MODEL), const2),
            pl.BlockSpec((1, D_MODEL), const2),
            pl.BlockSpec((1, D_MODEL), const2),
            pl.BlockSpec((N_EXPERTS, D_MODEL), const2),
            pl.BlockSpec((N_EXPERTS, LANES), const2),
        ],
        out_specs=(
            pl.BlockSpec((tm, D_MODEL), lambda i: (i, 0)),
            pl.BlockSpec((tm, D_MODEL), lambda i: (i, 0)),
            pl.BlockSpec((None, 8, tm), lambda i: (i, 0, 0)),
            pl.BlockSpec((None, 8, tm), lambda i: (i, 0, 0)),
            pl.BlockSpec((None, 8, LANES), lambda i: (i, 0, 0)),
            pl.BlockSpec((8, LANES), const2),
        ),
        scratch_shapes=[pltpu.VMEM((8, LANES), F32)],
        compiler_params=_cparams(("arbitrary",)),
        name="post_router",
    )(mix, x2d, gate, scale, shift, w_out_bf16, ln_g, ln_b, wr_t, br)


SEG_FIELDS = 3


def _run_sizes(tm):
    top = 1 << (tm - 1).bit_length()
    return [s for s in (512, 256, 128, 64, 32, 16, 8) if s <= max(top, 8)]


def _for_each_run_piece(seg_ref, rstart_ref, tile, tm, fn):
    base = tile * (SEG_FIELDS * N_EXPERTS)

    def body(e, c):
        n8 = seg_ref[base + e]
        src = seg_ref[base + N_EXPERTS + e]
        dst = rstart_ref[e] + seg_ref[base + 2 * N_EXPERTS + e]
        done = jnp.int32(0)
        for size in _run_sizes(tm):
            @pl.when((n8 & size) != 0)
            def _(done=done, size=size):
                fn(pl.multiple_of(src + done, 8), pl.multiple_of(dst + done, 8), size)
            done = done + (n8 & size)
        return c

    lax.fori_loop(0, N_EXPERTS, body, 0)


def _dispatch_kernel(zblk_ref, rstart_ref, seg_ref, ri_ref, h_ref, xs_ref, zero_s, cbuf, sem, zsem,
                     *, tm, bm, cb, ntiles):
    i = pl.program_id(0)
    slot = i % 2

    def run_copies(tile, sl, wait):
        def piece(src, dst, size):
            cp = pltpu.make_async_copy(cbuf.at[sl, pl.ds(src, size), :], xs_ref.at[pl.ds(dst, size), :], sem.at[sl])
            if wait:
                cp.wait()
            else:
                cp.start()
        _for_each_run_piece(seg_ref, rstart_ref, tile, tm, piece)

    def zero_copy(e):
        return pltpu.make_async_copy(zero_s, xs_ref.at[pl.ds(zblk_ref[e] * bm, bm), :], zsem)

    @pl.when(i == 0)
    def _():
        zero_s[...] = jnp.zeros_like(zero_s)

        def zstart(e, c):
            @pl.when(zblk_ref[e] >= 0)
            def _():
                zero_copy(e).start()
            return c

        def zwait(e, c):
            @pl.when(zblk_ref[e] >= 0)
            def _():
                zero_copy(e).wait()
            return c

        lax.fori_loop(0, N_EXPERTS, zstart, 0)
        lax.fori_loop(0, N_EXPERTS, zwait, 0)

    @pl.when(i >= 2)
    def _():
        run_copies(i - 2, slot, True)

    rows = lax.broadcasted_iota(I32, (cb, tm), 0)
    hit = rows == ri_ref[0:1, :]
    for k in range(1, TOP_K):
        hit = hit | (rows == ri_ref[k:k + 1, :])
    perm = jnp.where(hit, 1.0, 0.0).astype(BF16)
    cbuf[slot] = _dot(perm, h_ref[...])
    run_copies(i, slot, False)

    @pl.when(i == ntiles - 1)
    def _():
        if ntiles > 1:
            run_copies(i - 1, 1 - slot, True)
        run_copies(i, slot, True)


def _dispatch(h2, route_i, seg_flat, row_start, zblk, n_rows, tm, bm):
    t = h2.shape[0]
    ntiles = t // tm
    cb = TOP_K * tm + 8 * N_EXPERTS
    return pl.pallas_call(
        functools.partial(_dispatch_kernel, tm=tm, bm=bm, cb=cb, ntiles=ntiles),
        out_shape=jax.ShapeDtypeStruct((n_rows, D_MODEL), F32),
        grid_spec=pltpu.PrefetchScalarGridSpec(
            num_scalar_prefetch=3,
            grid=(ntiles,),
            in_specs=[
                pl.BlockSpec((None, 2 * TOP_K, tm), lambda i, z, r, s: (i, 0, 0)),
                pl.BlockSpec((tm, D_MODEL), lambda i, z, r, s: (i, 0)),
            ],
            out_specs=pl.BlockSpec(memory_space=pl.ANY),
            scratch_shapes=[
                pltpu.VMEM((bm, D_MODEL), F32),
                pltpu.VMEM((2, cb, D_MODEL), F32),
                pltpu.SemaphoreType.DMA((2,)),
                pltpu.SemaphoreType.DMA,
            ],
        ),
        compiler_params=_cparams(("arbitrary",)),
        name="dispatch",
    )(zblk, row_start, seg_flat, route_i, h2)


def _ffn_kernel(be_ref, nv_ref, x_ref, w1_ref, b1_ref, w2_ref, b2_ref, y_ref, w1_s, w2_s):
    b = pl.program_id(0)

    @pl.when(b < nv_ref[0])
    def _():
        @pl.when((b == 0) | (be_ref[b] != be_ref[jnp.maximum(b - 1, 0)]))
        def _():
            step = 256
            for r0 in range(0, D_MODEL, step):
                w1_s[r0:r0 + step, :] = w1_ref[r0:r0 + step, :].astype(BF16)
            for r0 in range(0, D_FF, step):
                w2_s[r0:r0 + step, :] = w2_ref[r0:r0 + step, :].astype(BF16)

        xb = x_ref[...].astype(BF16)
        glu = _dot(xb, w1_s[:, :D_FF]) + b1_ref[:, :D_FF]
        lin = _dot(xb, w1_s[:, D_FF:]) + b1_ref[:, D_FF:]
        glu = jnp.minimum(glu, SWIGLU_LIMIT)
        lin = jnp.clip(lin, -SWIGLU_LIMIT, SWIGLU_LIMIT)
        act = glu * _sigmoid(SWIGLU_ALPHA * glu) * (lin + 1.0)
        y_ref[...] = _dot(act.astype(BF16), w2_s[...]) + b2_ref[...]


def _ffn(xs, block_e, n_valid, layer, w1, b1, w2, b2, bm):
    nb = xs.shape[0] // bm
    row = lambda b, be, nv: (jnp.minimum(b, nv[0] - 1), 0)
    wsel = lambda b, be, nv: (layer, be[b], 0, 0)
    return pl.pallas_call(
        _ffn_kernel,
        out_shape=jax.ShapeDtypeStruct(xs.shape, F32),
        grid_spec=pltpu.PrefetchScalarGridSpec(
            num_scalar_prefetch=2,
            grid=(nb,),
            in_specs=[
                pl.BlockSpec((bm, D_MODEL), row),
                pl.BlockSpec((None, None, D_MODEL, 2 * D_FF), wsel),
                pl.BlockSpec((None, None, 1, 2 * D_FF), wsel),
                pl.BlockSpec((None, None, D_FF, D_MODEL), wsel),
                pl.BlockSpec((None, None, 1, D_MODEL), wsel),
            ],
            out_specs=pl.BlockSpec((bm, D_MODEL), row),
            scratch_shapes=[pltpu.VMEM((D_MODEL, 2 * D_FF), BF16), pltpu.VMEM((D_FF, D_MODEL), BF16)],
        ),
        compiler_params=_cparams(("arbitrary",)),
        name="expert_ffn",
    )(block_e, n_valid, xs, w1, b1, w2, b2)


def _combine_kernel(rstart_ref, seg_ref, ri_ref, x_ref, gate_ref, rg_ref, lng_ref, lnb_ref, ys_ref, o_ref,
                    ybuf, yb_s, wt_s, sem, *, tm, cb, ntiles):
    i = pl.program_id(0)
    slot = i % 2

    def run_copies(tile, sl, wait):
        def piece(loc, glob, size):
            cp = pltpu.make_async_copy(ys_ref.at[pl.ds(glob, size), :], ybuf.at[sl, pl.ds(loc, size), :], sem.at[sl])
            if wait:
                cp.wait()
            else:
                cp.start()
        _for_each_run_piece(seg_ref, rstart_ref, tile, tm, piece)

    @pl.when(i == 0)
    def _():
        ybuf[...] = jnp.zeros_like(ybuf)
        run_copies(0, 0, False)

    @pl.when(i + 1 < ntiles)
    def _():
        run_copies(i + 1, 1 - slot, False)

    dest_t = _tr(jnp.concatenate([ri_ref[...].astype(F32), jnp.zeros((LANES - 8, tm), F32)], axis=0))
    g_t = _tr(jnp.concatenate([rg_ref[...], jnp.zeros((LANES - 8, tm), F32)], axis=0))
    lane_step = 256
    for c0 in range(0, cb, lane_step):
        cols = (lax.broadcasted_iota(I32, (tm, lane_step), 1) + c0).astype(F32)
        w = jnp.where(cols == dest_t[:, 0:1], g_t[:, 0:1], 0.0)
        for k in range(1, TOP_K):
            w = w + jnp.where(cols == dest_t[:, k:k + 1], g_t[:, k:k + 1], 0.0)
        wt_s[:, c0:c0 + lane_step] = w.astype(BF16)

    run_copies(i, slot, True)
    row_step = 256
    for r0 in range(0, cb, row_step):
        yb_s[r0:r0 + row_step, :] = ybuf[slot, r0:r0 + row_step, :].astype(BF16)
    moe = _dot(wt_s[...], yb_s[...])
    o_ref[...] = _layer_norm_rows(ALPHA_DN * x_ref[...] + (1.0 + gate_ref[...]) * moe,
                                  lng_ref[...], lnb_ref[...])


def _combine(row_start, seg_flat, route_i, x1, gate, route_g, ln_g, ln_b, ys, seq_len, tm):
    t = x1.shape[0]
    per = seq_len // tm
    ntiles = t // tm
    cb = TOP_K * tm + 8 * N_EXPERTS
    return pl.pallas_call(
        functools.partial(_combine_kernel, tm=tm, cb=cb, ntiles=ntiles),
        out_shape=jax.ShapeDtypeStruct((t, D_MODEL), F32),
        grid_spec=pltpu.PrefetchScalarGridSpec(
            num_scalar_prefetch=2,
            grid=(ntiles,),
            in_specs=[
                pl.BlockSpec((None, 2 * TOP_K, tm), lambda i, r, s: (i, 0, 0)),
                pl.BlockSpec((tm, D_MODEL), lambda i, r, s: (i, 0)),
                pl.BlockSpec((None, 1, D_MODEL), lambda i, r, s: (i // per, 0, 0)),
                pl.BlockSpec((None, 8, tm), lambda i, r, s: (i, 0, 0)),
                pl.BlockSpec((1, D_MODEL), lambda i, r, s: (0, 0)),
                pl.BlockSpec((1, D_MODEL), lambda i, r, s: (0, 0)),
                pl.BlockSpec(memory_space=pl.ANY),
            ],
            out_specs=pl.BlockSpec((tm, D_MODEL), lambda i, r, s: (i, 0)),
            scratch_shapes=[
                pltpu.VMEM((2, cb, D_MODEL), F32),
                pltpu.VMEM((cb, D_MODEL), BF16),
                pltpu.VMEM((tm, cb), BF16),
                pltpu.SemaphoreType.DMA((2,)),
            ],
        ),
        compiler_params=_cparams(("arbitrary",)),
        name="combine",
    )(row_start, seg_flat, route_i, x1, gate, route_g, ln_g, ln_b, ys)


def _moe_tables(counts, bm, n_blocks):
    cnt = counts[0, :N_EXPERTS].astype(I32)
    nblk = (cnt + bm - 1) // bm
    blk_end = jnp.cumsum(nblk)
    row_start = (blk_end - nblk) * bm
    n_valid = blk_end[-1]
    bidx = jnp.minimum(jnp.arange(n_blocks, dtype=I32), n_valid - 1)
    block_e = jnp.sum((blk_end[None, :] <= bidx[:, None]).astype(I32), axis=1)
    block_e = jnp.minimum(block_e, N_EXPERTS - 1)
    zblk = jnp.where((cnt % bm) != 0, blk_end - 1, -1).astype(I32)
    return row_start.astype(I32), block_e, n_valid.reshape(1).astype(I32), zblk


def _moe_and_norm(h2, x1, route_i, route_g, seg, counts, gate, ln_g, ln_b, layer, w1, b1, w2, b2, seq_len, tm, bm):
    t = h2.shape[0]
    ntiles = t // tm
    n_blocks = -(-(t * TOP_K + 7 * N_EXPERTS * ntiles) // bm) + N_EXPERTS
    row_start, block_e, n_valid, zblk = _moe_tables(counts, bm, n_blocks)
    seg_flat = seg[:, :SEG_FIELDS, :N_EXPERTS].reshape(ntiles * SEG_FIELDS * N_EXPERTS)
    xs = _dispatch(h2, route_i, seg_flat, row_start, zblk, n_blocks * bm, tm, bm)
    ys = _ffn(xs, block_e, n_valid, layer, w1, b1, w2, b2, bm)
    return _combine(row_start, seg_flat, route_i, x1, gate, route_g, ln_g, ln_b, ys, seq_len, tm)


def _group_cfg(batch, seq_len):
    if seq_len >= 512:
        return dict(tm=512, rows=256, bm=512)
    return dict(tm=seq_len, rows=seq_len, bm=128)


def kernel(x_prompt, x_sample, c_prompt, c_sample, state_a_C, state_a_n, state_a_m, cache_b_k, cache_b_v,
           state_c_S, state_c_conv, w_ada, b_ada, ln_g, ln_b, w_in_even, b_in_even, norm_a, sink_b, rel_bias,
           w_out_even, w_in_odd, conv_c, a_log_c, dt_bias_c, norm_c, w_out_odd, w_router, b_router,
           w_e1, b_e1, w_e2, b_e2):
    bp, lp, _ = x_prompt.shape
    bs, ls, _ = x_sample.shape
    groups = [dict(b=bp, l=lp, x=x_prompt.reshape(bp * lp, D_MODEL), **_group_cfg(bp, lp)),
              dict(b=bs, l=ls, x=x_sample.reshape(bs * ls, D_MODEL), **_group_cfg(bs, ls))]
    mod = _ada(jnp.concatenate([c_prompt, c_sample], axis=0), w_ada, b_ada)
    offs = [0, bp]
    states = [dict(), dict()]
    b1_all = b_e1.reshape(DEPTH, N_EXPERTS, 1, 2 * D_FF)
    b2_all = b_e2.reshape(DEPTH, N_EXPERTS, 1, D_MODEL)
    for l in range(DEPTH):
        e = l // 2
        if l % 2 == 0:
            w = w_in_even[e]
            sz = (512, 512, 512, 512, 4, 4, 512, 128, 128)
            o = [sum(sz[:j]) for j in range(len(sz) + 1)]
            pad = jnp.zeros((D_MODEL, LANES - 2 * NH_A), w.dtype)
            w_in = jnp.concatenate([w[:, o[0]:o[4]], w[:, o[6]:o[9]], w[:, o[4]:o[6]], pad], axis=1).astype(BF16)
            bb = b_in_even[e]
            b_in = jnp.concatenate([bb[o[0]:o[4]], bb[o[6]:o[9]], bb[o[4]:o[6]],
                                    jnp.zeros((LANES - 2 * NH_A,), bb.dtype)]).reshape(1, N_EVEN_COLS)
            w_out = w_out_even[e].astype(BF16)
        else:
            w = w_in_odd[e]
            pad = jnp.zeros((D_MODEL, LANES - 2 * NH_C), w.dtype)
            w_in = jnp.concatenate([w, pad], axis=1).astype(BF16)
            b_in = jnp.zeros((1, N_ODD_COLS), F32)
            w_out = w_out_odd[e].astype(BF16)
        wr_t = w_router[l].T
        br = jnp.broadcast_to(b_router[l][:, None], (N_EXPERTS, LANES))
        for gi, gr in enumerate(groups):
            nb, sl = gr['b'], gr['l']
            m = mod[l, offs[gi]:offs[gi] + nb].reshape(nb, 6, 1, D_MODEL)
            m6 = [m[:, j] for j in range(6)]
            proj = _inproj(gr['x'], m6[1], m6[0], w_in, b_in, sl, gr['tm'])
            st = states[gi]
            if l % 2 == 0:
                if gi == 0:
                    kh = jnp.zeros((nb, WINDOW, NKV_B, DH_B), F32)
                    vh = kh
                    c0 = jnp.zeros((nb, NH_A, DK_A, DV_A), F32)
                    n0 = jnp.zeros((nb, NH_A, DK_A), F32)
                    m0 = jnp.zeros((nb, NH_A), F32)
                else:
                    kh, vh, c0, n0, m0 = cache_b_k[e], cache_b_v[e], state_a_C[e], state_a_n[e], state_a_m[e]
                bias_tab = _swa_bias_table(rel_bias, gr['rows'])
                mix, c1, n1, m1, k1, v1 = _even_mixer(proj, kh, vh, c0, n0, m0, bias_tab, norm_a[e], sink_b[e],
                                                      nb, sl, gr['rows'], gi == 1)
                for name, val in (('a_C', c1), ('a_n', n1), ('a_m', m1), ('b_k', k1), ('b_v', v1)):
                    st.setdefault(name, []).append(val)
            else:
                if gi == 0:
                    cv0 = jnp.zeros((nb, CONV_W - 1, QKV_C), F32)
                    s0 = jnp.zeros((nb, NH_C, DK_C, DV_C), F32)
                else:
                    cv0, s0 = state_c_conv[e], state_c_S[e]
                mix, s1, cv1 = _odd_mixer(proj, cv0, s0, conv_c[e], a_log_c[e], dt_bias_c[e], norm_c[e], nb, sl,
                                          gr['rows'])
                st.setdefault('c_S', []).append(s1)
                st.setdefault('c_conv', []).append(cv1)
            x1, h2, route_i, route_g, seg, counts = _post(
                mix, gr['x'], m6[2], m6[4], m6[3], w_out, ln_g[l, 0].reshape(1, D_MODEL),
                ln_b[l, 0].reshape(1, D_MODEL), wr_t, br, sl, gr['tm'])
            gr['x'] = _moe_and_norm(h2, x1, route_i, route_g, seg, counts, m6[5], ln_g[l, 1].reshape(1, D_MODEL),
                                    ln_b[l, 1].reshape(1, D_MODEL), l, w_e1, b1_all, w_e2, b2_all,
                                    sl, gr['tm'], gr['bm'])
    outs = [groups[0]['x'].reshape(bp, lp, D_MODEL), groups[1]['x'].reshape(bs, ls, D_MODEL)]
    for gi in range(2):
        for name in ('a_C', 'a_n', 'a_m', 'b_k', 'b_v', 'c_S', 'c_conv'):
            outs.append(jnp.stack(states[gi][name]))
    return tuple(outs)
```

```python
import functools
import math

import jax
import jax.numpy as jnp
from jax import lax
from jax.experimental import pallas as pl
from jax.experimental.pallas import tpu as pltpu

F32 = jnp.float32
BF16 = jnp.bfloat16
I32 = jnp.int32

D_MODEL = 1024
CHUNK = 64
NH_A, DK_A, DV_A = 4, 128, 128
W_A = NH_A * DV_A
NH_B, NKV_B, DH_B, WINDOW = 8, 2, 64, 128
G_B = NH_B // NKV_B
W_B = NH_B * DH_B
NUM_BUCKETS, MAX_DISTANCE = 32, 256
NH_C, DK_C, DV_C, CONV_W = 8, 128, 128, 4
QKV_C = NH_C * (2 * DK_C + DV_C)
N_EXPERTS, TOP_K, D_FF = 32, 4, 1024
SWIGLU_LIMIT, SWIGLU_ALPHA = 7.0, 1.702
DEPTH = 2
ALPHA_DN = (2 * DEPTH) ** 0.25
LN_EPS = 1e-5
RMS_EPS = 1e-6

LANES = 128
E_QA, E_KA, E_VA, E_OA, E_QB, E_KB, E_VB, E_G = 0, 512, 1024, 1536, 2048, 2560, 2688, 2816
N_EVEN_COLS = 2944
O_QKV, O_Z, O_G = 0, 3072, 4096
N_ODD_COLS = 4224

VMEM_LIMIT = 56 * 1024 * 1024


def _cparams(sem):
    return pltpu.CompilerParams(dimension_semantics=sem, vmem_limit_bytes=VMEM_LIMIT)


def _softplus(x):
    return jnp.maximum(x, 0.0) + jnp.log(1.0 + jnp.exp(-jnp.abs(x)))


def _sigmoid(x):
    return 1.0 / (1.0 + jnp.exp(-x))


def _split_bf16(a):
    hi = a.astype(BF16)
    lo = (a - hi.astype(F32)).astype(BF16)
    return hi, lo


_NN = (((1,), (0,)), ((), ()))
_NT = (((1,), (1,)), ((), ()))
_TN = (((0,), (0,)), ((), ()))


def _dot(a, b, dims=_NN):
    return lax.dot_general(a, b, dims, preferred_element_type=F32)


def _dot_x3(a, b, dims=_NN):
    ah, al = _split_bf16(a)
    bh, bl = _split_bf16(b)
    return _dot(ah, bh, dims) + _dot(ah, bl, dims) + _dot(al, bh, dims)


def _dot_exact_lhs(a_bf16, b, dims=_NN):
    b0 = b.astype(BF16)
    r1 = b - b0.astype(F32)
    b1 = r1.astype(BF16)
    b2 = (r1 - b1.astype(F32)).astype(BF16)
    return _dot(a_bf16, b0, dims) + _dot(a_bf16, b1, dims) + _dot(a_bf16, b2, dims)


def _tr(x):
    r = x.shape[0]
    rp = -(-r // LANES) * LANES
    if rp != r:
        x = jnp.concatenate([x, jnp.zeros((rp - r, x.shape[1]), x.dtype)], axis=0)
    return x.T[:, :r]


def _tri(n, strict=False):
    r = lax.broadcasted_iota(I32, (n, n), 0)
    c = lax.broadcasted_iota(I32, (n, n), 1)
    return (r > c) if strict else (r >= c)


def _layer_norm_rows(v, g, b):
    mu = jnp.mean(v, axis=-1, keepdims=True)
    d = v - mu
    var = jnp.mean(d * d, axis=-1, keepdims=True)
    return d * lax.rsqrt(var + LN_EPS) * g + b


def _ada_kernel(c_ref, w_ref, b_ref, o_ref):
    c = c_ref[...]
    a = (c * _sigmoid(c)).astype(BF16)
    o_ref[...] = _dot(a, w_ref[...].astype(BF16)) + b_ref[...]


def _ada(c_all, w_ada, b_ada):
    nb = c_all.shape[0]
    tn = 1536
    return pl.pallas_call(
        _ada_kernel,
        out_shape=jax.ShapeDtypeStruct((DEPTH, nb, 6 * D_MODEL), F32),
        grid=(DEPTH, 6 * D_MODEL // tn),
        in_specs=[
            pl.BlockSpec((nb, D_MODEL), lambda l, j: (0, 0)),
            pl.BlockSpec((None, D_MODEL, tn), lambda l, j: (l, 0, j)),
            pl.BlockSpec((None, 1, tn), lambda l, j: (l, 0, j)),
        ],
        out_specs=pl.BlockSpec((None, nb, tn), lambda l, j: (l, 0, j)),
        compiler_params=_cparams(("arbitrary", "arbitrary")),
        name="ada",
    )(c_all, w_ada, b_ada.reshape(DEPTH, 1, 6 * D_MODEL))


def _inproj_kernel(x_ref, sc_ref, sh_ref, w_ref, b_ref, o_ref, *, n_cols, col_step):
    h = (x_ref[...] * (1.0 + sc_ref[...]) + sh_ref[...]).astype(BF16)
    for c0 in range(0, n_cols, col_step):
        c1 = min(c0 + col_step, n_cols)
        o_ref[:, c0:c1] = _dot(h, w_ref[:, c0:c1]) + b_ref[:, c0:c1]


def _inproj(x2d, scale, shift, w_bf16, bias, seq_len, tm):
    t = x2d.shape[0]
    n = w_bf16.shape[1]
    per = seq_len // tm
    return pl.pallas_call(
        functools.partial(_inproj_kernel, n_cols=n, col_step=1024),
        out_shape=jax.ShapeDtypeStruct((t, n), F32),
        grid=(t // tm,),
        in_specs=[
            pl.BlockSpec((tm, D_MODEL), lambda i: (i, 0)),
            pl.BlockSpec((None, 1, D_MODEL), lambda i: (i // per, 0, 0)),
            pl.BlockSpec((None, 1, D_MODEL), lambda i: (i // per, 0, 0)),
            pl.BlockSpec((D_MODEL, n), lambda i: (0, 0)),
            pl.BlockSpec((1, n), lambda i: (0, 0)),
        ],
        out_specs=pl.BlockSpec((tm, n), lambda i: (i, 0)),
        compiler_params=_cparams(("arbitrary",)),
        name="inproj",
    )(x2d, scale, shift, w_bf16, bias)


def _even_kernel(proj_ref, kh0_ref, vh0_ref, c0_ref, n0_ref, m0_ref, bias_ref, norma_ref, sink_ref,
                 mix_ref, cout_ref, nout_ref, mout_ref, kout_ref, vout_ref,
                 c_s, n_s, m_s, kh_s, vh_s, *, rows, hist_valid):
    R = rows
    KW = WINDOW + R
    i = pl.program_id(1)

    @pl.when(i == 0)
    def _():
        c_s[...] = c0_ref[...]
        n_s[...] = n0_ref[...]
        m_s[...] = m0_ref[...]
        kh_s[...] = kh0_ref[...]
        vh_s[...] = vh0_ref[...]

    CS = CHUNK
    g = proj_ref[:, E_G:E_G + LANES]
    lf = -_softplus(-g)
    rr = lax.broadcasted_iota(I32, (R, R), 0)
    cc = lax.broadcasted_iota(I32, (R, R), 1)
    tri_b = jnp.where((rr >= cc) & (rr // CS == cc // CS), 1.0, 0.0).astype(BF16)
    b_all = _dot_exact_lhs(tri_b, lf)
    g_t = _tr(g)
    b_t = _tr(b_all)
    causal = _tri(CS)
    scale_a = DK_A ** -0.5
    chunks = list(range(0, R, CS))
    P = [(c0, h) for c0 in chunks for h in range(NH_A)]
    nP = len(P)
    rsl = [slice(c0, c0 + CS) for c0, _ in P]
    b_col = [b_all[rsl[p], NH_A + h:NH_A + h + 1] for p, (_, h) in enumerate(P)]
    b_row = [b_t[NH_A + h:NH_A + h + 1, rsl[p]] for p, (_, h) in enumerate(P)]
    i_row = [g_t[h:h + 1, rsl[p]] for p, (_, h) in enumerate(P)]
    i_col = [g[rsl[p], h:h + 1] for p, (_, h) in enumerate(P)]
    logw = [jnp.where(causal, b_col[p] - b_row[p] + i_row[p], -jnp.inf) for p in range(nP)]
    lmax = [jnp.max(logw[p], axis=-1, keepdims=True) for p in range(nP)]
    qb = [(proj_ref[rsl[p], E_QA + h * DK_A:E_QA + (h + 1) * DK_A] * scale_a).astype(BF16)
          for p, (_, h) in enumerate(P)]
    kf = [proj_ref[rsl[p], E_KA + h * DK_A:E_KA + (h + 1) * DK_A] for p, (_, h) in enumerate(P)]
    vb = [proj_ref[rsl[p], E_VA + h * DV_A:E_VA + (h + 1) * DV_A].astype(BF16) for p, (_, h) in enumerate(P)]
    qk = [_dot(qb[p], kf[p].astype(BF16), _NT) for p in range(nP)]
    b_last = [b_col[p][CS - 1:CS, :] for p in range(nP)]
    m_prev, m_inter, m_t, m_new = [None] * nP, [None] * nP, [None] * nP, [None] * nP
    for h in range(NH_A):
        run = m_s[h:h + 1, 0:1]
        for ci in range(len(chunks)):
            p = ci * NH_A + h
            m_prev[p] = run
            m_inter[p] = b_col[p] + run
            m_t[p] = jnp.maximum(m_inter[p], lmax[p])
            run = m_t[p][CS - 1:CS, :]
            m_new[p] = run
        m_s[h:h + 1, :] = jnp.broadcast_to(run, (1, LANES))
    w = [jnp.exp(logw[p] - m_t[p]) for p in range(nP)]
    s = [qk[p] * w[p] for p in range(nP)]
    sv = [_dot(s[p].astype(BF16), vb[p]) for p in range(nP)]
    ssum = [jnp.sum(s[p], axis=-1, keepdims=True) for p in range(nP)]
    kfac = [jnp.exp(b_last[p] - b_col[p] + i_col[p] - m_new[p]) for p in range(nP)]
    kw = [kf[p] * kfac[p] for p in range(nP)]
    kwt = [_tr(kw[p]).astype(BF16) for p in range(nP)]
    kv = [_dot(kwt[p], vb[p]) for p in range(nP)]
    ksum = [jnp.sum(kw[p], axis=0, keepdims=True) for p in range(nP)]
    carry = [jnp.exp(b_last[p] + m_prev[p] - m_new[p]) for p in range(nP)]
    dec = [jnp.exp(m_inter[p] - m_t[p]) for p in range(nP)]
    floor_ = [jnp.exp(-m_t[p]) for p in range(nP)]
    c_prev, n_prev = [None] * nP, [None] * nP
    for h in range(NH_A):
        n_run = n_s[h:h + 1, :]
        c_run = c_s[h]
        for ci in range(len(chunks)):
            p = ci * NH_A + h
            c_prev[p], n_prev[p] = c_run, n_run
            c_run = carry[p] * c_run + kv[p]
            n_run = carry[p] * n_run + ksum[p]
        c_s[h] = c_run
        n_s[h:h + 1, :] = n_run
    qc = [_dot(qb[p], c_prev[p].astype(BF16)) for p in range(nP)]
    qn = [jnp.sum(qb[p].astype(F32) * n_prev[p].astype(BF16).astype(F32), axis=-1, keepdims=True) for p in range(nP)]
    num = [dec[p] * qc[p] + sv[p] for p in range(nP)]
    den = [dec[p] * qn[p] + ssum[p] for p in range(nP)]
    hh = [num[p] / jnp.maximum(jnp.abs(den[p]), floor_[p]) for p in range(nP)]
    mu = [jnp.mean(hh[p], axis=-1, keepdims=True) for p in range(nP)]
    dd = [hh[p] - mu[p] for p in range(nP)]
    var = [jnp.mean(dd[p] * dd[p], axis=-1, keepdims=True) for p in range(nP)]
    rs_ = [lax.rsqrt(var[p] + LN_EPS) for p in range(nP)]
    for p, (_, h) in enumerate(P):
        og = proj_ref[rsl[p], E_OA + h * DV_A:E_OA + (h + 1) * DV_A]
        ha = dd[p] * rs_[p] * norma_ref[h:h + 1, :] * _sigmoid(og)
        mix_ref[rsl[p], h * DV_A:(h + 1) * DV_A] = ha.astype(mix_ref.dtype)

    cout_ref[...] = c_s[...]
    nout_ref[...] = n_s[...]
    mout_ref[...] = m_s[...]

    k_win = jnp.concatenate([kh_s[...], proj_ref[:, E_KB:E_KB + NKV_B * DH_B]], axis=0)
    v_win = jnp.concatenate([vh_s[...], proj_ref[:, E_VB:E_VB + NKV_B * DH_B]], axis=0)
    k_win_b = k_win.astype(BF16)
    v_win_b = v_win.astype(BF16)
    if not hist_valid:
        key_pos = lax.broadcasted_iota(I32, (R, KW), 1) + (i * R - WINDOW)
        key_ok = key_pos >= 0
    scale_b = DH_B ** -0.5
    kgs = [k_win_b[:, kv * DH_B:(kv + 1) * DH_B] for kv in range(NKV_B)]
    vgs = [v_win_b[:, kv * DH_B:(kv + 1) * DH_B] for kv in range(NKV_B)]
    scores = []
    for hd in range(NH_B):
        qh = proj_ref[:, E_QB + hd * DH_B:E_QB + (hd + 1) * DH_B].astype(BF16)
        s = _dot(qh, kgs[hd // G_B], _NT) * scale_b + bias_ref[hd]
        if not hist_valid:
            s = jnp.where(key_ok, s, -jnp.inf)
        scores.append(s)
    probs_b = []
    for hd, s in enumerate(scores):
        sk = sink_ref[hd]
        mx = jnp.maximum(jnp.max(s, axis=-1, keepdims=True), sk)
        p = jnp.exp(s - mx)
        p = p / (jnp.sum(p, axis=-1, keepdims=True) + jnp.exp(sk - mx))
        probs_b.append(p.astype(BF16))
    for hd, p in enumerate(probs_b):
        o = _dot(p, vgs[hd // G_B])
        mix_ref[:, W_A + hd * DH_B:W_A + (hd + 1) * DH_B] = o.astype(mix_ref.dtype)

    kh_s[...] = k_win[R:, :]
    vh_s[...] = v_win[R:, :]
    kout_ref[...] = kh_s[...]
    vout_ref[...] = vh_s[...]


def _rel_bucket(rel):
    nb = NUM_BUCKETS // 2
    max_exact = nb // 2
    n = jnp.abs(rel)
    nf = jnp.maximum(n, 1).astype(F32)
    large = max_exact + (jnp.log(nf / max_exact) / math.log(MAX_DISTANCE / max_exact)
                         * (nb - max_exact)).astype(I32)
    large = jnp.minimum(large, nb - 1)
    return jnp.where(rel > 0, nb, 0) + jnp.where(n < max_exact, n, large)


def _swa_bias_table(rel_bias, rows):
    kw = WINDOW + rows
    qi = jnp.arange(rows)[:, None]
    kj = jnp.arange(kw)[None, :]
    bucket = _rel_bucket(kj - WINDOW - qi)
    rb = rel_bias.astype(F32)
    bias = jnp.zeros((NH_B, rows, kw), F32)
    for b in range(NUM_BUCKETS):
        bias = jnp.where((bucket == b)[None], rb[b][:, None, None], bias)
    lo = (qi // CHUNK) * CHUNK
    ok = (kj >= lo) & (kj < lo + WINDOW + CHUNK)
    return jnp.where(ok[None], bias, -jnp.inf)


def _even_mixer(proj, k_hist, v_hist, c0, n0, m0, bias_tab, norm_a, sink, batch, seq_len, rows, hist_valid):
    nsteps = seq_len // rows
    kw = WINDOW + rows
    m0p = jnp.broadcast_to(jnp.pad(m0, ((0, 0), (0, 8 - NH_A)))[:, :, None], (batch, 8, LANES))
    n0p = jnp.pad(n0, ((0, 0), (0, 8 - NH_A), (0, 0)))
    full3 = lambda b, i: (b, 0, 0)
    outs = pl.pallas_call(
        functools.partial(_even_kernel, rows=rows, hist_valid=hist_valid),
        out_shape=(
            jax.ShapeDtypeStruct((batch * seq_len, D_MODEL), BF16),
            jax.ShapeDtypeStruct((batch, NH_A, DK_A, DV_A), F32),
            jax.ShapeDtypeStruct((batch, 8, DK_A), F32),
            jax.ShapeDtypeStruct((batch, 8, LANES), F32),
            jax.ShapeDtypeStruct((batch, WINDOW, NKV_B * DH_B), F32),
            jax.ShapeDtypeStruct((batch, WINDOW, NKV_B * DH_B), F32),
        ),
        grid=(batch, nsteps),
        in_specs=[
            pl.BlockSpec((rows, N_EVEN_COLS), lambda b, i: (b * nsteps + i, 0)),
            pl.BlockSpec((None, WINDOW, NKV_B * DH_B), full3),
            pl.BlockSpec((None, WINDOW, NKV_B * DH_B), full3),
            pl.BlockSpec((None, NH_A, DK_A, DV_A), lambda b, i: (b, 0, 0, 0)),
            pl.BlockSpec((None, 8, DK_A), full3),
            pl.BlockSpec((None, 8, LANES), full3),
            pl.BlockSpec((NH_B, rows, kw), lambda b, i: (0, 0, 0)),
            pl.BlockSpec((NH_A, DV_A), lambda b, i: (0, 0)),
            pl.BlockSpec(memory_space=pltpu.SMEM),
        ],
        out_specs=(
            pl.BlockSpec((rows, D_MODEL), lambda b, i: (b * nsteps + i, 0)),
            pl.BlockSpec((None, NH_A, DK_A, DV_A), lambda b, i: (b, 0, 0, 0)),
            pl.BlockSpec((None, 8, DK_A), full3),
            pl.BlockSpec((None, 8, LANES), full3),
            pl.BlockSpec((None, WINDOW, NKV_B * DH_B), full3),
            pl.BlockSpec((None, WINDOW, NKV_B * DH_B), full3),
        ),
        scratch_shapes=[
            pltpu.VMEM((NH_A, DK_A, DV_A), F32),
            pltpu.VMEM((8, DK_A), F32),
            pltpu.VMEM((8, LANES), F32),
            pltpu.VMEM((WINDOW, NKV_B * DH_B), F32),
            pltpu.VMEM((WINDOW, NKV_B * DH_B), F32),
        ],
        compiler_params=_cparams(("arbitrary", "arbitrary")),
        name="even_mixer",
    )(proj, k_hist.reshape(batch, WINDOW, NKV_B * DH_B), v_hist.reshape(batch, WINDOW, NKV_B * DH_B),
      c0, n0p, m0p, bias_tab, norm_a.reshape(NH_A, DV_A), sink)
    mix, c1, n1, m1, k1, v1 = outs
    return (mix, c1, n1[:, :NH_A, :], m1[:, :NH_A, 0],
            k1.reshape(batch, WINDOW, NKV_B, DH_B), v1.reshape(batch, WINDOW, NKV_B, DH_B))


def _odd_kernel(proj_ref, conv0_ref, s0_ref, convw_ref, hp_ref, normc_ref,
                mix_ref, sout_ref, convout_ref, s_s, xa_s, *, rows):
    R = rows
    i = pl.program_id(1)
    HB = 8

    @pl.when(i == 0)
    def _():
        s_s[...] = s0_ref[...]
        xa_s[HB - (CONV_W - 1):HB, :] = conv0_ref[...]

    xa_s[HB:HB + R, :] = proj_ref[:, O_QKV:O_QKV + QKV_C]
    convout_ref[...] = xa_s[HB + R - (CONV_W - 1):HB + R, :]

    CS = CHUNK
    P2 = 2 * CS
    gcols = proj_ref[:, O_G:O_G + LANES]
    beta_all = _sigmoid(gcols)
    g_all = -jnp.exp(hp_ref[0:1, :]) * _softplus(gcols + hp_ref[1:2, :])
    rr = lax.broadcasted_iota(I32, (R, R), 0)
    cc = lax.broadcasted_iota(I32, (R, R), 1)
    tri_b = jnp.where((rr >= cc) & (rr // CS == cc // CS), 1.0, 0.0).astype(BF16)
    G_all = _dot_exact_lhs(tri_b, g_all)
    G_t = _tr(G_all)
    eG_all = jnp.exp(G_all)
    r2 = lax.broadcasted_iota(I32, (P2, P2), 0)
    c2 = lax.broadcasted_iota(I32, (P2, P2), 1)
    same_head = (r2 // CS) == (c2 // CS)
    incl2 = same_head & (r2 >= c2)
    strict2 = same_head & (r2 > c2)
    eye2 = jnp.where(r2 == c2, 1.0, 0.0)
    lane_lo = lax.broadcasted_iota(I32, (DK_C, P2), 1) < CS
    scale_c = DK_C ** -0.5

    def conv_silu(rs0, c0):
        y = xa_s[HB - 3 + rs0:HB - 3 + rs0 + CS, c0:c0 + LANES] * convw_ref[0:1, c0:c0 + LANES]
        for j in range(1, CONV_W):
            y = y + xa_s[HB - 3 + j + rs0:HB - 3 + j + rs0 + CS, c0:c0 + LANES] * convw_ref[j:j + 1, c0:c0 + LANES]
        return y * _sigmoid(y)

    def l2n(x):
        return x * lax.rsqrt(jnp.sum(x * x, axis=-1, keepdims=True) + RMS_EPS)

    def stack(a, b):
        return jnp.concatenate([a, b], axis=0)

    def inv_unit_lower_all(a_list):
        ts = [eye2 - a for a in a_list]
        abs_ = [a.astype(BF16) for a in a_list]
        ps = [_dot(ab, ab) for ab in abs_]
        nlev = (CS - 1).bit_length() - 1
        n = len(ts)
        for lvl in range(nlev):
            th = [t.astype(BF16) for t in ts]
            ph = [p.astype(BF16) for p in ps]
            if lvl < nlev - 1:
                lhs = [stack(th[i], ph[i]) for i in range(n)]
                tp = [_dot(lhs[i], ph[i]) for i in range(n)]
                ts = [ts[i] + tp[i][:P2] for i in range(n)]
                ps = [tp[i][P2:] for i in range(n)]
            else:
                tp = [_dot(th[i], ph[i]) for i in range(n)]
                ts = [ts[i] + tp[i] for i in range(n)]
        return ts

    npair = NH_C // 2
    probs = [(c0, pr) for c0 in range(0, R, CS) for pr in range(npair)]
    nP = len(probs)
    rsl = [slice(c0, c0 + CS) for c0, _ in probs]
    hab = [(2 * pr, 2 * pr + 1) for _, pr in probs]

    def per_head(fn):
        return [[fn(p, h) for h in hab[p]] for p in range(nP)]

    def col2(arr, off):
        return [stack(arr[rsl[p], off + hab[p][0]:off + hab[p][0] + 1], arr[rsl[p], off + hab[p][1]:off + hab[p][1] + 1])
                for p in range(nP)]

    kc = per_head(lambda p, h: conv_silu(probs[p][0], NH_C * DK_C + h * DK_C))
    qc = per_head(lambda p, h: conv_silu(probs[p][0], h * DK_C))
    vc = per_head(lambda p, h: conv_silu(probs[p][0], 2 * NH_C * DK_C + h * DV_C))
    kss = [[jnp.sum(x * x, axis=-1, keepdims=True) for x in kc[p]] for p in range(nP)]
    qss = [[jnp.sum(x * x, axis=-1, keepdims=True) for x in qc[p]] for p in range(nP)]
    k2 = [stack(*[x * lax.rsqrt(ss + RMS_EPS) for x, ss in zip(kc[p], kss[p])]) for p in range(nP)]
    q2b = [stack(*[x * lax.rsqrt(ss + RMS_EPS) * scale_c for x, ss in zip(qc[p], qss[p])]).astype(BF16)
           for p in range(nP)]
    v2 = [stack(*vc[p]) for p in range(nP)]
    k2b = [k2[p].astype(BF16) for p in range(nP)]
    G_col = col2(G_all, NH_C)
    G_row = [jnp.concatenate([G_t[NH_C + hab[p][0]:NH_C + hab[p][0] + 1, rsl[p]],
                              G_t[NH_C + hab[p][1]:NH_C + hab[p][1] + 1, rsl[p]]], axis=1) for p in range(nP)]
    beta = col2(beta_all, 0)
    eg = col2(eG_all, NH_C)
    GL = [[G_all[probs[p][0] + CS - 1:probs[p][0] + CS, NH_C + h:NH_C + h + 1] for h in hab[p]] for p in range(nP)]
    GL2 = [stack(jnp.broadcast_to(GL[p][0], (CS, 1)), jnp.broadcast_to(GL[p][1], (CS, 1))) for p in range(nP)]
    egl = [[jnp.exp(x) for x in GL[p]] for p in range(nP)]
    kk = [_dot(k2b[p], k2b[p], _NT) for p in range(nP)]
    qk = [_dot(q2b[p], k2b[p], _NT) for p in range(nP)]
    dmat = [jnp.exp(jnp.where(incl2, G_col[p] - G_row[p], -jnp.inf)) for p in range(nP)]
    a_mat = [jnp.where(strict2, beta[p] * dmat[p] * kk[p], 0.0) for p in range(nP)]
    attn = [(qk[p] * dmat[p]).astype(BF16) for p in range(nP)]
    rhs = [jnp.concatenate([beta[p] * v2[p], (beta[p] * eg[p]) * k2[p]], axis=1).astype(BF16) for p in range(nP)]
    kdec = [k2[p] * jnp.exp(GL2[p] - G_col[p]) for p in range(nP)]
    kdec_t = [_tr(kdec[p]).astype(BF16) for p in range(nP)]
    zero_t = jnp.zeros((DK_C, P2), BF16)
    kdec_lo = [jnp.where(lane_lo, kdec_t[p], zero_t) for p in range(nP)]
    kdec_hi = [jnp.where(lane_lo, zero_t, kdec_t[p]) for p in range(nP)]
    tinv = [t.astype(BF16) for t in inv_unit_lower_all(a_mat)]
    uw = [_dot(tinv[p], rhs[p]) for p in range(nP)]
    wq_lhs = [[stack(uw[p][j * CS:(j + 1) * CS, DV_C:].astype(BF16), q2b[p][j * CS:(j + 1) * CS]) for j in range(2)]
              for p in range(nP)]

    S_cur = [s_s[h] for h in range(NH_C)]
    for ci, c0 in enumerate(range(0, R, CS)):
        rs = slice(c0, c0 + CS)
        ps_ = list(range(ci * npair, (ci + 1) * npair))
        Sb = [S_cur[h].astype(BF16) for h in range(NH_C)]
        wq = [[_dot(wq_lhs[p][j], Sb[hab[p][j]]) for j in range(2)] for p in ps_]
        db = [(uw[p][:, :DV_C] - stack(wq[i_][0][:CS], wq[i_][1][:CS])).astype(BF16) for i_, p in enumerate(ps_)]
        upd_lo = [_dot(kdec_lo[p], db[i_]) for i_, p in enumerate(ps_)]
        upd_hi = [_dot(kdec_hi[p], db[i_]) for i_, p in enumerate(ps_)]
        for i_, p in enumerate(ps_):
            S_cur[hab[p][0]] = egl[p][0] * S_cur[hab[p][0]] + upd_lo[i_]
            S_cur[hab[p][1]] = egl[p][1] * S_cur[hab[p][1]] + upd_hi[i_]
        ad = [_dot(attn[p], db[i_]) for i_, p in enumerate(ps_)]
        o2 = [eg[p] * stack(wq[i_][0][CS:], wq[i_][1][CS:]) + ad[i_] for i_, p in enumerate(ps_)]
        oh = [o2[i_][j * CS:(j + 1) * CS] for i_ in range(npair) for j in range(2)]
        ms = [jnp.mean(o * o, axis=-1, keepdims=True) for o in oh]
        zs = [proj_ref[rs, O_Z + h * DV_C:O_Z + (h + 1) * DV_C] for h in range(NH_C)]
        zg = [z * _sigmoid(z) for z in zs]
        for h in range(NH_C):
            o = oh[h] * lax.rsqrt(ms[h] + RMS_EPS) * normc_ref[...] * zg[h]
            mix_ref[rs, h * DV_C:(h + 1) * DV_C] = o.astype(mix_ref.dtype)
    for h in range(NH_C):
        s_s[h] = S_cur[h]

    xa_s[HB - (CONV_W - 1):HB, :] = xa_s[HB + R - (CONV_W - 1):HB + R, :]
    sout_ref[...] = s_s[...]


def _odd_mixer(proj, conv_hist, s0, conv_w, a_log, dt_bias, norm_c, batch, seq_len, rows):
    nsteps = seq_len // rows
    hp = jnp.zeros((8, LANES), F32)
    hp = hp.at[0, NH_C:2 * NH_C].set(a_log.astype(F32)).at[1, NH_C:2 * NH_C].set(dt_bias.astype(F32))
    outs = pl.pallas_call(
        functools.partial(_odd_kernel, rows=rows),
        out_shape=(
            jax.ShapeDtypeStruct((batch * seq_len, D_MODEL), BF16),
            jax.ShapeDtypeStruct((batch, NH_C, DK_C, DV_C), F32),
            jax.ShapeDtypeStruct((batch, CONV_W - 1, QKV_C), F32),
        ),
        grid=(batch, nsteps),
        in_specs=[
            pl.BlockSpec((rows, N_ODD_COLS), lambda b, i: (b * nsteps + i, 0)),
            pl.BlockSpec((None, CONV_W - 1, QKV_C), lambda b, i: (b, 0, 0)),
            pl.BlockSpec((None, NH_C, DK_C, DV_C), lambda b, i: (b, 0, 0, 0)),
            pl.BlockSpec((CONV_W, QKV_C), lambda b, i: (0, 0)),
            pl.BlockSpec((8, LANES), lambda b, i: (0, 0)),
            pl.BlockSpec((1, DV_C), lambda b, i: (0, 0)),
        ],
        out_specs=(
            pl.BlockSpec((rows, D_MODEL), lambda b, i: (b * nsteps + i, 0)),
            pl.BlockSpec((None, NH_C, DK_C, DV_C), lambda b, i: (b, 0, 0, 0)),
            pl.BlockSpec((None, CONV_W - 1, QKV_C), lambda b, i: (b, 0, 0)),
        ),
        scratch_shapes=[
            pltpu.VMEM((NH_C, DK_C, DV_C), F32),
            pltpu.VMEM((8 + rows, QKV_C), F32),
        ],
        compiler_params=_cparams(("arbitrary", "arbitrary")),
        name="odd_mixer",
    )(proj, conv_hist, s0, conv_w, hp, norm_c.reshape(1, DV_C))
    return outs


def _post_kernel(mix_ref, x_ref, gate_ref, sc_ref, sh_ref, wout_ref, lng_ref, lnb_ref, wr_ref, br_ref,
                 x1_ref, h2_ref, ri_ref, rg_ref, seg_ref, cnt_ref, carry_s, *, tm):
    i = pl.program_id(0)

    @pl.when(i == 0)
    def _():
        carry_s[...] = jnp.zeros_like(carry_s)

    y = _dot(mix_ref[...], wout_ref[...])
    x1 = _layer_norm_rows(ALPHA_DN * x_ref[...] + (1.0 + gate_ref[...]) * y, lng_ref[...], lnb_ref[...])
    x1_ref[...] = x1
    h2 = x1 * (1.0 + sc_ref[...]) + sh_ref[...]
    h2b = h2.astype(BF16)
    h2_ref[...] = h2b
    lt = _dot(wr_ref[...].astype(BF16), h2b, _NT) + br_ref[:, 0:1]
    e_iota = lax.broadcasted_iota(I32, (N_EXPERTS, tm), 0).astype(F32)
    vals, idxs = [], []
    for _ in range(TOP_K):
        mx = jnp.max(lt, axis=0, keepdims=True)
        idx = jnp.min(jnp.where(lt == mx, e_iota, float(N_EXPERTS)), axis=0, keepdims=True)
        vals.append(mx)
        idxs.append(idx)
        lt = jnp.where(e_iota == idx, -jnp.inf, lt)
    ex = [jnp.exp(v - vals[0]) for v in vals]
    tot = ex[0] + ex[1] + ex[2] + ex[3]
    hot = [jnp.where(e_iota == idx, 1.0, 0.0) for idx in idxs]
    m_all = hot[0] + hot[1] + hot[2] + hot[3]
    m_all_b = m_all.astype(BF16)
    r = lax.broadcasted_iota(I32, (tm, tm), 0)
    c = lax.broadcasted_iota(I32, (tm, tm), 1)
    upper = jnp.where(r < c, 1.0, 0.0).astype(BF16)
    prefix = _dot(m_all_b, upper)
    re = lax.broadcasted_iota(I32, (N_EXPERTS, N_EXPERTS), 0)
    ce = lax.broadcasted_iota(I32, (N_EXPERTS, N_EXPERTS), 1)
    cnt_col = jnp.sum(m_all, axis=1, keepdims=True)
    m8_col = jnp.floor((cnt_col + 7.0) * 0.125)
    lower_e = jnp.where(re > ce, 1.0, 0.0).astype(BF16)
    off8_col = 8.0 * _dot(lower_e, jnp.broadcast_to(m8_col, (N_EXPERTS, LANES)).astype(BF16))[:, 0:1]
    base = off8_col + prefix
    dests = [jnp.sum(hk * base, axis=0, keepdims=True) for hk in hot]
    cnt_row = _dot(jnp.ones((8, tm), BF16), m_all_b, _NT)
    m8_row = jnp.floor((cnt_row + 7.0) * 0.125)
    upper_e = jnp.where(re < ce, 1.0, 0.0).astype(BF16)
    off8_row = 8.0 * _dot(m8_row.astype(BF16), upper_e)
    gc = carry_s[:, 0:N_EXPERTS]
    srow = lax.broadcasted_iota(I32, (8, N_EXPERTS), 0)
    seg = jnp.where(srow == 0, 8.0 * m8_row, jnp.where(srow == 1, off8_row, jnp.where(srow == 2, gc, 0.0)))
    seg_ref[...] = jnp.concatenate([seg, jnp.zeros((8, LANES - N_EXPERTS), F32)], axis=1).astype(I32)
    carry_s[:, 0:N_EXPERTS] = gc + 8.0 * m8_row
    cnt_ref[...] = carry_s[...]
    ri_ref[...] = jnp.concatenate(dests + [jnp.zeros((4, tm), F32)], axis=0).astype(I32)
    rg_ref[...] = jnp.concatenate([e / tot for e in ex] + [jnp.zeros((4, tm), F32)], axis=0)


def _post(mix, x2d, gate, scale, shift, w_out_bf16, ln_g, ln_b, wr_t, br, seq_len, tm):
    t = x2d.shape[0]
    per = seq_len // tm
    vec = lambda i: (i // per, 0, 0)
    const2 = lambda i: (0, 0)
    return pl.pallas_call(
        functools.partial(_post_kernel, tm=tm),
        out_shape=(
            jax.ShapeDtypeStruct((t, D_MODEL), F32),
            jax.ShapeDtypeStruct((t, D_MODEL), BF16),
            jax.ShapeDtypeStruct((t // tm, 8, tm), I32),
            jax.ShapeDtypeStruct((t // tm, 8, tm), F32),
            jax.ShapeDtypeStruct((t // tm, 8, LANES), I32),
            jax.ShapeDtypeStruct((8, LANES), F32),
        ),
        grid=(t // tm,),
        in_specs=[
            pl.BlockSpec((tm, D_MODEL), lambda i: (i, 0)),
            pl.BlockSpec((tm, D_MODEL), lambda i: (i, 0)),
            pl.BlockSpec((None, 1, D_MODEL), vec),
            pl.BlockSpec((None, 1, D_MODEL), vec),
            pl.BlockSpec((None, 1, D_MODEL), vec),
            pl.BlockSpec((D_MODEL, D_MODEL), const2),
            pl.BlockSpec((1, D_MODEL), const2),
            pl.BlockSpec((1, D_MODEL), const2),
            pl.BlockSpec((N_EXPERTS, D_MODEL), const2),
            pl.BlockSpec((N_EXPERTS, LANES), const2),
        ],
        out_specs=(
            pl.BlockSpec((tm, D_MODEL), lambda i: (i, 0)),
            pl.BlockSpec((tm, D_MODEL), lambda i: (i, 0)),
            pl.BlockSpec((None, 8, tm), lambda i: (i, 0, 0)),
            pl.BlockSpec((None, 8, tm), lambda i: (i, 0, 0)),
            pl.BlockSpec((None, 8, LANES), lambda i: (i, 0, 0)),
            pl.BlockSpec((8, LANES), const2),
        ),
        scratch_shapes=[pltpu.VMEM((8, LANES), F32)],
        compiler_params=_cparams(("arbitrary",)),
        name="post_router",
    )(mix, x2d, gate, scale, shift, w_out_bf16, ln_g, ln_b, wr_t, br)


SEG_FIELDS = 3


def _run_sizes(tm):
    top = 1 << (tm - 1).bit_length()
    return [s for s in (512, 256, 128, 64, 32, 16, 8) if s <= max(top, 8)]


def _for_each_run_piece(seg_ref, rstart_ref, tile, tm, fn):
    base = tile * (SEG_FIELDS * N_EXPERTS)

    def body(e, c):
        n8 = seg_ref[base + e]
        src = seg_ref[base + N_EXPERTS + e]
        dst = rstart_ref[e] + seg_ref[base + 2 * N_EXPERTS + e]
        done = jnp.int32(0)
        for size in _run_sizes(tm):
            @pl.when((n8 & size) != 0)
            def _(done=done, size=size):
                fn(pl.multiple_of(src + done, 8), pl.multiple_of(dst + done, 8), size)
            done = done + (n8 & size)
        return c

    lax.fori_loop(0, N_EXPERTS, body, 0)


def _dispatch_kernel(zblk_ref, rstart_ref, seg_ref, ri_ref, h_ref, xs_ref, zero_s, cbuf, sem, zsem,
                     *, tm, bm, cb, ntiles):
    i = pl.program_id(0)
    slot = i % 2

    def run_copies(tile, sl, wait):
        def piece(src, dst, size):
            cp = pltpu.make_async_copy(cbuf.at[sl, pl.ds(src, size), :], xs_ref.at[pl.ds(dst, size), :], sem.at[sl])
            if wait:
                cp.wait()
            else:
                cp.start()
        _for_each_run_piece(seg_ref, rstart_ref, tile, tm, piece)

    def zero_copy(e):
        return pltpu.make_async_copy(zero_s, xs_ref.at[pl.ds(zblk_ref[e] * bm, bm), :], zsem)

    @pl.when(i == 0)
    def _():
        zero_s[...] = jnp.zeros_like(zero_s)

        def zstart(e, c):
            @pl.when(zblk_ref[e] >= 0)
            def _():
                zero_copy(e).start()
            return c

        def zwait(e, c):
            @pl.when(zblk_ref[e] >= 0)
            def _():
                zero_copy(e).wait()
            return c

        lax.fori_loop(0, N_EXPERTS, zstart, 0)
        lax.fori_loop(0, N_EXPERTS, zwait, 0)

    @pl.when(i >= 2)
    def _():
        run_copies(i - 2, slot, True)

    rows = lax.broadcasted_iota(I32, (cb, tm), 0)
    hit = rows == ri_ref[0:1, :]
    for k in range(1, TOP_K):
        hit = hit | (rows == ri_ref[k:k + 1, :])
    perm = jnp.where(hit, 1.0, 0.0).astype(BF16)
    cbuf[slot] = _dot(perm, h_ref[...])
    run_copies(i, slot, False)

    @pl.when(i == ntiles - 1)
    def _():
        if ntiles > 1:
            run_copies(i - 1, 1 - slot, True)
        run_copies(i, slot, True)


def _dispatch(h2, route_i, seg_flat, row_start, zblk, n_rows, tm, bm):
    t = h2.shape[0]
    ntiles = t // tm
    cb = TOP_K * tm + 8 * N_EXPERTS
    return pl.pallas_call(
        functools.partial(_dispatch_kernel, tm=tm, bm=bm, cb=cb, ntiles=ntiles),
        out_shape=jax.ShapeDtypeStruct((n_rows, D_MODEL), F32),
        grid_spec=pltpu.PrefetchScalarGridSpec(
            num_scalar_prefetch=3,
            grid=(ntiles,),
            in_specs=[
                pl.BlockSpec((None, 2 * TOP_K, tm), lambda i, z, r, s: (i, 0, 0)),
                pl.BlockSpec((tm, D_MODEL), lambda i, z, r, s: (i, 0)),
            ],
            out_specs=pl.BlockSpec(memory_space=pl.ANY),
            scratch_shapes=[
                pltpu.VMEM((bm, D_MODEL), F32),
                pltpu.VMEM((2, cb, D_MODEL), F32),
                pltpu.SemaphoreType.DMA((2,)),
                pltpu.SemaphoreType.DMA,
            ],
        ),
        compiler_params=_cparams(("arbitrary",)),
        name="dispatch",
    )(zblk, row_start, seg_flat, route_i, h2)


def _ffn_kernel(be_ref, nv_ref, x_ref, w1_ref, b1_ref, w2_ref, b2_ref, y_ref, w1_s, w2_s):
    b = pl.program_id(0)

    @pl.when(b < nv_ref[0])
    def _():
        @pl.when((b == 0) | (be_ref[b] != be_ref[jnp.maximum(b - 1, 0)]))
        def _():
            step = 256
            for r0 in range(0, D_MODEL, step):
                w1_s[r0:r0 + step, :] = w1_ref[r0:r0 + step, :].astype(BF16)
            for r0 in range(0, D_FF, step):
                w2_s[r0:r0 + step, :] = w2_ref[r0:r0 + step, :].astype(BF16)

        xb = x_ref[...].astype(BF16)
        glu = _dot(xb, w1_s[:, :D_FF]) + b1_ref[:, :D_FF]
        lin = _dot(xb, w1_s[:, D_FF:]) + b1_ref[:, D_FF:]
        glu = jnp.minimum(glu, SWIGLU_LIMIT)
        lin = jnp.clip(lin, -SWIGLU_LIMIT, SWIGLU_LIMIT)
        act = glu * _sigmoid(SWIGLU_ALPHA * glu) * (lin + 1.0)
        y_ref[...] = _dot(act.astype(BF16), w2_s[...]) + b2_ref[...]


def _ffn(xs, block_e, n_valid, layer, w1, b1, w2, b2, bm):
    nb = xs.shape[0] // bm
    row = lambda b, be, nv: (jnp.minimum(b, nv[0] - 1), 0)
    wsel = lambda b, be, nv: (layer, be[b], 0, 0)
    return pl.pallas_call(
        _ffn_kernel,
        out_shape=jax.ShapeDtypeStruct(xs.shape, F32),
        grid_spec=pltpu.PrefetchScalarGridSpec(
            num_scalar_prefetch=2,
            grid=(nb,),
            in_specs=[
                pl.BlockSpec((bm, D_MODEL), row),
                pl.BlockSpec((None, None, D_MODEL, 2 * D_FF), wsel),
                pl.BlockSpec((None, None, 1, 2 * D_FF), wsel),
                pl.BlockSpec((None, None, D_FF, D_MODEL), wsel),
                pl.BlockSpec((None, None, 1, D_MODEL), wsel),
            ],
            out_specs=pl.BlockSpec((bm, D_MODEL), row),
            scratch_shapes=[pltpu.VMEM((D_MODEL, 2 * D_FF), BF16), pltpu.VMEM((D_FF, D_MODEL), BF16)],
        ),
        compiler_params=_cparams(("arbitrary",)),
        name="expert_ffn",
    )(block_e, n_valid, xs, w1, b1, w2, b2)


def _combine_kernel(rstart_ref, seg_ref, ri_ref, x_ref, gate_ref, rg_ref, lng_ref, lnb_ref, ys_ref, o_ref,
                    ybuf, yb_s, wt_s, sem, *, tm, cb, ntiles):
    i = pl.program_id(0)
    slot = i % 2

    def run_copies(tile, sl, wait):
        def piece(loc, glob, size):
            cp = pltpu.make_async_copy(ys_ref.at[pl.ds(glob, size), :], ybuf.at[sl, pl.ds(loc, size), :], sem.at[sl])
            if wait:
                cp.wait()
            else:
                cp.start()
        _for_each_run_piece(seg_ref, rstart_ref, tile, tm, piece)

    @pl.when(i == 0)
    def _():
        ybuf[...] = jnp.zeros_like(ybuf)
        run_copies(0, 0, False)

    @pl.when(i + 1 < ntiles)
    def _():
        run_copies(i + 1, 1 - slot, False)

    dest_t = _tr(jnp.concatenate([ri_ref[...].astype(F32), jnp.zeros((LANES - 8, tm), F32)], axis=0))
    g_t = _tr(jnp.concatenate([rg_ref[...], jnp.zeros((LANES - 8, tm), F32)], axis=0))
    lane_step = 256
    for c0 in range(0, cb, lane_step):
        cols = (lax.broadcasted_iota(I32, (tm, lane_step), 1) + c0).astype(F32)
        w = jnp.where(cols == dest_t[:, 0:1], g_t[:, 0:1], 0.0)
        for k in range(1, TOP_K):
            w = w + jnp.where(cols == dest_t[:, k:k + 1], g_t[:, k:k + 1], 0.0)
        wt_s[:, c0:c0 + lane_step] = w.astype(BF16)

    run_copies(i, slot, True)
    row_step = 256
    for r0 in range(0, cb, row_step):
        yb_s[r0:r0 + row_step, :] = ybuf[slot, r0:r0 + row_step, :].astype(BF16)
    moe = _dot(wt_s[...], yb_s[...])
    o_ref[...] = _layer_norm_rows(ALPHA_DN * x_ref[...] + (1.0 + gate_ref[...]) * moe,
                                  lng_ref[...], lnb_ref[...])


def _combine(row_start, seg_flat, route_i, x1, gate, route_g, ln_g, ln_b, ys, seq_len, tm):
    t = x1.shape[0]
    per = seq_len // tm
    ntiles = t // tm
    cb = TOP_K * tm + 8 * N_EXPERTS
    return pl.pallas_call(
        functools.partial(_combine_kernel, tm=tm, cb=cb, ntiles=ntiles),
        out_shape=jax.ShapeDtypeStruct((t, D_MODEL), F32),
        grid_spec=pltpu.PrefetchScalarGridSpec(
            num_scalar_prefetch=2,
            grid=(ntiles,),
            in_specs=[
                pl.BlockSpec((None, 2 * TOP_K, tm), lambda i, r, s: (i, 0, 0)),
                pl.BlockSpec((tm, D_MODEL), lambda i, r, s: (i, 0)),
                pl.BlockSpec((None, 1, D_MODEL), lambda i, r, s: (i // per, 0, 0)),
                pl.BlockSpec((None, 8, tm), lambda i, r, s: (i, 0, 0)),
                pl.BlockSpec((1, D_MODEL), lambda i, r, s: (0, 0)),
                pl.BlockSpec((1, D_MODEL), lambda i, r, s: (0, 0)),
                pl.BlockSpec(memory_space=pl.ANY),
            ],
            out_specs=pl.BlockSpec((tm, D_MODEL), lambda i, r, s: (i, 0)),
            scratch_shapes=[
                pltpu.VMEM((2, cb, D_MODEL), F32),
                pltpu.VMEM((cb, D_MODEL), BF16),
                pltpu.VMEM((tm, cb), BF16),
                pltpu.SemaphoreType.DMA((2,)),
            ],
        ),
        compiler_params=_cparams(("arbitrary",)),
        name="combine",
    )(row_start, seg_flat, route_i, x1, gate, route_g, ln_g, ln_b, ys)


def _moe_tables(counts, bm, n_blocks):
    cnt = counts[0, :N_EXPERTS].astype(I32)
    nblk = (cnt + bm - 1) // bm
    blk_end = jnp.cumsum(nblk)
    row_start = (blk_end - nblk) * bm
    n_valid = blk_end[-1]
    bidx = jnp.minimum(jnp.arange(n_blocks, dtype=I32), n_valid - 1)
    block_e = jnp.sum((blk_end[None, :] <= bidx[:, None]).astype(I32), axis=1)
    block_e = jnp.minimum(block_e, N_EXPERTS - 1)
    zblk = jnp.where((cnt % bm) != 0, blk_end - 1, -1).astype(I32)
    return row_start.astype(I32), block_e, n_valid.reshape(1).astype(I32), zblk


def _moe_and_norm(h2, x1, route_i, route_g, seg, counts, gate, ln_g, ln_b, layer, w1, b1, w2, b2, seq_len, tm, bm):
    t = h2.shape[0]
    ntiles = t // tm
    n_blocks = -(-(t * TOP_K + 7 * N_EXPERTS * ntiles) // bm) + N_EXPERTS
    row_start, block_e, n_valid, zblk = _moe_tables(counts, bm, n_blocks)
    seg_flat = seg[:, :SEG_FIELDS, :N_EXPERTS].reshape(ntiles * SEG_FIELDS * N_EXPERTS)
    xs = _dispatch(h2, route_i, seg_flat, row_start, zblk, n_blocks * bm, tm, bm)
    ys = _ffn(xs, block_e, n_valid, layer, w1, b1, w2, b2, bm)
    return _combine(row_start, seg_flat, route_i, x1, gate, route_g, ln_g, ln_b, ys, seq_len, tm)


def _group_cfg(batch, seq_len):
    if seq_len >= 512:
        return dict(tm=512, rows=256, bm=512)
    return dict(tm=seq_len, rows=seq_len, bm=128)


def kernel(x_prompt, x_sample, c_prompt, c_sample, state_a_C, state_a_n, state_a_m, cache_b_k, cache_b_v,
           state_c_S, state_c_conv, w_ada, b_ada, ln_g, ln_b, w_in_even, b_in_even, norm_a, sink_b, rel_bias,
           w_out_even, w_in_odd, conv_c, a_log_c, dt_bias_c, norm_c, w_out_odd, w_router, b_router,
           w_e1, b_e1, w_e2, b_e2):
    bp, lp, _ = x_prompt.shape
    bs, ls, _ = x_sample.shape
    groups = [dict(b=bp, l=lp, x=x_prompt.reshape(bp * lp, D_MODEL), **_group_cfg(bp, lp)),
              dict(b=bs, l=ls, x=x_sample.reshape(bs * ls, D_MODEL), **_group_cfg(bs, ls))]
    mod = _ada(jnp.concatenate([c_prompt, c_sample], axis=0), w_ada, b_ada)
    offs = [0, bp]
    states = [dict(), dict()]
    b1_all = b_e1.reshape(DEPTH, N_EXPERTS, 1, 2 * D_FF)
    b2_all = b_e2.reshape(DEPTH, N_EXPERTS, 1, D_MODEL)
    for l in range(DEPTH):
        e = l // 2
        if l % 2 == 0:
            w = w_in_even[e]
            sz = (512, 512, 512, 512, 4, 4, 512, 128, 128)
            o = [sum(sz[:j]) for j in range(len(sz) + 1)]
            pad = jnp.zeros((D_MODEL, LANES - 2 * NH_A), w.dtype)
            w_in = jnp.concatenate([w[:, o[0]:o[4]], w[:, o[6]:o[9]], w[:, o[4]:o[6]], pad], axis=1).astype(BF16)
            bb = b_in_even[e]
            b_in = jnp.concatenate([bb[o[0]:o[4]], bb[o[6]:o[9]], bb[o[4]:o[6]],
                                    jnp.zeros((LANES - 2 * NH_A,), bb.dtype)]).reshape(1, N_EVEN_COLS)
            w_out = w_out_even[e].astype(BF16)
        else:
            w = w_in_odd[e]
            pad = jnp.zeros((D_MODEL, LANES - 2 * NH_C), w.dtype)
            w_in = jnp.concatenate([w, pad], axis=1).astype(BF16)
            b_in = jnp.zeros((1, N_ODD_COLS), F32)
            w_out = w_out_odd[e].astype(BF16)
        wr_t = w_router[l].T
        br = jnp.broadcast_to(b_router[l][:, None], (N_EXPERTS, LANES))
        for gi, gr in enumerate(groups):
            nb, sl = gr['b'], gr['l']
            m = mod[l, offs[gi]:offs[gi] + nb].reshape(nb, 6, 1, D_MODEL)
            m6 = [m[:, j] for j in range(6)]
            proj = _inproj(gr['x'], m6[1], m6[0], w_in, b_in, sl, gr['tm'])
            st = states[gi]
            if l % 2 == 0:
                if gi == 0:
                    kh = jnp.zeros((nb, WINDOW, NKV_B, DH_B), F32)
                    vh = kh
                    c0 = jnp.zeros((nb, NH_A, DK_A, DV_A), F32)
                    n0 = jnp.zeros((nb, NH_A, DK_A), F32)
                    m0 = jnp.zeros((nb, NH_A), F32)
                else:
                    kh, vh, c0, n0, m0 = cache_b_k[e], cache_b_v[e], state_a_C[e], state_a_n[e], state_a_m[e]
                bias_tab = _swa_bias_table(rel_bias, gr['rows'])
                mix, c1, n1, m1, k1, v1 = _even_mixer(proj, kh, vh, c0, n0, m0, bias_tab, norm_a[e], sink_b[e],
                                                      nb, sl, gr['rows'], gi == 1)
                for name, val in (('a_C', c1), ('a_n', n1), ('a_m', m1), ('b_k', k1), ('b_v', v1)):
                    st.setdefault(name, []).append(val)
            else:
                if gi == 0:
                    cv0 = jnp.zeros((nb, CONV_W - 1, QKV_C), F32)
                    s0 = jnp.zeros((nb, NH_C, DK_C, DV_C), F32)
                else:
                    cv0, s0 = state_c_conv[e], state_c_S[e]
                mix, s1, cv1 = _odd_mixer(proj, cv0, s0, conv_c[e], a_log_c[e], dt_bias_c[e], norm_c[e], nb, sl,
                                          gr['rows'])
                st.setdefault('c_S', []).append(s1)
                st.setdefault('c_conv', []).append(cv1)
            x1, h2, route_i, route_g, seg, counts = _post(
                mix, gr['x'], m6[2], m6[4], m6[3], w_out, ln_g[l, 0].reshape(1, D_MODEL),
                ln_b[l, 0].reshape(1, D_MODEL), wr_t, br, sl, gr['tm'])
            gr['x'] = _moe_and_norm(h2, x1, route_i, route_g, seg, counts, m6[5], ln_g[l, 1].reshape(1, D_MODEL),
                                    ln_b[l, 1].reshape(1, D_MODEL), l, w_e1, b1_all, w_e2, b2_all,
                                    sl, gr['tm'], gr['bm'])
    outs = [groups[0]['x'].reshape(bp, lp, D_MODEL), groups[1]['x'].reshape(bs, ls, D_MODEL)]
    for gi in range(2):
        for name in ('a_C', 'a_n', 'a_m', 'b_k', 'b_v', 'c_S', 'c_conv'):
            outs.append(jnp.stack(states[gi][name]))
    return tuple(outs)
```

```python
import functools
import math

import jax
import jax.numpy as jnp
from jax import lax
from jax.experimental import pallas as pl
from jax.experimental.pallas import tpu as pltpu

F32 = jnp.float32
BF16 = jnp.bfloat16
I32 = jnp.int32

D_MODEL = 1024
CHUNK = 64
NH_A, DK_A, DV_A = 4, 128, 128
W_A = NH_A * DV_A
NH_B, NKV_B, DH_B, WINDOW = 8, 2, 64, 128
G_B = NH_B // NKV_B
W_B = NH_B * DH_B
NUM_BUCKETS, MAX_DISTANCE = 32, 256
NH_C, DK_C, DV_C, CONV_W = 8, 128, 128, 4
QKV_C = NH_C * (2 * DK_C + DV_C)
N_EXPERTS, TOP_K, D_FF = 32, 4, 1024
SWIGLU_LIMIT, SWIGLU_ALPHA = 7.0, 1.702
DEPTH = 2
ALPHA_DN = (2 * DEPTH) ** 0.25
LN_EPS = 1e-5
RMS_EPS = 1e-6

LANES = 128
E_QA, E_KA, E_VA, E_OA, E_QB, E_KB, E_VB, E_G = 0, 512, 1024, 1536, 2048, 2560, 2688, 2816
N_EVEN_COLS = 2944
O_QKV, O_Z, O_G = 0, 3072, 4096
N_ODD_COLS = 4224

VMEM_LIMIT = 56 * 1024 * 1024


def _cparams(sem):
    return pltpu.CompilerParams(dimension_semantics=sem, vmem_limit_bytes=VMEM_LIMIT)


def _softplus(x):
    return jnp.maximum(x, 0.0) + jnp.log(1.0 + jnp.exp(-jnp.abs(x)))


def _sigmoid(x):
    return 1.0 / (1.0 + jnp.exp(-x))


def _split_bf16(a):
    hi = a.astype(BF16)
    lo = (a - hi.astype(F32)).astype(BF16)
    return hi, lo


_NN = (((1,), (0,)), ((), ()))
_NT = (((1,), (1,)), ((), ()))
_TN = (((0,), (0,)), ((), ()))


def _dot(a, b, dims=_NN):
    return lax.dot_general(a, b, dims, preferred_element_type=F32)


def _dot_x3(a, b, dims=_NN):
    ah, al = _split_bf16(a)
    bh, bl = _split_bf16(b)
    return _dot(ah, bh, dims) + _dot(ah, bl, dims) + _dot(al, bh, dims)


def _dot_exact_lhs(a_bf16, b, dims=_NN):
    b0 = b.astype(BF16)
    r1 = b - b0.astype(F32)
    b1 = r1.astype(BF16)
    b2 = (r1 - b1.astype(F32)).astype(BF16)
    return _dot(a_bf16, b0, dims) + _dot(a_bf16, b1, dims) + _dot(a_bf16, b2, dims)


def _tr(x):
    r = x.shape[0]
    rp = -(-r // LANES) * LANES
    if rp != r:
        x = jnp.concatenate([x, jnp.zeros((rp - r, x.shape[1]), x.dtype)], axis=0)
    return x.T[:, :r]


def _tri(n, strict=False):
    r = lax.broadcasted_iota(I32, (n, n), 0)
    c = lax.broadcasted_iota(I32, (n, n), 1)
    return (r > c) if strict else (r >= c)


def _layer_norm_rows(v, g, b):
    mu = jnp.mean(v, axis=-1, keepdims=True)
    d = v - mu
    var = jnp.mean(d * d, axis=-1, keepdims=True)
    return d * lax.rsqrt(var + LN_EPS) * g + b


def _ada_kernel(c_ref, w_ref, b_ref, o_ref):
    c = c_ref[...]
    a = (c * _sigmoid(c)).astype(BF16)
    o_ref[...] = _dot(a, w_ref[...].astype(BF16)) + b_ref[...]


def _ada(c_all, w_ada, b_ada):
    nb = c_all.shape[0]
    tn = 1536
    return pl.pallas_call(
        _ada_kernel,
        out_shape=jax.ShapeDtypeStruct((DEPTH, nb, 6 * D_MODEL), F32),
        grid=(DEPTH, 6 * D_MODEL // tn),
        in_specs=[
            pl.BlockSpec((nb, D_MODEL), lambda l, j: (0, 0)),
            pl.BlockSpec((None, D_MODEL, tn), lambda l, j: (l, 0, j)),
            pl.BlockSpec((None, 1, tn), lambda l, j: (l, 0, j)),
        ],
        out_specs=pl.BlockSpec((None, nb, tn), lambda l, j: (l, 0, j)),
        compiler_params=_cparams(("arbitrary", "arbitrary")),
        name="ada",
    )(c_all, w_ada, b_ada.reshape(DEPTH, 1, 6 * D_MODEL))


def _inproj_kernel(x_ref, sc_ref, sh_ref, w_ref, b_ref, o_ref, *, n_cols, col_step):
    h = (x_ref[...] * (1.0 + sc_ref[...]) + sh_ref[...]).astype(BF16)
    for c0 in range(0, n_cols, col_step):
        c1 = min(c0 + col_step, n_cols)
        o_ref[:, c0:c1] = _dot(h, w_ref[:, c0:c1]) + b_ref[:, c0:c1]


def _inproj(x2d, scale, shift, w_bf16, bias, seq_len, tm):
    t = x2d.shape[0]
    n = w_bf16.shape[1]
    per = seq_len // tm
    return pl.pallas_call(
        functools.partial(_inproj_kernel, n_cols=n, col_step=1024),
        out_shape=jax.ShapeDtypeStruct((t, n), F32),
        grid=(t // tm,),
        in_specs=[
            pl.BlockSpec((tm, D_MODEL), lambda i: (i, 0)),
            pl.BlockSpec((None, 1, D_MODEL), lambda i: (i // per, 0, 0)),
            pl.BlockSpec((None, 1, D_MODEL), lambda i: (i // per, 0, 0)),
            pl.BlockSpec((D_MODEL, n), lambda i: (0, 0)),
            pl.BlockSpec((1, n), lambda i: (0, 0)),
        ],
        out_specs=pl.BlockSpec((tm, n), lambda i: (i, 0)),
        compiler_params=_cparams(("arbitrary",)),
        name="inproj",
    )(x2d, scale, shift, w_bf16, bias)


def _even_kernel(proj_ref, kh0_ref, vh0_ref, c0_ref, n0_ref, m0_ref, bias_ref, norma_ref, sink_ref,
                 mix_ref, cout_ref, nout_ref, mout_ref, kout_ref, vout_ref,
                 c_s, n_s, m_s, kh_s, vh_s, *, rows, hist_valid):
    R = rows
    KW = WINDOW + R
    i = pl.program_id(1)

    @pl.when(i == 0)
    def _():
        c_s[...] = c0_ref[...]
        n_s[...] = n0_ref[...]
        m_s[...] = m0_ref[...]
        kh_s[...] = kh0_ref[...]
        vh_s[...] = vh0_ref[...]

    CS = CHUNK
    g = proj_ref[:, E_G:E_G + LANES]
    lf = -_softplus(-g)
    rr = lax.broadcasted_iota(I32, (R, R), 0)
    cc = lax.broadcasted_iota(I32, (R, R), 1)
    tri_b = jnp.where((rr >= cc) & (rr // CS == cc // CS), 1.0, 0.0).astype(BF16)
    b_all = _dot_exact_lhs(tri_b, lf)
    g_t = _tr(g)
    b_t = _tr(b_all)
    causal = _tri(CS)
    scale_a = DK_A ** -0.5
    chunks = list(range(0, R, CS))
    P = [(c0, h) for c0 in chunks for h in range(NH_A)]
    nP = len(P)
    rsl = [slice(c0, c0 + CS) for c0, _ in P]
    b_col = [b_all[rsl[p], NH_A + h:NH_A + h + 1] for p, (_, h) in enumerate(P)]
    b_row = [b_t[NH_A + h:NH_A + h + 1, rsl[p]] for p, (_, h) in enumerate(P)]
    i_row = [g_t[h:h + 1, rsl[p]] for p, (_, h) in enumerate(P)]
    i_col = [g[rsl[p], h:h + 1] for p, (_, h) in enumerate(P)]
    logw = [jnp.where(causal, b_col[p] - b_row[p] + i_row[p], -jnp.inf) for p in range(nP)]
    lmax = [jnp.max(logw[p], axis=-1, keepdims=True) for p in range(nP)]
    qb = [(proj_ref[rsl[p], E_QA + h * DK_A:E_QA + (h + 1) * DK_A] * scale_a).astype(BF16)
          for p, (_, h) in enumerate(P)]
    kf = [proj_ref[rsl[p], E_KA + h * DK_A:E_KA + (h + 1) * DK_A] for p, (_, h) in enumerate(P)]
    vb = [proj_ref[rsl[p], E_VA + h * DV_A:E_VA + (h + 1) * DV_A].astype(BF16) for p, (_, h) in enumerate(P)]
    qk = [_dot(qb[p], kf[p].astype(BF16), _NT) for p in range(nP)]
    b_last = [b_col[p][CS - 1:CS, :] for p in range(nP)]
    m_prev, m_inter, m_t, m_new = [None] * nP, [None] * nP, [None] * nP, [None] * nP
    for h in range(NH_A):
        run = m_s[h:h + 1, 0:1]
        for ci in range(len(chunks)):
            p = ci * NH_A + h
            m_prev[p] = run
            m_inter[p] = b_col[p] + run
            m_t[p] = jnp.maximum(m_inter[p], lmax[p])
            run = m_t[p][CS - 1:CS, :]
            m_new[p] = run
        m_s[h:h + 1, :] = jnp.broadcast_to(run, (1, LANES))
    w = [jnp.exp(logw[p] - m_t[p]) for p in range(nP)]
    s = [qk[p] * w[p] for p in range(nP)]
    sv = [_dot(s[p].astype(BF16), vb[p]) for p in range(nP)]
    ssum = [jnp.sum(s[p], axis=-1, keepdims=True) for p in range(nP)]
    kfac = [jnp.exp(b_last[p] - b_col[p] + i_col[p] - m_new[p]) for p in range(nP)]
    kw = [kf[p] * kfac[p] for p in range(nP)]
    kwt = [_tr(kw[p]).astype(BF16) for p in range(nP)]
    kv = [_dot(kwt[p], vb[p]) for p in range(nP)]
    ksum = [jnp.sum(kw[p], axis=0, keepdims=True) for p in range(nP)]
    carry = [jnp.exp(b_last[p] + m_prev[p] - m_new[p]) for p in range(nP)]
    dec = [jnp.exp(m_inter[p] - m_t[p]) for p in range(nP)]
    floor_ = [jnp.exp(-m_t[p]) for p in range(nP)]
    c_prev, n_prev = [None] * nP, [None] * nP
    for h in range(NH_A):
        n_run = n_s[h:h + 1, :]
        c_run = c_s[h]
        for ci in range(len(chunks)):
            p = ci * NH_A + h
            c_prev[p], n_prev[p] = c_run, n_run
            c_run = carry[p] * c_run + kv[p]
            n_run = carry[p] * n_run + ksum[p]
        c_s[h] = c_run
        n_s[h:h + 1, :] = n_run
    qc = [_dot(qb[p], c_prev[p].astype(BF16)) for p in range(nP)]
    qn = [jnp.sum(qb[p].astype(F32) * n_prev[p].astype(BF16).astype(F32), axis=-1, keepdims=True) for p in range(nP)]
    num = [dec[p] * qc[p] + sv[p] for p in range(nP)]
    den = [dec[p] * qn[p] + ssum[p] for p in range(nP)]
    hh = [num[p] / jnp.maximum(jnp.abs(den[p]), floor_[p]) for p in range(nP)]
    mu = [jnp.mean(hh[p], axis=-1, keepdims=True) for p in range(nP)]
    dd = [hh[p] - mu[p] for p in range(nP)]
    var = [jnp.mean(dd[p] * dd[p], axis=-1, keepdims=True) for p in range(nP)]
    rs_ = [lax.rsqrt(var[p] + LN_EPS) for p in range(nP)]
    for p, (_, h) in enumerate(P):
        og = proj_ref[rsl[p], E_OA + h * DV_A:E_OA + (h + 1) * DV_A]
        ha = dd[p] * rs_[p] * norma_ref[h:h + 1, :] * _sigmoid(og)
        mix_ref[rsl[p], h * DV_A:(h + 1) * DV_A] = ha.astype(mix_ref.dtype)

    cout_ref[...] = c_s[...]
    nout_ref[...] = n_s[...]
    mout_ref[...] = m_s[...]

    k_win = jnp.concatenate([kh_s[...], proj_ref[:, E_KB:E_KB + NKV_B * DH_B]], axis=0)
    v_win = jnp.concatenate([vh_s[...], proj_ref[:, E_VB:E_VB + NKV_B * DH_B]], axis=0)
    k_win_b = k_win.astype(BF16)
    v_win_b = v_win.astype(BF16)
    if not hist_valid:
        key_pos = lax.broadcasted_iota(I32, (R, KW), 1) + (i * R - WINDOW)
        key_ok = key_pos >= 0
    scale_b = DH_B ** -0.5
    kgs = [k_win_b[:, kv * DH_B:(kv + 1) * DH_B] for kv in range(NKV_B)]
    vgs = [v_win_b[:, kv * DH_B:(kv + 1) * DH_B] for kv in range(NKV_B)]
    scores = []
    for hd in range(NH_B):
        qh = proj_ref[:, E_QB + hd * DH_B:E_QB + (hd + 1) * DH_B].astype(BF16)
        s = _dot(qh, kgs[hd // G_B], _NT) * scale_b + bias_ref[hd]
        if not hist_valid:
            s = jnp.where(key_ok, s, -jnp.inf)
        scores.append(s)
    probs_b = []
    for hd, s in enumerate(scores):
        sk = sink_ref[hd]
        mx = jnp.maximum(jnp.max(s, axis=-1, keepdims=True), sk)
        p = jnp.exp(s - mx)
        p = p / (jnp.sum(p, axis=-1, keepdims=True) + jnp.exp(sk - mx))
        probs_b.append(p.astype(BF16))
    for hd, p in enumerate(probs_b):
        o = _dot(p, vgs[hd // G_B])
        mix_ref[:, W_A + hd * DH_B:W_A + (hd + 1) * DH_B] = o.astype(mix_ref.dtype)

    kh_s[...] = k_win[R:, :]
    vh_s[...] = v_win[R:, :]
    kout_ref[...] = kh_s[...]
    vout_ref[...] = vh_s[...]


def _rel_bucket(rel):
    nb = NUM_BUCKETS // 2
    max_exact = nb // 2
    n = jnp.abs(rel)
    nf = jnp.maximum(n, 1).astype(F32)
    large = max_exact + (jnp.log(nf / max_exact) / math.log(MAX_DISTANCE / max_exact)
                         * (nb - max_exact)).astype(I32)
    large = jnp.minimum(large, nb - 1)
    return jnp.where(rel > 0, nb, 0) + jnp.where(n < max_exact, n, large)


def _swa_bias_table(rel_bias, rows):
    kw = WINDOW + rows
    qi = jnp.arange(rows)[:, None]
    kj = jnp.arange(kw)[None, :]
    bucket = _rel_bucket(kj - WINDOW - qi)
    rb = rel_bias.astype(F32)
    bias = jnp.zeros((NH_B, rows, kw), F32)
    for b in range(NUM_BUCKETS):
        bias = jnp.where((bucket == b)[None], rb[b][:, None, None], bias)
    lo = (qi // CHUNK) * CHUNK
    ok = (kj >= lo) & (kj < lo + WINDOW + CHUNK)
    return jnp.where(ok[None], bias, -jnp.inf)


def _even_mixer(proj, k_hist, v_hist, c0, n0, m0, bias_tab, norm_a, sink, batch, seq_len, rows, hist_valid):
    nsteps = seq_len // rows
    kw = WINDOW + rows
    m0p = jnp.broadcast_to(jnp.pad(m0, ((0, 0), (0, 8 - NH_A)))[:, :, None], (batch, 8, LANES))
    n0p = jnp.pad(n0, ((0, 0), (0, 8 - NH_A), (0, 0)))
    full3 = lambda b, i: (b, 0, 0)
    outs = pl.pallas_call(
        functools.partial(_even_kernel, rows=rows, hist_valid=hist_valid),
        out_shape=(
            jax.ShapeDtypeStruct((batch * seq_len, D_MODEL), BF16),
            jax.ShapeDtypeStruct((batch, NH_A, DK_A, DV_A), F32),
            jax.ShapeDtypeStruct((batch, 8, DK_A), F32),
            jax.ShapeDtypeStruct((batch, 8, LANES), F32),
            jax.ShapeDtypeStruct((batch, WINDOW, NKV_B * DH_B), F32),
            jax.ShapeDtypeStruct((batch, WINDOW, NKV_B * DH_B), F32),
        ),
        grid=(batch, nsteps),
        in_specs=[
            pl.BlockSpec((rows, N_EVEN_COLS), lambda b, i: (b * nsteps + i, 0)),
            pl.BlockSpec((None, WINDOW, NKV_B * DH_B), full3),
            pl.BlockSpec((None, WINDOW, NKV_B * DH_B), full3),
            pl.BlockSpec((None, NH_A, DK_A, DV_A), lambda b, i: (b, 0, 0, 0)),
            pl.BlockSpec((None, 8, DK_A), full3),
            pl.BlockSpec((None, 8, LANES), full3),
            pl.BlockSpec((NH_B, rows, kw), lambda b, i: (0, 0, 0)),
            pl.BlockSpec((NH_A, DV_A), lambda b, i: (0, 0)),
            pl.BlockSpec(memory_space=pltpu.SMEM),
        ],
        out_specs=(
            pl.BlockSpec((rows, D_MODEL), lambda b, i: (b * nsteps + i, 0)),
            pl.BlockSpec((None, NH_A, DK_A, DV_A), lambda b, i: (b, 0, 0, 0)),
            pl.BlockSpec((None, 8, DK_A), full3),
            pl.BlockSpec((None, 8, LANES), full3),
            pl.BlockSpec((None, WINDOW, NKV_B * DH_B), full3),
            pl.BlockSpec((None, WINDOW, NKV_B * DH_B), full3),
        ),
        scratch_shapes=[
            pltpu.VMEM((NH_A, DK_A, DV_A), F32),
            pltpu.VMEM((8, DK_A), F32),
            pltpu.VMEM((8, LANES), F32),
            pltpu.VMEM((WINDOW, NKV_B * DH_B), F32),
            pltpu.VMEM((WINDOW, NKV_B * DH_B), F32),
        ],
        compiler_params=_cparams(("arbitrary", "arbitrary")),
        name="even_mixer",
    )(proj, k_hist.reshape(batch, WINDOW, NKV_B * DH_B), v_hist.reshape(batch, WINDOW, NKV_B * DH_B),
      c0, n0p, m0p, bias_tab, norm_a.reshape(NH_A, DV_A), sink)
    mix, c1, n1, m1, k1, v1 = outs
    return (mix, c1, n1[:, :NH_A, :], m1[:, :NH_A, 0],
            k1.reshape(batch, WINDOW, NKV_B, DH_B), v1.reshape(batch, WINDOW, NKV_B, DH_B))


def _odd_kernel(proj_ref, conv0_ref, s0_ref, convw_ref, hp_ref, normc_ref,
                mix_ref, sout_ref, convout_ref, s_s, xa_s, *, rows):
    R = rows
    i = pl.program_id(1)
    HB = 8

    @pl.when(i == 0)
    def _():
        s_s[...] = s0_ref[...]
        xa_s[HB - (CONV_W - 1):HB, :] = conv0_ref[...]

    xa_s[HB:HB + R, :] = proj_ref[:, O_QKV:O_QKV + QKV_C]
    convout_ref[...] = xa_s[HB + R - (CONV_W - 1):HB + R, :]

    CS = CHUNK
    P2 = 2 * CS
    gcols = proj_ref[:, O_G:O_G + LANES]
    beta_all = _sigmoid(gcols)
    g_all = -jnp.exp(hp_ref[0:1, :]) * _softplus(gcols + hp_ref[1:2, :])
    rr = lax.broadcasted_iota(I32, (R, R), 0)
    cc = lax.broadcasted_iota(I32, (R, R), 1)
    tri_b = jnp.where((rr >= cc) & (rr // CS == cc // CS), 1.0, 0.0).astype(BF16)
    G_all = _dot_exact_lhs(tri_b, g_all)
    G_t = _tr(G_all)
    eG_all = jnp.exp(G_all)
    r2 = lax.broadcasted_iota(I32, (P2, P2), 0)
    c2 = lax.broadcasted_iota(I32, (P2, P2), 1)
    same_head = (r2 // CS) == (c2 // CS)
    incl2 = same_head & (r2 >= c2)
    strict2 = same_head & (r2 > c2)
    eye2 = jnp.where(r2 == c2, 1.0, 0.0)
    lane_lo = lax.broadcasted_iota(I32, (DK_C, P2), 1) < CS
    scale_c = DK_C ** -0.5

    def conv_silu(rs0, c0):
        y = xa_s[HB - 3 + rs0:HB - 3 + rs0 + CS, c0:c0 + LANES] * convw_ref[0:1, c0:c0 + LANES]
        for j in range(1, CONV_W):
            y = y + xa_s[HB - 3 + j + rs0:HB - 3 + j + rs0 + CS, c0:c0 + LANES] * convw_ref[j:j + 1, c0:c0 + LANES]
        return y * _sigmoid(y)

    def l2n(x):
        return x * lax.rsqrt(jnp.sum(x * x, axis=-1, keepdims=True) + RMS_EPS)

    def stack(a, b):
        return jnp.concatenate([a, b], axis=0)

    def inv_unit_lower_all(a_list):
        ts = [eye2 - a for a in a_list]
        abs_ = [a.astype(BF16) for a in a_list]
        ps = [_dot(ab, ab) for ab in abs_]
        nlev = (CS - 1).bit_length() - 1
        n = len(ts)
        for lvl in range(nlev):
            th = [t.astype(BF16) for t in ts]
            ph = [p.astype(BF16) for p in ps]
            if lvl < nlev - 1:
                lhs = [stack(th[i], ph[i]) for i in range(n)]
                tp = [_dot(lhs[i], ph[i]) for i in range(n)]
                ts = [ts[i] + tp[i][:P2] for i in range(n)]
                ps = [tp[i][P2:] for i in range(n)]
            else:
                tp = [_dot(th[i], ph[i]) for i in range(n)]
                ts = [ts[i] + tp[i] for i in range(n)]
        return ts

    npair = NH_C // 2
    probs = [(c0, pr) for c0 in range(0, R, CS) for pr in range(npair)]
    nP = len(probs)
    rsl = [slice(c0, c0 + CS) for c0, _ in probs]
    hab = [(2 * pr, 2 * pr + 1) for _, pr in probs]

    def per_head(fn):
        return [[fn(p, h) for h in hab[p]] for p in range(nP)]

    def col2(arr, off):
        return [stack(arr[rsl[p], off + hab[p][0]:off + hab[p][0] + 1], arr[rsl[p], off + hab[p][1]:off + hab[p][1] + 1])
                for p in range(nP)]

    kc = per_head(lambda p, h: conv_silu(probs[p][0], NH_C * DK_C + h * DK_C))
    qc = per_head(lambda p, h: conv_silu(probs[p][0], h * DK_C))
    vc = per_head(lambda p, h: conv_silu(probs[p][0], 2 * NH_C * DK_C + h * DV_C))
    kss = [[jnp.sum(x * x, axis=-1, keepdims=True) for x in kc[p]] for p in range(nP)]
    qss = [[jnp.sum(x * x, axis=-1, keepdims=True) for x in qc[p]] for p in range(nP)]
    k2 = [stack(*[x * lax.rsqrt(ss + RMS_EPS) for x, ss in zip(kc[p], kss[p])]) for p in range(nP)]
    q2b = [stack(*[x * lax.rsqrt(ss + RMS_EPS) * scale_c for x, ss in zip(qc[p], qss[p])]).astype(BF16)
           for p in range(nP)]
    v2 = [stack(*vc[p]) for p in range(nP)]
    k2b = [k2[p].astype(BF16) for p in range(nP)]
    G_col = col2(G_all, NH_C)
    G_row = [jnp.concatenate([G_t[NH_C + hab[p][0]:NH_C + hab[p][0] + 1, rsl[p]],
                              G_t[NH_C + hab[p][1]:NH_C + hab[p][1] + 1, rsl[p]]], axis=1) for p in range(nP)]
    beta = col2(beta_all, 0)
    eg = col2(eG_all, NH_C)
    GL = [[G_all[probs[p][0] + CS - 1:probs[p][0] + CS, NH_C + h:NH_C + h + 1] for h in hab[p]] for p in range(nP)]
    GL2 = [stack(jnp.broadcast_to(GL[p][0], (CS, 1)), jnp.broadcast_to(GL[p][1], (CS, 1))) for p in range(nP)]
    egl = [[jnp.exp(x) for x in GL[p]] for p in range(nP)]
    kk = [_dot(k2b[p], k2b[p], _NT) for p in range(nP)]
    qk = [_dot(q2b[p], k2b[p], _NT) for p in range(nP)]
    dmat = [jnp.exp(jnp.where(incl2, G_col[p] - G_row[p], -jnp.inf)) for p in range(nP)]
    a_mat = [jnp.where(strict2, beta[p] * dmat[p] * kk[p], 0.0) for p in range(nP)]
    attn = [(qk[p] * dmat[p]).astype(BF16) for p in range(nP)]
    rhs = [jnp.concatenate([beta[p] * v2[p], (beta[p] * eg[p]) * k2[p]], axis=1).astype(BF16) for p in range(nP)]
    kdec = [k2[p] * jnp.exp(GL2[p] - G_col[p]) for p in range(nP)]
    kdec_t = [_tr(kdec[p]).astype(BF16) for p in range(nP)]
    zero_t = jnp.zeros((DK_C, P2), BF16)
    kdec_lo = [jnp.where(lane_lo, kdec_t[p], zero_t) for p in range(nP)]
    kdec_hi = [jnp.where(lane_lo, zero_t, kdec_t[p]) for p in range(nP)]
    tinv = [t.astype(BF16) for t in inv_unit_lower_all(a_mat)]
    uw = [_dot(tinv[p], rhs[p]) for p in range(nP)]
    wq_lhs = [[stack(uw[p][j * CS:(j + 1) * CS, DV_C:].astype(BF16), q2b[p][j * CS:(j + 1) * CS]) for j in range(2)]
              for p in range(nP)]

    S_cur = [s_s[h] for h in range(NH_C)]
    for ci, c0 in enumerate(range(0, R, CS)):
        rs = slice(c0, c0 + CS)
        ps_ = list(range(ci * npair, (ci + 1) * npair))
        Sb = [S_cur[h].astype(BF16) for h in range(NH_C)]
        wq = [[_dot(wq_lhs[p][j], Sb[hab[p][j]]) for j in range(2)] for p in ps_]
        db = [(uw[p][:, :DV_C] - stack(wq[i_][0][:CS], wq[i_][1][:CS])).astype(BF16) for i_, p in enumerate(ps_)]
        upd_lo = [_dot(kdec_lo[p], db[i_]) for i_, p in enumerate(ps_)]
        upd_hi = [_dot(kdec_hi[p], db[i_]) for i_, p in enumerate(ps_)]
        for i_, p in enumerate(ps_):
            S_cur[hab[p][0]] = egl[p][0] * S_cur[hab[p][0]] + upd_lo[i_]
            S_cur[hab[p][1]] = egl[p][1] * S_cur[hab[p][1]] + upd_hi[i_]
        ad = [_dot(attn[p], db[i_]) for i_, p in enumerate(ps_)]
        o2 = [eg[p] * stack(wq[i_][0][CS:], wq[i_][1][CS:]) + ad[i_] for i_, p in enumerate(ps_)]
        oh = [o2[i_][j * CS:(j + 1) * CS] for i_ in range(npair) for j in range(2)]
        ms = [jnp.mean(o * o, axis=-1, keepdims=True) for o in oh]
        zs = [proj_ref[rs, O_Z + h * DV_C:O_Z + (h + 1) * DV_C] for h in range(NH_C)]
        zg = [z * _sigmoid(z) for z in zs]
        for h in range(NH_C):
            o = oh[h] * lax.rsqrt(ms[h] + RMS_EPS) * normc_ref[...] * zg[h]
            mix_ref[rs, h * DV_C:(h + 1) * DV_C] = o.astype(mix_ref.dtype)
    for h in range(NH_C):
        s_s[h] = S_cur[h]

    xa_s[HB - (CONV_W - 1):HB, :] = xa_s[HB + R - (CONV_W - 1):HB + R, :]
    sout_ref[...] = s_s[...]


def _odd_mixer(proj, conv_hist, s0, conv_w, a_log, dt_bias, norm_c, batch, seq_len, rows):
    nsteps = seq_len // rows
    hp = jnp.zeros((8, LANES), F32)
    hp = hp.at[0, NH_C:2 * NH_C].set(a_log.astype(F32)).at[1, NH_C:2 * NH_C].set(dt_bias.astype(F32))
    outs = pl.pallas_call(
        functools.partial(_odd_kernel, rows=rows),
        out_shape=(
            jax.ShapeDtypeStruct((batch * seq_len, D_MODEL), BF16),
            jax.ShapeDtypeStruct((batch, NH_C, DK_C, DV_C), F32),
            jax.ShapeDtypeStruct((batch, CONV_W - 1, QKV_C), F32),
        ),
        grid=(batch, nsteps),
        in_specs=[
            pl.BlockSpec((rows, N_ODD_COLS), lambda b, i: (b * nsteps + i, 0)),
            pl.BlockSpec((None, CONV_W - 1, QKV_C), lambda b, i: (b, 0, 0)),
            pl.BlockSpec((None, NH_C, DK_C, DV_C), lambda b, i: (b, 0, 0, 0)),
            pl.BlockSpec((CONV_W, QKV_C), lambda b, i: (0, 0)),
            pl.BlockSpec((8, LANES), lambda b, i: (0, 0)),
            pl.BlockSpec((1, DV_C), lambda b, i: (0, 0)),
        ],
        out_specs=(
            pl.BlockSpec((rows, D_MODEL), lambda b, i: (b * nsteps + i, 0)),
            pl.BlockSpec((None, NH_C, DK_C, DV_C), lambda b, i: (b, 0, 0, 0)),
            pl.BlockSpec((None, CONV_W - 1, QKV_C), lambda b, i: (b, 0, 0)),
        ),
        scratch_shapes=[
            pltpu.VMEM((NH_C, DK_C, DV_C), F32),
            pltpu.VMEM((8 + rows, QKV_C), F32),
        ],
        compiler_params=_cparams(("arbitrary", "arbitrary")),
        name="odd_mixer",
    )(proj, conv_hist, s0, conv_w, hp, norm_c.reshape(1, DV_C))
    return outs


def _post_kernel(mix_ref, x_ref, gate_ref, sc_ref, sh_ref, wout_ref, lng_ref, lnb_ref, wr_ref, br_ref,
                 x1_ref, h2_ref, ri_ref, rg_ref, seg_ref, cnt_ref, carry_s, *, tm):
    i = pl.program_id(0)

    @pl.when(i == 0)
    def _():
        carry_s[...] = jnp.zeros_like(carry_s)

    ng = max(1, tm // 256)
    tg = tm // ng
    ys = [_dot(mix_ref[q * tg:(q + 1) * tg, :], wout_ref[...]) for q in range(min(2, ng))]
    wr_b = wr_ref[...].astype(BF16)
    lts = []
    for q in range(ng):
        rq = slice(q * tg, (q + 1) * tg)
        x1 = _layer_norm_rows(ALPHA_DN * x_ref[rq, :] + (1.0 + gate_ref[...]) * ys[q], lng_ref[...], lnb_ref[...])
        if q + 2 < ng:
            ys.append(_dot(mix_ref[(q + 2) * tg:(q + 3) * tg, :], wout_ref[...]))
        x1_ref[rq, :] = x1
        h2 = x1 * (1.0 + sc_ref[...]) + sh_ref[...]
        h2b = h2.astype(BF16)
        h2_ref[rq, :] = h2b
        lts.append(_dot(wr_b, h2b, _NT))
    lt = (lts[0] if ng == 1 else jnp.concatenate(lts, axis=1)) + br_ref[:, 0:1]
    e_iota = lax.broadcasted_iota(I32, (N_EXPERTS, tm), 0).astype(F32)
    vals, idxs = [], []
    for _ in range(TOP_K):
        mx = jnp.max(lt, axis=0, keepdims=True)
        idx = jnp.min(jnp.where(lt == mx, e_iota, float(N_EXPERTS)), axis=0, keepdims=True)
        vals.append(mx)
        idxs.append(idx)
        lt = jnp.where(e_iota == idx, -jnp.inf, lt)
    ex = [jnp.exp(v - vals[0]) for v in vals]
    tot = ex[0] + ex[1] + ex[2] + ex[3]
    hot = [jnp.where(e_iota == idx, 1.0, 0.0) for idx in idxs]
    m_all = hot[0] + hot[1] + hot[2] + hot[3]
    m_all_b = m_all.astype(BF16)
    r = lax.broadcasted_iota(I32, (tm, tm), 0)
    c = lax.broadcasted_iota(I32, (tm, tm), 1)
    upper = jnp.where(r < c, 1.0, 0.0).astype(BF16)
    prefix = _dot(m_all_b, upper)
    re = lax.broadcasted_iota(I32, (N_EXPERTS, N_EXPERTS), 0)
    ce = lax.broadcasted_iota(I32, (N_EXPERTS, N_EXPERTS), 1)
    cnt_col = jnp.sum(m_all, axis=1, keepdims=True)
    m8_col = jnp.floor((cnt_col + 7.0) * 0.125)
    lower_e = jnp.where(re > ce, 1.0, 0.0).astype(BF16)
    off8_col = 8.0 * _dot(lower_e, jnp.broadcast_to(m8_col, (N_EXPERTS, LANES)).astype(BF16))[:, 0:1]
    base = off8_col + prefix
    dests = [jnp.sum(hk * base, axis=0, keepdims=True) for hk in hot]
    cnt_row = _dot(jnp.ones((8, tm), BF16), m_all_b, _NT)
    m8_row = jnp.floor((cnt_row + 7.0) * 0.125)
    upper_e = jnp.where(re < ce, 1.0, 0.0).astype(BF16)
    off8_row = 8.0 * _dot(m8_row.astype(BF16), upper_e)
    gc = carry_s[:, 0:N_EXPERTS]
    srow = lax.broadcasted_iota(I32, (8, N_EXPERTS), 0)
    seg = jnp.where(srow == 0, 8.0 * m8_row, jnp.where(srow == 1, off8_row, jnp.where(srow == 2, gc, 0.0)))
    seg_ref[...] = jnp.concatenate([seg, jnp.zeros((8, LANES - N_EXPERTS), F32)], axis=1).astype(I32)
    carry_s[:, 0:N_EXPERTS] = gc + 8.0 * m8_row
    cnt_ref[...] = carry_s[...]
    ri_ref[...] = jnp.concatenate(dests + [jnp.zeros((4, tm), F32)], axis=0).astype(I32)
    rg_ref[...] = jnp.concatenate([e / tot for e in ex] + [jnp.zeros((4, tm), F32)], axis=0)


def _post(mix, x2d, gate, scale, shift, w_out_bf16, ln_g, ln_b, wr_t, br, seq_len, tm):
    t = x2d.shape[0]
    per = seq_len // tm
    vec = lambda i: (i // per, 0, 0)
    const2 = lambda i: (0, 0)
    return pl.pallas_call(
        functools.partial(_post_kernel, tm=tm),
        out_shape=(
            jax.ShapeDtypeStruct((t, D_MODEL), F32),
            jax.ShapeDtypeStruct((t, D_MODEL), BF16),
            jax.ShapeDtypeStruct((t // tm, 8, tm), I32),
            jax.ShapeDtypeStruct((t // tm, 8, tm), F32),
            jax.ShapeDtypeStruct((t // tm, 8, LANES), I32),
            jax.ShapeDtypeStruct((8, LANES), F32),
        ),
        grid=(t // tm,),
        in_specs=[
            pl.BlockSpec((tm, D_MODEL), lambda i: (i, 0)),
            pl.BlockSpec((tm, D_MODEL), lambda i: (i, 0)),
            pl.BlockSpec((None, 1, D_MODEL), vec),
            pl.BlockSpec((None, 1, D_MODEL), vec),
            pl.BlockSpec((None, 1, D_MODEL), vec),
            pl.BlockSpec((D_MODEL, D_MODEL), const2),
            pl.BlockSpec((1, D_MODEL), const2),
            pl.BlockSpec((1, D_MODEL), const2),
            pl.BlockSpec((N_EXPERTS, D_MODEL), const2),
            pl.BlockSpec((N_EXPERTS, LANES), const2),
        ],
        out_specs=(
            pl.BlockSpec((tm, D_MODEL), lambda i: (i, 0)),
            pl.BlockSpec((tm, D_MODEL), lambda i: (i, 0)),
            pl.BlockSpec((None, 8, tm), lambda i: (i, 0, 0)),
            pl.BlockSpec((None, 8, tm), lambda i: (i, 0, 0)),
            pl.BlockSpec((None, 8, LANES), lambda i: (i, 0, 0)),
            pl.BlockSpec((8, LANES), const2),
        ),
        scratch_shapes=[pltpu.VMEM((8, LANES), F32)],
        compiler_params=_cparams(("arbitrary",)),
        name="post_router",
    )(mix, x2d, gate, scale, shift, w_out_bf16, ln_g, ln_b, wr_t, br)


SEG_FIELDS = 3


def _run_sizes(tm):
    top = 1 << (tm - 1).bit_length()
    return [s for s in (512, 256, 128, 64, 32, 16, 8) if s <= max(top, 8)]


def _for_each_run_piece(seg_ref, rstart_ref, tile, tm, fn):
    base = tile * (SEG_FIELDS * N_EXPERTS)

    def body(e, c):
        n8 = seg_ref[base + e]
        src = seg_ref[base + N_EXPERTS + e]
        dst = rstart_ref[e] + seg_ref[base + 2 * N_EXPERTS + e]
        def pieces(sizes, done):
            for size in sizes:
                @pl.when((n8 & size) != 0)
                def _(done=done, size=size):
                    fn(pl.multiple_of(src + done, 8), pl.multiple_of(dst + done, 8), size)
                done = done + (n8 & size)

        split = 128
        big = [s for s in _run_sizes(tm) if s >= split]
        if big:
            @pl.when(n8 >= split)
            def _():
                pieces(big, jnp.int32(0))
        pieces([s for s in _run_sizes(tm) if s < split], n8 & ~jnp.int32(split - 1))
        return c

    lax.fori_loop(0, N_EXPERTS, body, 0)


def _wait_tile_rows(seg_ref, tile, cb, wait_rows):
    base = tile * (SEG_FIELDS * N_EXPERTS)
    last = N_EXPERTS - 1
    total = seg_ref[base + last] + seg_ref[base + N_EXPERTS + last]
    size = 1 << (cb.bit_length() - 1)
    while size >= 8:
        @pl.when((total & size) != 0)
        def _(size=size):
            wait_rows(size)
        size //= 2


def _dispatch_kernel(zblk_ref, rstart_ref, seg_ref, ri_ref, h_ref, xs_ref, zero_s, cbuf, sem, zsem,
                     *, tm, bm, cb, ntiles):
    i = pl.program_id(0)
    slot = i % 2

    def run_copies(tile, sl, wait):
        if wait:
            def wait_rows(size):
                pltpu.make_async_copy(cbuf.at[sl, pl.ds(0, size), :], xs_ref.at[pl.ds(0, size), :], sem.at[sl]).wait()
            _wait_tile_rows(seg_ref, tile, cb, wait_rows)
            return

        def piece(src, dst, size):
            pltpu.make_async_copy(cbuf.at[sl, pl.ds(src, size), :], xs_ref.at[pl.ds(dst, size), :], sem.at[sl]).start()
        _for_each_run_piece(seg_ref, rstart_ref, tile, tm, piece)

    def zero_copy(e):
        return pltpu.make_async_copy(zero_s, xs_ref.at[pl.ds(zblk_ref[e] * bm, bm), :], zsem)

    @pl.when(i == 0)
    def _():
        zero_s[...] = jnp.zeros_like(zero_s)

        def zstart(e, c):
            @pl.when(zblk_ref[e] >= 0)
            def _():
                zero_copy(e).start()
            return c

        def zwait(e, c):
            @pl.when(zblk_ref[e] >= 0)
            def _():
                zero_copy(e).wait()
            return c

        lax.fori_loop(0, N_EXPERTS, zstart, 0)
        lax.fori_loop(0, N_EXPERTS, zwait, 0)

    @pl.when(i >= 2)
    def _():
        run_copies(i - 2, slot, True)

    rows = lax.broadcasted_iota(I32, (cb, tm), 0)
    hit = rows == ri_ref[0:1, :]
    for k in range(1, TOP_K):
        hit = hit | (rows == ri_ref[k:k + 1, :])
    perm = jnp.where(hit, 1.0, 0.0).astype(BF16)
    cbuf[slot] = _dot(perm, h_ref[...])
    run_copies(i, slot, False)

    @pl.when(i == ntiles - 1)
    def _():
        if ntiles > 1:
            run_copies(i - 1, 1 - slot, True)
        run_copies(i, slot, True)


def _dispatch(h2, route_i, seg_flat, row_start, zblk, n_rows, tm, bm):
    t = h2.shape[0]
    ntiles = t // tm
    cb = TOP_K * tm + 8 * N_EXPERTS
    return pl.pallas_call(
        functools.partial(_dispatch_kernel, tm=tm, bm=bm, cb=cb, ntiles=ntiles),
        out_shape=jax.ShapeDtypeStruct((n_rows, D_MODEL), F32),
        grid_spec=pltpu.PrefetchScalarGridSpec(
            num_scalar_prefetch=3,
            grid=(ntiles,),
            in_specs=[
                pl.BlockSpec((None, 2 * TOP_K, tm), lambda i, z, r, s: (i, 0, 0)),
                pl.BlockSpec((tm, D_MODEL), lambda i, z, r, s: (i, 0)),
            ],
            out_specs=pl.BlockSpec(memory_space=pl.ANY),
            scratch_shapes=[
                pltpu.VMEM((bm, D_MODEL), F32),
                pltpu.VMEM((2, cb, D_MODEL), F32),
                pltpu.SemaphoreType.DMA((2,)),
                pltpu.SemaphoreType.DMA,
            ],
        ),
        compiler_params=_cparams(("arbitrary",)),
        name="dispatch",
    )(zblk, row_start, seg_flat, route_i, h2)


def _ffn_kernel(be_ref, nv_ref, x_ref, w1_ref, b1_ref, w2_ref, b2_ref, y_ref, w1_s, w2_s):
    b = pl.program_id(0)

    @pl.when(b < nv_ref[0])
    def _():
        @pl.when((b == 0) | (be_ref[b] != be_ref[jnp.maximum(b - 1, 0)]))
        def _():
            step = 256
            for r0 in range(0, D_MODEL, step):
                w1_s[r0:r0 + step, :] = w1_ref[r0:r0 + step, :].astype(BF16)
            for r0 in range(0, D_FF, step):
                w2_s[r0:r0 + step, :] = w2_ref[r0:r0 + step, :].astype(BF16)

        xb = x_ref[...].astype(BF16)
        cw = 256
        ngr = D_FF // cw

        def first(c):
            glu = _dot(xb, w1_s[:, c * cw:(c + 1) * cw]) + b1_ref[:, c * cw:(c + 1) * cw]
            lin = _dot(xb, w1_s[:, D_FF + c * cw:D_FF + (c + 1) * cw]) + b1_ref[:, D_FF + c * cw:D_FF + (c + 1) * cw]
            return glu, lin

        def activate(gl):
            glu = jnp.minimum(gl[0], SWIGLU_LIMIT)
            lin = jnp.clip(gl[1], -SWIGLU_LIMIT, SWIGLU_LIMIT)
            return (glu * _sigmoid(SWIGLU_ALPHA * glu) * (lin + 1.0)).astype(BF16)

        def second(c, act):
            return _dot(act, w2_s[c * cw:(c + 1) * cw, :])

        gls = {0: first(0), 1: first(1)}
        acts, y = {}, b2_ref[...]
        for c in range(ngr):
            acts[c] = activate(gls[c])
            if c + 2 < ngr:
                gls[c + 2] = first(c + 2)
            y = y + second(c, acts[c])
        y_ref[...] = y


def _ffn(xs, block_e, n_valid, layer, w1, b1, w2, b2, bm):
    nb = xs.shape[0] // bm
    row = lambda b, be, nv: (jnp.minimum(b, nv[0] - 1), 0)
    wsel = lambda b, be, nv: (layer, be[b], 0, 0)
    return pl.pallas_call(
        _ffn_kernel,
        out_shape=jax.ShapeDtypeStruct(xs.shape, F32),
        grid_spec=pltpu.PrefetchScalarGridSpec(
            num_scalar_prefetch=2,
            grid=(nb,),
            in_specs=[
                pl.BlockSpec((bm, D_MODEL), row),
                pl.BlockSpec((None, None, D_MODEL, 2 * D_FF), wsel),
                pl.BlockSpec((None, None, 1, 2 * D_FF), wsel),
                pl.BlockSpec((None, None, D_FF, D_MODEL), wsel),
                pl.BlockSpec((None, None, 1, D_MODEL), wsel),
            ],
            out_specs=pl.BlockSpec((bm, D_MODEL), row),
            scratch_shapes=[pltpu.VMEM((D_MODEL, 2 * D_FF), BF16), pltpu.VMEM((D_FF, D_MODEL), BF16)],
        ),
        compiler_params=_cparams(("arbitrary",)),
        name="expert_ffn",
    )(block_e, n_valid, xs, w1, b1, w2, b2)


def _combine_kernel(rstart_ref, seg_ref, ri_ref, x_ref, gate_ref, rg_ref, lng_ref, lnb_ref, ys_ref, o_ref,
                    ybuf, yb_s, wt_s, sem, *, tm, cb, ntiles):
    i = pl.program_id(0)
    slot = i % 2

    def run_copies(tile, sl, wait):
        if wait:
            def wait_rows(size):
                pltpu.make_async_copy(ys_ref.at[pl.ds(0, size), :], ybuf.at[sl, pl.ds(0, size), :], sem.at[sl]).wait()
            _wait_tile_rows(seg_ref, tile, cb, wait_rows)
            return

        def piece(loc, glob, size):
            pltpu.make_async_copy(ys_ref.at[pl.ds(glob, size), :], ybuf.at[sl, pl.ds(loc, size), :], sem.at[sl]).start()
        _for_each_run_piece(seg_ref, rstart_ref, tile, tm, piece)

    @pl.when(i == 0)
    def _():
        ybuf[...] = jnp.zeros_like(ybuf)
        run_copies(0, 0, False)

    @pl.when(i + 1 < ntiles)
    def _():
        run_copies(i + 1, 1 - slot, False)

    dest_t = _tr(jnp.concatenate([ri_ref[...].astype(F32), jnp.zeros((LANES - 8, tm), F32)], axis=0))
    g_t = _tr(jnp.concatenate([rg_ref[...], jnp.zeros((LANES - 8, tm), F32)], axis=0))
    lane_step = 256
    nq = max(1, tm // 256)
    tq = tm // nq

    def build(q):
        rq = slice(q * tq, (q + 1) * tq)
        for c0 in range(0, cb, lane_step):
            cols = (lax.broadcasted_iota(I32, (tq, lane_step), 1) + c0).astype(F32)
            w = jnp.where(cols == dest_t[rq, 0:1], g_t[rq, 0:1], 0.0)
            for k in range(1, TOP_K):
                w = w + jnp.where(cols == dest_t[rq, k:k + 1], g_t[rq, k:k + 1], 0.0)
            wt_s[rq, c0:c0 + lane_step] = w.astype(BF16)

    def product(q):
        rq = slice(q * tq, (q + 1) * tq)
        return _dot(wt_s[rq, :], yb_s[...])

    def finish(q, moe):
        rq = slice(q * tq, (q + 1) * tq)
        o_ref[rq, :] = _layer_norm_rows(ALPHA_DN * x_ref[rq, :] + (1.0 + gate_ref[...]) * moe,
                                        lng_ref[...], lnb_ref[...])

    build(0)
    run_copies(i, slot, True)
    row_step = 256
    for r0 in range(0, cb, row_step):
        yb_s[r0:r0 + row_step, :] = ybuf[slot, r0:r0 + row_step, :].astype(BF16)
    moes = {}
    for q in range(nq):
        if q + 1 < nq:
            build(q + 1)
        moes[q] = product(q)
        if q >= 1:
            finish(q - 1, moes.pop(q - 1))
    finish(nq - 1, moes.pop(nq - 1))


def _combine(row_start, seg_flat, route_i, x1, gate, route_g, ln_g, ln_b, ys, seq_len, tm):
    t = x1.shape[0]
    per = seq_len // tm
    ntiles = t // tm
    cb = TOP_K * tm + 8 * N_EXPERTS
    return pl.pallas_call(
        functools.partial(_combine_kernel, tm=tm, cb=cb, ntiles=ntiles),
        out_shape=jax.ShapeDtypeStruct((t, D_MODEL), F32),
        grid_spec=pltpu.PrefetchScalarGridSpec(
            num_scalar_prefetch=2,
            grid=(ntiles,),
            in_specs=[
                pl.BlockSpec((None, 2 * TOP_K, tm), lambda i, r, s: (i, 0, 0)),
                pl.BlockSpec((tm, D_MODEL), lambda i, r, s: (i, 0)),
                pl.BlockSpec((None, 1, D_MODEL), lambda i, r, s: (i // per, 0, 0)),
                pl.BlockSpec((None, 8, tm), lambda i, r, s: (i, 0, 0)),
                pl.BlockSpec((1, D_MODEL), lambda i, r, s: (0, 0)),
                pl.BlockSpec((1, D_MODEL), lambda i, r, s: (0, 0)),
                pl.BlockSpec(memory_space=pl.ANY),
            ],
            out_specs=pl.BlockSpec((tm, D_MODEL), lambda i, r, s: (i, 0)),
            scratch_shapes=[
                pltpu.VMEM((2, cb, D_MODEL), F32),
                pltpu.VMEM((cb, D_MODEL), BF16),
                pltpu.VMEM((tm, cb), BF16),
                pltpu.SemaphoreType.DMA((2,)),
            ],
        ),
        compiler_params=_cparams(("arbitrary",)),
        name="combine",
    )(row_start, seg_flat, route_i, x1, gate, route_g, ln_g, ln_b, ys)


def _moe_tables(counts, bm, n_blocks):
    cnt = counts[0, :N_EXPERTS].astype(I32)
    nblk = (cnt + bm - 1) // bm
    blk_end = jnp.cumsum(nblk)
    row_start = (blk_end - nblk) * bm
    n_valid = blk_end[-1]
    bidx = jnp.minimum(jnp.arange(n_blocks, dtype=I32), n_valid - 1)
    block_e = jnp.sum((blk_end[None, :] <= bidx[:, None]).astype(I32), axis=1)
    block_e = jnp.minimum(block_e, N_EXPERTS - 1)
    zblk = jnp.where((cnt % bm) != 0, blk_end - 1, -1).astype(I32)
    return row_start.astype(I32), block_e, n_valid.reshape(1).astype(I32), zblk


def _moe_and_norm(h2, x1, route_i, route_g, seg, counts, gate, ln_g, ln_b, layer, w1, b1, w2, b2, seq_len, tm, bm):
    t = h2.shape[0]
    ntiles = t // tm
    n_blocks = -(-(t * TOP_K + 7 * N_EXPERTS * ntiles) // bm) + N_EXPERTS
    row_start, block_e, n_valid, zblk = _moe_tables(counts, bm, n_blocks)
    seg_flat = seg[:, :SEG_FIELDS, :N_EXPERTS].reshape(ntiles * SEG_FIELDS * N_EXPERTS)
    xs = _dispatch(h2, route_i, seg_flat, row_start, zblk, n_blocks * bm, tm, bm)
    ys = _ffn(xs, block_e, n_valid, layer, w1, b1, w2, b2, bm)
    return _combine(row_start, seg_flat, route_i, x1, gate, route_g, ln_g, ln_b, ys, seq_len, tm)


def _group_cfg(batch, seq_len):
    if seq_len >= 512:
        return dict(tm=512, rows=256, bm=512)
    return dict(tm=seq_len, rows=seq_len, bm=128)


def kernel(x_prompt, x_sample, c_prompt, c_sample, state_a_C, state_a_n, state_a_m, cache_b_k, cache_b_v,
           state_c_S, state_c_conv, w_ada, b_ada, ln_g, ln_b, w_in_even, b_in_even, norm_a, sink_b, rel_bias,
           w_out_even, w_in_odd, conv_c, a_log_c, dt_bias_c, norm_c, w_out_odd, w_router, b_router,
           w_e1, b_e1, w_e2, b_e2):
    bp, lp, _ = x_prompt.shape
    bs, ls, _ = x_sample.shape
    groups = [dict(b=bp, l=lp, x=x_prompt.reshape(bp * lp, D_MODEL), **_group_cfg(bp, lp)),
              dict(b=bs, l=ls, x=x_sample.reshape(bs * ls, D_MODEL), **_group_cfg(bs, ls))]
    mod = _ada(jnp.concatenate([c_prompt, c_sample], axis=0), w_ada, b_ada)
    offs = [0, bp]
    states = [dict(), dict()]
    b1_all = b_e1.reshape(DEPTH, N_EXPERTS, 1, 2 * D_FF)
    b2_all = b_e2.reshape(DEPTH, N_EXPERTS, 1, D_MODEL)
    for l in range(DEPTH):
        e = l // 2
        if l % 2 == 0:
            w = w_in_even[e]
            sz = (512, 512, 512, 512, 4, 4, 512, 128, 128)
            o = [sum(sz[:j]) for j in range(len(sz) + 1)]
            pad = jnp.zeros((D_MODEL, LANES - 2 * NH_A), w.dtype)
            w_in = jnp.concatenate([w[:, o[0]:o[4]], w[:, o[6]:o[9]], w[:, o[4]:o[6]], pad], axis=1).astype(BF16)
            bb = b_in_even[e]
            b_in = jnp.concatenate([bb[o[0]:o[4]], bb[o[6]:o[9]], bb[o[4]:o[6]],
                                    jnp.zeros((LANES - 2 * NH_A,), bb.dtype)]).reshape(1, N_EVEN_COLS)
            w_out = w_out_even[e].astype(BF16)
        else:
            w = w_in_odd[e]
            pad = jnp.zeros((D_MODEL, LANES - 2 * NH_C), w.dtype)
            w_in = jnp.concatenate([w, pad], axis=1).astype(BF16)
            b_in = jnp.zeros((1, N_ODD_COLS), F32)
            w_out = w_out_odd[e].astype(BF16)
        wr_t = w_router[l].T
        br = jnp.broadcast_to(b_router[l][:, None], (N_EXPERTS, LANES))
        for gi, gr in enumerate(groups):
            nb, sl = gr['b'], gr['l']
            m = mod[l, offs[gi]:offs[gi] + nb].reshape(nb, 6, 1, D_MODEL)
            m6 = [m[:, j] for j in range(6)]
            proj = _inproj(gr['x'], m6[1], m6[0], w_in, b_in, sl, gr['tm'])
            st = states[gi]
            if l % 2 == 0:
                if gi == 0:
                    kh = jnp.zeros((nb, WINDOW, NKV_B, DH_B), F32)
                    vh = kh
                    c0 = jnp.zeros((nb, NH_A, DK_A, DV_A), F32)
                    n0 = jnp.zeros((nb, NH_A, DK_A), F32)
                    m0 = jnp.zeros((nb, NH_A), F32)
                else:
                    kh, vh, c0, n0, m0 = cache_b_k[e], cache_b_v[e], state_a_C[e], state_a_n[e], state_a_m[e]
                bias_tab = _swa_bias_table(rel_bias, gr['rows'])
                mix, c1, n1, m1, k1, v1 = _even_mixer(proj, kh, vh, c0, n0, m0, bias_tab, norm_a[e], sink_b[e],
                                                      nb, sl, gr['rows'], gi == 1)
                for name, val in (('a_C', c1), ('a_n', n1), ('a_m', m1), ('b_k', k1), ('b_v', v1)):
                    st.setdefault(name, []).append(val)
            else:
                if gi == 0:
                    cv0 = jnp.zeros((nb, CONV_W - 1, QKV_C), F32)
                    s0 = jnp.zeros((nb, NH_C, DK_C, DV_C), F32)
                else:
                    cv0, s0 = state_c_conv[e], state_c_S[e]
                mix, s1, cv1 = _odd_mixer(proj, cv0, s0, conv_c[e], a_log_c[e], dt_bias_c[e], norm_c[e], nb, sl,
                                          gr['rows'])
                st.setdefault('c_S', []).append(s1)
                st.setdefault('c_conv', []).append(cv1)
            x1, h2, route_i, route_g, seg, counts = _post(
                mix, gr['x'], m6[2], m6[4], m6[3], w_out, ln_g[l, 0].reshape(1, D_MODEL),
                ln_b[l, 0].reshape(1, D_MODEL), wr_t, br, sl, gr['tm'])
            gr['x'] = _moe_and_norm(h2, x1, route_i, route_g, seg, counts, m6[5], ln_g[l, 1].reshape(1, D_MODEL),
                                    ln_b[l, 1].reshape(1, D_MODEL), l, w_e1, b1_all, w_e2, b2_all,
                                    sl, gr['tm'], gr['bm'])
    outs = [groups[0]['x'].reshape(bp, lp, D_MODEL), groups[1]['x'].reshape(bs, ls, D_MODEL)]
    for gi in range(2):
        for name in ('a_C', 'a_n', 'a_m', 'b_k', 'b_v', 'c_S', 'c_conv'):
            outs.append(jnp.stack(states[gi][name]))
    return tuple(outs)
```

```python
import functools
import math

import jax
import jax.numpy as jnp
from jax import lax
from jax.experimental import pallas as pl
from jax.experimental.pallas import tpu as pltpu

F32 = jnp.float32
BF16 = jnp.bfloat16
I32 = jnp.int32

D_MODEL = 1024
CHUNK = 64
NH_A, DK_A, DV_A = 4, 128, 128
W_A = NH_A * DV_A
NH_B, NKV_B, DH_B, WINDOW = 8, 2, 64, 128
G_B = NH_B // NKV_B
W_B = NH_B * DH_B
NUM_BUCKETS, MAX_DISTANCE = 32, 256
NH_C, DK_C, DV_C, CONV_W = 8, 128, 128, 4
QKV_C = NH_C * (2 * DK_C + DV_C)
N_EXPERTS, TOP_K, D_FF = 32, 4, 1024
SWIGLU_LIMIT, SWIGLU_ALPHA = 7.0, 1.702
DEPTH = 2
ALPHA_DN = (2 * DEPTH) ** 0.25
LN_EPS = 1e-5
RMS_EPS = 1e-6

LANES = 128
E_QA, E_KA, E_VA, E_OA, E_QB, E_KB, E_VB, E_G = 0, 512, 1024, 1536, 2048, 2560, 2688, 2816
N_EVEN_COLS = 2944
O_QKV, O_Z, O_G = 0, 3072, 4096
N_ODD_COLS = 4224

VMEM_LIMIT = 56 * 1024 * 1024


def _cparams(sem):
    return pltpu.CompilerParams(dimension_semantics=sem, vmem_limit_bytes=VMEM_LIMIT)


def _softplus(x):
    return jnp.maximum(x, 0.0) + jnp.log(1.0 + jnp.exp(-jnp.abs(x)))


def _sigmoid(x):
    return 1.0 / (1.0 + jnp.exp(-x))


def _split_bf16(a):
    hi = a.astype(BF16)
    lo = (a - hi.astype(F32)).astype(BF16)
    return hi, lo


_NN = (((1,), (0,)), ((), ()))
_NT = (((1,), (1,)), ((), ()))
_TN = (((0,), (0,)), ((), ()))


def _dot(a, b, dims=_NN):
    return lax.dot_general(a, b, dims, preferred_element_type=F32)


def _dot_x3(a, b, dims=_NN):
    ah, al = _split_bf16(a)
    bh, bl = _split_bf16(b)
    return _dot(ah, bh, dims) + _dot(ah, bl, dims) + _dot(al, bh, dims)


def _dot_exact_lhs(a_bf16, b, dims=_NN):
    b0 = b.astype(BF16)
    r1 = b - b0.astype(F32)
    b1 = r1.astype(BF16)
    b2 = (r1 - b1.astype(F32)).astype(BF16)
    return _dot(a_bf16, b0, dims) + _dot(a_bf16, b1, dims) + _dot(a_bf16, b2, dims)


def _tr(x):
    r = x.shape[0]
    rp = -(-r // LANES) * LANES
    if rp != r:
        x = jnp.concatenate([x, jnp.zeros((rp - r, x.shape[1]), x.dtype)], axis=0)
    return x.T[:, :r]


def _tri(n, strict=False):
    r = lax.broadcasted_iota(I32, (n, n), 0)
    c = lax.broadcasted_iota(I32, (n, n), 1)
    return (r > c) if strict else (r >= c)


def _layer_norm_rows(v, g, b):
    mu = jnp.mean(v, axis=-1, keepdims=True)
    d = v - mu
    var = jnp.mean(d * d, axis=-1, keepdims=True)
    return d * lax.rsqrt(var + LN_EPS) * g + b


def _ada_kernel(c_ref, w_ref, b_ref, o_ref):
    c = c_ref[...]
    a = (c * _sigmoid(c)).astype(BF16)
    o_ref[...] = _dot(a, w_ref[...].astype(BF16)) + b_ref[...]


def _ada(c_all, w_ada, b_ada):
    nb = c_all.shape[0]
    tn = 1536
    return pl.pallas_call(
        _ada_kernel,
        out_shape=jax.ShapeDtypeStruct((DEPTH, nb, 6 * D_MODEL), F32),
        grid=(DEPTH, 6 * D_MODEL // tn),
        in_specs=[
            pl.BlockSpec((nb, D_MODEL), lambda l, j: (0, 0)),
            pl.BlockSpec((None, D_MODEL, tn), lambda l, j: (l, 0, j)),
            pl.BlockSpec((None, 1, tn), lambda l, j: (l, 0, j)),
        ],
        out_specs=pl.BlockSpec((None, nb, tn), lambda l, j: (l, 0, j)),
        compiler_params=_cparams(("arbitrary", "arbitrary")),
        name="ada",
    )(c_all, w_ada, b_ada.reshape(DEPTH, 1, 6 * D_MODEL))


def _inproj_kernel(x_ref, sc_ref, sh_ref, w_ref, b_ref, o_ref, *, n_cols, col_step):
    h = (x_ref[...] * (1.0 + sc_ref[...]) + sh_ref[...]).astype(BF16)
    for c0 in range(0, n_cols, col_step):
        c1 = min(c0 + col_step, n_cols)
        o_ref[:, c0:c1] = _dot(h, w_ref[:, c0:c1]) + b_ref[:, c0:c1]


def _inproj(x2d, scale, shift, w_bf16, bias, seq_len, tm):
    t = x2d.shape[0]
    n = w_bf16.shape[1]
    per = seq_len // tm
    return pl.pallas_call(
        functools.partial(_inproj_kernel, n_cols=n, col_step=1024),
        out_shape=jax.ShapeDtypeStruct((t, n), F32),
        grid=(t // tm,),
        in_specs=[
            pl.BlockSpec((tm, D_MODEL), lambda i: (i, 0)),
            pl.BlockSpec((None, 1, D_MODEL), lambda i: (i // per, 0, 0)),
            pl.BlockSpec((None, 1, D_MODEL), lambda i: (i // per, 0, 0)),
            pl.BlockSpec((D_MODEL, n), lambda i: (0, 0)),
            pl.BlockSpec((1, n), lambda i: (0, 0)),
        ],
        out_specs=pl.BlockSpec((tm, n), lambda i: (i, 0)),
        compiler_params=_cparams(("arbitrary",)),
        name="inproj",
    )(x2d, scale, shift, w_bf16, bias)


def _even_kernel(proj_ref, kh0_ref, vh0_ref, c0_ref, n0_ref, m0_ref, bias_ref, norma_ref, sink_ref,
                 mix_ref, cout_ref, nout_ref, mout_ref, kout_ref, vout_ref,
                 c_s, n_s, m_s, kh_s, vh_s, *, rows, hist_valid):
    R = rows
    KW = WINDOW + R
    i = pl.program_id(1)

    @pl.when(i == 0)
    def _():
        c_s[...] = c0_ref[...]
        n_s[...] = n0_ref[...]
        m_s[...] = m0_ref[...]
        kh_s[...] = kh0_ref[...]
        vh_s[...] = vh0_ref[...]

    CS = CHUNK
    g = proj_ref[:, E_G:E_G + LANES]
    lf = -_softplus(-g)
    rr = lax.broadcasted_iota(I32, (R, R), 0)
    cc = lax.broadcasted_iota(I32, (R, R), 1)
    tri_b = jnp.where((rr >= cc) & (rr // CS == cc // CS), 1.0, 0.0).astype(BF16)
    b_all = _dot_exact_lhs(tri_b, lf)
    g_t = _tr(g)
    b_t = _tr(b_all)
    causal = _tri(CS)
    scale_a = DK_A ** -0.5
    chunks = list(range(0, R, CS))
    P = [(c0, h) for c0 in chunks for h in range(NH_A)]
    nP = len(P)
    rsl = [slice(c0, c0 + CS) for c0, _ in P]
    b_col = [b_all[rsl[p], NH_A + h:NH_A + h + 1] for p, (_, h) in enumerate(P)]
    b_row = [b_t[NH_A + h:NH_A + h + 1, rsl[p]] for p, (_, h) in enumerate(P)]
    i_row = [g_t[h:h + 1, rsl[p]] for p, (_, h) in enumerate(P)]
    i_col = [g[rsl[p], h:h + 1] for p, (_, h) in enumerate(P)]
    logw = [jnp.where(causal, b_col[p] - b_row[p] + i_row[p], -jnp.inf) for p in range(nP)]
    lmax = [jnp.max(logw[p], axis=-1, keepdims=True) for p in range(nP)]
    qb = [(proj_ref[rsl[p], E_QA + h * DK_A:E_QA + (h + 1) * DK_A] * scale_a).astype(BF16)
          for p, (_, h) in enumerate(P)]
    kf = [proj_ref[rsl[p], E_KA + h * DK_A:E_KA + (h + 1) * DK_A] for p, (_, h) in enumerate(P)]
    vb = [proj_ref[rsl[p], E_VA + h * DV_A:E_VA + (h + 1) * DV_A].astype(BF16) for p, (_, h) in enumerate(P)]
    qk = [_dot(qb[p], kf[p].astype(BF16), _NT) for p in range(nP)]
    b_last = [b_col[p][CS - 1:CS, :] for p in range(nP)]
    m_prev, m_inter, m_t, m_new = [None] * nP, [None] * nP, [None] * nP, [None] * nP
    for h in range(NH_A):
        run = m_s[h:h + 1, 0:1]
        for ci in range(len(chunks)):
            p = ci * NH_A + h
            m_prev[p] = run
            m_inter[p] = b_col[p] + run
            m_t[p] = jnp.maximum(m_inter[p], lmax[p])
            run = m_t[p][CS - 1:CS, :]
            m_new[p] = run
        m_s[h:h + 1, :] = jnp.broadcast_to(run, (1, LANES))
    w = [jnp.exp(logw[p] - m_t[p]) for p in range(nP)]
    s = [qk[p] * w[p] for p in range(nP)]
    sv = [_dot(s[p].astype(BF16), vb[p]) for p in range(nP)]
    ssum = [jnp.sum(s[p], axis=-1, keepdims=True) for p in range(nP)]
    kfac = [jnp.exp(b_last[p] - b_col[p] + i_col[p] - m_new[p]) for p in range(nP)]
    kw = [kf[p] * kfac[p] for p in range(nP)]
    kwt = [_tr(kw[p]).astype(BF16) for p in range(nP)]
    kv = [_dot(kwt[p], vb[p]) for p in range(nP)]
    ksum = [jnp.sum(kw[p], axis=0, keepdims=True) for p in range(nP)]
    carry = [jnp.exp(b_last[p] + m_prev[p] - m_new[p]) for p in range(nP)]
    dec = [jnp.exp(m_inter[p] - m_t[p]) for p in range(nP)]
    floor_ = [jnp.exp(-m_t[p]) for p in range(nP)]
    c_prev, n_prev = [None] * nP, [None] * nP
    for h in range(NH_A):
        n_run = n_s[h:h + 1, :]
        c_run = c_s[h]
        for ci in range(len(chunks)):
            p = ci * NH_A + h
            c_prev[p], n_prev[p] = c_run, n_run
            c_run = carry[p] * c_run + kv[p]
            n_run = carry[p] * n_run + ksum[p]
        c_s[h] = c_run
        n_s[h:h + 1, :] = n_run
    qc = [_dot(qb[p], c_prev[p].astype(BF16)) for p in range(nP)]
    qn = [jnp.sum(qb[p].astype(F32) * n_prev[p].astype(BF16).astype(F32), axis=-1, keepdims=True) for p in range(nP)]
    num = [dec[p] * qc[p] + sv[p] for p in range(nP)]
    den = [dec[p] * qn[p] + ssum[p] for p in range(nP)]
    hh = [num[p] / jnp.maximum(jnp.abs(den[p]), floor_[p]) for p in range(nP)]
    mu = [jnp.mean(hh[p], axis=-1, keepdims=True) for p in range(nP)]
    dd = [hh[p] - mu[p] for p in range(nP)]
    var = [jnp.mean(dd[p] * dd[p], axis=-1, keepdims=True) for p in range(nP)]
    rs_ = [lax.rsqrt(var[p] + LN_EPS) for p in range(nP)]
    for p, (_, h) in enumerate(P):
        og = proj_ref[rsl[p], E_OA + h * DV_A:E_OA + (h + 1) * DV_A]
        ha = dd[p] * rs_[p] * norma_ref[h:h + 1, :] * _sigmoid(og)
        mix_ref[rsl[p], h * DV_A:(h + 1) * DV_A] = ha.astype(mix_ref.dtype)

    cout_ref[...] = c_s[...]
    nout_ref[...] = n_s[...]
    mout_ref[...] = m_s[...]

    k_win = jnp.concatenate([kh_s[...], proj_ref[:, E_KB:E_KB + NKV_B * DH_B]], axis=0)
    v_win = jnp.concatenate([vh_s[...], proj_ref[:, E_VB:E_VB + NKV_B * DH_B]], axis=0)
    k_win_b = k_win.astype(BF16)
    v_win_b = v_win.astype(BF16)
    if not hist_valid:
        key_pos = lax.broadcasted_iota(I32, (R, KW), 1) + (i * R - WINDOW)
        key_ok = key_pos >= 0
    scale_b = DH_B ** -0.5
    kgs = [k_win_b[:, kv * DH_B:(kv + 1) * DH_B] for kv in range(NKV_B)]
    vgs = [v_win_b[:, kv * DH_B:(kv + 1) * DH_B] for kv in range(NKV_B)]
    scores = []
    for hd in range(NH_B):
        qh = proj_ref[:, E_QB + hd * DH_B:E_QB + (hd + 1) * DH_B].astype(BF16)
        s = _dot(qh, kgs[hd // G_B], _NT) * scale_b + bias_ref[hd]
        if not hist_valid:
            s = jnp.where(key_ok, s, -jnp.inf)
        scores.append(s)
    probs_b = []
    for hd, s in enumerate(scores):
        sk = sink_ref[hd]
        mx = jnp.maximum(jnp.max(s, axis=-1, keepdims=True), sk)
        p = jnp.exp(s - mx)
        p = p / (jnp.sum(p, axis=-1, keepdims=True) + jnp.exp(sk - mx))
        probs_b.append(p.astype(BF16))
    for hd, p in enumerate(probs_b):
        o = _dot(p, vgs[hd // G_B])
        mix_ref[:, W_A + hd * DH_B:W_A + (hd + 1) * DH_B] = o.astype(mix_ref.dtype)

    kh_s[...] = k_win[R:, :]
    vh_s[...] = v_win[R:, :]
    kout_ref[...] = kh_s[...]
    vout_ref[...] = vh_s[...]


def _rel_bucket(rel):
    nb = NUM_BUCKETS // 2
    max_exact = nb // 2
    n = jnp.abs(rel)
    nf = jnp.maximum(n, 1).astype(F32)
    large = max_exact + (jnp.log(nf / max_exact) / math.log(MAX_DISTANCE / max_exact)
                         * (nb - max_exact)).astype(I32)
    large = jnp.minimum(large, nb - 1)
    return jnp.where(rel > 0, nb, 0) + jnp.where(n < max_exact, n, large)


def _swa_bias_table(rel_bias, rows):
    kw = WINDOW + rows
    qi = jnp.arange(rows)[:, None]
    kj = jnp.arange(kw)[None, :]
    bucket = _rel_bucket(kj - WINDOW - qi)
    rb = rel_bias.astype(F32)
    bias = jnp.zeros((NH_B, rows, kw), F32)
    for b in range(NUM_BUCKETS):
        bias = jnp.where((bucket == b)[None], rb[b][:, None, None], bias)
    lo = (qi // CHUNK) * CHUNK
    ok = (kj >= lo) & (kj < lo + WINDOW + CHUNK)
    return jnp.where(ok[None], bias, -jnp.inf)


def _even_mixer(proj, k_hist, v_hist, c0, n0, m0, bias_tab, norm_a, sink, batch, seq_len, rows, hist_valid):
    nsteps = seq_len // rows
    kw = WINDOW + rows
    m0p = jnp.broadcast_to(jnp.pad(m0, ((0, 0), (0, 8 - NH_A)))[:, :, None], (batch, 8, LANES))
    n0p = jnp.pad(n0, ((0, 0), (0, 8 - NH_A), (0, 0)))
    full3 = lambda b, i: (b, 0, 0)
    outs = pl.pallas_call(
        functools.partial(_even_kernel, rows=rows, hist_valid=hist_valid),
        out_shape=(
            jax.ShapeDtypeStruct((batch * seq_len, D_MODEL), BF16),
            jax.ShapeDtypeStruct((batch, NH_A, DK_A, DV_A), F32),
            jax.ShapeDtypeStruct((batch, 8, DK_A), F32),
            jax.ShapeDtypeStruct((batch, 8, LANES), F32),
            jax.ShapeDtypeStruct((batch, WINDOW, NKV_B * DH_B), F32),
            jax.ShapeDtypeStruct((batch, WINDOW, NKV_B * DH_B), F32),
        ),
        grid=(batch, nsteps),
        in_specs=[
            pl.BlockSpec((rows, N_EVEN_COLS), lambda b, i: (b * nsteps + i, 0)),
            pl.BlockSpec((None, WINDOW, NKV_B * DH_B), full3),
            pl.BlockSpec((None, WINDOW, NKV_B * DH_B), full3),
            pl.BlockSpec((None, NH_A, DK_A, DV_A), lambda b, i: (b, 0, 0, 0)),
            pl.BlockSpec((None, 8, DK_A), full3),
            pl.BlockSpec((None, 8, LANES), full3),
            pl.BlockSpec((NH_B, rows, kw), lambda b, i: (0, 0, 0)),
            pl.BlockSpec((NH_A, DV_A), lambda b, i: (0, 0)),
            pl.BlockSpec(memory_space=pltpu.SMEM),
        ],
        out_specs=(
            pl.BlockSpec((rows, D_MODEL), lambda b, i: (b * nsteps + i, 0)),
            pl.BlockSpec((None, NH_A, DK_A, DV_A), lambda b, i: (b, 0, 0, 0)),
            pl.BlockSpec((None, 8, DK_A), full3),
            pl.BlockSpec((None, 8, LANES), full3),
            pl.BlockSpec((None, WINDOW, NKV_B * DH_B), full3),
            pl.BlockSpec((None, WINDOW, NKV_B * DH_B), full3),
        ),
        scratch_shapes=[
            pltpu.VMEM((NH_A, DK_A, DV_A), F32),
            pltpu.VMEM((8, DK_A), F32),
            pltpu.VMEM((8, LANES), F32),
            pltpu.VMEM((WINDOW, NKV_B * DH_B), F32),
            pltpu.VMEM((WINDOW, NKV_B * DH_B), F32),
        ],
        compiler_params=_cparams(("arbitrary", "arbitrary")),
        name="even_mixer",
    )(proj, k_hist.reshape(batch, WINDOW, NKV_B * DH_B), v_hist.reshape(batch, WINDOW, NKV_B * DH_B),
      c0, n0p, m0p, bias_tab, norm_a.reshape(NH_A, DV_A), sink)
    mix, c1, n1, m1, k1, v1 = outs
    return (mix, c1, n1[:, :NH_A, :], m1[:, :NH_A, 0],
            k1.reshape(batch, WINDOW, NKV_B, DH_B), v1.reshape(batch, WINDOW, NKV_B, DH_B))


def _odd_kernel(proj_ref, conv0_ref, s0_ref, convw_ref, hp_ref, normc_ref,
                mix_ref, sout_ref, convout_ref, s_s, xa_s, *, rows):
    R = rows
    i = pl.program_id(1)
    HB = 8

    @pl.when(i == 0)
    def _():
        s_s[...] = s0_ref[...]
        xa_s[HB - (CONV_W - 1):HB, :] = conv0_ref[...]

    xa_s[HB:HB + R, :] = proj_ref[:, O_QKV:O_QKV + QKV_C]
    convout_ref[...] = xa_s[HB + R - (CONV_W - 1):HB + R, :]

    CS = CHUNK
    P2 = 2 * CS
    gcols = proj_ref[:, O_G:O_G + LANES]
    beta_all = _sigmoid(gcols)
    g_all = -jnp.exp(hp_ref[0:1, :]) * _softplus(gcols + hp_ref[1:2, :])
    rr = lax.broadcasted_iota(I32, (R, R), 0)
    cc = lax.broadcasted_iota(I32, (R, R), 1)
    tri_b = jnp.where((rr >= cc) & (rr // CS == cc // CS), 1.0, 0.0).astype(BF16)
    G_all = _dot_exact_lhs(tri_b, g_all)
    G_t = _tr(G_all)
    eG_all = jnp.exp(G_all)
    r2 = lax.broadcasted_iota(I32, (P2, P2), 0)
    c2 = lax.broadcasted_iota(I32, (P2, P2), 1)
    same_head = (r2 // CS) == (c2 // CS)
    incl2 = same_head & (r2 >= c2)
    strict2 = same_head & (r2 > c2)
    eye2 = jnp.where(r2 == c2, 1.0, 0.0)
    lane_lo = lax.broadcasted_iota(I32, (DK_C, P2), 1) < CS
    scale_c = DK_C ** -0.5

    def conv_silu(rs0, c0):
        y = xa_s[HB - 3 + rs0:HB - 3 + rs0 + CS, c0:c0 + LANES] * convw_ref[0:1, c0:c0 + LANES]
        for j in range(1, CONV_W):
            y = y + xa_s[HB - 3 + j + rs0:HB - 3 + j + rs0 + CS, c0:c0 + LANES] * convw_ref[j:j + 1, c0:c0 + LANES]
        return y * _sigmoid(y)

    def l2n(x):
        return x * lax.rsqrt(jnp.sum(x * x, axis=-1, keepdims=True) + RMS_EPS)

    def stack(a, b):
        return jnp.concatenate([a, b], axis=0)

    def inv_unit_lower_all(a_list):
        ts = [eye2 - a for a in a_list]
        abs_ = [a.astype(BF16) for a in a_list]
        ps = [_dot(ab, ab) for ab in abs_]
        nlev = (CS - 1).bit_length() - 1
        n = len(ts)
        for lvl in range(nlev):
            th = [t.astype(BF16) for t in ts]
            ph = [p.astype(BF16) for p in ps]
            if lvl < nlev - 1:
                lhs = [stack(th[i], ph[i]) for i in range(n)]
                tp = [_dot(lhs[i], ph[i]) for i in range(n)]
                ts = [ts[i] + tp[i][:P2] for i in range(n)]
                ps = [tp[i][P2:] for i in range(n)]
            else:
                tp = [_dot(th[i], ph[i]) for i in range(n)]
                ts = [ts[i] + tp[i] for i in range(n)]
        return ts

    npair = NH_C // 2
    probs = [(c0, pr) for c0 in range(0, R, CS) for pr in range(npair)]
    nP = len(probs)
    rsl = [slice(c0, c0 + CS) for c0, _ in probs]
    hab = [(2 * pr, 2 * pr + 1) for _, pr in probs]

    def per_head(fn):
        return [[fn(p, h) for h in hab[p]] for p in range(nP)]

    def col2(arr, off):
        return [stack(arr[rsl[p], off + hab[p][0]:off + hab[p][0] + 1], arr[rsl[p], off + hab[p][1]:off + hab[p][1] + 1])
                for p in range(nP)]

    kc = per_head(lambda p, h: conv_silu(probs[p][0], NH_C * DK_C + h * DK_C))
    qc = per_head(lambda p, h: conv_silu(probs[p][0], h * DK_C))
    vc = per_head(lambda p, h: conv_silu(probs[p][0], 2 * NH_C * DK_C + h * DV_C))
    kss = [[jnp.sum(x * x, axis=-1, keepdims=True) for x in kc[p]] for p in range(nP)]
    qss = [[jnp.sum(x * x, axis=-1, keepdims=True) for x in qc[p]] for p in range(nP)]
    k2 = [stack(*[x * lax.rsqrt(ss + RMS_EPS) for x, ss in zip(kc[p], kss[p])]) for p in range(nP)]
    q2b = [stack(*[x * lax.rsqrt(ss + RMS_EPS) * scale_c for x, ss in zip(qc[p], qss[p])]).astype(BF16)
           for p in range(nP)]
    v2 = [stack(*vc[p]) for p in range(nP)]
    k2b = [k2[p].astype(BF16) for p in range(nP)]
    G_col = col2(G_all, NH_C)
    G_row = [jnp.concatenate([G_t[NH_C + hab[p][0]:NH_C + hab[p][0] + 1, rsl[p]],
                              G_t[NH_C + hab[p][1]:NH_C + hab[p][1] + 1, rsl[p]]], axis=1) for p in range(nP)]
    beta = col2(beta_all, 0)
    eg = col2(eG_all, NH_C)
    GL = [[G_all[probs[p][0] + CS - 1:probs[p][0] + CS, NH_C + h:NH_C + h + 1] for h in hab[p]] for p in range(nP)]
    GL2 = [stack(jnp.broadcast_to(GL[p][0], (CS, 1)), jnp.broadcast_to(GL[p][1], (CS, 1))) for p in range(nP)]
    egl = [[jnp.exp(x) for x in GL[p]] for p in range(nP)]
    kk = [_dot(k2b[p], k2b[p], _NT) for p in range(nP)]
    qk = [_dot(q2b[p], k2b[p], _NT) for p in range(nP)]
    dmat = [jnp.exp(jnp.where(incl2, G_col[p] - G_row[p], -jnp.inf)) for p in range(nP)]
    a_mat = [jnp.where(strict2, beta[p] * dmat[p] * kk[p], 0.0) for p in range(nP)]
    attn = [(qk[p] * dmat[p]).astype(BF16) for p in range(nP)]
    rhs = [jnp.concatenate([beta[p] * v2[p], (beta[p] * eg[p]) * k2[p]], axis=1).astype(BF16) for p in range(nP)]
    kdec = [k2[p] * jnp.exp(GL2[p] - G_col[p]) for p in range(nP)]
    kdec_t = [_tr(kdec[p]).astype(BF16) for p in range(nP)]
    zero_t = jnp.zeros((DK_C, P2), BF16)
    kdec_lo = [jnp.where(lane_lo, kdec_t[p], zero_t) for p in range(nP)]
    kdec_hi = [jnp.where(lane_lo, zero_t, kdec_t[p]) for p in range(nP)]
    tinv = [t.astype(BF16) for t in inv_unit_lower_all(a_mat)]
    uw = [_dot(tinv[p], rhs[p]) for p in range(nP)]
    wq_lhs = [[stack(uw[p][j * CS:(j + 1) * CS, DV_C:].astype(BF16), q2b[p][j * CS:(j + 1) * CS]) for j in range(2)]
              for p in range(nP)]

    S_cur = [s_s[h] for h in range(NH_C)]
    for ci, c0 in enumerate(range(0, R, CS)):
        rs = slice(c0, c0 + CS)
        ps_ = list(range(ci * npair, (ci + 1) * npair))
        Sb = [S_cur[h].astype(BF16) for h in range(NH_C)]
        wq = [[_dot(wq_lhs[p][j], Sb[hab[p][j]]) for j in range(2)] for p in ps_]
        db = [(uw[p][:, :DV_C] - stack(wq[i_][0][:CS], wq[i_][1][:CS])).astype(BF16) for i_, p in enumerate(ps_)]
        upd_lo = [_dot(kdec_lo[p], db[i_]) for i_, p in enumerate(ps_)]
        upd_hi = [_dot(kdec_hi[p], db[i_]) for i_, p in enumerate(ps_)]
        for i_, p in enumerate(ps_):
            S_cur[hab[p][0]] = egl[p][0] * S_cur[hab[p][0]] + upd_lo[i_]
            S_cur[hab[p][1]] = egl[p][1] * S_cur[hab[p][1]] + upd_hi[i_]
        ad = [_dot(attn[p], db[i_]) for i_, p in enumerate(ps_)]
        o2 = [eg[p] * stack(wq[i_][0][CS:], wq[i_][1][CS:]) + ad[i_] for i_, p in enumerate(ps_)]
        oh = [o2[i_][j * CS:(j + 1) * CS] for i_ in range(npair) for j in range(2)]
        ms = [jnp.mean(o * o, axis=-1, keepdims=True) for o in oh]
        zs = [proj_ref[rs, O_Z + h * DV_C:O_Z + (h + 1) * DV_C] for h in range(NH_C)]
        zg = [z * _sigmoid(z) for z in zs]
        for h in range(NH_C):
            o = oh[h] * lax.rsqrt(ms[h] + RMS_EPS) * normc_ref[...] * zg[h]
            mix_ref[rs, h * DV_C:(h + 1) * DV_C] = o.astype(mix_ref.dtype)
    for h in range(NH_C):
        s_s[h] = S_cur[h]

    xa_s[HB - (CONV_W - 1):HB, :] = xa_s[HB + R - (CONV_W - 1):HB + R, :]
    sout_ref[...] = s_s[...]


def _odd_mixer(proj, conv_hist, s0, conv_w, a_log, dt_bias, norm_c, batch, seq_len, rows):
    nsteps = seq_len // rows
    hp = jnp.zeros((8, LANES), F32)
    hp = hp.at[0, NH_C:2 * NH_C].set(a_log.astype(F32)).at[1, NH_C:2 * NH_C].set(dt_bias.astype(F32))
    outs = pl.pallas_call(
        functools.partial(_odd_kernel, rows=rows),
        out_shape=(
            jax.ShapeDtypeStruct((batch * seq_len, D_MODEL), BF16),
            jax.ShapeDtypeStruct((batch, NH_C, DK_C, DV_C), F32),
            jax.ShapeDtypeStruct((batch, CONV_W - 1, QKV_C), F32),
        ),
        grid=(batch, nsteps),
        in_specs=[
            pl.BlockSpec((rows, N_ODD_COLS), lambda b, i: (b * nsteps + i, 0)),
            pl.BlockSpec((None, CONV_W - 1, QKV_C), lambda b, i: (b, 0, 0)),
            pl.BlockSpec((None, NH_C, DK_C, DV_C), lambda b, i: (b, 0, 0, 0)),
            pl.BlockSpec((CONV_W, QKV_C), lambda b, i: (0, 0)),
            pl.BlockSpec((8, LANES), lambda b, i: (0, 0)),
            pl.BlockSpec((1, DV_C), lambda b, i: (0, 0)),
        ],
        out_specs=(
            pl.BlockSpec((rows, D_MODEL), lambda b, i: (b * nsteps + i, 0)),
            pl.BlockSpec((None, NH_C, DK_C, DV_C), lambda b, i: (b, 0, 0, 0)),
            pl.BlockSpec((None, CONV_W - 1, QKV_C), lambda b, i: (b, 0, 0)),
        ),
        scratch_shapes=[
            pltpu.VMEM((NH_C, DK_C, DV_C), F32),
            pltpu.VMEM((8 + rows, QKV_C), F32),
        ],
        compiler_params=_cparams(("arbitrary", "arbitrary")),
        name="odd_mixer",
    )(proj, conv_hist, s0, conv_w, hp, norm_c.reshape(1, DV_C))
    return outs


def _post_kernel(mix_ref, x_ref, gate_ref, sc_ref, sh_ref, wout_ref, lng_ref, lnb_ref, wr_ref, br_ref,
                 x1_ref, h2_ref, ri_ref, rg_ref, seg_ref, cnt_ref, carry_s, *, tm):
    i = pl.program_id(0)

    @pl.when(i == 0)
    def _():
        carry_s[...] = jnp.zeros_like(carry_s)

    ng = max(1, tm // 256)
    tg = tm // ng
    ys = [_dot(mix_ref[q * tg:(q + 1) * tg, :], wout_ref[...]) for q in range(min(2, ng))]
    wr_b = wr_ref[...].astype(BF16)
    lts = []
    for q in range(ng):
        rq = slice(q * tg, (q + 1) * tg)
        x1 = _layer_norm_rows(ALPHA_DN * x_ref[rq, :] + (1.0 + gate_ref[...]) * ys[q], lng_ref[...], lnb_ref[...])
        if q + 2 < ng:
            ys.append(_dot(mix_ref[(q + 2) * tg:(q + 3) * tg, :], wout_ref[...]))
        x1_ref[rq, :] = x1
        h2 = x1 * (1.0 + sc_ref[...]) + sh_ref[...]
        h2b = h2.astype(BF16)
        h2_ref[rq, :] = h2b
        lts.append(_dot(wr_b, h2b, _NT))
    lt = (lts[0] if ng == 1 else jnp.concatenate(lts, axis=1)) + br_ref[:, 0:1]
    e_iota = lax.broadcasted_iota(I32, (N_EXPERTS, tm), 0).astype(F32)
    vals, idxs = [], []
    for _ in range(TOP_K):
        mx = jnp.max(lt, axis=0, keepdims=True)
        idx = jnp.min(jnp.where(lt == mx, e_iota, float(N_EXPERTS)), axis=0, keepdims=True)
        vals.append(mx)
        idxs.append(idx)
        lt = jnp.where(e_iota == idx, -jnp.inf, lt)
    ex = [jnp.exp(v - vals[0]) for v in vals]
    tot = ex[0] + ex[1] + ex[2] + ex[3]
    hot = [jnp.where(e_iota == idx, 1.0, 0.0) for idx in idxs]
    m_all = hot[0] + hot[1] + hot[2] + hot[3]
    m_all_b = m_all.astype(BF16)
    r = lax.broadcasted_iota(I32, (tm, tm), 0)
    c = lax.broadcasted_iota(I32, (tm, tm), 1)
    upper = jnp.where(r < c, 1.0, 0.0).astype(BF16)
    prefix = _dot(m_all_b, upper)
    re = lax.broadcasted_iota(I32, (N_EXPERTS, N_EXPERTS), 0)
    ce = lax.broadcasted_iota(I32, (N_EXPERTS, N_EXPERTS), 1)
    cnt_col = jnp.sum(m_all, axis=1, keepdims=True)
    m8_col = jnp.floor((cnt_col + 7.0) * 0.125)
    lower_e = jnp.where(re > ce, 1.0, 0.0).astype(BF16)
    off8_col = 8.0 * _dot(lower_e, jnp.broadcast_to(m8_col, (N_EXPERTS, LANES)).astype(BF16))[:, 0:1]
    base = off8_col + prefix
    dests = [jnp.sum(hk * base, axis=0, keepdims=True) for hk in hot]
    cnt_row = _dot(jnp.ones((8, tm), BF16), m_all_b, _NT)
    m8_row = jnp.floor((cnt_row + 7.0) * 0.125)
    upper_e = jnp.where(re < ce, 1.0, 0.0).astype(BF16)
    off8_row = 8.0 * _dot(m8_row.astype(BF16), upper_e)
    gc = carry_s[:, 0:N_EXPERTS]
    srow = lax.broadcasted_iota(I32, (8, N_EXPERTS), 0)
    seg = jnp.where(srow == 0, 8.0 * m8_row, jnp.where(srow == 1, off8_row, jnp.where(srow == 2, gc, 0.0)))
    seg_ref[...] = jnp.concatenate([seg, jnp.zeros((8, LANES - N_EXPERTS), F32)], axis=1).astype(I32)
    carry_s[:, 0:N_EXPERTS] = gc + 8.0 * m8_row
    cnt_ref[...] = carry_s[...]
    ri_ref[...] = jnp.concatenate(dests + [jnp.zeros((4, tm), F32)], axis=0).astype(I32)
    rg_ref[...] = jnp.concatenate([e / tot for e in ex] + [jnp.zeros((4, tm), F32)], axis=0)


def _post(mix, x2d, gate, scale, shift, w_out_bf16, ln_g, ln_b, wr_t, br, seq_len, tm):
    t = x2d.shape[0]
    per = seq_len // tm
    vec = lambda i: (i // per, 0, 0)
    const2 = lambda i: (0, 0)
    return pl.pallas_call(
        functools.partial(_post_kernel, tm=tm),
        out_shape=(
            jax.ShapeDtypeStruct((t, D_MODEL), F32),
            jax.ShapeDtypeStruct((t, D_MODEL), BF16),
            jax.ShapeDtypeStruct((t // tm, 8, tm), I32),
            jax.ShapeDtypeStruct((t // tm, 8, tm), F32),
            jax.ShapeDtypeStruct((t // tm, 8, LANES), I32),
            jax.ShapeDtypeStruct((8, LANES), F32),
        ),
        grid=(t // tm,),
        in_specs=[
            pl.BlockSpec((tm, D_MODEL), lambda i: (i, 0)),
            pl.BlockSpec((tm, D_MODEL), lambda i: (i, 0)),
            pl.BlockSpec((None, 1, D_MODEL), vec),
            pl.BlockSpec((None, 1, D_MODEL), vec),
            pl.BlockSpec((None, 1, D_MODEL), vec),
            pl.BlockSpec((D_MODEL, D_MODEL), const2),
            pl.BlockSpec((1, D_MODEL), const2),
            pl.BlockSpec((1, D_MODEL), const2),
            pl.BlockSpec((N_EXPERTS, D_MODEL), const2),
            pl.BlockSpec((N_EXPERTS, LANES), const2),
        ],
        out_specs=(
            pl.BlockSpec((tm, D_MODEL), lambda i: (i, 0)),
            pl.BlockSpec((tm, D_MODEL), lambda i: (i, 0)),
            pl.BlockSpec((None, 8, tm), lambda i: (i, 0, 0)),
            pl.BlockSpec((None, 8, tm), lambda i: (i, 0, 0)),
            pl.BlockSpec((None, 8, LANES), lambda i: (i, 0, 0)),
            pl.BlockSpec((8, LANES), const2),
        ),
        scratch_shapes=[pltpu.VMEM((8, LANES), F32)],
        compiler_params=_cparams(("arbitrary",)),
        name="post_router",
    )(mix, x2d, gate, scale, shift, w_out_bf16, ln_g, ln_b, wr_t, br)


SEG_FIELDS = 3


def _run_sizes(tm):
    top = 1 << (tm - 1).bit_length()
    return [s for s in (512, 256, 128, 64, 32, 16, 8) if s <= max(top, 8)]


def _for_each_run_piece(seg_ref, rstart_ref, tile, tm, fn):
    base = tile * (SEG_FIELDS * N_EXPERTS)

    def body(e, c):
        n8 = seg_ref[base + e]
        src = seg_ref[base + N_EXPERTS + e]
        dst = rstart_ref[e] + seg_ref[base + 2 * N_EXPERTS + e]
        def pieces(sizes, done):
            for size in sizes:
                @pl.when((n8 & size) != 0)
                def _(done=done, size=size):
                    fn(pl.multiple_of(src + done, 8), pl.multiple_of(dst + done, 8), size)
                done = done + (n8 & size)

        split = 128
        big = [s for s in _run_sizes(tm) if s >= split]
        if big:
            @pl.when(n8 >= split)
            def _():
                pieces(big, jnp.int32(0))
        pieces([s for s in _run_sizes(tm) if s < split], n8 & ~jnp.int32(split - 1))
        return c

    lax.fori_loop(0, N_EXPERTS, body, 0)


def _wait_tile_rows(seg_ref, tile, cb, wait_rows):
    base = tile * (SEG_FIELDS * N_EXPERTS)
    last = N_EXPERTS - 1
    total = seg_ref[base + last] + seg_ref[base + N_EXPERTS + last]
    size = 1 << (cb.bit_length() - 1)
    while size >= 8:
        @pl.when((total & size) != 0)
        def _(size=size):
            wait_rows(size)
        size //= 2


def _dispatch_kernel(zblk_ref, rstart_ref, seg_ref, ri_ref, h_ref, xs_ref, zero_s, cbuf, sem, zsem,
                     *, tm, bm, cb, ntiles):
    i = pl.program_id(0)
    slot = i % 2

    def run_copies(tile, sl, wait):
        if wait:
            def wait_rows(size):
                pltpu.make_async_copy(cbuf.at[sl, pl.ds(0, size), :], xs_ref.at[pl.ds(0, size), :], sem.at[sl]).wait()
            _wait_tile_rows(seg_ref, tile, cb, wait_rows)
            return

        def piece(src, dst, size):
            pltpu.make_async_copy(cbuf.at[sl, pl.ds(src, size), :], xs_ref.at[pl.ds(dst, size), :], sem.at[sl]).start()
        _for_each_run_piece(seg_ref, rstart_ref, tile, tm, piece)

    def zero_copy(e):
        return pltpu.make_async_copy(zero_s, xs_ref.at[pl.ds(zblk_ref[e] * bm, bm), :], zsem)

    @pl.when(i == 0)
    def _():
        zero_s[...] = jnp.zeros_like(zero_s)

        def zstart(e, c):
            @pl.when(zblk_ref[e] >= 0)
            def _():
                zero_copy(e).start()
            return c

        def zwait(e, c):
            @pl.when(zblk_ref[e] >= 0)
            def _():
                zero_copy(e).wait()
            return c

        lax.fori_loop(0, N_EXPERTS, zstart, 0)
        lax.fori_loop(0, N_EXPERTS, zwait, 0)

    @pl.when(i >= 2)
    def _():
        run_copies(i - 2, slot, True)

    rows = lax.broadcasted_iota(I32, (cb, tm), 0)
    hit = rows == ri_ref[0:1, :]
    for k in range(1, TOP_K):
        hit = hit | (rows == ri_ref[k:k + 1, :])
    perm = jnp.where(hit, 1.0, 0.0).astype(BF16)
    cbuf[slot] = _dot(perm, h_ref[...])
    run_copies(i, slot, False)

    @pl.when(i == ntiles - 1)
    def _():
        if ntiles > 1:
            run_copies(i - 1, 1 - slot, True)
        run_copies(i, slot, True)


def _dispatch(h2, route_i, seg_flat, row_start, zblk, n_rows, tm, bm):
    t = h2.shape[0]
    ntiles = t // tm
    cb = TOP_K * tm + 8 * N_EXPERTS
    return pl.pallas_call(
        functools.partial(_dispatch_kernel, tm=tm, bm=bm, cb=cb, ntiles=ntiles),
        out_shape=jax.ShapeDtypeStruct((n_rows, D_MODEL), F32),
        grid_spec=pltpu.PrefetchScalarGridSpec(
            num_scalar_prefetch=3,
            grid=(ntiles,),
            in_specs=[
                pl.BlockSpec((None, 2 * TOP_K, tm), lambda i, z, r, s: (i, 0, 0)),
                pl.BlockSpec((tm, D_MODEL), lambda i, z, r, s: (i, 0)),
            ],
            out_specs=pl.BlockSpec(memory_space=pl.ANY),
            scratch_shapes=[
                pltpu.VMEM((bm, D_MODEL), F32),
                pltpu.VMEM((2, cb, D_MODEL), F32),
                pltpu.SemaphoreType.DMA((2,)),
                pltpu.SemaphoreType.DMA,
            ],
        ),
        compiler_params=_cparams(("arbitrary",)),
        name="dispatch",
    )(zblk, row_start, seg_flat, route_i, h2)


def _ffn_kernel(be_ref, grp_ref, ia_ref, ib_ref, nv_ref, xa_ref, xb_ref, w1_ref, b1_ref, w2_ref, b2_ref,
                ya_ref, yb_ref, w1_s, w2_s):
    b = pl.program_id(0)

    def expert_block(x_ref, y_ref):
        xb = x_ref[...].astype(BF16)
        glu = _dot(xb, w1_s[:, :D_FF]) + b1_ref[:, :D_FF]
        lin = _dot(xb, w1_s[:, D_FF:]) + b1_ref[:, D_FF:]
        glu = jnp.minimum(glu, SWIGLU_LIMIT)
        lin = jnp.clip(lin, -SWIGLU_LIMIT, SWIGLU_LIMIT)
        act = glu * _sigmoid(SWIGLU_ALPHA * glu) * (lin + 1.0)
        y_ref[...] = _dot(act.astype(BF16), w2_s[...]) + b2_ref[...]

    @pl.when(b < nv_ref[0])
    def _():
        @pl.when((b == 0) | (be_ref[b] != be_ref[jnp.maximum(b - 1, 0)]))
        def _():
            step = 256
            for r0 in range(0, D_MODEL, step):
                w1_s[r0:r0 + step, :] = w1_ref[r0:r0 + step, :].astype(BF16)
            for r0 in range(0, D_FF, step):
                w2_s[r0:r0 + step, :] = w2_ref[r0:r0 + step, :].astype(BF16)

        @pl.when(grp_ref[b] == 0)
        def _():
            expert_block(xa_ref, ya_ref)

        @pl.when(grp_ref[b] == 1)
        def _():
            expert_block(xb_ref, yb_ref)


def _ffn(xs_a, xs_b, steps, layer, w1, b1, w2, b2, bm_a, bm_b):
    step_e, step_g, step_ia, step_ib, n_valid = steps
    nsteps = step_e.shape[0]
    row_a = lambda b, be, g, ia, ib, nv: (ia[b], 0)
    row_b = lambda b, be, g, ia, ib, nv: (ib[b], 0)
    wsel = lambda b, be, g, ia, ib, nv: (layer, be[b], 0, 0)
    return pl.pallas_call(
        _ffn_kernel,
        out_shape=(jax.ShapeDtypeStruct(xs_a.shape, F32), jax.ShapeDtypeStruct(xs_b.shape, F32)),
        grid_spec=pltpu.PrefetchScalarGridSpec(
            num_scalar_prefetch=5,
            grid=(nsteps,),
            in_specs=[
                pl.BlockSpec((bm_a, D_MODEL), row_a),
                pl.BlockSpec((bm_b, D_MODEL), row_b),
                pl.BlockSpec((None, None, D_MODEL, 2 * D_FF), wsel),
                pl.BlockSpec((None, None, 1, 2 * D_FF), wsel),
                pl.BlockSpec((None, None, D_FF, D_MODEL), wsel),
                pl.BlockSpec((None, None, 1, D_MODEL), wsel),
            ],
            out_specs=(pl.BlockSpec((bm_a, D_MODEL), row_a), pl.BlockSpec((bm_b, D_MODEL), row_b)),
            scratch_shapes=[pltpu.VMEM((D_MODEL, 2 * D_FF), BF16), pltpu.VMEM((D_FF, D_MODEL), BF16)],
        ),
        compiler_params=_cparams(("arbitrary",)),
        name="expert_ffn",
    )(step_e, step_g, step_ia, step_ib, n_valid, xs_a, xs_b, w1, b1, w2, b2)


def _combine_kernel(rstart_ref, seg_ref, ri_ref, x_ref, gate_ref, rg_ref, lng_ref, lnb_ref, ys_ref, o_ref,
                    ybuf, yb_s, wt_s, sem, *, tm, cb, ntiles):
    i = pl.program_id(0)
    slot = i % 2

    def run_copies(tile, sl, wait):
        if wait:
            def wait_rows(size):
                pltpu.make_async_copy(ys_ref.at[pl.ds(0, size), :], ybuf.at[sl, pl.ds(0, size), :], sem.at[sl]).wait()
            _wait_tile_rows(seg_ref, tile, cb, wait_rows)
            return

        def piece(loc, glob, size):
            pltpu.make_async_copy(ys_ref.at[pl.ds(glob, size), :], ybuf.at[sl, pl.ds(loc, size), :], sem.at[sl]).start()
        _for_each_run_piece(seg_ref, rstart_ref, tile, tm, piece)

    @pl.when(i == 0)
    def _():
        ybuf[...] = jnp.zeros_like(ybuf)
        run_copies(0, 0, False)

    @pl.when(i + 1 < ntiles)
    def _():
        run_copies(i + 1, 1 - slot, False)

    dest_t = _tr(jnp.concatenate([ri_ref[...].astype(F32), jnp.zeros((LANES - 8, tm), F32)], axis=0))
    g_t = _tr(jnp.concatenate([rg_ref[...], jnp.zeros((LANES - 8, tm), F32)], axis=0))
    lane_step = 256
    nq = max(1, tm // 256)
    tq = tm // nq

    def build(q):
        rq = slice(q * tq, (q + 1) * tq)
        for c0 in range(0, cb, lane_step):
            cols = (lax.broadcasted_iota(I32, (tq, lane_step), 1) + c0).astype(F32)
            w = jnp.where(cols == dest_t[rq, 0:1], g_t[rq, 0:1], 0.0)
            for k in range(1, TOP_K):
                w = w + jnp.where(cols == dest_t[rq, k:k + 1], g_t[rq, k:k + 1], 0.0)
            wt_s[rq, c0:c0 + lane_step] = w.astype(BF16)

    def product(q):
        rq = slice(q * tq, (q + 1) * tq)
        return _dot(wt_s[rq, :], yb_s[...])

    def finish(q, moe):
        rq = slice(q * tq, (q + 1) * tq)
        o_ref[rq, :] = _layer_norm_rows(ALPHA_DN * x_ref[rq, :] + (1.0 + gate_ref[...]) * moe,
                                        lng_ref[...], lnb_ref[...])

    build(0)
    run_copies(i, slot, True)
    row_step = 256
    for r0 in range(0, cb, row_step):
        yb_s[r0:r0 + row_step, :] = ybuf[slot, r0:r0 + row_step, :].astype(BF16)
    moes = {}
    for q in range(nq):
        if q + 1 < nq:
            build(q + 1)
        moes[q] = product(q)
        if q >= 1:
            finish(q - 1, moes.pop(q - 1))
    finish(nq - 1, moes.pop(nq - 1))


def _combine(row_start, seg_flat, route_i, x1, gate, route_g, ln_g, ln_b, ys, seq_len, tm):
    t = x1.shape[0]
    per = seq_len // tm
    ntiles = t // tm
    cb = TOP_K * tm + 8 * N_EXPERTS
    return pl.pallas_call(
        functools.partial(_combine_kernel, tm=tm, cb=cb, ntiles=ntiles),
        out_shape=jax.ShapeDtypeStruct((t, D_MODEL), F32),
        grid_spec=pltpu.PrefetchScalarGridSpec(
            num_scalar_prefetch=2,
            grid=(ntiles,),
            in_specs=[
                pl.BlockSpec((None, 2 * TOP_K, tm), lambda i, r, s: (i, 0, 0)),
                pl.BlockSpec((tm, D_MODEL), lambda i, r, s: (i, 0)),
                pl.BlockSpec((None, 1, D_MODEL), lambda i, r, s: (i // per, 0, 0)),
                pl.BlockSpec((None, 8, tm), lambda i, r, s: (i, 0, 0)),
                pl.BlockSpec((1, D_MODEL), lambda i, r, s: (0, 0)),
                pl.BlockSpec((1, D_MODEL), lambda i, r, s: (0, 0)),
                pl.BlockSpec(memory_space=pl.ANY),
            ],
            out_specs=pl.BlockSpec((tm, D_MODEL), lambda i, r, s: (i, 0)),
            scratch_shapes=[
                pltpu.VMEM((2, cb, D_MODEL), F32),
                pltpu.VMEM((cb, D_MODEL), BF16),
                pltpu.VMEM((tm, cb), BF16),
                pltpu.SemaphoreType.DMA((2,)),
            ],
        ),
        compiler_params=_cparams(("arbitrary",)),
        name="combine",
    )(row_start, seg_flat, route_i, x1, gate, route_g, ln_g, ln_b, ys)


def _max_blocks(t, tm, bm):
    return -(-(t * TOP_K + 7 * N_EXPERTS * (t // tm)) // bm) + N_EXPERTS


def _group_tables(counts, bm):
    cnt = counts[0, :N_EXPERTS].astype(I32)
    nblk = (cnt + bm - 1) // bm
    blk_end = jnp.cumsum(nblk)
    row_start = ((blk_end - nblk) * bm).astype(I32)
    zblk = jnp.where((cnt % bm) != 0, blk_end - 1, -1).astype(I32)
    return nblk, row_start, zblk


def _ffn_steps(nblk_a, nblk_b, nsteps):
    per_e = nblk_a + nblk_b
    ends = jnp.cumsum(per_e)
    n_valid = ends[-1]
    s = jnp.minimum(jnp.arange(nsteps, dtype=I32), n_valid - 1)
    step_e = jnp.minimum(jnp.sum((ends[None, :] <= s[:, None]).astype(I32), axis=1), N_EXPERTS - 1)
    onehot = (step_e[:, None] == jnp.arange(N_EXPERTS, dtype=I32)[None, :]).astype(I32)
    off = s - jnp.sum(onehot * (ends - per_e)[None, :], axis=1)
    in_b = off >= jnp.sum(onehot * nblk_a[None, :], axis=1)
    live = jnp.arange(nsteps, dtype=I32) < n_valid
    step_g = jnp.where(live, in_b.astype(I32), 2)
    step_ia = jnp.maximum(jnp.cumsum((step_g == 0).astype(I32)) - 1, 0)
    step_ib = jnp.maximum(jnp.cumsum((step_g == 1).astype(I32)) - 1, 0)
    return (step_e.astype(I32), step_g.astype(I32), step_ia.astype(I32), step_ib.astype(I32),
            n_valid.reshape(1).astype(I32))


def _group_cfg(batch, seq_len):
    if seq_len >= 512:
        return dict(tm=512, rows=256, bm=512)
    return dict(tm=seq_len, rows=seq_len, bm=128)


def kernel(x_prompt, x_sample, c_prompt, c_sample, state_a_C, state_a_n, state_a_m, cache_b_k, cache_b_v,
           state_c_S, state_c_conv, w_ada, b_ada, ln_g, ln_b, w_in_even, b_in_even, norm_a, sink_b, rel_bias,
           w_out_even, w_in_odd, conv_c, a_log_c, dt_bias_c, norm_c, w_out_odd, w_router, b_router,
           w_e1, b_e1, w_e2, b_e2):
    bp, lp, _ = x_prompt.shape
    bs, ls, _ = x_sample.shape
    groups = [dict(b=bp, l=lp, x=x_prompt.reshape(bp * lp, D_MODEL), **_group_cfg(bp, lp)),
              dict(b=bs, l=ls, x=x_sample.reshape(bs * ls, D_MODEL), **_group_cfg(bs, ls))]
    mod = _ada(jnp.concatenate([c_prompt, c_sample], axis=0), w_ada, b_ada)
    offs = [0, bp]
    states = [dict(), dict()]
    b1_all = b_e1.reshape(DEPTH, N_EXPERTS, 1, 2 * D_FF)
    b2_all = b_e2.reshape(DEPTH, N_EXPERTS, 1, D_MODEL)
    for l in range(DEPTH):
        e = l // 2
        if l % 2 == 0:
            w = w_in_even[e]
            sz = (512, 512, 512, 512, 4, 4, 512, 128, 128)
            o = [sum(sz[:j]) for j in range(len(sz) + 1)]
            pad = jnp.zeros((D_MODEL, LANES - 2 * NH_A), w.dtype)
            w_in = jnp.concatenate([w[:, o[0]:o[4]], w[:, o[6]:o[9]], w[:, o[4]:o[6]], pad], axis=1).astype(BF16)
            bb = b_in_even[e]
            b_in = jnp.concatenate([bb[o[0]:o[4]], bb[o[6]:o[9]], bb[o[4]:o[6]],
                                    jnp.zeros((LANES - 2 * NH_A,), bb.dtype)]).reshape(1, N_EVEN_COLS)
            w_out = w_out_even[e].astype(BF16)
        else:
            w = w_in_odd[e]
            pad = jnp.zeros((D_MODEL, LANES - 2 * NH_C), w.dtype)
            w_in = jnp.concatenate([w, pad], axis=1).astype(BF16)
            b_in = jnp.zeros((1, N_ODD_COLS), F32)
            w_out = w_out_odd[e].astype(BF16)
        wr_t = w_router[l].T
        br = jnp.broadcast_to(b_router[l][:, None], (N_EXPERTS, LANES))
        moe_in = []
        for gi, gr in enumerate(groups):
            nb, sl = gr['b'], gr['l']
            m = mod[l, offs[gi]:offs[gi] + nb].reshape(nb, 6, 1, D_MODEL)
            m6 = [m[:, j] for j in range(6)]
            proj = _inproj(gr['x'], m6[1], m6[0], w_in, b_in, sl, gr['tm'])
            st = states[gi]
            if l % 2 == 0:
                if gi == 0:
                    kh = jnp.zeros((nb, WINDOW, NKV_B, DH_B), F32)
                    vh = kh
                    c0 = jnp.zeros((nb, NH_A, DK_A, DV_A), F32)
                    n0 = jnp.zeros((nb, NH_A, DK_A), F32)
                    m0 = jnp.zeros((nb, NH_A), F32)
                else:
                    kh, vh, c0, n0, m0 = cache_b_k[e], cache_b_v[e], state_a_C[e], state_a_n[e], state_a_m[e]
                bias_tab = _swa_bias_table(rel_bias, gr['rows'])
                mix, c1, n1, m1, k1, v1 = _even_mixer(proj, kh, vh, c0, n0, m0, bias_tab, norm_a[e], sink_b[e],
                                                      nb, sl, gr['rows'], gi == 1)
                for name, val in (('a_C', c1), ('a_n', n1), ('a_m', m1), ('b_k', k1), ('b_v', v1)):
                    st.setdefault(name, []).append(val)
            else:
                if gi == 0:
                    cv0 = jnp.zeros((nb, CONV_W - 1, QKV_C), F32)
                    s0 = jnp.zeros((nb, NH_C, DK_C, DV_C), F32)
                else:
                    cv0, s0 = state_c_conv[e], state_c_S[e]
                mix, s1, cv1 = _odd_mixer(proj, cv0, s0, conv_c[e], a_log_c[e], dt_bias_c[e], norm_c[e], nb, sl,
                                          gr['rows'])
                st.setdefault('c_S', []).append(s1)
                st.setdefault('c_conv', []).append(cv1)
            x1, h2, route_i, route_g, seg, counts = _post(
                mix, gr['x'], m6[2], m6[4], m6[3], w_out, ln_g[l, 0].reshape(1, D_MODEL),
                ln_b[l, 0].reshape(1, D_MODEL), wr_t, br, sl, gr['tm'])
            t, tm, bm = nb * sl, gr['tm'], gr['bm']
            ntiles = t // tm
            nblk, row_start, zblk = _group_tables(counts, bm)
            seg_flat = seg[:, :SEG_FIELDS, :N_EXPERTS].reshape(ntiles * SEG_FIELDS * N_EXPERTS)
            xs = _dispatch(h2, route_i, seg_flat, row_start, zblk, _max_blocks(t, tm, bm) * bm, tm, bm)
            moe_in.append(dict(xs=xs, nblk=nblk, row_start=row_start, seg_flat=seg_flat, route_i=route_i,
                               route_g=route_g, x1=x1, gate=m6[5]))
        ga, gb = moe_in
        steps = _ffn_steps(ga['nblk'], gb['nblk'], sum(_max_blocks(g['b'] * g['l'], g['tm'], g['bm']) for g in groups))
        ys = _ffn(ga['xs'], gb['xs'], steps, l, w_e1, b1_all, w_e2, b2_all, groups[0]['bm'], groups[1]['bm'])
        for gi, gr in enumerate(groups):
            mi = moe_in[gi]
            gr['x'] = _combine(mi['row_start'], mi['seg_flat'], mi['route_i'], mi['x1'], mi['gate'], mi['route_g'],
                               ln_g[l, 1].reshape(1, D_MODEL), ln_b[l, 1].reshape(1, D_MODEL), ys[gi],
                               gr['l'], gr['tm'])
    outs = [groups[0]['x'].reshape(bp, lp, D_MODEL), groups[1]['x'].reshape(bs, ls, D_MODEL)]
    for gi in range(2):
        for name in ('a_C', 'a_n', 'a_m', 'b_k', 'b_v', 'c_S', 'c_conv'):
            outs.append(jnp.stack(states[gi][name]))
    return tuple(outs)
```

```python
import functools
import math

import jax
import jax.numpy as jnp
from jax import lax
from jax.experimental import pallas as pl
from jax.experimental.pallas import tpu as pltpu

F32 = jnp.float32
BF16 = jnp.bfloat16
I32 = jnp.int32

D_MODEL = 1024
CHUNK = 64
NH_A, DK_A, DV_A = 4, 128, 128
W_A = NH_A * DV_A
NH_B, NKV_B, DH_B, WINDOW = 8, 2, 64, 128
G_B = NH_B // NKV_B
W_B = NH_B * DH_B
NUM_BUCKETS, MAX_DISTANCE = 32, 256
NH_C, DK_C, DV_C, CONV_W = 8, 128, 128, 4
QKV_C = NH_C * (2 * DK_C + DV_C)
N_EXPERTS, TOP_K, D_FF = 32, 4, 1024
SWIGLU_LIMIT, SWIGLU_ALPHA = 7.0, 1.702
DEPTH = 2
ALPHA_DN = (2 * DEPTH) ** 0.25
LN_EPS = 1e-5
RMS_EPS = 1e-6

LANES = 128
E_QA, E_KA, E_VA, E_OA, E_QB, E_KB, E_VB, E_G = 0, 512, 1024, 1536, 2048, 2560, 2688, 2816
N_EVEN_COLS = 2944
O_QKV, O_Z, O_G = 0, 3072, 4096
N_ODD_COLS = 4224

VMEM_LIMIT = 56 * 1024 * 1024


def _cparams(sem):
    return pltpu.CompilerParams(dimension_semantics=sem, vmem_limit_bytes=VMEM_LIMIT)


def _softplus(x):
    return jnp.maximum(x, 0.0) + jnp.log(1.0 + jnp.exp(-jnp.abs(x)))


def _sigmoid(x):
    return 1.0 / (1.0 + jnp.exp(-x))


def _split_bf16(a):
    hi = a.astype(BF16)
    lo = (a - hi.astype(F32)).astype(BF16)
    return hi, lo


_NN = (((1,), (0,)), ((), ()))
_NT = (((1,), (1,)), ((), ()))
_TN = (((0,), (0,)), ((), ()))


def _dot(a, b, dims=_NN):
    return lax.dot_general(a, b, dims, preferred_element_type=F32)


def _dot_x3(a, b, dims=_NN):
    ah, al = _split_bf16(a)
    bh, bl = _split_bf16(b)
    return _dot(ah, bh, dims) + _dot(ah, bl, dims) + _dot(al, bh, dims)


def _dot_exact_lhs(a_bf16, b, dims=_NN):
    b0 = b.astype(BF16)
    r1 = b - b0.astype(F32)
    b1 = r1.astype(BF16)
    b2 = (r1 - b1.astype(F32)).astype(BF16)
    return _dot(a_bf16, b0, dims) + _dot(a_bf16, b1, dims) + _dot(a_bf16, b2, dims)


def _tr(x):
    r = x.shape[0]
    rp = -(-r // LANES) * LANES
    if rp != r:
        x = jnp.concatenate([x, jnp.zeros((rp - r, x.shape[1]), x.dtype)], axis=0)
    return x.T[:, :r]


def _tri(n, strict=False):
    r = lax.broadcasted_iota(I32, (n, n), 0)
    c = lax.broadcasted_iota(I32, (n, n), 1)
    return (r > c) if strict else (r >= c)


def _layer_norm_rows(v, g, b):
    mu = jnp.mean(v, axis=-1, keepdims=True)
    d = v - mu
    var = jnp.mean(d * d, axis=-1, keepdims=True)
    return d * lax.rsqrt(var + LN_EPS) * g + b


def _ada_kernel(c_ref, w_ref, b_ref, o_ref):
    c = c_ref[...]
    a = (c * _sigmoid(c)).astype(BF16)
    o_ref[...] = _dot(a, w_ref[...].astype(BF16)) + b_ref[...]


def _ada(c_all, w_ada, b_ada):
    nb = c_all.shape[0]
    tn = 1536
    return pl.pallas_call(
        _ada_kernel,
        out_shape=jax.ShapeDtypeStruct((DEPTH, nb, 6 * D_MODEL), F32),
        grid=(DEPTH, 6 * D_MODEL // tn),
        in_specs=[
            pl.BlockSpec((nb, D_MODEL), lambda l, j: (0, 0)),
            pl.BlockSpec((None, D_MODEL, tn), lambda l, j: (l, 0, j)),
            pl.BlockSpec((None, 1, tn), lambda l, j: (l, 0, j)),
        ],
        out_specs=pl.BlockSpec((None, nb, tn), lambda l, j: (l, 0, j)),
        compiler_params=_cparams(("arbitrary", "arbitrary")),
        name="ada",
    )(c_all, w_ada, b_ada.reshape(DEPTH, 1, 6 * D_MODEL))


def _inproj_kernel(x_ref, sc_ref, sh_ref, w_ref, b_ref, o_ref, *, n_cols, col_step):
    h = (x_ref[...] * (1.0 + sc_ref[...]) + sh_ref[...]).astype(BF16)
    for c0 in range(0, n_cols, col_step):
        c1 = min(c0 + col_step, n_cols)
        o_ref[:, c0:c1] = _dot(h, w_ref[:, c0:c1]) + b_ref[:, c0:c1]


def _inproj(x2d, scale, shift, w_bf16, bias, seq_len, tm):
    t = x2d.shape[0]
    n = w_bf16.shape[1]
    per = seq_len // tm
    return pl.pallas_call(
        functools.partial(_inproj_kernel, n_cols=n, col_step=1024),
        out_shape=jax.ShapeDtypeStruct((t, n), F32),
        grid=(t // tm,),
        in_specs=[
            pl.BlockSpec((tm, D_MODEL), lambda i: (i, 0)),
            pl.BlockSpec((None, 1, D_MODEL), lambda i: (i // per, 0, 0)),
            pl.BlockSpec((None, 1, D_MODEL), lambda i: (i // per, 0, 0)),
            pl.BlockSpec((D_MODEL, n), lambda i: (0, 0)),
            pl.BlockSpec((1, n), lambda i: (0, 0)),
        ],
        out_specs=pl.BlockSpec((tm, n), lambda i: (i, 0)),
        compiler_params=_cparams(("arbitrary",)),
        name="inproj",
    )(x2d, scale, shift, w_bf16, bias)


def _even_kernel(proj_ref, kh0_ref, vh0_ref, c0_ref, n0_ref, m0_ref, bias_ref, norma_ref, sink_ref,
                 mix_ref, cout_ref, nout_ref, mout_ref, kout_ref, vout_ref,
                 c_s, n_s, m_s, kh_s, vh_s, *, rows, hist_valid):
    R = rows
    KW = WINDOW + R
    i = pl.program_id(1)

    @pl.when(i == 0)
    def _():
        c_s[...] = c0_ref[...]
        n_s[...] = n0_ref[...]
        m_s[...] = m0_ref[...]
        kh_s[...] = kh0_ref[...]
        vh_s[...] = vh0_ref[...]

    CS = CHUNK
    g = proj_ref[:, E_G:E_G + LANES]
    lf = -_softplus(-g)
    rr = lax.broadcasted_iota(I32, (R, R), 0)
    cc = lax.broadcasted_iota(I32, (R, R), 1)
    tri_b = jnp.where((rr >= cc) & (rr // CS == cc // CS), 1.0, 0.0).astype(BF16)
    b_all = _dot_exact_lhs(tri_b, lf)
    g_t = _tr(g)
    b_t = _tr(b_all)
    causal = _tri(CS)
    scale_a = DK_A ** -0.5
    chunks = list(range(0, R, CS))
    P = [(c0, h) for c0 in chunks for h in range(NH_A)]
    nP = len(P)
    rsl = [slice(c0, c0 + CS) for c0, _ in P]
    b_col = [b_all[rsl[p], NH_A + h:NH_A + h + 1] for p, (_, h) in enumerate(P)]
    b_row = [b_t[NH_A + h:NH_A + h + 1, rsl[p]] for p, (_, h) in enumerate(P)]
    i_row = [g_t[h:h + 1, rsl[p]] for p, (_, h) in enumerate(P)]
    i_col = [g[rsl[p], h:h + 1] for p, (_, h) in enumerate(P)]
    logw = [jnp.where(causal, b_col[p] - b_row[p] + i_row[p], -jnp.inf) for p in range(nP)]
    lmax = [jnp.max(logw[p], axis=-1, keepdims=True) for p in range(nP)]
    qb = [(proj_ref[rsl[p], E_QA + h * DK_A:E_QA + (h + 1) * DK_A] * scale_a).astype(BF16)
          for p, (_, h) in enumerate(P)]
    kf = [proj_ref[rsl[p], E_KA + h * DK_A:E_KA + (h + 1) * DK_A] for p, (_, h) in enumerate(P)]
    vb = [proj_ref[rsl[p], E_VA + h * DV_A:E_VA + (h + 1) * DV_A].astype(BF16) for p, (_, h) in enumerate(P)]
    qk = [_dot(qb[p], kf[p].astype(BF16), _NT) for p in range(nP)]
    b_last = [b_col[p][CS - 1:CS, :] for p in range(nP)]

    k_win = jnp.concatenate([kh_s[...], proj_ref[:, E_KB:E_KB + NKV_B * DH_B]], axis=0)
    v_win = jnp.concatenate([vh_s[...], proj_ref[:, E_VB:E_VB + NKV_B * DH_B]], axis=0)
    k_win_b = k_win.astype(BF16)
    v_win_b = v_win.astype(BF16)
    if not hist_valid:
        key_pos = lax.broadcasted_iota(I32, (R, KW), 1) + (i * R - WINDOW)
        key_ok = key_pos >= 0
    scale_b = DH_B ** -0.5
    lo = lax.broadcasted_iota(I32, (1, LANES), 1) < DH_B
    k_swp = pltpu.roll(k_win, DH_B, 1).astype(BF16)
    v_swp = pltpu.roll(v_win, DH_B, 1).astype(BF16)
    zk = jnp.zeros_like(k_win_b)
    k_both = [jnp.where(lo, k_win_b, k_swp), jnp.where(lo, k_swp, k_win_b)]
    v_lo = [jnp.where(lo, v_win_b, zk), jnp.where(lo, v_swp, zk)]
    v_hi = [jnp.where(lo, zk, v_swp), jnp.where(lo, zk, v_win_b)]
    npairs = NH_B // 2
    qp = [proj_ref[:, E_QB + j * LANES:E_QB + (j + 1) * LANES].astype(BF16) for j in range(npairs)]
    zq = jnp.zeros_like(qp[0])
    qh = [jnp.where(lo, qp[hd // 2], zq) if hd % 2 == 0 else jnp.where(lo, zq, qp[hd // 2]) for hd in range(NH_B)]
    scores = [_dot(qh[hd], k_both[hd // G_B], _NT) * scale_b + bias_ref[hd] for hd in range(NH_B)]
    if not hist_valid:
        scores = [jnp.where(key_ok, s, -jnp.inf) for s in scores]
    sks = [sink_ref[hd] for hd in range(NH_B)]
    mxs = [jnp.maximum(jnp.max(scores[hd], axis=-1, keepdims=True), sks[hd]) for hd in range(NH_B)]
    ps = [jnp.exp(scores[hd] - mxs[hd]) for hd in range(NH_B)]
    dens = [jnp.sum(ps[hd], axis=-1, keepdims=True) + jnp.exp(sks[hd] - mxs[hd]) for hd in range(NH_B)]
    pbs = [(ps[hd] / dens[hd]).astype(BF16) for hd in range(NH_B)]

    m_prev, m_inter, m_t, m_new = [None] * nP, [None] * nP, [None] * nP, [None] * nP
    for h in range(NH_A):
        run = m_s[h:h + 1, 0:1]
        for ci in range(len(chunks)):
            p = ci * NH_A + h
            m_prev[p] = run
            m_inter[p] = b_col[p] + run
            m_t[p] = jnp.maximum(m_inter[p], lmax[p])
            run = m_t[p][CS - 1:CS, :]
            m_new[p] = run
        m_s[h:h + 1, :] = jnp.broadcast_to(run, (1, LANES))
    w = [jnp.exp(logw[p] - m_t[p]) for p in range(nP)]
    s = [qk[p] * w[p] for p in range(nP)]
    sv = [_dot(s[p].astype(BF16), vb[p]) for p in range(nP)]
    ssum = [jnp.sum(s[p], axis=-1, keepdims=True) for p in range(nP)]
    kfac = [jnp.exp(b_last[p] - b_col[p] + i_col[p] - m_new[p]) for p in range(nP)]
    kw = [kf[p] * kfac[p] for p in range(nP)]
    kwt = [_tr(kw[p]).astype(BF16) for p in range(nP)]
    kv = [_dot(kwt[p], vb[p]) for p in range(nP)]
    ksum = [jnp.sum(kw[p], axis=0, keepdims=True) for p in range(nP)]
    carry = [jnp.exp(b_last[p] + m_prev[p] - m_new[p]) for p in range(nP)]
    dec = [jnp.exp(m_inter[p] - m_t[p]) for p in range(nP)]
    floor_ = [jnp.exp(-m_t[p]) for p in range(nP)]
    c_prev, n_prev = [None] * nP, [None] * nP
    for h in range(NH_A):
        n_run = n_s[h:h + 1, :]
        c_run = c_s[h]
        for ci in range(len(chunks)):
            p = ci * NH_A + h
            c_prev[p], n_prev[p] = c_run, n_run
            c_run = carry[p] * c_run + kv[p]
            n_run = carry[p] * n_run + ksum[p]
        c_s[h] = c_run
        n_s[h:h + 1, :] = n_run
    qc = [_dot(qb[p], c_prev[p].astype(BF16)) for p in range(nP)]
    qn = [jnp.sum(qb[p].astype(F32) * n_prev[p].astype(BF16).astype(F32), axis=-1, keepdims=True) for p in range(nP)]
    num = [dec[p] * qc[p] + sv[p] for p in range(nP)]
    den = [dec[p] * qn[p] + ssum[p] for p in range(nP)]
    hh = [num[p] / jnp.maximum(jnp.abs(den[p]), floor_[p]) for p in range(nP)]
    mu = [jnp.mean(hh[p], axis=-1, keepdims=True) for p in range(nP)]
    dd = [hh[p] - mu[p] for p in range(nP)]
    var = [jnp.mean(dd[p] * dd[p], axis=-1, keepdims=True) for p in range(nP)]
    rs_ = [lax.rsqrt(var[p] + LN_EPS) for p in range(nP)]
    for p, (_, h) in enumerate(P):
        og = proj_ref[rsl[p], E_OA + h * DV_A:E_OA + (h + 1) * DV_A]
        ha = dd[p] * rs_[p] * norma_ref[h:h + 1, :] * _sigmoid(og)
        mix_ref[rsl[p], h * DV_A:(h + 1) * DV_A] = ha.astype(mix_ref.dtype)

    cout_ref[...] = c_s[...]
    nout_ref[...] = n_s[...]
    mout_ref[...] = m_s[...]

    outs = [_dot(pbs[hd], (v_lo if hd % 2 == 0 else v_hi)[hd // G_B]) for hd in range(NH_B)]
    for j in range(npairs):
        mix_ref[:, W_A + j * LANES:W_A + (j + 1) * LANES] = (outs[2 * j] + outs[2 * j + 1]).astype(mix_ref.dtype)

    kh_s[...] = k_win[R:, :]
    vh_s[...] = v_win[R:, :]
    kout_ref[...] = kh_s[...]
    vout_ref[...] = vh_s[...]


def _rel_bucket(rel):
    nb = NUM_BUCKETS // 2
    max_exact = nb // 2
    n = jnp.abs(rel)
    nf = jnp.maximum(n, 1).astype(F32)
    large = max_exact + (jnp.log(nf / max_exact) / math.log(MAX_DISTANCE / max_exact)
                         * (nb - max_exact)).astype(I32)
    large = jnp.minimum(large, nb - 1)
    return jnp.where(rel > 0, nb, 0) + jnp.where(n < max_exact, n, large)


def _swa_bias_table(rel_bias, rows):
    kw = WINDOW + rows
    qi = jnp.arange(rows)[:, None]
    kj = jnp.arange(kw)[None, :]
    bucket = _rel_bucket(kj - WINDOW - qi)
    rb = rel_bias.astype(F32)
    bias = jnp.zeros((NH_B, rows, kw), F32)
    for b in range(NUM_BUCKETS):
        bias = jnp.where((bucket == b)[None], rb[b][:, None, None], bias)
    lo = (qi // CHUNK) * CHUNK
    ok = (kj >= lo) & (kj < lo + WINDOW + CHUNK)
    return jnp.where(ok[None], bias, -jnp.inf)


def _even_mixer(proj, k_hist, v_hist, c0, n0, m0, bias_tab, norm_a, sink, batch, seq_len, rows, hist_valid):
    nsteps = seq_len // rows
    kw = WINDOW + rows
    m0p = jnp.broadcast_to(jnp.pad(m0, ((0, 0), (0, 8 - NH_A)))[:, :, None], (batch, 8, LANES))
    n0p = jnp.pad(n0, ((0, 0), (0, 8 - NH_A), (0, 0)))
    full3 = lambda b, i: (b, 0, 0)
    outs = pl.pallas_call(
        functools.partial(_even_kernel, rows=rows, hist_valid=hist_valid),
        out_shape=(
            jax.ShapeDtypeStruct((batch * seq_len, D_MODEL), BF16),
            jax.ShapeDtypeStruct((batch, NH_A, DK_A, DV_A), F32),
            jax.ShapeDtypeStruct((batch, 8, DK_A), F32),
            jax.ShapeDtypeStruct((batch, 8, LANES), F32),
            jax.ShapeDtypeStruct((batch, WINDOW, NKV_B * DH_B), F32),
            jax.ShapeDtypeStruct((batch, WINDOW, NKV_B * DH_B), F32),
        ),
        grid=(batch, nsteps),
        in_specs=[
            pl.BlockSpec((rows, N_EVEN_COLS), lambda b, i: (b * nsteps + i, 0)),
            pl.BlockSpec((None, WINDOW, NKV_B * DH_B), full3),
            pl.BlockSpec((None, WINDOW, NKV_B * DH_B), full3),
            pl.BlockSpec((None, NH_A, DK_A, DV_A), lambda b, i: (b, 0, 0, 0)),
            pl.BlockSpec((None, 8, DK_A), full3),
            pl.BlockSpec((None, 8, LANES), full3),
            pl.BlockSpec((NH_B, rows, kw), lambda b, i: (0, 0, 0)),
            pl.BlockSpec((NH_A, DV_A), lambda b, i: (0, 0)),
            pl.BlockSpec(memory_space=pltpu.SMEM),
        ],
        out_specs=(
            pl.BlockSpec((rows, D_MODEL), lambda b, i: (b * nsteps + i, 0)),
            pl.BlockSpec((None, NH_A, DK_A, DV_A), lambda b, i: (b, 0, 0, 0)),
            pl.BlockSpec((None, 8, DK_A), full3),
            pl.BlockSpec((None, 8, LANES), full3),
            pl.BlockSpec((None, WINDOW, NKV_B * DH_B), full3),
            pl.BlockSpec((None, WINDOW, NKV_B * DH_B), full3),
        ),
        scratch_shapes=[
            pltpu.VMEM((NH_A, DK_A, DV_A), F32),
            pltpu.VMEM((8, DK_A), F32),
            pltpu.VMEM((8, LANES), F32),
            pltpu.VMEM((WINDOW, NKV_B * DH_B), F32),
            pltpu.VMEM((WINDOW, NKV_B * DH_B), F32),
        ],
        compiler_params=_cparams(("arbitrary", "arbitrary")),
        name="even_mixer",
    )(proj, k_hist.reshape(batch, WINDOW, NKV_B * DH_B), v_hist.reshape(batch, WINDOW, NKV_B * DH_B),
      c0, n0p, m0p, bias_tab, norm_a.reshape(NH_A, DV_A), sink)
    mix, c1, n1, m1, k1, v1 = outs
    return (mix, c1, n1[:, :NH_A, :], m1[:, :NH_A, 0],
            k1.reshape(batch, WINDOW, NKV_B, DH_B), v1.reshape(batch, WINDOW, NKV_B, DH_B))


def _odd_kernel(proj_ref, conv0_ref, s0_ref, convw_ref, hp_ref, normc_ref,
                mix_ref, sout_ref, convout_ref, s_s, xa_s, *, rows):
    R = rows
    i = pl.program_id(1)
    HB = 8

    @pl.when(i == 0)
    def _():
        s_s[...] = s0_ref[...]
        xa_s[HB - (CONV_W - 1):HB, :] = conv0_ref[...]

    xa_s[HB:HB + R, :] = proj_ref[:, O_QKV:O_QKV + QKV_C]
    convout_ref[...] = xa_s[HB + R - (CONV_W - 1):HB + R, :]

    CS = CHUNK
    P2 = 2 * CS
    gcols = proj_ref[:, O_G:O_G + LANES]
    beta_all = _sigmoid(gcols)
    g_all = -jnp.exp(hp_ref[0:1, :]) * _softplus(gcols + hp_ref[1:2, :])
    rr = lax.broadcasted_iota(I32, (R, R), 0)
    cc = lax.broadcasted_iota(I32, (R, R), 1)
    tri_b = jnp.where((rr >= cc) & (rr // CS == cc // CS), 1.0, 0.0).astype(BF16)
    G_all = _dot_exact_lhs(tri_b, g_all)
    G_t = _tr(G_all)
    eG_all = jnp.exp(G_all)
    r2 = lax.broadcasted_iota(I32, (P2, P2), 0)
    c2 = lax.broadcasted_iota(I32, (P2, P2), 1)
    same_head = (r2 // CS) == (c2 // CS)
    incl2 = same_head & (r2 >= c2)
    strict2 = same_head & (r2 > c2)
    eye2 = jnp.where(r2 == c2, 1.0, 0.0)
    lane_lo = lax.broadcasted_iota(I32, (DK_C, P2), 1) < CS
    scale_c = DK_C ** -0.5

    def conv_silu(rs0, c0):
        y = xa_s[HB - 3 + rs0:HB - 3 + rs0 + CS, c0:c0 + LANES] * convw_ref[0:1, c0:c0 + LANES]
        for j in range(1, CONV_W):
            y = y + xa_s[HB - 3 + j + rs0:HB - 3 + j + rs0 + CS, c0:c0 + LANES] * convw_ref[j:j + 1, c0:c0 + LANES]
        return y * _sigmoid(y)

    def l2n(x):
        return x * lax.rsqrt(jnp.sum(x * x, axis=-1, keepdims=True) + RMS_EPS)

    def stack(a, b):
        return jnp.concatenate([a, b], axis=0)

    def inv_unit_lower_all(a_list):
        ts = [eye2 - a for a in a_list]
        abs_ = [a.astype(BF16) for a in a_list]
        ps = [_dot(ab, ab) for ab in abs_]
        nlev = (CS - 1).bit_length() - 1
        n = len(ts)
        for lvl in range(nlev):
            th = [t.astype(BF16) for t in ts]
            ph = [p.astype(BF16) for p in ps]
            if lvl < nlev - 1:
                lhs = [stack(th[i], ph[i]) for i in range(n)]
                tp = [_dot(lhs[i], ph[i]) for i in range(n)]
                ts = [ts[i] + tp[i][:P2] for i in range(n)]
                ps = [tp[i][P2:] for i in range(n)]
            else:
                tp = [_dot(th[i], ph[i]) for i in range(n)]
                ts = [ts[i] + tp[i] for i in range(n)]
        return ts

    npair = NH_C // 2
    probs = [(c0, pr) for c0 in range(0, R, CS) for pr in range(npair)]
    nP = len(probs)
    rsl = [slice(c0, c0 + CS) for c0, _ in probs]
    hab = [(2 * pr, 2 * pr + 1) for _, pr in probs]

    def per_head(fn):
        return [[fn(p, h) for h in hab[p]] for p in range(nP)]

    def col2(arr, off):
        return [stack(arr[rsl[p], off + hab[p][0]:off + hab[p][0] + 1], arr[rsl[p], off + hab[p][1]:off + hab[p][1] + 1])
                for p in range(nP)]

    kc = per_head(lambda p, h: conv_silu(probs[p][0], NH_C * DK_C + h * DK_C))
    qc = per_head(lambda p, h: conv_silu(probs[p][0], h * DK_C))
    vc = per_head(lambda p, h: conv_silu(probs[p][0], 2 * NH_C * DK_C + h * DV_C))
    kss = [[jnp.sum(x * x, axis=-1, keepdims=True) for x in kc[p]] for p in range(nP)]
    qss = [[jnp.sum(x * x, axis=-1, keepdims=True) for x in qc[p]] for p in range(nP)]
    k2 = [stack(*[x * lax.rsqrt(ss + RMS_EPS) for x, ss in zip(kc[p], kss[p])]) for p in range(nP)]
    q2b = [stack(*[x * lax.rsqrt(ss + RMS_EPS) * scale_c for x, ss in zip(qc[p], qss[p])]).astype(BF16)
           for p in range(nP)]
    v2 = [stack(*vc[p]) for p in range(nP)]
    k2b = [k2[p].astype(BF16) for p in range(nP)]
    G_col = col2(G_all, NH_C)
    G_row = [jnp.concatenate([G_t[NH_C + hab[p][0]:NH_C + hab[p][0] + 1, rsl[p]],
                              G_t[NH_C + hab[p][1]:NH_C + hab[p][1] + 1, rsl[p]]], axis=1) for p in range(nP)]
    beta = col2(beta_all, 0)
    eg = col2(eG_all, NH_C)
    GL = [[G_all[probs[p][0] + CS - 1:probs[p][0] + CS, NH_C + h:NH_C + h + 1] for h in hab[p]] for p in range(nP)]
    GL2 = [stack(jnp.broadcast_to(GL[p][0], (CS, 1)), jnp.broadcast_to(GL[p][1], (CS, 1))) for p in range(nP)]
    egl = [[jnp.exp(x) for x in GL[p]] for p in range(nP)]
    kk = [_dot(k2b[p], k2b[p], _NT) for p in range(nP)]
    qk = [_dot(q2b[p], k2b[p], _NT) for p in range(nP)]
    dmat = [jnp.exp(jnp.where(incl2, G_col[p] - G_row[p], -jnp.inf)) for p in range(nP)]
    a_mat = [jnp.where(strict2, beta[p] * dmat[p] * kk[p], 0.0) for p in range(nP)]
    attn = [(qk[p] * dmat[p]).astype(BF16) for p in range(nP)]
    rhs = [jnp.concatenate([beta[p] * v2[p], (beta[p] * eg[p]) * k2[p]], axis=1).astype(BF16) for p in range(nP)]
    kdec = [k2[p] * jnp.exp(GL2[p] - G_col[p]) for p in range(nP)]
    kdec_t = [_tr(kdec[p]).astype(BF16) for p in range(nP)]
    zero_t = jnp.zeros((DK_C, P2), BF16)
    kdec_lo = [jnp.where(lane_lo, kdec_t[p], zero_t) for p in range(nP)]
    kdec_hi = [jnp.where(lane_lo, zero_t, kdec_t[p]) for p in range(nP)]
    tinv = [t.astype(BF16) for t in inv_unit_lower_all(a_mat)]
    uw = [_dot(tinv[p], rhs[p]) for p in range(nP)]
    wq_lhs = [[stack(uw[p][j * CS:(j + 1) * CS, DV_C:].astype(BF16), q2b[p][j * CS:(j + 1) * CS]) for j in range(2)]
              for p in range(nP)]

    S_cur = [s_s[h] for h in range(NH_C)]
    for ci, c0 in enumerate(range(0, R, CS)):
        rs = slice(c0, c0 + CS)
        ps_ = list(range(ci * npair, (ci + 1) * npair))
        Sb = [S_cur[h].astype(BF16) for h in range(NH_C)]
        wq = [[_dot(wq_lhs[p][j], Sb[hab[p][j]]) for j in range(2)] for p in ps_]
        db = [(uw[p][:, :DV_C] - stack(wq[i_][0][:CS], wq[i_][1][:CS])).astype(BF16) for i_, p in enumerate(ps_)]
        upd_lo = [_dot(kdec_lo[p], db[i_]) for i_, p in enumerate(ps_)]
        upd_hi = [_dot(kdec_hi[p], db[i_]) for i_, p in enumerate(ps_)]
        for i_, p in enumerate(ps_):
            S_cur[hab[p][0]] = egl[p][0] * S_cur[hab[p][0]] + upd_lo[i_]
            S_cur[hab[p][1]] = egl[p][1] * S_cur[hab[p][1]] + upd_hi[i_]
        ad = [_dot(attn[p], db[i_]) for i_, p in enumerate(ps_)]
        o2 = [eg[p] * stack(wq[i_][0][CS:], wq[i_][1][CS:]) + ad[i_] for i_, p in enumerate(ps_)]
        oh = [o2[i_][j * CS:(j + 1) * CS] for i_ in range(npair) for j in range(2)]
        ms = [jnp.mean(o * o, axis=-1, keepdims=True) for o in oh]
        zs = [proj_ref[rs, O_Z + h * DV_C:O_Z + (h + 1) * DV_C] for h in range(NH_C)]
        zg = [z * _sigmoid(z) for z in zs]
        for h in range(NH_C):
            o = oh[h] * lax.rsqrt(ms[h] + RMS_EPS) * normc_ref[...] * zg[h]
            mix_ref[rs, h * DV_C:(h + 1) * DV_C] = o.astype(mix_ref.dtype)
    for h in range(NH_C):
        s_s[h] = S_cur[h]

    xa_s[HB - (CONV_W - 1):HB, :] = xa_s[HB + R - (CONV_W - 1):HB + R, :]
    sout_ref[...] = s_s[...]


def _odd_mixer(proj, conv_hist, s0, conv_w, a_log, dt_bias, norm_c, batch, seq_len, rows):
    nsteps = seq_len // rows
    hp = jnp.zeros((8, LANES), F32)
    hp = hp.at[0, NH_C:2 * NH_C].set(a_log.astype(F32)).at[1, NH_C:2 * NH_C].set(dt_bias.astype(F32))
    outs = pl.pallas_call(
        functools.partial(_odd_kernel, rows=rows),
        out_shape=(
            jax.ShapeDtypeStruct((batch * seq_len, D_MODEL), BF16),
            jax.ShapeDtypeStruct((batch, NH_C, DK_C, DV_C), F32),
            jax.ShapeDtypeStruct((batch, CONV_W - 1, QKV_C), F32),
        ),
        grid=(batch, nsteps),
        in_specs=[
            pl.BlockSpec((rows, N_ODD_COLS), lambda b, i: (b * nsteps + i, 0)),
            pl.BlockSpec((None, CONV_W - 1, QKV_C), lambda b, i: (b, 0, 0)),
            pl.BlockSpec((None, NH_C, DK_C, DV_C), lambda b, i: (b, 0, 0, 0)),
            pl.BlockSpec((CONV_W, QKV_C), lambda b, i: (0, 0)),
            pl.BlockSpec((8, LANES), lambda b, i: (0, 0)),
            pl.BlockSpec((1, DV_C), lambda b, i: (0, 0)),
        ],
        out_specs=(
            pl.BlockSpec((rows, D_MODEL), lambda b, i: (b * nsteps + i, 0)),
            pl.BlockSpec((None, NH_C, DK_C, DV_C), lambda b, i: (b, 0, 0, 0)),
            pl.BlockSpec((None, CONV_W - 1, QKV_C), lambda b, i: (b, 0, 0)),
        ),
        scratch_shapes=[
            pltpu.VMEM((NH_C, DK_C, DV_C), F32),
            pltpu.VMEM((8 + rows, QKV_C), F32),
        ],
        compiler_params=_cparams(("arbitrary", "arbitrary")),
        name="odd_mixer",
    )(proj, conv_hist, s0, conv_w, hp, norm_c.reshape(1, DV_C))
    return outs


def _post_kernel(mix_ref, x_ref, gate_ref, sc_ref, sh_ref, wout_ref, lng_ref, lnb_ref, wr_ref, br_ref,
                 x1_ref, h2_ref, ri_ref, rg_ref, seg_ref, cnt_ref, carry_s, *, tm):
    i = pl.program_id(0)

    @pl.when(i == 0)
    def _():
        carry_s[...] = jnp.zeros_like(carry_s)

    ng = max(1, tm // 256)
    tg = tm // ng
    ys = [_dot(mix_ref[q * tg:(q + 1) * tg, :], wout_ref[...]) for q in range(min(2, ng))]
    wr_b = wr_ref[...].astype(BF16)
    lts = []
    for q in range(ng):
        rq = slice(q * tg, (q + 1) * tg)
        x1 = _layer_norm_rows(ALPHA_DN * x_ref[rq, :] + (1.0 + gate_ref[...]) * ys[q], lng_ref[...], lnb_ref[...])
        if q + 2 < ng:
            ys.append(_dot(mix_ref[(q + 2) * tg:(q + 3) * tg, :], wout_ref[...]))
        x1_ref[rq, :] = x1
        h2 = x1 * (1.0 + sc_ref[...]) + sh_ref[...]
        h2b = h2.astype(BF16)
        h2_ref[rq, :] = h2b
        lts.append(_dot(wr_b, h2b, _NT))
    lt = (lts[0] if ng == 1 else jnp.concatenate(lts, axis=1)) + br_ref[:, 0:1]
    e_iota = lax.broadcasted_iota(I32, (N_EXPERTS, tm), 0).astype(F32)
    vals, idxs = [], []
    for _ in range(TOP_K):
        mx = jnp.max(lt, axis=0, keepdims=True)
        idx = jnp.min(jnp.where(lt == mx, e_iota, float(N_EXPERTS)), axis=0, keepdims=True)
        vals.append(mx)
        idxs.append(idx)
        lt = jnp.where(e_iota == idx, -jnp.inf, lt)
    ex = [jnp.exp(v - vals[0]) for v in vals]
    tot = ex[0] + ex[1] + ex[2] + ex[3]
    hot = [jnp.where(e_iota == idx, 1.0, 0.0) for idx in idxs]
    m_all = hot[0] + hot[1] + hot[2] + hot[3]
    m_all_b = m_all.astype(BF16)
    r = lax.broadcasted_iota(I32, (tm, tm), 0)
    c = lax.broadcasted_iota(I32, (tm, tm), 1)
    upper = jnp.where(r < c, 1.0, 0.0).astype(BF16)
    prefix = _dot(m_all_b, upper)
    re = lax.broadcasted_iota(I32, (N_EXPERTS, N_EXPERTS), 0)
    ce = lax.broadcasted_iota(I32, (N_EXPERTS, N_EXPERTS), 1)
    cnt_col = jnp.sum(m_all, axis=1, keepdims=True)
    m8_col = jnp.floor((cnt_col + 7.0) * 0.125)
    lower_e = jnp.where(re > ce, 1.0, 0.0).astype(BF16)
    off8_col = 8.0 * _dot(lower_e, jnp.broadcast_to(m8_col, (N_EXPERTS, LANES)).astype(BF16))[:, 0:1]
    base = off8_col + prefix
    dests = [jnp.sum(hk * base, axis=0, keepdims=True) for hk in hot]
    cnt_row = _dot(jnp.ones((8, tm), BF16), m_all_b, _NT)
    m8_row = jnp.floor((cnt_row + 7.0) * 0.125)
    upper_e = jnp.where(re < ce, 1.0, 0.0).astype(BF16)
    off8_row = 8.0 * _dot(m8_row.astype(BF16), upper_e)
    gc = carry_s[:, 0:N_EXPERTS]
    srow = lax.broadcasted_iota(I32, (8, N_EXPERTS), 0)
    seg = jnp.where(srow == 0, 8.0 * m8_row, jnp.where(srow == 1, off8_row, jnp.where(srow == 2, gc, 0.0)))
    seg_ref[...] = jnp.concatenate([seg, jnp.zeros((8, LANES - N_EXPERTS), F32)], axis=1).astype(I32)
    carry_s[:, 0:N_EXPERTS] = gc + 8.0 * m8_row
    cnt_ref[...] = carry_s[...]
    ri_ref[...] = jnp.concatenate(dests + [jnp.zeros((4, tm), F32)], axis=0).astype(I32)
    rg_ref[...] = jnp.concatenate([e / tot for e in ex] + [jnp.zeros((4, tm), F32)], axis=0)


def _post(mix, x2d, gate, scale, shift, w_out_bf16, ln_g, ln_b, wr_t, br, seq_len, tm):
    t = x2d.shape[0]
    per = seq_len // tm
    vec = lambda i: (i // per, 0, 0)
    const2 = lambda i: (0, 0)
    return pl.pallas_call(
        functools.partial(_post_kernel, tm=tm),
        out_shape=(
            jax.ShapeDtypeStruct((t, D_MODEL), F32),
            jax.ShapeDtypeStruct((t, D_MODEL), BF16),
            jax.ShapeDtypeStruct((t // tm, 8, tm), I32),
            jax.ShapeDtypeStruct((t // tm, 8, tm), F32),
            jax.ShapeDtypeStruct((t // tm, 8, LANES), I32),
            jax.ShapeDtypeStruct((8, LANES), F32),
        ),
        grid=(t // tm,),
        in_specs=[
            pl.BlockSpec((tm, D_MODEL), lambda i: (i, 0)),
            pl.BlockSpec((tm, D_MODEL), lambda i: (i, 0)),
            pl.BlockSpec((None, 1, D_MODEL), vec),
            pl.BlockSpec((None, 1, D_MODEL), vec),
            pl.BlockSpec((None, 1, D_MODEL), vec),
            pl.BlockSpec((D_MODEL, D_MODEL), const2),
            pl.BlockSpec((1, D_MODEL), const2),
            pl.BlockSpec((1, D_MODEL), const2),
            pl.BlockSpec((N_EXPERTS, D_MODEL), const2),
            pl.BlockSpec((N_EXPERTS, LANES), const2),
        ],
        out_specs=(
            pl.BlockSpec((tm, D_MODEL), lambda i: (i, 0)),
            pl.BlockSpec((tm, D_MODEL), lambda i: (i, 0)),
            pl.BlockSpec((None, 8, tm), lambda i: (i, 0, 0)),
            pl.BlockSpec((None, 8, tm), lambda i: (i, 0, 0)),
            pl.BlockSpec((None, 8, LANES), lambda i: (i, 0, 0)),
            pl.BlockSpec((8, LANES), const2),
        ),
        scratch_shapes=[pltpu.VMEM((8, LANES), F32)],
        compiler_params=_cparams(("arbitrary",)),
        name="post_router",
    )(mix, x2d, gate, scale, shift, w_out_bf16, ln_g, ln_b, wr_t, br)


SEG_FIELDS = 3


def _run_sizes(tm):
    top = 1 << (tm - 1).bit_length()
    return [s for s in (512, 256, 128, 64, 32, 16, 8) if s <= max(top, 8)]


def _for_each_run_piece(seg_ref, rstart_ref, tile, tm, fn):
    base = tile * (SEG_FIELDS * N_EXPERTS)

    def body(e, c):
        n8 = seg_ref[base + e]
        src = seg_ref[base + N_EXPERTS + e]
        dst = rstart_ref[e] + seg_ref[base + 2 * N_EXPERTS + e]
        def pieces(sizes, done):
            for size in sizes:
                @pl.when((n8 & size) != 0)
                def _(done=done, size=size):
                    fn(pl.multiple_of(src + done, 8), pl.multiple_of(dst + done, 8), size)
                done = done + (n8 & size)

        split = 128
        big = [s for s in _run_sizes(tm) if s >= split]
        if big:
            @pl.when(n8 >= split)
            def _():
                pieces(big, jnp.int32(0))
        pieces([s for s in _run_sizes(tm) if s < split], n8 & ~jnp.int32(split - 1))
        return c

    lax.fori_loop(0, N_EXPERTS, body, 0)


def _wait_tile_rows(seg_ref, tile, cb, wait_rows):
    base = tile * (SEG_FIELDS * N_EXPERTS)
    last = N_EXPERTS - 1
    total = seg_ref[base + last] + seg_ref[base + N_EXPERTS + last]
    size = 1 << (cb.bit_length() - 1)
    while size >= 8:
        @pl.when((total & size) != 0)
        def _(size=size):
            wait_rows(size)
        size //= 2


def _dispatch_kernel(zblk_ref, rstart_ref, seg_ref, ri_ref, h_ref, xs_ref, zero_s, cbuf, sem, zsem,
                     *, tm, bm, cb, ntiles):
    i = pl.program_id(0)
    slot = i % 2

    def run_copies(tile, sl, wait):
        if wait:
            def wait_rows(size):
                pltpu.make_async_copy(cbuf.at[sl, pl.ds(0, size), :], xs_ref.at[pl.ds(0, size), :], sem.at[sl]).wait()
            _wait_tile_rows(seg_ref, tile, cb, wait_rows)
            return

        def piece(src, dst, size):
            pltpu.make_async_copy(cbuf.at[sl, pl.ds(src, size), :], xs_ref.at[pl.ds(dst, size), :], sem.at[sl]).start()
        _for_each_run_piece(seg_ref, rstart_ref, tile, tm, piece)

    def zero_copy(e):
        return pltpu.make_async_copy(zero_s, xs_ref.at[pl.ds(zblk_ref[e] * bm, bm), :], zsem)

    @pl.when(i == 0)
    def _():
        zero_s[...] = jnp.zeros_like(zero_s)

        def zstart(e, c):
            @pl.when(zblk_ref[e] >= 0)
            def _():
                zero_copy(e).start()
            return c

        def zwait(e, c):
            @pl.when(zblk_ref[e] >= 0)
            def _():
                zero_copy(e).wait()
            return c

        lax.fori_loop(0, N_EXPERTS, zstart, 0)
        lax.fori_loop(0, N_EXPERTS, zwait, 0)

    @pl.when(i >= 2)
    def _():
        run_copies(i - 2, slot, True)

    rows = lax.broadcasted_iota(I32, (cb, tm), 0)
    hit = rows == ri_ref[0:1, :]
    for k in range(1, TOP_K):
        hit = hit | (rows == ri_ref[k:k + 1, :])
    perm = jnp.where(hit, 1.0, 0.0).astype(BF16)
    cbuf[slot] = _dot(perm, h_ref[...])
    run_copies(i, slot, False)

    @pl.when(i == ntiles - 1)
    def _():
        if ntiles > 1:
            run_copies(i - 1, 1 - slot, True)
        run_copies(i, slot, True)


def _dispatch(h2, route_i, seg_flat, row_start, zblk, n_rows, tm, bm):
    t = h2.shape[0]
    ntiles = t // tm
    cb = TOP_K * tm + 8 * N_EXPERTS
    return pl.pallas_call(
        functools.partial(_dispatch_kernel, tm=tm, bm=bm, cb=cb, ntiles=ntiles),
        out_shape=jax.ShapeDtypeStruct((n_rows, D_MODEL), F32),
        grid_spec=pltpu.PrefetchScalarGridSpec(
            num_scalar_prefetch=3,
            grid=(ntiles,),
            in_specs=[
                pl.BlockSpec((None, 2 * TOP_K, tm), lambda i, z, r, s: (i, 0, 0)),
                pl.BlockSpec((tm, D_MODEL), lambda i, z, r, s: (i, 0)),
            ],
            out_specs=pl.BlockSpec(memory_space=pl.ANY),
            scratch_shapes=[
                pltpu.VMEM((bm, D_MODEL), F32),
                pltpu.VMEM((2, cb, D_MODEL), F32),
                pltpu.SemaphoreType.DMA((2,)),
                pltpu.SemaphoreType.DMA,
            ],
        ),
        compiler_params=_cparams(("arbitrary",)),
        name="dispatch",
    )(zblk, row_start, seg_flat, route_i, h2)


def _ffn_kernel(be_ref, grp_ref, ia_ref, ib_ref, nv_ref, xa_ref, xb_ref, w1_ref, b1_ref, w2_ref, b2_ref,
                ya_ref, yb_ref, w1_s, w2_s):
    b = pl.program_id(0)

    def expert_block(x_ref, y_ref):
        xb = x_ref[...].astype(BF16)
        glu = _dot(xb, w1_s[:, :D_FF]) + b1_ref[:, :D_FF]
        lin = _dot(xb, w1_s[:, D_FF:]) + b1_ref[:, D_FF:]
        glu = jnp.minimum(glu, SWIGLU_LIMIT)
        lin = jnp.clip(lin, -SWIGLU_LIMIT, SWIGLU_LIMIT)
        act = glu * _sigmoid(SWIGLU_ALPHA * glu) * (lin + 1.0)
        y_ref[...] = _dot(act.astype(BF16), w2_s[...]) + b2_ref[...]

    @pl.when(b < nv_ref[0])
    def _():
        @pl.when((b == 0) | (be_ref[b] != be_ref[jnp.maximum(b - 1, 0)]))
        def _():
            step = 256
            for r0 in range(0, D_MODEL, step):
                w1_s[r0:r0 + step, :] = w1_ref[r0:r0 + step, :].astype(BF16)
            for r0 in range(0, D_FF, step):
                w2_s[r0:r0 + step, :] = w2_ref[r0:r0 + step, :].astype(BF16)

        @pl.when(grp_ref[b] == 0)
        def _():
            expert_block(xa_ref, ya_ref)

        @pl.when(grp_ref[b] == 1)
        def _():
            expert_block(xb_ref, yb_ref)


def _ffn(xs_a, xs_b, steps, layer, w1, b1, w2, b2, bm_a, bm_b):
    step_e, step_g, step_ia, step_ib, n_valid = steps
    nsteps = step_e.shape[0]
    row_a = lambda b, be, g, ia, ib, nv: (ia[b], 0)
    row_b = lambda b, be, g, ia, ib, nv: (ib[b], 0)
    wsel = lambda b, be, g, ia, ib, nv: (layer, be[b], 0, 0)
    return pl.pallas_call(
        _ffn_kernel,
        out_shape=(jax.ShapeDtypeStruct(xs_a.shape, F32), jax.ShapeDtypeStruct(xs_b.shape, F32)),
        grid_spec=pltpu.PrefetchScalarGridSpec(
            num_scalar_prefetch=5,
            grid=(nsteps,),
            in_specs=[
                pl.BlockSpec((bm_a, D_MODEL), row_a),
                pl.BlockSpec((bm_b, D_MODEL), row_b),
                pl.BlockSpec((None, None, D_MODEL, 2 * D_FF), wsel),
                pl.BlockSpec((None, None, 1, 2 * D_FF), wsel),
                pl.BlockSpec((None, None, D_FF, D_MODEL), wsel),
                pl.BlockSpec((None, None, 1, D_MODEL), wsel),
            ],
            out_specs=(pl.BlockSpec((bm_a, D_MODEL), row_a), pl.BlockSpec((bm_b, D_MODEL), row_b)),
            scratch_shapes=[pltpu.VMEM((D_MODEL, 2 * D_FF), BF16), pltpu.VMEM((D_FF, D_MODEL), BF16)],
        ),
        compiler_params=_cparams(("arbitrary",)),
        name="expert_ffn",
    )(step_e, step_g, step_ia, step_ib, n_valid, xs_a, xs_b, w1, b1, w2, b2)


def _combine_kernel(rstart_ref, seg_ref, ri_ref, x_ref, gate_ref, rg_ref, lng_ref, lnb_ref, ys_ref, o_ref,
                    ybuf, yb_s, wt_s, sem, *, tm, cb, ntiles):
    i = pl.program_id(0)
    slot = i % 2

    def run_copies(tile, sl, wait):
        if wait:
            def wait_rows(size):
                pltpu.make_async_copy(ys_ref.at[pl.ds(0, size), :], ybuf.at[sl, pl.ds(0, size), :], sem.at[sl]).wait()
            _wait_tile_rows(seg_ref, tile, cb, wait_rows)
            return

        def piece(loc, glob, size):
            pltpu.make_async_copy(ys_ref.at[pl.ds(glob, size), :], ybuf.at[sl, pl.ds(loc, size), :], sem.at[sl]).start()
        _for_each_run_piece(seg_ref, rstart_ref, tile, tm, piece)

    @pl.when(i == 0)
    def _():
        ybuf[...] = jnp.zeros_like(ybuf)
        run_copies(0, 0, False)

    @pl.when(i + 1 < ntiles)
    def _():
        run_copies(i + 1, 1 - slot, False)

    dest_t = _tr(jnp.concatenate([ri_ref[...].astype(F32), jnp.zeros((LANES - 8, tm), F32)], axis=0))
    g_t = _tr(jnp.concatenate([rg_ref[...], jnp.zeros((LANES - 8, tm), F32)], axis=0))
    lane_step = 256
    nq = max(1, tm // 256)
    tq = tm // nq

    def build(q):
        rq = slice(q * tq, (q + 1) * tq)
        for c0 in range(0, cb, lane_step):
            cols = (lax.broadcasted_iota(I32, (tq, lane_step), 1) + c0).astype(F32)
            w = jnp.where(cols == dest_t[rq, 0:1], g_t[rq, 0:1], 0.0)
            for k in range(1, TOP_K):
                w = w + jnp.where(cols == dest_t[rq, k:k + 1], g_t[rq, k:k + 1], 0.0)
            wt_s[rq, c0:c0 + lane_step] = w.astype(BF16)

    def product(q):
        rq = slice(q * tq, (q + 1) * tq)
        return _dot(wt_s[rq, :], yb_s[...])

    def finish(q, moe):
        rq = slice(q * tq, (q + 1) * tq)
        o_ref[rq, :] = _layer_norm_rows(ALPHA_DN * x_ref[rq, :] + (1.0 + gate_ref[...]) * moe,
                                        lng_ref[...], lnb_ref[...])

    build(0)
    run_copies(i, slot, True)
    row_step = 256
    for r0 in range(0, cb, row_step):
        yb_s[r0:r0 + row_step, :] = ybuf[slot, r0:r0 + row_step, :].astype(BF16)
    moes = {}
    for q in range(nq):
        if q + 1 < nq:
            build(q + 1)
        moes[q] = product(q)
        if q >= 1:
            finish(q - 1, moes.pop(q - 1))
    finish(nq - 1, moes.pop(nq - 1))


def _combine(row_start, seg_flat, route_i, x1, gate, route_g, ln_g, ln_b, ys, seq_len, tm):
    t = x1.shape[0]
    per = seq_len // tm
    ntiles = t // tm
    cb = TOP_K * tm + 8 * N_EXPERTS
    return pl.pallas_call(
        functools.partial(_combine_kernel, tm=tm, cb=cb, ntiles=ntiles),
        out_shape=jax.ShapeDtypeStruct((t, D_MODEL), F32),
        grid_spec=pltpu.PrefetchScalarGridSpec(
            num_scalar_prefetch=2,
            grid=(ntiles,),
            in_specs=[
                pl.BlockSpec((None, 2 * TOP_K, tm), lambda i, r, s: (i, 0, 0)),
                pl.BlockSpec((tm, D_MODEL), lambda i, r, s: (i, 0)),
                pl.BlockSpec((None, 1, D_MODEL), lambda i, r, s: (i // per, 0, 0)),
                pl.BlockSpec((None, 8, tm), lambda i, r, s: (i, 0, 0)),
                pl.BlockSpec((1, D_MODEL), lambda i, r, s: (0, 0)),
                pl.BlockSpec((1, D_MODEL), lambda i, r, s: (0, 0)),
                pl.BlockSpec(memory_space=pl.ANY),
            ],
            out_specs=pl.BlockSpec((tm, D_MODEL), lambda i, r, s: (i, 0)),
            scratch_shapes=[
                pltpu.VMEM((2, cb, D_MODEL), F32),
                pltpu.VMEM((cb, D_MODEL), BF16),
                pltpu.VMEM((tm, cb), BF16),
                pltpu.SemaphoreType.DMA((2,)),
            ],
        ),
        compiler_params=_cparams(("arbitrary",)),
        name="combine",
    )(row_start, seg_flat, route_i, x1, gate, route_g, ln_g, ln_b, ys)


def _max_blocks(t, tm, bm):
    return -(-(t * TOP_K + 7 * N_EXPERTS * (t // tm)) // bm) + N_EXPERTS


def _group_tables(counts, bm):
    cnt = counts[0, :N_EXPERTS].astype(I32)
    nblk = (cnt + bm - 1) // bm
    blk_end = jnp.cumsum(nblk)
    row_start = ((blk_end - nblk) * bm).astype(I32)
    zblk = jnp.where((cnt % bm) != 0, blk_end - 1, -1).astype(I32)
    return nblk, row_start, zblk


def _ffn_steps(nblk_a, nblk_b, nsteps):
    per_e = nblk_a + nblk_b
    ends = jnp.cumsum(per_e)
    n_valid = ends[-1]
    s = jnp.minimum(jnp.arange(nsteps, dtype=I32), n_valid - 1)
    step_e = jnp.minimum(jnp.sum((ends[None, :] <= s[:, None]).astype(I32), axis=1), N_EXPERTS - 1)
    onehot = (step_e[:, None] == jnp.arange(N_EXPERTS, dtype=I32)[None, :]).astype(I32)
    off = s - jnp.sum(onehot * (ends - per_e)[None, :], axis=1)
    in_b = off >= jnp.sum(onehot * nblk_a[None, :], axis=1)
    live = jnp.arange(nsteps, dtype=I32) < n_valid
    step_g = jnp.where(live, in_b.astype(I32), 2)
    step_ia = jnp.maximum(jnp.cumsum((step_g == 0).astype(I32)) - 1, 0)
    step_ib = jnp.maximum(jnp.cumsum((step_g == 1).astype(I32)) - 1, 0)
    return (step_e.astype(I32), step_g.astype(I32), step_ia.astype(I32), step_ib.astype(I32),
            n_valid.reshape(1).astype(I32))


def _group_cfg(batch, seq_len):
    if seq_len >= 512:
        return dict(tm=512, rows=256, bm=512)
    return dict(tm=seq_len, rows=seq_len, bm=128)


def kernel(x_prompt, x_sample, c_prompt, c_sample, state_a_C, state_a_n, state_a_m, cache_b_k, cache_b_v,
           state_c_S, state_c_conv, w_ada, b_ada, ln_g, ln_b, w_in_even, b_in_even, norm_a, sink_b, rel_bias,
           w_out_even, w_in_odd, conv_c, a_log_c, dt_bias_c, norm_c, w_out_odd, w_router, b_router,
           w_e1, b_e1, w_e2, b_e2):
    bp, lp, _ = x_prompt.shape
    bs, ls, _ = x_sample.shape
    groups = [dict(b=bp, l=lp, x=x_prompt.reshape(bp * lp, D_MODEL), **_group_cfg(bp, lp)),
              dict(b=bs, l=ls, x=x_sample.reshape(bs * ls, D_MODEL), **_group_cfg(bs, ls))]
    mod = _ada(jnp.concatenate([c_prompt, c_sample], axis=0), w_ada, b_ada)
    offs = [0, bp]
    states = [dict(), dict()]
    b1_all = b_e1.reshape(DEPTH, N_EXPERTS, 1, 2 * D_FF)
    b2_all = b_e2.reshape(DEPTH, N_EXPERTS, 1, D_MODEL)
    for l in range(DEPTH):
        e = l // 2
        if l % 2 == 0:
            w = w_in_even[e]
            sz = (512, 512, 512, 512, 4, 4, 512, 128, 128)
            o = [sum(sz[:j]) for j in range(len(sz) + 1)]
            pad = jnp.zeros((D_MODEL, LANES - 2 * NH_A), w.dtype)
            w_in = jnp.concatenate([w[:, o[0]:o[4]], w[:, o[6]:o[9]], w[:, o[4]:o[6]], pad], axis=1).astype(BF16)
            bb = b_in_even[e]
            b_in = jnp.concatenate([bb[o[0]:o[4]], bb[o[6]:o[9]], bb[o[4]:o[6]],
                                    jnp.zeros((LANES - 2 * NH_A,), bb.dtype)]).reshape(1, N_EVEN_COLS)
            w_out = w_out_even[e].astype(BF16)
        else:
            w = w_in_odd[e]
            pad = jnp.zeros((D_MODEL, LANES - 2 * NH_C), w.dtype)
            w_in = jnp.concatenate([w, pad], axis=1).astype(BF16)
            b_in = jnp.zeros((1, N_ODD_COLS), F32)
            w_out = w_out_odd[e].astype(BF16)
        wr_t = w_router[l].T
        br = jnp.broadcast_to(b_router[l][:, None], (N_EXPERTS, LANES))
        moe_in = []
        for gi, gr in enumerate(groups):
            nb, sl = gr['b'], gr['l']
            m = mod[l, offs[gi]:offs[gi] + nb].reshape(nb, 6, 1, D_MODEL)
            m6 = [m[:, j] for j in range(6)]
            proj = _inproj(gr['x'], m6[1], m6[0], w_in, b_in, sl, gr['tm'])
            st = states[gi]
            if l % 2 == 0:
                if gi == 0:
                    kh = jnp.zeros((nb, WINDOW, NKV_B, DH_B), F32)
                    vh = kh
                    c0 = jnp.zeros((nb, NH_A, DK_A, DV_A), F32)
                    n0 = jnp.zeros((nb, NH_A, DK_A), F32)
                    m0 = jnp.zeros((nb, NH_A), F32)
                else:
                    kh, vh, c0, n0, m0 = cache_b_k[e], cache_b_v[e], state_a_C[e], state_a_n[e], state_a_m[e]
                bias_tab = _swa_bias_table(rel_bias, gr['rows'])
                mix, c1, n1, m1, k1, v1 = _even_mixer(proj, kh, vh, c0, n0, m0, bias_tab, norm_a[e], sink_b[e],
                                                      nb, sl, gr['rows'], gi == 1)
                for name, val in (('a_C', c1), ('a_n', n1), ('a_m', m1), ('b_k', k1), ('b_v', v1)):
                    st.setdefault(name, []).append(val)
            else:
                if gi == 0:
                    cv0 = jnp.zeros((nb, CONV_W - 1, QKV_C), F32)
                    s0 = jnp.zeros((nb, NH_C, DK_C, DV_C), F32)
                else:
                    cv0, s0 = state_c_conv[e], state_c_S[e]
                mix, s1, cv1 = _odd_mixer(proj, cv0, s0, conv_c[e], a_log_c[e], dt_bias_c[e], norm_c[e], nb, sl,
                                          gr['rows'])
                st.setdefault('c_S', []).append(s1)
                st.setdefault('c_conv', []).append(cv1)
            x1, h2, route_i, route_g, seg, counts = _post(
                mix, gr['x'], m6[2], m6[4], m6[3], w_out, ln_g[l, 0].reshape(1, D_MODEL),
                ln_b[l, 0].reshape(1, D_MODEL), wr_t, br, sl, gr['tm'])
            t, tm, bm = nb * sl, gr['tm'], gr['bm']
            ntiles = t // tm
            nblk, row_start, zblk = _group_tables(counts, bm)
            seg_flat = seg[:, :SEG_FIELDS, :N_EXPERTS].reshape(ntiles * SEG_FIELDS * N_EXPERTS)
            xs = _dispatch(h2, route_i, seg_flat, row_start, zblk, _max_blocks(t, tm, bm) * bm, tm, bm)
            moe_in.append(dict(xs=xs, nblk=nblk, row_start=row_start, seg_flat=seg_flat, route_i=route_i,
                               route_g=route_g, x1=x1, gate=m6[5]))
        ga, gb = moe_in
        steps = _ffn_steps(ga['nblk'], gb['nblk'], sum(_max_blocks(g['b'] * g['l'], g['tm'], g['bm']) for g in groups))
        ys = _ffn(ga['xs'], gb['xs'], steps, l, w_e1, b1_all, w_e2, b2_all, groups[0]['bm'], groups[1]['bm'])
        for gi, gr in enumerate(groups):
            mi = moe_in[gi]
            gr['x'] = _combine(mi['row_start'], mi['seg_flat'], mi['route_i'], mi['x1'], mi['gate'], mi['route_g'],
                               ln_g[l, 1].reshape(1, D_MODEL), ln_b[l, 1].reshape(1, D_MODEL), ys[gi],
                               gr['l'], gr['tm'])
    outs = [groups[0]['x'].reshape(bp, lp, D_MODEL), groups[1]['x'].reshape(bs, ls, D_MODEL)]
    for gi in range(2):
        for name in ('a_C', 'a_n', 'a_m', 'b_k', 'b_v', 'c_S', 'c_conv'):
            outs.append(jnp.stack(states[gi][name]))
    return tuple(outs)
```

```python
import functools
import math

import jax
import jax.numpy as jnp
from jax import lax
from jax.experimental import pallas as pl
from jax.experimental.pallas import tpu as pltpu

F32 = jnp.float32
BF16 = jnp.bfloat16
I32 = jnp.int32

D_MODEL = 1024
CHUNK = 64
NH_A, DK_A, DV_A = 4, 128, 128
W_A = NH_A * DV_A
NH_B, NKV_B, DH_B, WINDOW = 8, 2, 64, 128
G_B = NH_B // NKV_B
W_B = NH_B * DH_B
NUM_BUCKETS, MAX_DISTANCE = 32, 256
NH_C, DK_C, DV_C, CONV_W = 8, 128, 128, 4
QKV_C = NH_C * (2 * DK_C + DV_C)
N_EXPERTS, TOP_K, D_FF = 32, 4, 1024
SWIGLU_LIMIT, SWIGLU_ALPHA = 7.0, 1.702
DEPTH = 2
ALPHA_DN = (2 * DEPTH) ** 0.25
LN_EPS = 1e-5
RMS_EPS = 1e-6

LANES = 128
E_QA, E_KA, E_VA, E_OA, E_QB, E_KB, E_VB, E_G = 0, 512, 1024, 1536, 2048, 2560, 2688, 2816
N_EVEN_COLS = 2944
O_QKV, O_Z, O_G = 0, 3072, 4096
N_ODD_COLS = 4224

VMEM_LIMIT = 56 * 1024 * 1024


def _cparams(sem):
    return pltpu.CompilerParams(dimension_semantics=sem, vmem_limit_bytes=VMEM_LIMIT)


def _softplus(x):
    return jnp.maximum(x, 0.0) + jnp.log(1.0 + jnp.exp(-jnp.abs(x)))


def _sigmoid(x):
    return 1.0 / (1.0 + jnp.exp(-x))


def _split_bf16(a):
    hi = a.astype(BF16)
    lo = (a - hi.astype(F32)).astype(BF16)
    return hi, lo


_NN = (((1,), (0,)), ((), ()))
_NT = (((1,), (1,)), ((), ()))
_TN = (((0,), (0,)), ((), ()))


def _dot(a, b, dims=_NN):
    return lax.dot_general(a, b, dims, preferred_element_type=F32)


def _dot_x3(a, b, dims=_NN):
    ah, al = _split_bf16(a)
    bh, bl = _split_bf16(b)
    return _dot(ah, bh, dims) + _dot(ah, bl, dims) + _dot(al, bh, dims)


def _dot_exact_lhs(a_bf16, b, dims=_NN):
    b0 = b.astype(BF16)
    r1 = b - b0.astype(F32)
    b1 = r1.astype(BF16)
    b2 = (r1 - b1.astype(F32)).astype(BF16)
    return _dot(a_bf16, b0, dims) + _dot(a_bf16, b1, dims) + _dot(a_bf16, b2, dims)


def _tr(x):
    r = x.shape[0]
    rp = -(-r // LANES) * LANES
    if rp != r:
        x = jnp.concatenate([x, jnp.zeros((rp - r, x.shape[1]), x.dtype)], axis=0)
    return x.T[:, :r]


def _tri(n, strict=False):
    r = lax.broadcasted_iota(I32, (n, n), 0)
    c = lax.broadcasted_iota(I32, (n, n), 1)
    return (r > c) if strict else (r >= c)


def _layer_norm_rows(v, g, b):
    mu = jnp.mean(v, axis=-1, keepdims=True)
    d = v - mu
    var = jnp.mean(d * d, axis=-1, keepdims=True)
    return d * lax.rsqrt(var + LN_EPS) * g + b


def _ada_kernel(c_ref, w_ref, b_ref, o_ref):
    c = c_ref[...]
    a = (c * _sigmoid(c)).astype(BF16)
    o_ref[...] = _dot(a, w_ref[...].astype(BF16)) + b_ref[...]


def _ada(c_all, w_ada, b_ada):
    nb = c_all.shape[0]
    tn = 1536
    return pl.pallas_call(
        _ada_kernel,
        out_shape=jax.ShapeDtypeStruct((DEPTH, nb, 6 * D_MODEL), F32),
        grid=(DEPTH, 6 * D_MODEL // tn),
        in_specs=[
            pl.BlockSpec((nb, D_MODEL), lambda l, j: (0, 0)),
            pl.BlockSpec((None, D_MODEL, tn), lambda l, j: (l, 0, j)),
            pl.BlockSpec((None, 1, tn), lambda l, j: (l, 0, j)),
        ],
        out_specs=pl.BlockSpec((None, nb, tn), lambda l, j: (l, 0, j)),
        compiler_params=_cparams(("arbitrary", "arbitrary")),
        name="ada",
    )(c_all, w_ada, b_ada.reshape(DEPTH, 1, 6 * D_MODEL))


def _inproj_kernel(x_ref, sc_ref, sh_ref, w_ref, b_ref, o_ref, *, n_cols, col_step):
    h = (x_ref[...] * (1.0 + sc_ref[...]) + sh_ref[...]).astype(BF16)
    for c0 in range(0, n_cols, col_step):
        c1 = min(c0 + col_step, n_cols)
        o_ref[:, c0:c1] = _dot(h, w_ref[:, c0:c1]) + b_ref[:, c0:c1]


def _inproj(x2d, scale, shift, w_bf16, bias, seq_len, tm):
    t = x2d.shape[0]
    n = w_bf16.shape[1]
    per = seq_len // tm
    return pl.pallas_call(
        functools.partial(_inproj_kernel, n_cols=n, col_step=1024),
        out_shape=jax.ShapeDtypeStruct((t, n), F32),
        grid=(t // tm,),
        in_specs=[
            pl.BlockSpec((tm, D_MODEL), lambda i: (i, 0)),
            pl.BlockSpec((None, 1, D_MODEL), lambda i: (i // per, 0, 0)),
            pl.BlockSpec((None, 1, D_MODEL), lambda i: (i // per, 0, 0)),
            pl.BlockSpec((D_MODEL, n), lambda i: (0, 0)),
            pl.BlockSpec((1, n), lambda i: (0, 0)),
        ],
        out_specs=pl.BlockSpec((tm, n), lambda i: (i, 0)),
        compiler_params=_cparams(("arbitrary",)),
        name="inproj",
    )(x2d, scale, shift, w_bf16, bias)


CONV_HB = 8


def _inproj_conv_kernel(x_ref, sc_ref, sh_ref, w_ref, conv0_ref, convw_ref, o_ref, convout_ref, xa_s,
                        *, tm, per):
    i = pl.program_id(0)
    HB = CONV_HB

    @pl.when(i % per == 0)
    def _():
        xa_s[0:HB - (CONV_W - 1), :] = jnp.zeros((HB - (CONV_W - 1), QKV_C), F32)
        xa_s[HB - (CONV_W - 1):HB, :] = conv0_ref[...]

    h = (x_ref[...] * (1.0 + sc_ref[...]) + sh_ref[...]).astype(BF16)
    step = 512

    def project(c0):
        return _dot(h, w_ref[:, c0:c0 + step])

    def conv_group(g0, pre):
        xa = jnp.concatenate([xa_s[:, g0:g0 + step], pre], axis=0)
        for c in range(0, step, LANES):
            c0 = g0 + c
            y = xa[HB - 3:HB - 3 + tm, c:c + LANES] * convw_ref[0:1, c0:c0 + LANES]
            for j in range(1, CONV_W):
                y = y + xa[HB - 3 + j:HB - 3 + j + tm, c:c + LANES] * convw_ref[j:j + 1, c0:c0 + LANES]
            y = y * _sigmoid(y)
            if c0 < 2 * NH_C * DK_C:
                y = y * lax.rsqrt(jnp.sum(y * y, axis=-1, keepdims=True) + RMS_EPS)
            o_ref[:, c0:c0 + LANES] = y
        convout_ref[:, g0:g0 + step] = pre[tm - (CONV_W - 1):tm, :]
        return pre[tm - HB:tm, :]

    groups = list(range(0, QKV_C, step))
    pres = {0: project(groups[0])}
    tails = []
    for gi, g0 in enumerate(groups):
        if gi + 1 < len(groups):
            pres[gi + 1] = project(groups[gi + 1])
        else:
            o_ref[:, O_Z:N_ODD_COLS] = _dot(h, w_ref[:, O_Z:N_ODD_COLS])
        tails.append(conv_group(g0, pres.pop(gi)))
    for g0, tail in zip(groups, tails):
        xa_s[:, g0:g0 + step] = tail


def _inproj_conv(x2d, scale, shift, w_bf16, conv_hist, conv_w, seq_len, tm):
    t = x2d.shape[0]
    per = seq_len // tm
    batch = t // seq_len
    return pl.pallas_call(
        functools.partial(_inproj_conv_kernel, tm=tm, per=per),
        out_shape=(jax.ShapeDtypeStruct((t, N_ODD_COLS), F32),
                   jax.ShapeDtypeStruct((batch, CONV_W - 1, QKV_C), F32)),
        grid=(t // tm,),
        in_specs=[
            pl.BlockSpec((tm, D_MODEL), lambda i: (i, 0)),
            pl.BlockSpec((None, 1, D_MODEL), lambda i: (i // per, 0, 0)),
            pl.BlockSpec((None, 1, D_MODEL), lambda i: (i // per, 0, 0)),
            pl.BlockSpec((D_MODEL, N_ODD_COLS), lambda i: (0, 0)),
            pl.BlockSpec((None, CONV_W - 1, QKV_C), lambda i: (i // per, 0, 0)),
            pl.BlockSpec((CONV_W, QKV_C), lambda i: (0, 0)),
        ],
        out_specs=(pl.BlockSpec((tm, N_ODD_COLS), lambda i: (i, 0)),
                   pl.BlockSpec((None, CONV_W - 1, QKV_C), lambda i: (i // per, 0, 0))),
        scratch_shapes=[pltpu.VMEM((CONV_HB, QKV_C), F32)],
        compiler_params=_cparams(("arbitrary",)),
        name="inproj_conv",
    )(x2d, scale, shift, w_bf16, conv_hist, conv_w)


def _even_kernel(proj_ref, kh0_ref, vh0_ref, c0_ref, n0_ref, m0_ref, bias_ref, norma_ref, sink_ref,
                 mix_ref, cout_ref, nout_ref, mout_ref, kout_ref, vout_ref,
                 c_s, n_s, m_s, kh_s, vh_s, *, rows, hist_valid):
    R = rows
    KW = WINDOW + R
    i = pl.program_id(1)

    @pl.when(i == 0)
    def _():
        c_s[...] = c0_ref[...]
        n_s[...] = n0_ref[...]
        m_s[...] = m0_ref[...]
        kh_s[...] = kh0_ref[...]
        vh_s[...] = vh0_ref[...]

    CS = CHUNK
    g = proj_ref[:, E_G:E_G + LANES]
    lf = -_softplus(-g)
    rr = lax.broadcasted_iota(I32, (R, R), 0)
    cc = lax.broadcasted_iota(I32, (R, R), 1)
    tri_b = jnp.where((rr >= cc) & (rr // CS == cc // CS), 1.0, 0.0).astype(BF16)
    b_all = _dot_exact_lhs(tri_b, lf)
    g_t = _tr(g)
    b_t = _tr(b_all)
    causal = _tri(CS)
    scale_a = DK_A ** -0.5
    chunks = list(range(0, R, CS))
    P = [(c0, h) for c0 in chunks for h in range(NH_A)]
    nP = len(P)
    rsl = [slice(c0, c0 + CS) for c0, _ in P]
    b_col = [b_all[rsl[p], NH_A + h:NH_A + h + 1] for p, (_, h) in enumerate(P)]
    b_row = [b_t[NH_A + h:NH_A + h + 1, rsl[p]] for p, (_, h) in enumerate(P)]
    i_row = [g_t[h:h + 1, rsl[p]] for p, (_, h) in enumerate(P)]
    i_col = [g[rsl[p], h:h + 1] for p, (_, h) in enumerate(P)]
    logw = [jnp.where(causal, b_col[p] - b_row[p] + i_row[p], -jnp.inf) for p in range(nP)]
    lmax = [jnp.max(logw[p], axis=-1, keepdims=True) for p in range(nP)]
    qb = [(proj_ref[rsl[p], E_QA + h * DK_A:E_QA + (h + 1) * DK_A] * scale_a).astype(BF16)
          for p, (_, h) in enumerate(P)]
    kf = [proj_ref[rsl[p], E_KA + h * DK_A:E_KA + (h + 1) * DK_A] for p, (_, h) in enumerate(P)]
    vb = [proj_ref[rsl[p], E_VA + h * DV_A:E_VA + (h + 1) * DV_A].astype(BF16) for p, (_, h) in enumerate(P)]
    qk = [_dot(qb[p], kf[p].astype(BF16), _NT) for p in range(nP)]
    b_last = [b_col[p][CS - 1:CS, :] for p in range(nP)]

    k_win = jnp.concatenate([kh_s[...], proj_ref[:, E_KB:E_KB + NKV_B * DH_B]], axis=0)
    v_win = jnp.concatenate([vh_s[...], proj_ref[:, E_VB:E_VB + NKV_B * DH_B]], axis=0)
    k_win_b = k_win.astype(BF16)
    v_win_b = v_win.astype(BF16)
    if not hist_valid:
        key_pos = lax.broadcasted_iota(I32, (R, KW), 1) + (i * R - WINDOW)
        key_ok = key_pos >= 0
    scale_b = DH_B ** -0.5
    lo = lax.broadcasted_iota(I32, (1, LANES), 1) < DH_B
    k_swp = pltpu.roll(k_win, DH_B, 1).astype(BF16)
    v_swp = pltpu.roll(v_win, DH_B, 1).astype(BF16)
    zk = jnp.zeros_like(k_win_b)
    k_both = [jnp.where(lo, k_win_b, k_swp), jnp.where(lo, k_swp, k_win_b)]
    v_lo = [jnp.where(lo, v_win_b, zk), jnp.where(lo, v_swp, zk)]
    v_hi = [jnp.where(lo, zk, v_swp), jnp.where(lo, zk, v_win_b)]
    npairs = NH_B // 2
    qp = [proj_ref[:, E_QB + j * LANES:E_QB + (j + 1) * LANES].astype(BF16) for j in range(npairs)]
    zq = jnp.zeros_like(qp[0])
    qh = [jnp.where(lo, qp[hd // 2], zq) if hd % 2 == 0 else jnp.where(lo, zq, qp[hd // 2]) for hd in range(NH_B)]
    scores = [_dot(qh[hd], k_both[hd // G_B], _NT) * scale_b + bias_ref[hd] for hd in range(NH_B)]
    if not hist_valid:
        scores = [jnp.where(key_ok, s, -jnp.inf) for s in scores]
    sks = [sink_ref[hd] for hd in range(NH_B)]
    mxs = [jnp.maximum(jnp.max(scores[hd], axis=-1, keepdims=True), sks[hd]) for hd in range(NH_B)]
    ps = [jnp.exp(scores[hd] - mxs[hd]) for hd in range(NH_B)]
    dens = [jnp.sum(ps[hd], axis=-1, keepdims=True) + jnp.exp(sks[hd] - mxs[hd]) for hd in range(NH_B)]
    pbs = [(ps[hd] / dens[hd]).astype(BF16) for hd in range(NH_B)]

    m_prev, m_inter, m_t, m_new = [None] * nP, [None] * nP, [None] * nP, [None] * nP
    for h in range(NH_A):
        run = m_s[h:h + 1, 0:1]
        for ci in range(len(chunks)):
            p = ci * NH_A + h
            m_prev[p] = run
            m_inter[p] = b_col[p] + run
            m_t[p] = jnp.maximum(m_inter[p], lmax[p])
            run = m_t[p][CS - 1:CS, :]
            m_new[p] = run
        m_s[h:h + 1, :] = jnp.broadcast_to(run, (1, LANES))
    w = [jnp.exp(logw[p] - m_t[p]) for p in range(nP)]
    s = [qk[p] * w[p] for p in range(nP)]
    sv = [_dot(s[p].astype(BF16), vb[p]) for p in range(nP)]
    ssum = [jnp.sum(s[p], axis=-1, keepdims=True) for p in range(nP)]
    kfac = [jnp.exp(b_last[p] - b_col[p] + i_col[p] - m_new[p]) for p in range(nP)]
    kw = [kf[p] * kfac[p] for p in range(nP)]
    kwt = [_tr(kw[p]).astype(BF16) for p in range(nP)]
    kv = [_dot(kwt[p], vb[p]) for p in range(nP)]
    ksum = [jnp.sum(kw[p], axis=0, keepdims=True) for p in range(nP)]
    carry = [jnp.exp(b_last[p] + m_prev[p] - m_new[p]) for p in range(nP)]
    dec = [jnp.exp(m_inter[p] - m_t[p]) for p in range(nP)]
    floor_ = [jnp.exp(-m_t[p]) for p in range(nP)]
    c_prev, n_prev = [None] * nP, [None] * nP
    for h in range(NH_A):
        n_run = n_s[h:h + 1, :]
        c_run = c_s[h]
        for ci in range(len(chunks)):
            p = ci * NH_A + h
            c_prev[p], n_prev[p] = c_run, n_run
            c_run = carry[p] * c_run + kv[p]
            n_run = carry[p] * n_run + ksum[p]
        c_s[h] = c_run
        n_s[h:h + 1, :] = n_run
    qc = [_dot(qb[p], c_prev[p].astype(BF16)) for p in range(nP)]
    qn = [jnp.sum(qb[p].astype(F32) * n_prev[p].astype(BF16).astype(F32), axis=-1, keepdims=True) for p in range(nP)]
    num = [dec[p] * qc[p] + sv[p] for p in range(nP)]
    den = [dec[p] * qn[p] + ssum[p] for p in range(nP)]
    hh = [num[p] / jnp.maximum(jnp.abs(den[p]), floor_[p]) for p in range(nP)]
    mu = [jnp.mean(hh[p], axis=-1, keepdims=True) for p in range(nP)]
    dd = [hh[p] - mu[p] for p in range(nP)]
    var = [jnp.mean(dd[p] * dd[p], axis=-1, keepdims=True) for p in range(nP)]
    rs_ = [lax.rsqrt(var[p] + LN_EPS) for p in range(nP)]
    for p, (_, h) in enumerate(P):
        og = proj_ref[rsl[p], E_OA + h * DV_A:E_OA + (h + 1) * DV_A]
        ha = dd[p] * rs_[p] * norma_ref[h:h + 1, :] * _sigmoid(og)
        mix_ref[rsl[p], h * DV_A:(h + 1) * DV_A] = ha.astype(mix_ref.dtype)

    cout_ref[...] = c_s[...]
    nout_ref[...] = n_s[...]
    mout_ref[...] = m_s[...]

    outs = [_dot(pbs[hd], (v_lo if hd % 2 == 0 else v_hi)[hd // G_B]) for hd in range(NH_B)]
    for j in range(npairs):
        mix_ref[:, W_A + j * LANES:W_A + (j + 1) * LANES] = (outs[2 * j] + outs[2 * j + 1]).astype(mix_ref.dtype)

    kh_s[...] = k_win[R:, :]
    vh_s[...] = v_win[R:, :]
    kout_ref[...] = kh_s[...]
    vout_ref[...] = vh_s[...]


def _rel_bucket(rel):
    nb = NUM_BUCKETS // 2
    max_exact = nb // 2
    n = jnp.abs(rel)
    nf = jnp.maximum(n, 1).astype(F32)
    large = max_exact + (jnp.log(nf / max_exact) / math.log(MAX_DISTANCE / max_exact)
                         * (nb - max_exact)).astype(I32)
    large = jnp.minimum(large, nb - 1)
    return jnp.where(rel > 0, nb, 0) + jnp.where(n < max_exact, n, large)


def _swa_bias_table(rel_bias, rows):
    kw = WINDOW + rows
    qi = jnp.arange(rows)[:, None]
    kj = jnp.arange(kw)[None, :]
    bucket = _rel_bucket(kj - WINDOW - qi)
    rb = rel_bias.astype(F32)
    bias = jnp.zeros((NH_B, rows, kw), F32)
    for b in range(NUM_BUCKETS):
        bias = jnp.where((bucket == b)[None], rb[b][:, None, None], bias)
    lo = (qi // CHUNK) * CHUNK
    ok = (kj >= lo) & (kj < lo + WINDOW + CHUNK)
    return jnp.where(ok[None], bias, -jnp.inf)


def _even_mixer(proj, k_hist, v_hist, c0, n0, m0, bias_tab, norm_a, sink, batch, seq_len, rows, hist_valid):
    nsteps = seq_len // rows
    kw = WINDOW + rows
    m0p = jnp.broadcast_to(jnp.pad(m0, ((0, 0), (0, 8 - NH_A)))[:, :, None], (batch, 8, LANES))
    n0p = jnp.pad(n0, ((0, 0), (0, 8 - NH_A), (0, 0)))
    full3 = lambda b, i: (b, 0, 0)
    outs = pl.pallas_call(
        functools.partial(_even_kernel, rows=rows, hist_valid=hist_valid),
        out_shape=(
            jax.ShapeDtypeStruct((batch * seq_len, D_MODEL), BF16),
            jax.ShapeDtypeStruct((batch, NH_A, DK_A, DV_A), F32),
            jax.ShapeDtypeStruct((batch, 8, DK_A), F32),
            jax.ShapeDtypeStruct((batch, 8, LANES), F32),
            jax.ShapeDtypeStruct((batch, WINDOW, NKV_B * DH_B), F32),
            jax.ShapeDtypeStruct((batch, WINDOW, NKV_B * DH_B), F32),
        ),
        grid=(batch, nsteps),
        in_specs=[
            pl.BlockSpec((rows, N_EVEN_COLS), lambda b, i: (b * nsteps + i, 0)),
            pl.BlockSpec((None, WINDOW, NKV_B * DH_B), full3),
            pl.BlockSpec((None, WINDOW, NKV_B * DH_B), full3),
            pl.BlockSpec((None, NH_A, DK_A, DV_A), lambda b, i: (b, 0, 0, 0)),
            pl.BlockSpec((None, 8, DK_A), full3),
            pl.BlockSpec((None, 8, LANES), full3),
            pl.BlockSpec((NH_B, rows, kw), lambda b, i: (0, 0, 0)),
            pl.BlockSpec((NH_A, DV_A), lambda b, i: (0, 0)),
            pl.BlockSpec(memory_space=pltpu.SMEM),
        ],
        out_specs=(
            pl.BlockSpec((rows, D_MODEL), lambda b, i: (b * nsteps + i, 0)),
            pl.BlockSpec((None, NH_A, DK_A, DV_A), lambda b, i: (b, 0, 0, 0)),
            pl.BlockSpec((None, 8, DK_A), full3),
            pl.BlockSpec((None, 8, LANES), full3),
            pl.BlockSpec((None, WINDOW, NKV_B * DH_B), full3),
            pl.BlockSpec((None, WINDOW, NKV_B * DH_B), full3),
        ),
        scratch_shapes=[
            pltpu.VMEM((NH_A, DK_A, DV_A), F32),
            pltpu.VMEM((8, DK_A), F32),
            pltpu.VMEM((8, LANES), F32),
            pltpu.VMEM((WINDOW, NKV_B * DH_B), F32),
            pltpu.VMEM((WINDOW, NKV_B * DH_B), F32),
        ],
        compiler_params=_cparams(("arbitrary", "arbitrary")),
        name="even_mixer",
    )(proj, k_hist.reshape(batch, WINDOW, NKV_B * DH_B), v_hist.reshape(batch, WINDOW, NKV_B * DH_B),
      c0, n0p, m0p, bias_tab, norm_a.reshape(NH_A, DV_A), sink)
    mix, c1, n1, m1, k1, v1 = outs
    return (mix, c1, n1[:, :NH_A, :], m1[:, :NH_A, 0],
            k1.reshape(batch, WINDOW, NKV_B, DH_B), v1.reshape(batch, WINDOW, NKV_B, DH_B))


def _odd_kernel(proj_ref, s0_ref, hp_ref, normc_ref, mix_ref, sout_ref, s_s, *, rows):
    R = rows
    i = pl.program_id(1)

    @pl.when(i == 0)
    def _():
        s_s[...] = s0_ref[...]

    CS = CHUNK
    P2 = 2 * CS
    gcols = proj_ref[:, O_G:O_G + LANES]
    beta_all = _sigmoid(gcols)
    g_all = -jnp.exp(hp_ref[0:1, :]) * _softplus(gcols + hp_ref[1:2, :])
    rr = lax.broadcasted_iota(I32, (R, R), 0)
    cc = lax.broadcasted_iota(I32, (R, R), 1)
    tri_b = jnp.where((rr >= cc) & (rr // CS == cc // CS), 1.0, 0.0).astype(BF16)
    G_all = _dot_exact_lhs(tri_b, g_all)
    G_t = _tr(G_all)
    eG_all = jnp.exp(G_all)
    r2 = lax.broadcasted_iota(I32, (P2, P2), 0)
    c2 = lax.broadcasted_iota(I32, (P2, P2), 1)
    same_head = (r2 // CS) == (c2 // CS)
    incl2 = same_head & (r2 >= c2)
    strict2 = same_head & (r2 > c2)
    eye2 = jnp.where(r2 == c2, 1.0, 0.0)
    lane_lo = lax.broadcasted_iota(I32, (DK_C, P2), 1) < CS
    scale_c = DK_C ** -0.5

    def qkv(rs0, c0):
        return proj_ref[rs0:rs0 + CS, O_QKV + c0:O_QKV + c0 + LANES]

    def stack(a, b):
        return jnp.concatenate([a, b], axis=0)

    def inv_unit_lower_all(a_list):
        ts = [eye2 - a for a in a_list]
        abs_ = [a.astype(BF16) for a in a_list]
        ps = [_dot(ab, ab) for ab in abs_]
        nlev = (CS - 1).bit_length() - 1
        n = len(ts)
        for lvl in range(nlev):
            th = [t.astype(BF16) for t in ts]
            ph = [p.astype(BF16) for p in ps]
            if lvl < nlev - 1:
                lhs = [stack(th[i], ph[i]) for i in range(n)]
                tp = [_dot(lhs[i], ph[i]) for i in range(n)]
                ts = [ts[i] + tp[i][:P2] for i in range(n)]
                ps = [tp[i][P2:] for i in range(n)]
            else:
                tp = [_dot(th[i], ph[i]) for i in range(n)]
                ts = [ts[i] + tp[i] for i in range(n)]
        return ts

    npair = NH_C // 2
    probs = [(c0, pr) for c0 in range(0, R, CS) for pr in range(npair)]
    nP = len(probs)
    rsl = [slice(c0, c0 + CS) for c0, _ in probs]
    hab = [(2 * pr, 2 * pr + 1) for _, pr in probs]

    def per_head(fn):
        return [[fn(p, h) for h in hab[p]] for p in range(nP)]

    def col2(arr, off):
        return [stack(arr[rsl[p], off + hab[p][0]:off + hab[p][0] + 1], arr[rsl[p], off + hab[p][1]:off + hab[p][1] + 1])
                for p in range(nP)]

    kc = per_head(lambda p, h: qkv(probs[p][0], NH_C * DK_C + h * DK_C))
    qc = per_head(lambda p, h: qkv(probs[p][0], h * DK_C))
    vc = per_head(lambda p, h: qkv(probs[p][0], 2 * NH_C * DK_C + h * DV_C))
    k2 = [stack(*kc[p]) for p in range(nP)]
    q2b = [stack(*[x * scale_c for x in qc[p]]).astype(BF16) for p in range(nP)]
    v2 = [stack(*vc[p]) for p in range(nP)]
    k2b = [k2[p].astype(BF16) for p in range(nP)]
    G_col = col2(G_all, NH_C)
    G_row = [jnp.concatenate([G_t[NH_C + hab[p][0]:NH_C + hab[p][0] + 1, rsl[p]],
                              G_t[NH_C + hab[p][1]:NH_C + hab[p][1] + 1, rsl[p]]], axis=1) for p in range(nP)]
    beta = col2(beta_all, 0)
    eg = col2(eG_all, NH_C)
    GL = [[G_all[probs[p][0] + CS - 1:probs[p][0] + CS, NH_C + h:NH_C + h + 1] for h in hab[p]] for p in range(nP)]
    GL2 = [stack(jnp.broadcast_to(GL[p][0], (CS, 1)), jnp.broadcast_to(GL[p][1], (CS, 1))) for p in range(nP)]
    egl = [[jnp.exp(x) for x in GL[p]] for p in range(nP)]
    kk = [_dot(k2b[p], k2b[p], _NT) for p in range(nP)]
    qk = [_dot(q2b[p], k2b[p], _NT) for p in range(nP)]
    dmat = [jnp.exp(jnp.where(incl2, G_col[p] - G_row[p], -jnp.inf)) for p in range(nP)]
    a_mat = [jnp.where(strict2, beta[p] * dmat[p] * kk[p], 0.0) for p in range(nP)]
    attn = [(qk[p] * dmat[p]).astype(BF16) for p in range(nP)]
    rhs = [jnp.concatenate([beta[p] * v2[p], (beta[p] * eg[p]) * k2[p]], axis=1).astype(BF16) for p in range(nP)]
    kdec = [k2[p] * jnp.exp(GL2[p] - G_col[p]) for p in range(nP)]
    kdec_t = [_tr(kdec[p]).astype(BF16) for p in range(nP)]
    zero_t = jnp.zeros((DK_C, P2), BF16)
    kdec_lo = [jnp.where(lane_lo, kdec_t[p], zero_t) for p in range(nP)]
    kdec_hi = [jnp.where(lane_lo, zero_t, kdec_t[p]) for p in range(nP)]
    tinv = [t.astype(BF16) for t in inv_unit_lower_all(a_mat)]
    uw = [_dot(tinv[p], rhs[p]) for p in range(nP)]
    wq_lhs = [[stack(uw[p][j * CS:(j + 1) * CS, DV_C:].astype(BF16), q2b[p][j * CS:(j + 1) * CS]) for j in range(2)]
              for p in range(nP)]

    S_cur = [s_s[h] for h in range(NH_C)]
    for ci, c0 in enumerate(range(0, R, CS)):
        rs = slice(c0, c0 + CS)
        ps_ = list(range(ci * npair, (ci + 1) * npair))
        Sb = [S_cur[h].astype(BF16) for h in range(NH_C)]
        wq = [[_dot(wq_lhs[p][j], Sb[hab[p][j]]) for j in range(2)] for p in ps_]
        db = [(uw[p][:, :DV_C] - stack(wq[i_][0][:CS], wq[i_][1][:CS])).astype(BF16) for i_, p in enumerate(ps_)]
        upd_lo = [_dot(kdec_lo[p], db[i_]) for i_, p in enumerate(ps_)]
        upd_hi = [_dot(kdec_hi[p], db[i_]) for i_, p in enumerate(ps_)]
        for i_, p in enumerate(ps_):
            S_cur[hab[p][0]] = egl[p][0] * S_cur[hab[p][0]] + upd_lo[i_]
            S_cur[hab[p][1]] = egl[p][1] * S_cur[hab[p][1]] + upd_hi[i_]
        ad = [_dot(attn[p], db[i_]) for i_, p in enumerate(ps_)]
        o2 = [eg[p] * stack(wq[i_][0][CS:], wq[i_][1][CS:]) + ad[i_] for i_, p in enumerate(ps_)]
        oh = [o2[i_][j * CS:(j + 1) * CS] for i_ in range(npair) for j in range(2)]
        ms = [jnp.mean(o * o, axis=-1, keepdims=True) for o in oh]
        zs = [proj_ref[rs, O_Z + h * DV_C:O_Z + (h + 1) * DV_C] for h in range(NH_C)]
        zg = [z * _sigmoid(z) for z in zs]
        for h in range(NH_C):
            o = oh[h] * lax.rsqrt(ms[h] + RMS_EPS) * normc_ref[...] * zg[h]
            mix_ref[rs, h * DV_C:(h + 1) * DV_C] = o.astype(mix_ref.dtype)
    for h in range(NH_C):
        s_s[h] = S_cur[h]

    sout_ref[...] = s_s[...]


def _odd_mixer(proj, s0, a_log, dt_bias, norm_c, batch, seq_len, rows):
    nsteps = seq_len // rows
    hp = jnp.zeros((8, LANES), F32)
    hp = hp.at[0, NH_C:2 * NH_C].set(a_log.astype(F32)).at[1, NH_C:2 * NH_C].set(dt_bias.astype(F32))
    outs = pl.pallas_call(
        functools.partial(_odd_kernel, rows=rows),
        out_shape=(
            jax.ShapeDtypeStruct((batch * seq_len, D_MODEL), BF16),
            jax.ShapeDtypeStruct((batch, NH_C, DK_C, DV_C), F32),
        ),
        grid=(batch, nsteps),
        in_specs=[
            pl.BlockSpec((rows, N_ODD_COLS), lambda b, i: (b * nsteps + i, 0)),
            pl.BlockSpec((None, NH_C, DK_C, DV_C), lambda b, i: (b, 0, 0, 0)),
            pl.BlockSpec((8, LANES), lambda b, i: (0, 0)),
            pl.BlockSpec((1, DV_C), lambda b, i: (0, 0)),
        ],
        out_specs=(
            pl.BlockSpec((rows, D_MODEL), lambda b, i: (b * nsteps + i, 0)),
            pl.BlockSpec((None, NH_C, DK_C, DV_C), lambda b, i: (b, 0, 0, 0)),
        ),
        scratch_shapes=[pltpu.VMEM((NH_C, DK_C, DV_C), F32)],
        compiler_params=_cparams(("arbitrary", "arbitrary")),
        name="odd_mixer",
    )(proj, s0, hp, norm_c.reshape(1, DV_C))
    return outs


def _post_kernel(mix_ref, x_ref, gate_ref, sc_ref, sh_ref, wout_ref, lng_ref, lnb_ref, wr_ref, br_ref,
                 x1_ref, h2_ref, ri_ref, rg_ref, seg_ref, cnt_ref, carry_s, *, tm):
    i = pl.program_id(0)

    @pl.when(i == 0)
    def _():
        carry_s[...] = jnp.zeros_like(carry_s)

    ng = max(1, tm // 256)
    tg = tm // ng
    ys = [_dot(mix_ref[q * tg:(q + 1) * tg, :], wout_ref[...]) for q in range(min(2, ng))]
    wr_b = wr_ref[...].astype(BF16)
    lts = []
    for q in range(ng):
        rq = slice(q * tg, (q + 1) * tg)
        x1 = _layer_norm_rows(ALPHA_DN * x_ref[rq, :] + (1.0 + gate_ref[...]) * ys[q], lng_ref[...], lnb_ref[...])
        if q + 2 < ng:
            ys.append(_dot(mix_ref[(q + 2) * tg:(q + 3) * tg, :], wout_ref[...]))
        x1_ref[rq, :] = x1
        h2 = x1 * (1.0 + sc_ref[...]) + sh_ref[...]
        h2b = h2.astype(BF16)
        h2_ref[rq, :] = h2b
        lts.append(_dot(wr_b, h2b, _NT))
    lt = (lts[0] if ng == 1 else jnp.concatenate(lts, axis=1)) + br_ref[:, 0:1]
    e_iota = lax.broadcasted_iota(I32, (N_EXPERTS, tm), 0).astype(F32)
    vals, idxs = [], []
    for _ in range(TOP_K):
        mx = jnp.max(lt, axis=0, keepdims=True)
        idx = jnp.min(jnp.where(lt == mx, e_iota, float(N_EXPERTS)), axis=0, keepdims=True)
        vals.append(mx)
        idxs.append(idx)
        lt = jnp.where(e_iota == idx, -jnp.inf, lt)
    ex = [jnp.exp(v - vals[0]) for v in vals]
    tot = ex[0] + ex[1] + ex[2] + ex[3]
    hot = [jnp.where(e_iota == idx, 1.0, 0.0) for idx in idxs]
    m_all = hot[0] + hot[1] + hot[2] + hot[3]
    m_all_b = m_all.astype(BF16)
    r = lax.broadcasted_iota(I32, (tm, tm), 0)
    c = lax.broadcasted_iota(I32, (tm, tm), 1)
    upper = jnp.where(r < c, 1.0, 0.0).astype(BF16)
    prefix = _dot(m_all_b, upper)
    re = lax.broadcasted_iota(I32, (N_EXPERTS, N_EXPERTS), 0)
    ce = lax.broadcasted_iota(I32, (N_EXPERTS, N_EXPERTS), 1)
    cnt_col = jnp.sum(m_all, axis=1, keepdims=True)
    m8_col = jnp.floor((cnt_col + 7.0) * 0.125)
    lower_e = jnp.where(re > ce, 1.0, 0.0).astype(BF16)
    off8_col = 8.0 * _dot(lower_e, jnp.broadcast_to(m8_col, (N_EXPERTS, LANES)).astype(BF16))[:, 0:1]
    base = off8_col + prefix
    dests = [jnp.sum(hk * base, axis=0, keepdims=True) for hk in hot]
    cnt_row = _dot(jnp.ones((8, tm), BF16), m_all_b, _NT)
    m8_row = jnp.floor((cnt_row + 7.0) * 0.125)
    upper_e = jnp.where(re < ce, 1.0, 0.0).astype(BF16)
    off8_row = 8.0 * _dot(m8_row.astype(BF16), upper_e)
    gc = carry_s[:, 0:N_EXPERTS]
    srow = lax.broadcasted_iota(I32, (8, N_EXPERTS), 0)
    seg = jnp.where(srow == 0, 8.0 * m8_row, jnp.where(srow == 1, off8_row, jnp.where(srow == 2, gc, 0.0)))
    seg_ref[...] = jnp.concatenate([seg, jnp.zeros((8, LANES - N_EXPERTS), F32)], axis=1).astype(I32)
    carry_s[:, 0:N_EXPERTS] = gc + 8.0 * m8_row
    cnt_ref[...] = carry_s[...]
    ri_ref[...] = jnp.concatenate(dests + [jnp.zeros((4, tm), F32)], axis=0).astype(I32)
    rg_ref[...] = jnp.concatenate([e / tot for e in ex] + [jnp.zeros((4, tm), F32)], axis=0)


def _post(mix, x2d, gate, scale, shift, w_out_bf16, ln_g, ln_b, wr_t, br, seq_len, tm):
    t = x2d.shape[0]
    per = seq_len // tm
    vec = lambda i: (i // per, 0, 0)
    const2 = lambda i: (0, 0)
    return pl.pallas_call(
        functools.partial(_post_kernel, tm=tm),
        out_shape=(
            jax.ShapeDtypeStruct((t, D_MODEL), F32),
            jax.ShapeDtypeStruct((t, D_MODEL), BF16),
            jax.ShapeDtypeStruct((t // tm, 8, tm), I32),
            jax.ShapeDtypeStruct((t // tm, 8, tm), F32),
            jax.ShapeDtypeStruct((t // tm, 8, LANES), I32),
            jax.ShapeDtypeStruct((8, LANES), F32),
        ),
        grid=(t // tm,),
        in_specs=[
            pl.BlockSpec((tm, D_MODEL), lambda i: (i, 0)),
            pl.BlockSpec((tm, D_MODEL), lambda i: (i, 0)),
            pl.BlockSpec((None, 1, D_MODEL), vec),
            pl.BlockSpec((None, 1, D_MODEL), vec),
            pl.BlockSpec((None, 1, D_MODEL), vec),
            pl.BlockSpec((D_MODEL, D_MODEL), const2),
            pl.BlockSpec((1, D_MODEL), const2),
            pl.BlockSpec((1, D_MODEL), const2),
            pl.BlockSpec((N_EXPERTS, D_MODEL), const2),
            pl.BlockSpec((N_EXPERTS, LANES), const2),
        ],
        out_specs=(
            pl.BlockSpec((tm, D_MODEL), lambda i: (i, 0)),
            pl.BlockSpec((tm, D_MODEL), lambda i: (i, 0)),
            pl.BlockSpec((None, 8, tm), lambda i: (i, 0, 0)),
            pl.BlockSpec((None, 8, tm), lambda i: (i, 0, 0)),
            pl.BlockSpec((None, 8, LANES), lambda i: (i, 0, 0)),
            pl.BlockSpec((8, LANES), const2),
        ),
        scratch_shapes=[pltpu.VMEM((8, LANES), F32)],
        compiler_params=_cparams(("arbitrary",)),
        name="post_router",
    )(mix, x2d, gate, scale, shift, w_out_bf16, ln_g, ln_b, wr_t, br)


SEG_FIELDS = 3


def _run_sizes(tm):
    top = 1 << (tm - 1).bit_length()
    return [s for s in (512, 256, 128, 64, 32, 16, 8) if s <= max(top, 8)]


def _for_each_run_piece(seg_ref, rstart_ref, tile, tm, fn):
    base = tile * (SEG_FIELDS * N_EXPERTS)

    def body(e, c):
        n8 = seg_ref[base + e]
        src = seg_ref[base + N_EXPERTS + e]
        dst = rstart_ref[e] + seg_ref[base + 2 * N_EXPERTS + e]
        def pieces(sizes, done):
            for size in sizes:
                @pl.when((n8 & size) != 0)
                def _(done=done, size=size):
                    fn(pl.multiple_of(src + done, 8), pl.multiple_of(dst + done, 8), size)
                done = done + (n8 & size)

        split = 128
        big = [s for s in _run_sizes(tm) if s >= split]
        if big:
            @pl.when(n8 >= split)
            def _():
                pieces(big, jnp.int32(0))
        pieces([s for s in _run_sizes(tm) if s < split], n8 & ~jnp.int32(split - 1))
        return c

    lax.fori_loop(0, N_EXPERTS, body, 0)


def _wait_tile_rows(seg_ref, tile, cb, wait_rows):
    base = tile * (SEG_FIELDS * N_EXPERTS)
    last = N_EXPERTS - 1
    total = seg_ref[base + last] + seg_ref[base + N_EXPERTS + last]
    size = 1 << (cb.bit_length() - 1)
    while size >= 8:
        @pl.when((total & size) != 0)
        def _(size=size):
            wait_rows(size)
        size //= 2


def _dispatch_kernel(zblk_ref, rstart_ref, seg_ref, ri_ref, h_ref, xs_ref, zero_s, cbuf, sem, zsem,
                     *, tm, bm, cb, ntiles):
    i = pl.program_id(0)
    slot = i % 2

    def run_copies(tile, sl, wait):
        if wait:
            def wait_rows(size):
                pltpu.make_async_copy(cbuf.at[sl, pl.ds(0, size), :], xs_ref.at[pl.ds(0, size), :], sem.at[sl]).wait()
            _wait_tile_rows(seg_ref, tile, cb, wait_rows)
            return

        def piece(src, dst, size):
            pltpu.make_async_copy(cbuf.at[sl, pl.ds(src, size), :], xs_ref.at[pl.ds(dst, size), :], sem.at[sl]).start()
        _for_each_run_piece(seg_ref, rstart_ref, tile, tm, piece)

    def zero_copy(e):
        return pltpu.make_async_copy(zero_s, xs_ref.at[pl.ds(zblk_ref[e] * bm, bm), :], zsem)

    @pl.when(i == 0)
    def _():
        zero_s[...] = jnp.zeros_like(zero_s)

        def zstart(e, c):
            @pl.when(zblk_ref[e] >= 0)
            def _():
                zero_copy(e).start()
            return c

        def zwait(e, c):
            @pl.when(zblk_ref[e] >= 0)
            def _():
                zero_copy(e).wait()
            return c

        lax.fori_loop(0, N_EXPERTS, zstart, 0)
        lax.fori_loop(0, N_EXPERTS, zwait, 0)

    @pl.when(i >= 2)
    def _():
        run_copies(i - 2, slot, True)

    rows = lax.broadcasted_iota(I32, (cb, tm), 0)
    hit = rows == ri_ref[0:1, :]
    for k in range(1, TOP_K):
        hit = hit | (rows == ri_ref[k:k + 1, :])
    perm = jnp.where(hit, 1.0, 0.0).astype(BF16)
    cbuf[slot] = _dot(perm, h_ref[...])
    run_copies(i, slot, False)

    @pl.when(i == ntiles - 1)
    def _():
        if ntiles > 1:
            run_copies(i - 1, 1 - slot, True)
        run_copies(i, slot, True)


def _dispatch(h2, route_i, seg_flat, row_start, zblk, n_rows, tm, bm):
    t = h2.shape[0]
    ntiles = t // tm
    cb = TOP_K * tm + 8 * N_EXPERTS
    return pl.pallas_call(
        functools.partial(_dispatch_kernel, tm=tm, bm=bm, cb=cb, ntiles=ntiles),
        out_shape=jax.ShapeDtypeStruct((n_rows, D_MODEL), F32),
        grid_spec=pltpu.PrefetchScalarGridSpec(
            num_scalar_prefetch=3,
            grid=(ntiles,),
            in_specs=[
                pl.BlockSpec((None, 2 * TOP_K, tm), lambda i, z, r, s: (i, 0, 0)),
                pl.BlockSpec((tm, D_MODEL), lambda i, z, r, s: (i, 0)),
            ],
            out_specs=pl.BlockSpec(memory_space=pl.ANY),
            scratch_shapes=[
                pltpu.VMEM((bm, D_MODEL), F32),
                pltpu.VMEM((2, cb, D_MODEL), F32),
                pltpu.SemaphoreType.DMA((2,)),
                pltpu.SemaphoreType.DMA,
            ],
        ),
        compiler_params=_cparams(("arbitrary",)),
        name="dispatch",
    )(zblk, row_start, seg_flat, route_i, h2)


def _ffn_kernel(be_ref, grp_ref, ia_ref, ib_ref, nv_ref, xa_ref, xb_ref, w1_ref, b1_ref, w2_ref, b2_ref,
                ya_ref, yb_ref, w1_s, w2_s):
    b = pl.program_id(0)

    def expert_block(x_ref, y_ref):
        xb = x_ref[...].astype(BF16)
        glu = _dot(xb, w1_s[:, :D_FF]) + b1_ref[:, :D_FF]
        lin = _dot(xb, w1_s[:, D_FF:]) + b1_ref[:, D_FF:]
        glu = jnp.minimum(glu, SWIGLU_LIMIT)
        lin = jnp.clip(lin, -SWIGLU_LIMIT, SWIGLU_LIMIT)
        act = glu * _sigmoid(SWIGLU_ALPHA * glu) * (lin + 1.0)
        y_ref[...] = _dot(act.astype(BF16), w2_s[...]) + b2_ref[...]

    @pl.when(b < nv_ref[0])
    def _():
        @pl.when((b == 0) | (be_ref[b] != be_ref[jnp.maximum(b - 1, 0)]))
        def _():
            step = 256
            for r0 in range(0, D_MODEL, step):
                w1_s[r0:r0 + step, :] = w1_ref[r0:r0 + step, :].astype(BF16)
            for r0 in range(0, D_FF, step):
                w2_s[r0:r0 + step, :] = w2_ref[r0:r0 + step, :].astype(BF16)

        @pl.when(grp_ref[b] == 0)
        def _():
            expert_block(xa_ref, ya_ref)

        @pl.when(grp_ref[b] == 1)
        def _():
            expert_block(xb_ref, yb_ref)


def _ffn(xs_a, xs_b, steps, layer, w1, b1, w2, b2, bm_a, bm_b):
    step_e, step_g, step_ia, step_ib, n_valid = steps
    nsteps = step_e.shape[0]
    row_a = lambda b, be, g, ia, ib, nv: (ia[b], 0)
    row_b = lambda b, be, g, ia, ib, nv: (ib[b], 0)
    wsel = lambda b, be, g, ia, ib, nv: (layer, be[b], 0, 0)
    return pl.pallas_call(
        _ffn_kernel,
        out_shape=(jax.ShapeDtypeStruct(xs_a.shape, F32), jax.ShapeDtypeStruct(xs_b.shape, F32)),
        grid_spec=pltpu.PrefetchScalarGridSpec(
            num_scalar_prefetch=5,
            grid=(nsteps,),
            in_specs=[
                pl.BlockSpec((bm_a, D_MODEL), row_a),
                pl.BlockSpec((bm_b, D_MODEL), row_b),
                pl.BlockSpec((None, None, D_MODEL, 2 * D_FF), wsel),
                pl.BlockSpec((None, None, 1, 2 * D_FF), wsel),
                pl.BlockSpec((None, None, D_FF, D_MODEL), wsel),
                pl.BlockSpec((None, None, 1, D_MODEL), wsel),
            ],
            out_specs=(pl.BlockSpec((bm_a, D_MODEL), row_a), pl.BlockSpec((bm_b, D_MODEL), row_b)),
            scratch_shapes=[pltpu.VMEM((D_MODEL, 2 * D_FF), BF16), pltpu.VMEM((D_FF, D_MODEL), BF16)],
        ),
        compiler_params=_cparams(("arbitrary",)),
        name="expert_ffn",
    )(step_e, step_g, step_ia, step_ib, n_valid, xs_a, xs_b, w1, b1, w2, b2)


def _combine_kernel(rstart_ref, seg_ref, ri_ref, x_ref, gate_ref, rg_ref, lng_ref, lnb_ref, ys_ref, o_ref,
                    ybuf, yb_s, wt_s, sem, *, tm, cb, ntiles):
    i = pl.program_id(0)
    slot = i % 2

    def run_copies(tile, sl, wait):
        if wait:
            def wait_rows(size):
                pltpu.make_async_copy(ys_ref.at[pl.ds(0, size), :], ybuf.at[sl, pl.ds(0, size), :], sem.at[sl]).wait()
            _wait_tile_rows(seg_ref, tile, cb, wait_rows)
            return

        def piece(loc, glob, size):
            pltpu.make_async_copy(ys_ref.at[pl.ds(glob, size), :], ybuf.at[sl, pl.ds(loc, size), :], sem.at[sl]).start()
        _for_each_run_piece(seg_ref, rstart_ref, tile, tm, piece)

    @pl.when(i == 0)
    def _():
        ybuf[...] = jnp.zeros_like(ybuf)
        run_copies(0, 0, False)

    @pl.when(i + 1 < ntiles)
    def _():
        run_copies(i + 1, 1 - slot, False)

    dest_t = _tr(jnp.concatenate([ri_ref[...].astype(F32), jnp.zeros((LANES - 8, tm), F32)], axis=0))
    g_t = _tr(jnp.concatenate([rg_ref[...], jnp.zeros((LANES - 8, tm), F32)], axis=0))
    lane_step = 256
    nq = max(1, tm // 256)
    tq = tm // nq

    def build(q):
        rq = slice(q * tq, (q + 1) * tq)
        for c0 in range(0, cb, lane_step):
            cols = (lax.broadcasted_iota(I32, (tq, lane_step), 1) + c0).astype(F32)
            w = jnp.where(cols == dest_t[rq, 0:1], g_t[rq, 0:1], 0.0)
            for k in range(1, TOP_K):
                w = w + jnp.where(cols == dest_t[rq, k:k + 1], g_t[rq, k:k + 1], 0.0)
            wt_s[rq, c0:c0 + lane_step] = w.astype(BF16)

    def product(q):
        rq = slice(q * tq, (q + 1) * tq)
        return _dot(wt_s[rq, :], yb_s[...])

    def finish(q, moe):
        rq = slice(q * tq, (q + 1) * tq)
        o_ref[rq, :] = _layer_norm_rows(ALPHA_DN * x_ref[rq, :] + (1.0 + gate_ref[...]) * moe,
                                        lng_ref[...], lnb_ref[...])

    build(0)
    run_copies(i, slot, True)
    row_step = 256
    for r0 in range(0, cb, row_step):
        yb_s[r0:r0 + row_step, :] = ybuf[slot, r0:r0 + row_step, :].astype(BF16)
    moes = {}
    for q in range(nq):
        if q + 1 < nq:
            build(q + 1)
        moes[q] = product(q)
        if q >= 1:
            finish(q - 1, moes.pop(q - 1))
    finish(nq - 1, moes.pop(nq - 1))


def _combine(row_start, seg_flat, route_i, x1, gate, route_g, ln_g, ln_b, ys, seq_len, tm):
    t = x1.shape[0]
    per = seq_len // tm
    ntiles = t // tm
    cb = TOP_K * tm + 8 * N_EXPERTS
    return pl.pallas_call(
        functools.partial(_combine_kernel, tm=tm, cb=cb, ntiles=ntiles),
        out_shape=jax.ShapeDtypeStruct((t, D_MODEL), F32),
        grid_spec=pltpu.PrefetchScalarGridSpec(
            num_scalar_prefetch=2,
            grid=(ntiles,),
            in_specs=[
                pl.BlockSpec((None, 2 * TOP_K, tm), lambda i, r, s: (i, 0, 0)),
                pl.BlockSpec((tm, D_MODEL), lambda i, r, s: (i, 0)),
                pl.BlockSpec((None, 1, D_MODEL), lambda i, r, s: (i // per, 0, 0)),
                pl.BlockSpec((None, 8, tm), lambda i, r, s: (i, 0, 0)),
                pl.BlockSpec((1, D_MODEL), lambda i, r, s: (0, 0)),
                pl.BlockSpec((1, D_MODEL), lambda i, r, s: (0, 0)),
                pl.BlockSpec(memory_space=pl.ANY),
            ],
            out_specs=pl.BlockSpec((tm, D_MODEL), lambda i, r, s: (i, 0)),
            scratch_shapes=[
                pltpu.VMEM((2, cb, D_MODEL), F32),
                pltpu.VMEM((cb, D_MODEL), BF16),
                pltpu.VMEM((tm, cb), BF16),
                pltpu.SemaphoreType.DMA((2,)),
            ],
        ),
        compiler_params=_cparams(("arbitrary",)),
        name="combine",
    )(row_start, seg_flat, route_i, x1, gate, route_g, ln_g, ln_b, ys)


def _max_blocks(t, tm, bm):
    return -(-(t * TOP_K + 7 * N_EXPERTS * (t // tm)) // bm) + N_EXPERTS


def _group_tables(counts, bm):
    cnt = counts[0, :N_EXPERTS].astype(I32)
    nblk = (cnt + bm - 1) // bm
    blk_end = jnp.cumsum(nblk)
    row_start = ((blk_end - nblk) * bm).astype(I32)
    zblk = jnp.where((cnt % bm) != 0, blk_end - 1, -1).astype(I32)
    return nblk, row_start, zblk


def _ffn_steps(nblk_a, nblk_b, nsteps):
    per_e = nblk_a + nblk_b
    ends = jnp.cumsum(per_e)
    n_valid = ends[-1]
    s = jnp.minimum(jnp.arange(nsteps, dtype=I32), n_valid - 1)
    step_e = jnp.minimum(jnp.sum((ends[None, :] <= s[:, None]).astype(I32), axis=1), N_EXPERTS - 1)
    onehot = (step_e[:, None] == jnp.arange(N_EXPERTS, dtype=I32)[None, :]).astype(I32)
    off = s - jnp.sum(onehot * (ends - per_e)[None, :], axis=1)
    in_b = off >= jnp.sum(onehot * nblk_a[None, :], axis=1)
    live = jnp.arange(nsteps, dtype=I32) < n_valid
    step_g = jnp.where(live, in_b.astype(I32), 2)
    step_ia = jnp.maximum(jnp.cumsum((step_g == 0).astype(I32)) - 1, 0)
    step_ib = jnp.maximum(jnp.cumsum((step_g == 1).astype(I32)) - 1, 0)
    return (step_e.astype(I32), step_g.astype(I32), step_ia.astype(I32), step_ib.astype(I32),
            n_valid.reshape(1).astype(I32))


def _group_cfg(batch, seq_len):
    if seq_len >= 512:
        return dict(tm=512, rows=256, bm=512)
    return dict(tm=seq_len, rows=seq_len, bm=128)


def kernel(x_prompt, x_sample, c_prompt, c_sample, state_a_C, state_a_n, state_a_m, cache_b_k, cache_b_v,
           state_c_S, state_c_conv, w_ada, b_ada, ln_g, ln_b, w_in_even, b_in_even, norm_a, sink_b, rel_bias,
           w_out_even, w_in_odd, conv_c, a_log_c, dt_bias_c, norm_c, w_out_odd, w_router, b_router,
           w_e1, b_e1, w_e2, b_e2):
    bp, lp, _ = x_prompt.shape
    bs, ls, _ = x_sample.shape
    groups = [dict(b=bp, l=lp, x=x_prompt.reshape(bp * lp, D_MODEL), **_group_cfg(bp, lp)),
              dict(b=bs, l=ls, x=x_sample.reshape(bs * ls, D_MODEL), **_group_cfg(bs, ls))]
    mod = _ada(jnp.concatenate([c_prompt, c_sample], axis=0), w_ada, b_ada)
    offs = [0, bp]
    states = [dict(), dict()]
    b1_all = b_e1.reshape(DEPTH, N_EXPERTS, 1, 2 * D_FF)
    b2_all = b_e2.reshape(DEPTH, N_EXPERTS, 1, D_MODEL)
    for l in range(DEPTH):
        e = l // 2
        if l % 2 == 0:
            w = w_in_even[e]
            sz = (512, 512, 512, 512, 4, 4, 512, 128, 128)
            o = [sum(sz[:j]) for j in range(len(sz) + 1)]
            pad = jnp.zeros((D_MODEL, LANES - 2 * NH_A), w.dtype)
            w_in = jnp.concatenate([w[:, o[0]:o[4]], w[:, o[6]:o[9]], w[:, o[4]:o[6]], pad], axis=1).astype(BF16)
            bb = b_in_even[e]
            b_in = jnp.concatenate([bb[o[0]:o[4]], bb[o[6]:o[9]], bb[o[4]:o[6]],
                                    jnp.zeros((LANES - 2 * NH_A,), bb.dtype)]).reshape(1, N_EVEN_COLS)
            w_out = w_out_even[e].astype(BF16)
        else:
            w = w_in_odd[e]
            pad = jnp.zeros((D_MODEL, LANES - 2 * NH_C), w.dtype)
            w_in = jnp.concatenate([w, pad], axis=1).astype(BF16)
            b_in = jnp.zeros((1, N_ODD_COLS), F32)
            w_out = w_out_odd[e].astype(BF16)
        wr_t = w_router[l].T
        br = jnp.broadcast_to(b_router[l][:, None], (N_EXPERTS, LANES))
        moe_in = []
        for gi, gr in enumerate(groups):
            nb, sl = gr['b'], gr['l']
            m = mod[l, offs[gi]:offs[gi] + nb].reshape(nb, 6, 1, D_MODEL)
            m6 = [m[:, j] for j in range(6)]
            st = states[gi]
            if l % 2 == 0:
                proj = _inproj(gr['x'], m6[1], m6[0], w_in, b_in, sl, gr['tm'])
                if gi == 0:
                    kh = jnp.zeros((nb, WINDOW, NKV_B, DH_B), F32)
                    vh = kh
                    c0 = jnp.zeros((nb, NH_A, DK_A, DV_A), F32)
                    n0 = jnp.zeros((nb, NH_A, DK_A), F32)
                    m0 = jnp.zeros((nb, NH_A), F32)
                else:
                    kh, vh, c0, n0, m0 = cache_b_k[e], cache_b_v[e], state_a_C[e], state_a_n[e], state_a_m[e]
                bias_tab = _swa_bias_table(rel_bias, gr['rows'])
                mix, c1, n1, m1, k1, v1 = _even_mixer(proj, kh, vh, c0, n0, m0, bias_tab, norm_a[e], sink_b[e],
                                                      nb, sl, gr['rows'], gi == 1)
                for name, val in (('a_C', c1), ('a_n', n1), ('a_m', m1), ('b_k', k1), ('b_v', v1)):
                    st.setdefault(name, []).append(val)
            else:
                if gi == 0:
                    cv0 = jnp.zeros((nb, CONV_W - 1, QKV_C), F32)
                    s0 = jnp.zeros((nb, NH_C, DK_C, DV_C), F32)
                else:
                    cv0, s0 = state_c_conv[e], state_c_S[e]
                proj, cv1 = _inproj_conv(gr['x'], m6[1], m6[0], w_in, cv0, conv_c[e], sl, gr['tm'])
                mix, s1 = _odd_mixer(proj, s0, a_log_c[e], dt_bias_c[e], norm_c[e], nb, sl, gr['rows'])
                st.setdefault('c_S', []).append(s1)
                st.setdefault('c_conv', []).append(cv1)
            x1, h2, route_i, route_g, seg, counts = _post(
                mix, gr['x'], m6[2], m6[4], m6[3], w_out, ln_g[l, 0].reshape(1, D_MODEL),
                ln_b[l, 0].reshape(1, D_MODEL), wr_t, br, sl, gr['tm'])
            t, tm, bm = nb * sl, gr['tm'], gr['bm']
            ntiles = t // tm
            nblk, row_start, zblk = _group_tables(counts, bm)
            seg_flat = seg[:, :SEG_FIELDS, :N_EXPERTS].reshape(ntiles * SEG_FIELDS * N_EXPERTS)
            xs = _dispatch(h2, route_i, seg_flat, row_start, zblk, _max_blocks(t, tm, bm) * bm, tm, bm)
            moe_in.append(dict(xs=xs, nblk=nblk, row_start=row_start, seg_flat=seg_flat, route_i=route_i,
                               route_g=route_g, x1=x1, gate=m6[5]))
        ga, gb = moe_in
        steps = _ffn_steps(ga['nblk'], gb['nblk'], sum(_max_blocks(g['b'] * g['l'], g['tm'], g['bm']) for g in groups))
        ys = _ffn(ga['xs'], gb['xs'], steps, l, w_e1, b1_all, w_e2, b2_all, groups[0]['bm'], groups[1]['bm'])
        for gi, gr in enumerate(groups):
            mi = moe_in[gi]
            gr['x'] = _combine(mi['row_start'], mi['seg_flat'], mi['route_i'], mi['x1'], mi['gate'], mi['route_g'],
                               ln_g[l, 1].reshape(1, D_MODEL), ln_b[l, 1].reshape(1, D_MODEL), ys[gi],
                               gr['l'], gr['tm'])
    outs = [groups[0]['x'].reshape(bp, lp, D_MODEL), groups[1]['x'].reshape(bs, ls, D_MODEL)]
    for gi in range(2):
        for name in ('a_C', 'a_n', 'a_m', 'b_k', 'b_v', 'c_S', 'c_conv'):
            outs.append(jnp.stack(states[gi][name]))
    return tuple(outs)
```

```python
import functools
import math

import jax
import jax.numpy as jnp
from jax import lax
from jax.experimental import pallas as pl
from jax.experimental.pallas import tpu as pltpu

F32 = jnp.float32
BF16 = jnp.bfloat16
I32 = jnp.int32

D_MODEL = 1024
CHUNK = 64
NH_A, DK_A, DV_A = 4, 128, 128
W_A = NH_A * DV_A
NH_B, NKV_B, DH_B, WINDOW = 8, 2, 64, 128
G_B = NH_B // NKV_B
W_B = NH_B * DH_B
NUM_BUCKETS, MAX_DISTANCE = 32, 256
NH_C, DK_C, DV_C, CONV_W = 8, 128, 128, 4
QKV_C = NH_C * (2 * DK_C + DV_C)
N_EXPERTS, TOP_K, D_FF = 32, 4, 1024
SWIGLU_LIMIT, SWIGLU_ALPHA = 7.0, 1.702
DEPTH = 2
ALPHA_DN = (2 * DEPTH) ** 0.25
LN_EPS = 1e-5
RMS_EPS = 1e-6

LANES = 128
E_QA, E_KA, E_VA, E_OA, E_QB, E_KB, E_VB, E_G = 0, 512, 1024, 1536, 2048, 2560, 2688, 2816
N_EVEN_COLS = 2944
O_QKV, O_Z, O_G = 0, 3072, 4096
N_ODD_COLS = 4224

VMEM_LIMIT = 56 * 1024 * 1024


def _cparams(sem):
    return pltpu.CompilerParams(dimension_semantics=sem, vmem_limit_bytes=VMEM_LIMIT)


def _softplus(x):
    return jnp.maximum(x, 0.0) + jnp.log(1.0 + jnp.exp(-jnp.abs(x)))


def _sigmoid(x):
    return 1.0 / (1.0 + jnp.exp(-x))


_NN = (((1,), (0,)), ((), ()))
_NT = (((1,), (1,)), ((), ()))


def _dot(a, b, dims=_NN):
    return lax.dot_general(a, b, dims, preferred_element_type=F32)


def _dot_exact_lhs(a_bf16, b, dims=_NN):
    b0 = b.astype(BF16)
    r1 = b - b0.astype(F32)
    b1 = r1.astype(BF16)
    b2 = (r1 - b1.astype(F32)).astype(BF16)
    return _dot(a_bf16, b0, dims) + _dot(a_bf16, b1, dims) + _dot(a_bf16, b2, dims)


def _tr(x):
    r = x.shape[0]
    rp = -(-r // LANES) * LANES
    if rp != r:
        x = jnp.concatenate([x, jnp.zeros((rp - r, x.shape[1]), x.dtype)], axis=0)
    return x.T[:, :r]


def _tri(n, strict=False):
    r = lax.broadcasted_iota(I32, (n, n), 0)
    c = lax.broadcasted_iota(I32, (n, n), 1)
    return (r > c) if strict else (r >= c)


def _layer_norm_rows(v, g, b):
    mu = jnp.mean(v, axis=-1, keepdims=True)
    d = v - mu
    var = jnp.mean(d * d, axis=-1, keepdims=True)
    return d * lax.rsqrt(var + LN_EPS) * g + b


def _ada_kernel(c_ref, w_ref, b_ref, o_ref):
    c = c_ref[...]
    a = (c * _sigmoid(c)).astype(BF16)
    o_ref[...] = _dot(a, w_ref[...].astype(BF16)) + b_ref[...]


def _ada(c_all, w_ada, b_ada):
    nb = c_all.shape[0]
    tn = 1536
    return pl.pallas_call(
        _ada_kernel,
        out_shape=jax.ShapeDtypeStruct((DEPTH, nb, 6 * D_MODEL), F32),
        grid=(DEPTH, 6 * D_MODEL // tn),
        in_specs=[
            pl.BlockSpec((nb, D_MODEL), lambda l, j: (0, 0)),
            pl.BlockSpec((None, D_MODEL, tn), lambda l, j: (l, 0, j)),
            pl.BlockSpec((None, 1, tn), lambda l, j: (l, 0, j)),
        ],
        out_specs=pl.BlockSpec((None, nb, tn), lambda l, j: (l, 0, j)),
        compiler_params=_cparams(("arbitrary", "arbitrary")),
        name="ada",
    )(c_all, w_ada, b_ada.reshape(DEPTH, 1, 6 * D_MODEL))


def _inproj_kernel(x_ref, sc_ref, sh_ref, w_ref, b_ref, o_ref, *, n_cols, col_step):
    h = (x_ref[...] * (1.0 + sc_ref[...]) + sh_ref[...]).astype(BF16)
    for c0 in range(0, n_cols, col_step):
        c1 = min(c0 + col_step, n_cols)
        o_ref[:, c0:c1] = _dot(h, w_ref[:, c0:c1]) + b_ref[:, c0:c1]


def _inproj(x2d, scale, shift, w_bf16, bias, seq_len, tm):
    t = x2d.shape[0]
    n = w_bf16.shape[1]
    per = seq_len // tm
    return pl.pallas_call(
        functools.partial(_inproj_kernel, n_cols=n, col_step=1024),
        out_shape=jax.ShapeDtypeStruct((t, n), F32),
        grid=(t // tm,),
        in_specs=[
            pl.BlockSpec((tm, D_MODEL), lambda i: (i, 0)),
            pl.BlockSpec((None, 1, D_MODEL), lambda i: (i // per, 0, 0)),
            pl.BlockSpec((None, 1, D_MODEL), lambda i: (i // per, 0, 0)),
            pl.BlockSpec((D_MODEL, n), lambda i: (0, 0)),
            pl.BlockSpec((1, n), lambda i: (0, 0)),
        ],
        out_specs=pl.BlockSpec((tm, n), lambda i: (i, 0)),
        compiler_params=_cparams(("arbitrary",)),
        name="inproj",
    )(x2d, scale, shift, w_bf16, bias)


CONV_HB = 8


def _inproj_conv_kernel(x_ref, sc_ref, sh_ref, w_ref, conv0_ref, convw_ref, o_ref, convout_ref, xa_s,
                        *, tm, per):
    i = pl.program_id(0)
    HB = CONV_HB

    @pl.when(i % per == 0)
    def _():
        xa_s[0:HB - (CONV_W - 1), :] = jnp.zeros((HB - (CONV_W - 1), QKV_C), F32)
        xa_s[HB - (CONV_W - 1):HB, :] = conv0_ref[...]

    h = (x_ref[...] * (1.0 + sc_ref[...]) + sh_ref[...]).astype(BF16)
    step = 512

    def project(c0):
        return _dot(h, w_ref[:, c0:c0 + step])

    def conv_group(g0, pre):
        xa = jnp.concatenate([xa_s[:, g0:g0 + step], pre], axis=0)
        for c in range(0, step, LANES):
            c0 = g0 + c
            xc = xa[:, c:c + LANES]
            y = None
            for j in range(CONV_W):
                back = xc if j == CONV_W - 1 else pltpu.roll(xc, CONV_W - 1 - j, 0)
                term = back[HB:HB + tm, :] * convw_ref[j:j + 1, c0:c0 + LANES]
                y = term if y is None else y + term
            y = y * _sigmoid(y)
            if c0 < 2 * NH_C * DK_C:
                y = y * lax.rsqrt(jnp.sum(y * y, axis=-1, keepdims=True) + RMS_EPS)
            o_ref[:, c0:c0 + LANES] = y
        convout_ref[:, g0:g0 + step] = pre[tm - (CONV_W - 1):tm, :]
        return pre[tm - HB:tm, :]

    groups = list(range(0, QKV_C, step))
    pres = {0: project(groups[0])}
    tails = []
    for gi, g0 in enumerate(groups):
        if gi + 1 < len(groups):
            pres[gi + 1] = project(groups[gi + 1])
        else:
            o_ref[:, O_Z:N_ODD_COLS] = _dot(h, w_ref[:, O_Z:N_ODD_COLS])
        tails.append(conv_group(g0, pres.pop(gi)))
    for g0, tail in zip(groups, tails):
        xa_s[:, g0:g0 + step] = tail


def _inproj_conv(x2d, scale, shift, w_bf16, conv_hist, conv_w, seq_len, tm):
    t = x2d.shape[0]
    per = seq_len // tm
    batch = t // seq_len
    return pl.pallas_call(
        functools.partial(_inproj_conv_kernel, tm=tm, per=per),
        out_shape=(jax.ShapeDtypeStruct((t, N_ODD_COLS), F32),
                   jax.ShapeDtypeStruct((batch, CONV_W - 1, QKV_C), F32)),
        grid=(t // tm,),
        in_specs=[
            pl.BlockSpec((tm, D_MODEL), lambda i: (i, 0)),
            pl.BlockSpec((None, 1, D_MODEL), lambda i: (i // per, 0, 0)),
            pl.BlockSpec((None, 1, D_MODEL), lambda i: (i // per, 0, 0)),
            pl.BlockSpec((D_MODEL, N_ODD_COLS), lambda i: (0, 0)),
            pl.BlockSpec((None, CONV_W - 1, QKV_C), lambda i: (i // per, 0, 0)),
            pl.BlockSpec((CONV_W, QKV_C), lambda i: (0, 0)),
        ],
        out_specs=(pl.BlockSpec((tm, N_ODD_COLS), lambda i: (i, 0)),
                   pl.BlockSpec((None, CONV_W - 1, QKV_C), lambda i: (i // per, 0, 0))),
        scratch_shapes=[pltpu.VMEM((CONV_HB, QKV_C), F32)],
        compiler_params=_cparams(("arbitrary",)),
        name="inproj_conv",
    )(x2d, scale, shift, w_bf16, conv_hist, conv_w)


def _even_kernel(proj_ref, kh0_ref, vh0_ref, c0_ref, n0_ref, m0_ref, bias_ref, norma_ref, sink_ref,
                 mix_ref, cout_ref, nout_ref, mout_ref, kout_ref, vout_ref,
                 c_s, n_s, m_s, kh_s, vh_s, *, rows, hist_valid):
    R = rows
    KW = WINDOW + R
    i = pl.program_id(1)

    @pl.when(i == 0)
    def _():
        c_s[...] = c0_ref[...]
        n_s[...] = n0_ref[...]
        m_s[...] = m0_ref[...]
        kh_s[...] = kh0_ref[...]
        vh_s[...] = vh0_ref[...]

    CS = CHUNK
    g = proj_ref[:, E_G:E_G + LANES]
    lf = -_softplus(-g)
    rr = lax.broadcasted_iota(I32, (R, R), 0)
    cc = lax.broadcasted_iota(I32, (R, R), 1)
    tri_b = jnp.where((rr >= cc) & (rr // CS == cc // CS), 1.0, 0.0).astype(BF16)
    b_all = _dot_exact_lhs(tri_b, lf)
    g_t = _tr(g)
    b_t = _tr(b_all)
    causal = _tri(CS)
    scale_a = DK_A ** -0.5
    chunks = list(range(0, R, CS))
    P = [(c0, h) for c0 in chunks for h in range(NH_A)]
    nP = len(P)
    rsl = [slice(c0, c0 + CS) for c0, _ in P]
    b_col = [b_all[rsl[p], NH_A + h:NH_A + h + 1] for p, (_, h) in enumerate(P)]
    b_row = [b_t[NH_A + h:NH_A + h + 1, rsl[p]] for p, (_, h) in enumerate(P)]
    i_row = [g_t[h:h + 1, rsl[p]] for p, (_, h) in enumerate(P)]
    i_col = [g[rsl[p], h:h + 1] for p, (_, h) in enumerate(P)]
    logw = [jnp.where(causal, b_col[p] - b_row[p] + i_row[p], -jnp.inf) for p in range(nP)]
    lmax = [jnp.max(logw[p], axis=-1, keepdims=True) for p in range(nP)]
    qb = [(proj_ref[rsl[p], E_QA + h * DK_A:E_QA + (h + 1) * DK_A] * scale_a).astype(BF16)
          for p, (_, h) in enumerate(P)]
    kf = [proj_ref[rsl[p], E_KA + h * DK_A:E_KA + (h + 1) * DK_A] for p, (_, h) in enumerate(P)]
    vb = [proj_ref[rsl[p], E_VA + h * DV_A:E_VA + (h + 1) * DV_A].astype(BF16) for p, (_, h) in enumerate(P)]
    qk = [_dot(qb[p], kf[p].astype(BF16), _NT) for p in range(nP)]
    b_last = [b_col[p][CS - 1:CS, :] for p in range(nP)]

    k_win = jnp.concatenate([kh_s[...], proj_ref[:, E_KB:E_KB + NKV_B * DH_B]], axis=0)
    v_win = jnp.concatenate([vh_s[...], proj_ref[:, E_VB:E_VB + NKV_B * DH_B]], axis=0)
    k_win_b = k_win.astype(BF16)
    v_win_b = v_win.astype(BF16)
    if not hist_valid:
        key_pos = lax.broadcasted_iota(I32, (R, KW), 1) + (i * R - WINDOW)
        key_ok = key_pos >= 0
    scale_b = DH_B ** -0.5
    lo = lax.broadcasted_iota(I32, (1, LANES), 1) < DH_B
    k_swp = pltpu.roll(k_win, DH_B, 1).astype(BF16)
    v_swp = pltpu.roll(v_win, DH_B, 1).astype(BF16)
    zk = jnp.zeros_like(k_win_b)
    k_both = [jnp.where(lo, k_win_b, k_swp), jnp.where(lo, k_swp, k_win_b)]
    v_lo = [jnp.where(lo, v_win_b, zk), jnp.where(lo, v_swp, zk)]
    v_hi = [jnp.where(lo, zk, v_swp), jnp.where(lo, zk, v_win_b)]
    npairs = NH_B // 2
    qp = [proj_ref[:, E_QB + j * LANES:E_QB + (j + 1) * LANES].astype(BF16) for j in range(npairs)]
    zq = jnp.zeros_like(qp[0])
    qh = [jnp.where(lo, qp[hd // 2], zq) if hd % 2 == 0 else jnp.where(lo, zq, qp[hd // 2]) for hd in range(NH_B)]
    scores = [_dot(qh[hd], k_both[hd // G_B], _NT) * scale_b + bias_ref[hd] for hd in range(NH_B)]
    if not hist_valid:
        scores = [jnp.where(key_ok, s, -jnp.inf) for s in scores]
    sks = [sink_ref[hd] for hd in range(NH_B)]
    mxs = [jnp.maximum(jnp.max(scores[hd], axis=-1, keepdims=True), sks[hd]) for hd in range(NH_B)]
    ps = [jnp.exp(scores[hd] - mxs[hd]) for hd in range(NH_B)]
    dens = [jnp.sum(ps[hd], axis=-1, keepdims=True) + jnp.exp(sks[hd] - mxs[hd]) for hd in range(NH_B)]
    pbs = [(ps[hd] / dens[hd]).astype(BF16) for hd in range(NH_B)]

    m_prev, m_inter, m_t, m_new = [None] * nP, [None] * nP, [None] * nP, [None] * nP
    for h in range(NH_A):
        run = m_s[h:h + 1, 0:1]
        for ci in range(len(chunks)):
            p = ci * NH_A + h
            m_prev[p] = run
            m_inter[p] = b_col[p] + run
            m_t[p] = jnp.maximum(m_inter[p], lmax[p])
            run = m_t[p][CS - 1:CS, :]
            m_new[p] = run
        m_s[h:h + 1, :] = jnp.broadcast_to(run, (1, LANES))
    w = [jnp.exp(logw[p] - m_t[p]) for p in range(nP)]
    s = [qk[p] * w[p] for p in range(nP)]
    sv = [_dot(s[p].astype(BF16), vb[p]) for p in range(nP)]
    ssum = [jnp.sum(s[p], axis=-1, keepdims=True) for p in range(nP)]
    kfac = [jnp.exp(b_last[p] - b_col[p] + i_col[p] - m_new[p]) for p in range(nP)]
    kw = [kf[p] * kfac[p] for p in range(nP)]
    kwt = [_tr(kw[p]).astype(BF16) for p in range(nP)]
    kv = [_dot(kwt[p], vb[p]) for p in range(nP)]
    ksum = [jnp.sum(kw[p], axis=0, keepdims=True) for p in range(nP)]
    carry = [jnp.exp(b_last[p] + m_prev[p] - m_new[p]) for p in range(nP)]
    dec = [jnp.exp(m_inter[p] - m_t[p]) for p in range(nP)]
    floor_ = [jnp.exp(-m_t[p]) for p in range(nP)]
    c_prev, n_prev = [None] * nP, [None] * nP
    for h in range(NH_A):
        n_run = n_s[h:h + 1, :]
        c_run = c_s[h]
        for ci in range(len(chunks)):
            p = ci * NH_A + h
            c_prev[p], n_prev[p] = c_run, n_run
            c_run = carry[p] * c_run + kv[p]
            n_run = carry[p] * n_run + ksum[p]
        c_s[h] = c_run
        n_s[h:h + 1, :] = n_run
    qc = [_dot(qb[p], c_prev[p].astype(BF16)) for p in range(nP)]
    qn = [jnp.sum(qb[p].astype(F32) * n_prev[p].astype(BF16).astype(F32), axis=-1, keepdims=True) for p in range(nP)]
    num = [dec[p] * qc[p] + sv[p] for p in range(nP)]
    den = [dec[p] * qn[p] + ssum[p] for p in range(nP)]
    hh = [num[p] / jnp.maximum(jnp.abs(den[p]), floor_[p]) for p in range(nP)]
    mu = [jnp.mean(hh[p], axis=-1, keepdims=True) for p in range(nP)]
    dd = [hh[p] - mu[p] for p in range(nP)]
    var = [jnp.mean(dd[p] * dd[p], axis=-1, keepdims=True) for p in range(nP)]
    rs_ = [lax.rsqrt(var[p] + LN_EPS) for p in range(nP)]
    for p, (_, h) in enumerate(P):
        og = proj_ref[rsl[p], E_OA + h * DV_A:E_OA + (h + 1) * DV_A]
        ha = dd[p] * rs_[p] * norma_ref[h:h + 1, :] * _sigmoid(og)
        mix_ref[rsl[p], h * DV_A:(h + 1) * DV_A] = ha.astype(mix_ref.dtype)

    cout_ref[...] = c_s[...]
    nout_ref[...] = n_s[...]
    mout_ref[...] = m_s[...]

    outs = [_dot(pbs[hd], (v_lo if hd % 2 == 0 else v_hi)[hd // G_B]) for hd in range(NH_B)]
    for j in range(npairs):
        mix_ref[:, W_A + j * LANES:W_A + (j + 1) * LANES] = (outs[2 * j] + outs[2 * j + 1]).astype(mix_ref.dtype)

    kh_s[...] = k_win[R:, :]
    vh_s[...] = v_win[R:, :]
    kout_ref[...] = kh_s[...]
    vout_ref[...] = vh_s[...]


def _rel_bucket(rel):
    nb = NUM_BUCKETS // 2
    max_exact = nb // 2
    n = jnp.abs(rel)
    nf = jnp.maximum(n, 1).astype(F32)
    large = max_exact + (jnp.log(nf / max_exact) / math.log(MAX_DISTANCE / max_exact)
                         * (nb - max_exact)).astype(I32)
    large = jnp.minimum(large, nb - 1)
    return jnp.where(rel > 0, nb, 0) + jnp.where(n < max_exact, n, large)


def _swa_bias_table(rel_bias, rows):
    kw = WINDOW + rows
    qi = jnp.arange(rows)[:, None]
    kj = jnp.arange(kw)[None, :]
    bucket = _rel_bucket(kj - WINDOW - qi)
    rb = rel_bias.astype(F32)
    bias = jnp.zeros((NH_B, rows, kw), F32)
    for b in range(NUM_BUCKETS):
        bias = jnp.where((bucket == b)[None], rb[b][:, None, None], bias)
    lo = (qi // CHUNK) * CHUNK
    ok = (kj >= lo) & (kj < lo + WINDOW + CHUNK)
    return jnp.where(ok[None], bias, -jnp.inf)


def _even_mixer(proj, k_hist, v_hist, c0, n0, m0, bias_tab, norm_a, sink, batch, seq_len, rows, hist_valid):
    nsteps = seq_len // rows
    kw = WINDOW + rows
    m0p = jnp.broadcast_to(jnp.pad(m0, ((0, 0), (0, 8 - NH_A)))[:, :, None], (batch, 8, LANES))
    n0p = jnp.pad(n0, ((0, 0), (0, 8 - NH_A), (0, 0)))
    full3 = lambda b, i: (b, 0, 0)
    outs = pl.pallas_call(
        functools.partial(_even_kernel, rows=rows, hist_valid=hist_valid),
        out_shape=(
            jax.ShapeDtypeStruct((batch * seq_len, D_MODEL), BF16),
            jax.ShapeDtypeStruct((batch, NH_A, DK_A, DV_A), F32),
            jax.ShapeDtypeStruct((batch, 8, DK_A), F32),
            jax.ShapeDtypeStruct((batch, 8, LANES), F32),
            jax.ShapeDtypeStruct((batch, WINDOW, NKV_B * DH_B), F32),
            jax.ShapeDtypeStruct((batch, WINDOW, NKV_B * DH_B), F32),
        ),
        grid=(batch, nsteps),
        in_specs=[
            pl.BlockSpec((rows, N_EVEN_COLS), lambda b, i: (b * nsteps + i, 0)),
            pl.BlockSpec((None, WINDOW, NKV_B * DH_B), full3),
            pl.BlockSpec((None, WINDOW, NKV_B * DH_B), full3),
            pl.BlockSpec((None, NH_A, DK_A, DV_A), lambda b, i: (b, 0, 0, 0)),
            pl.BlockSpec((None, 8, DK_A), full3),
            pl.BlockSpec((None, 8, LANES), full3),
            pl.BlockSpec((NH_B, rows, kw), lambda b, i: (0, 0, 0)),
            pl.BlockSpec((NH_A, DV_A), lambda b, i: (0, 0)),
            pl.BlockSpec(memory_space=pltpu.SMEM),
        ],
        out_specs=(
            pl.BlockSpec((rows, D_MODEL), lambda b, i: (b * nsteps + i, 0)),
            pl.BlockSpec((None, NH_A, DK_A, DV_A), lambda b, i: (b, 0, 0, 0)),
            pl.BlockSpec((None, 8, DK_A), full3),
            pl.BlockSpec((None, 8, LANES), full3),
            pl.BlockSpec((None, WINDOW, NKV_B * DH_B), full3),
            pl.BlockSpec((None, WINDOW, NKV_B * DH_B), full3),
        ),
        scratch_shapes=[
            pltpu.VMEM((NH_A, DK_A, DV_A), F32),
            pltpu.VMEM((8, DK_A), F32),
            pltpu.VMEM((8, LANES), F32),
            pltpu.VMEM((WINDOW, NKV_B * DH_B), F32),
            pltpu.VMEM((WINDOW, NKV_B * DH_B), F32),
        ],
        compiler_params=_cparams(("arbitrary", "arbitrary")),
        name="even_mixer",
    )(proj, k_hist.reshape(batch, WINDOW, NKV_B * DH_B), v_hist.reshape(batch, WINDOW, NKV_B * DH_B),
      c0, n0p, m0p, bias_tab, norm_a.reshape(NH_A, DV_A), sink)
    mix, c1, n1, m1, k1, v1 = outs
    return (mix, c1, n1[:, :NH_A, :], m1[:, :NH_A, 0],
            k1.reshape(batch, WINDOW, NKV_B, DH_B), v1.reshape(batch, WINDOW, NKV_B, DH_B))


def _odd_kernel(proj_ref, s0_ref, hp_ref, normc_ref, mix_ref, sout_ref, s_s, *, rows):
    R = rows
    i = pl.program_id(1)

    @pl.when(i == 0)
    def _():
        s_s[...] = s0_ref[...]

    CS = CHUNK
    P2 = 2 * CS
    gcols = proj_ref[:, O_G:O_G + LANES]
    beta_all = _sigmoid(gcols)
    g_all = -jnp.exp(hp_ref[0:1, :]) * _softplus(gcols + hp_ref[1:2, :])
    rr = lax.broadcasted_iota(I32, (R, R), 0)
    cc = lax.broadcasted_iota(I32, (R, R), 1)
    tri_b = jnp.where((rr >= cc) & (rr // CS == cc // CS), 1.0, 0.0).astype(BF16)
    G_all = _dot_exact_lhs(tri_b, g_all)
    G_t = _tr(G_all)
    eG_all = jnp.exp(G_all)
    r2 = lax.broadcasted_iota(I32, (P2, P2), 0)
    c2 = lax.broadcasted_iota(I32, (P2, P2), 1)
    same_head = (r2 // CS) == (c2 // CS)
    incl2 = same_head & (r2 >= c2)
    strict2 = same_head & (r2 > c2)
    eye2 = jnp.where(r2 == c2, 1.0, 0.0)
    lane_lo = lax.broadcasted_iota(I32, (DK_C, P2), 1) < CS
    scale_c = DK_C ** -0.5

    def qkv(rs0, c0):
        return proj_ref[rs0:rs0 + CS, O_QKV + c0:O_QKV + c0 + LANES]

    def stack(a, b):
        return jnp.concatenate([a, b], axis=0)

    def inv_unit_lower_all(a_list):
        ts = [eye2 - a for a in a_list]
        abs_ = [a.astype(BF16) for a in a_list]
        ps = [_dot(ab, ab) for ab in abs_]
        nlev = (CS - 1).bit_length() - 1
        n = len(ts)
        for lvl in range(nlev):
            th = [t.astype(BF16) for t in ts]
            ph = [p.astype(BF16) for p in ps]
            if lvl < nlev - 1:
                lhs = [stack(th[i], ph[i]) for i in range(n)]
                tp = [_dot(lhs[i], ph[i]) for i in range(n)]
                ts = [ts[i] + tp[i][:P2] for i in range(n)]
                ps = [tp[i][P2:] for i in range(n)]
            else:
                tp = [_dot(th[i], ph[i]) for i in range(n)]
                ts = [ts[i] + tp[i] for i in range(n)]
        return ts

    npair = NH_C // 2
    probs = [(c0, pr) for c0 in range(0, R, CS) for pr in range(npair)]
    nP = len(probs)
    rsl = [slice(c0, c0 + CS) for c0, _ in probs]
    hab = [(2 * pr, 2 * pr + 1) for _, pr in probs]

    def per_head(fn):
        return [[fn(p, h) for h in hab[p]] for p in range(nP)]

    def col2(arr, off):
        return [stack(arr[rsl[p], off + hab[p][0]:off + hab[p][0] + 1], arr[rsl[p], off + hab[p][1]:off + hab[p][1] + 1])
                for p in range(nP)]

    kc = per_head(lambda p, h: qkv(probs[p][0], NH_C * DK_C + h * DK_C))
    qc = per_head(lambda p, h: qkv(probs[p][0], h * DK_C))
    vc = per_head(lambda p, h: qkv(probs[p][0], 2 * NH_C * DK_C + h * DV_C))
    k2 = [stack(*kc[p]) for p in range(nP)]
    q2b = [stack(*[x * scale_c for x in qc[p]]).astype(BF16) for p in range(nP)]
    v2 = [stack(*vc[p]) for p in range(nP)]
    k2b = [k2[p].astype(BF16) for p in range(nP)]
    G_col = col2(G_all, NH_C)
    G_row = [jnp.concatenate([G_t[NH_C + hab[p][0]:NH_C + hab[p][0] + 1, rsl[p]],
                              G_t[NH_C + hab[p][1]:NH_C + hab[p][1] + 1, rsl[p]]], axis=1) for p in range(nP)]
    beta = col2(beta_all, 0)
    eg = col2(eG_all, NH_C)
    GL = [[G_all[probs[p][0] + CS - 1:probs[p][0] + CS, NH_C + h:NH_C + h + 1] for h in hab[p]] for p in range(nP)]
    GL2 = [stack(jnp.broadcast_to(GL[p][0], (CS, 1)), jnp.broadcast_to(GL[p][1], (CS, 1))) for p in range(nP)]
    egl = [[jnp.exp(x) for x in GL[p]] for p in range(nP)]
    kk = [_dot(k2b[p], k2b[p], _NT) for p in range(nP)]
    qk = [_dot(q2b[p], k2b[p], _NT) for p in range(nP)]
    dmat = [jnp.exp(jnp.where(incl2, G_col[p] - G_row[p], -jnp.inf)) for p in range(nP)]
    a_mat = [jnp.where(strict2, beta[p] * dmat[p] * kk[p], 0.0) for p in range(nP)]
    attn = [(qk[p] * dmat[p]).astype(BF16) for p in range(nP)]
    rhs = [jnp.concatenate([beta[p] * v2[p], (beta[p] * eg[p]) * k2[p]], axis=1).astype(BF16) for p in range(nP)]
    kdec = [k2[p] * jnp.exp(GL2[p] - G_col[p]) for p in range(nP)]
    kdec_t = [_tr(kdec[p]).astype(BF16) for p in range(nP)]
    zero_t = jnp.zeros((DK_C, P2), BF16)
    kdec_lo = [jnp.where(lane_lo, kdec_t[p], zero_t) for p in range(nP)]
    kdec_hi = [jnp.where(lane_lo, zero_t, kdec_t[p]) for p in range(nP)]
    tinv = [t.astype(BF16) for t in inv_unit_lower_all(a_mat)]
    uw = [_dot(tinv[p], rhs[p]) for p in range(nP)]
    wq_lhs = [[stack(uw[p][j * CS:(j + 1) * CS, DV_C:].astype(BF16), q2b[p][j * CS:(j + 1) * CS]) for j in range(2)]
              for p in range(nP)]

    S_cur = [s_s[h] for h in range(NH_C)]
    for ci, c0 in enumerate(range(0, R, CS)):
        rs = slice(c0, c0 + CS)
        ps_ = list(range(ci * npair, (ci + 1) * npair))
        Sb = [S_cur[h].astype(BF16) for h in range(NH_C)]
        wq = [[_dot(wq_lhs[p][j], Sb[hab[p][j]]) for j in range(2)] for p in ps_]
        db = [(uw[p][:, :DV_C] - stack(wq[i_][0][:CS], wq[i_][1][:CS])).astype(BF16) for i_, p in enumerate(ps_)]
        upd_lo = [_dot(kdec_lo[p], db[i_]) for i_, p in enumerate(ps_)]
        upd_hi = [_dot(kdec_hi[p], db[i_]) for i_, p in enumerate(ps_)]
        for i_, p in enumerate(ps_):
            S_cur[hab[p][0]] = egl[p][0] * S_cur[hab[p][0]] + upd_lo[i_]
            S_cur[hab[p][1]] = egl[p][1] * S_cur[hab[p][1]] + upd_hi[i_]
        ad = [_dot(attn[p], db[i_]) for i_, p in enumerate(ps_)]
        o2 = [eg[p] * stack(wq[i_][0][CS:], wq[i_][1][CS:]) + ad[i_] for i_, p in enumerate(ps_)]
        oh = [o2[i_][j * CS:(j + 1) * CS] for i_ in range(npair) for j in range(2)]
        ms = [jnp.mean(o * o, axis=-1, keepdims=True) for o in oh]
        zs = [proj_ref[rs, O_Z + h * DV_C:O_Z + (h + 1) * DV_C] for h in range(NH_C)]
        zg = [z * _sigmoid(z) for z in zs]
        for h in range(NH_C):
            o = oh[h] * lax.rsqrt(ms[h] + RMS_EPS) * normc_ref[...] * zg[h]
            mix_ref[rs, h * DV_C:(h + 1) * DV_C] = o.astype(mix_ref.dtype)
    for h in range(NH_C):
        s_s[h] = S_cur[h]

    sout_ref[...] = s_s[...]


def _odd_mixer(proj, s0, a_log, dt_bias, norm_c, batch, seq_len, rows):
    nsteps = seq_len // rows
    hp = jnp.zeros((8, LANES), F32)
    hp = hp.at[0, NH_C:2 * NH_C].set(a_log.astype(F32)).at[1, NH_C:2 * NH_C].set(dt_bias.astype(F32))
    outs = pl.pallas_call(
        functools.partial(_odd_kernel, rows=rows),
        out_shape=(
            jax.ShapeDtypeStruct((batch * seq_len, D_MODEL), BF16),
            jax.ShapeDtypeStruct((batch, NH_C, DK_C, DV_C), F32),
        ),
        grid=(batch, nsteps),
        in_specs=[
            pl.BlockSpec((rows, N_ODD_COLS), lambda b, i: (b * nsteps + i, 0)),
            pl.BlockSpec((None, NH_C, DK_C, DV_C), lambda b, i: (b, 0, 0, 0)),
            pl.BlockSpec((8, LANES), lambda b, i: (0, 0)),
            pl.BlockSpec((1, DV_C), lambda b, i: (0, 0)),
        ],
        out_specs=(
            pl.BlockSpec((rows, D_MODEL), lambda b, i: (b * nsteps + i, 0)),
            pl.BlockSpec((None, NH_C, DK_C, DV_C), lambda b, i: (b, 0, 0, 0)),
        ),
        scratch_shapes=[pltpu.VMEM((NH_C, DK_C, DV_C), F32)],
        compiler_params=_cparams(("arbitrary", "arbitrary")),
        name="odd_mixer",
    )(proj, s0, hp, norm_c.reshape(1, DV_C))
    return outs


def _post_kernel(mix_ref, x_ref, gate_ref, sc_ref, sh_ref, wout_ref, lng_ref, lnb_ref, wr_ref, br_ref,
                 x1_ref, h2_ref, ri_ref, rg_ref, seg_ref, cnt_ref, carry_s, *, tm):
    i = pl.program_id(0)

    @pl.when(i == 0)
    def _():
        carry_s[...] = jnp.zeros_like(carry_s)

    ng = max(1, tm // 256)
    tg = tm // ng
    ys = [_dot(mix_ref[q * tg:(q + 1) * tg, :], wout_ref[...]) for q in range(min(2, ng))]
    wr_b = wr_ref[...].astype(BF16)
    lts = []
    for q in range(ng):
        rq = slice(q * tg, (q + 1) * tg)
        x1 = _layer_norm_rows(ALPHA_DN * x_ref[rq, :] + (1.0 + gate_ref[...]) * ys[q], lng_ref[...], lnb_ref[...])
        if q + 2 < ng:
            ys.append(_dot(mix_ref[(q + 2) * tg:(q + 3) * tg, :], wout_ref[...]))
        x1_ref[rq, :] = x1
        h2 = x1 * (1.0 + sc_ref[...]) + sh_ref[...]
        h2b = h2.astype(BF16)
        h2_ref[rq, :] = h2b
        lts.append(_dot(wr_b, h2b, _NT))
    lt = (lts[0] if ng == 1 else jnp.concatenate(lts, axis=1)) + br_ref[:, 0:1]
    e_iota = lax.broadcasted_iota(I32, (N_EXPERTS, tm), 0).astype(F32)
    vals, idxs = [], []
    for _ in range(TOP_K):
        mx = jnp.max(lt, axis=0, keepdims=True)
        idx = jnp.min(jnp.where(lt == mx, e_iota, float(N_EXPERTS)), axis=0, keepdims=True)
        vals.append(mx)
        idxs.append(idx)
        lt = jnp.where(e_iota == idx, -jnp.inf, lt)
    ex = [jnp.exp(v - vals[0]) for v in vals]
    tot = ex[0] + ex[1] + ex[2] + ex[3]
    hot = [jnp.where(e_iota == idx, 1.0, 0.0) for idx in idxs]
    m_all = hot[0] + hot[1] + hot[2] + hot[3]
    m_all_b = m_all.astype(BF16)
    r = lax.broadcasted_iota(I32, (tm, tm), 0)
    c = lax.broadcasted_iota(I32, (tm, tm), 1)
    upper = jnp.where(r < c, 1.0, 0.0).astype(BF16)
    prefix = _dot(m_all_b, upper)
    re = lax.broadcasted_iota(I32, (N_EXPERTS, N_EXPERTS), 0)
    ce = lax.broadcasted_iota(I32, (N_EXPERTS, N_EXPERTS), 1)
    cnt_col = jnp.sum(m_all, axis=1, keepdims=True)
    m8_col = jnp.floor((cnt_col + 7.0) * 0.125)
    lower_e = jnp.where(re > ce, 1.0, 0.0).astype(BF16)
    off8_col = 8.0 * _dot(lower_e, jnp.broadcast_to(m8_col, (N_EXPERTS, LANES)).astype(BF16))[:, 0:1]
    base = off8_col + prefix
    dests = [jnp.sum(hk * base, axis=0, keepdims=True) for hk in hot]
    cnt_row = _dot(jnp.ones((8, tm), BF16), m_all_b, _NT)
    m8_row = jnp.floor((cnt_row + 7.0) * 0.125)
    upper_e = jnp.where(re < ce, 1.0, 0.0).astype(BF16)
    off8_row = 8.0 * _dot(m8_row.astype(BF16), upper_e)
    gc = carry_s[:, 0:N_EXPERTS]
    srow = lax.broadcasted_iota(I32, (8, N_EXPERTS), 0)
    seg = jnp.where(srow == 0, 8.0 * m8_row, jnp.where(srow == 1, off8_row, jnp.where(srow == 2, gc, 0.0)))
    seg_ref[...] = jnp.concatenate([seg, jnp.zeros((8, LANES - N_EXPERTS), F32)], axis=1).astype(I32)
    carry_s[:, 0:N_EXPERTS] = gc + 8.0 * m8_row
    cnt_ref[...] = carry_s[...]
    ri_ref[...] = jnp.concatenate(dests + [jnp.zeros((4, tm), F32)], axis=0).astype(I32)
    rg_ref[...] = jnp.concatenate([e / tot for e in ex] + [jnp.zeros((4, tm), F32)], axis=0)


def _post(mix, x2d, gate, scale, shift, w_out_bf16, ln_g, ln_b, wr_t, br, seq_len, tm):
    t = x2d.shape[0]
    per = seq_len // tm
    vec = lambda i: (i // per, 0, 0)
    const2 = lambda i: (0, 0)
    return pl.pallas_call(
        functools.partial(_post_kernel, tm=tm),
        out_shape=(
            jax.ShapeDtypeStruct((t, D_MODEL), F32),
            jax.ShapeDtypeStruct((t, D_MODEL), BF16),
            jax.ShapeDtypeStruct((t // tm, 8, tm), I32),
            jax.ShapeDtypeStruct((t // tm, 8, tm), F32),
            jax.ShapeDtypeStruct((t // tm, 8, LANES), I32),
            jax.ShapeDtypeStruct((8, LANES), F32),
        ),
        grid=(t // tm,),
        in_specs=[
            pl.BlockSpec((tm, D_MODEL), lambda i: (i, 0)),
            pl.BlockSpec((tm, D_MODEL), lambda i: (i, 0)),
            pl.BlockSpec((None, 1, D_MODEL), vec),
            pl.BlockSpec((None, 1, D_MODEL), vec),
            pl.BlockSpec((None, 1, D_MODEL), vec),
            pl.BlockSpec((D_MODEL, D_MODEL), const2),
            pl.BlockSpec((1, D_MODEL), const2),
            pl.BlockSpec((1, D_MODEL), const2),
            pl.BlockSpec((N_EXPERTS, D_MODEL), const2),
            pl.BlockSpec((N_EXPERTS, LANES), const2),
        ],
        out_specs=(
            pl.BlockSpec((tm, D_MODEL), lambda i: (i, 0)),
            pl.BlockSpec((tm, D_MODEL), lambda i: (i, 0)),
            pl.BlockSpec((None, 8, tm), lambda i: (i, 0, 0)),
            pl.BlockSpec((None, 8, tm), lambda i: (i, 0, 0)),
            pl.BlockSpec((None, 8, LANES), lambda i: (i, 0, 0)),
            pl.BlockSpec((8, LANES), const2),
        ),
        scratch_shapes=[pltpu.VMEM((8, LANES), F32)],
        compiler_params=_cparams(("arbitrary",)),
        name="post_router",
    )(mix, x2d, gate, scale, shift, w_out_bf16, ln_g, ln_b, wr_t, br)


SEG_FIELDS = 3


def _run_sizes(tm):
    top = 1 << (tm - 1).bit_length()
    return [s for s in (512, 256, 128, 64, 32, 16, 8) if s <= max(top, 8)]


def _for_each_run_piece(seg_ref, rstart_ref, tile, tm, fn):
    base = tile * (SEG_FIELDS * N_EXPERTS)

    def body(e, c):
        n8 = seg_ref[base + e]
        src = seg_ref[base + N_EXPERTS + e]
        dst = rstart_ref[e] + seg_ref[base + 2 * N_EXPERTS + e]
        def pieces(sizes, done):
            for size in sizes:
                @pl.when((n8 & size) != 0)
                def _(done=done, size=size):
                    fn(pl.multiple_of(src + done, 8), pl.multiple_of(dst + done, 8), size)
                done = done + (n8 & size)

        split = 128
        big = [s for s in _run_sizes(tm) if s >= split]
        if big:
            @pl.when(n8 >= split)
            def _():
                pieces(big, jnp.int32(0))
        pieces([s for s in _run_sizes(tm) if s < split], n8 & ~jnp.int32(split - 1))
        return c

    lax.fori_loop(0, N_EXPERTS, body, 0)


def _wait_tile_rows(seg_ref, tile, cb, wait_rows):
    base = tile * (SEG_FIELDS * N_EXPERTS)
    last = N_EXPERTS - 1
    total = seg_ref[base + last] + seg_ref[base + N_EXPERTS + last]
    size = 1 << (cb.bit_length() - 1)
    while size >= 8:
        @pl.when((total & size) != 0)
        def _(size=size):
            wait_rows(size)
        size //= 2


def _dispatch_kernel(zblk_ref, rstart_ref, seg_ref, ri_ref, h_ref, xs_ref, zero_s, cbuf, sem, zsem,
                     *, tm, bm, cb, ntiles):
    i = pl.program_id(0)
    slot = i % 2

    def run_copies(tile, sl, wait):
        if wait:
            def wait_rows(size):
                pltpu.make_async_copy(cbuf.at[sl, pl.ds(0, size), :], xs_ref.at[pl.ds(0, size), :], sem.at[sl]).wait()
            _wait_tile_rows(seg_ref, tile, cb, wait_rows)
            return

        def piece(src, dst, size):
            pltpu.make_async_copy(cbuf.at[sl, pl.ds(src, size), :], xs_ref.at[pl.ds(dst, size), :], sem.at[sl]).start()
        _for_each_run_piece(seg_ref, rstart_ref, tile, tm, piece)

    def zero_copy(e):
        return pltpu.make_async_copy(zero_s, xs_ref.at[pl.ds(zblk_ref[e] * bm, bm), :], zsem)

    @pl.when(i == 0)
    def _():
        zero_s[...] = jnp.zeros_like(zero_s)

        def zstart(e, c):
            @pl.when(zblk_ref[e] >= 0)
            def _():
                zero_copy(e).start()
            return c

        def zwait(e, c):
            @pl.when(zblk_ref[e] >= 0)
            def _():
                zero_copy(e).wait()
            return c

        lax.fori_loop(0, N_EXPERTS, zstart, 0)
        lax.fori_loop(0, N_EXPERTS, zwait, 0)

    @pl.when(i >= 2)
    def _():
        run_copies(i - 2, slot, True)

    rows = lax.broadcasted_iota(I32, (cb, tm), 0)
    hit = rows == ri_ref[0:1, :]
    for k in range(1, TOP_K):
        hit = hit | (rows == ri_ref[k:k + 1, :])
    perm = jnp.where(hit, 1.0, 0.0).astype(BF16)
    cbuf[slot] = _dot(perm, h_ref[...])
    run_copies(i, slot, False)

    @pl.when(i == ntiles - 1)
    def _():
        if ntiles > 1:
            run_copies(i - 1, 1 - slot, True)
        run_copies(i, slot, True)


def _dispatch(h2, route_i, seg_flat, row_start, zblk, n_rows, tm, bm):
    t = h2.shape[0]
    ntiles = t // tm
    cb = TOP_K * tm + 8 * N_EXPERTS
    return pl.pallas_call(
        functools.partial(_dispatch_kernel, tm=tm, bm=bm, cb=cb, ntiles=ntiles),
        out_shape=jax.ShapeDtypeStruct((n_rows, D_MODEL), F32),
        grid_spec=pltpu.PrefetchScalarGridSpec(
            num_scalar_prefetch=3,
            grid=(ntiles,),
            in_specs=[
                pl.BlockSpec((None, 2 * TOP_K, tm), lambda i, z, r, s: (i, 0, 0)),
                pl.BlockSpec((tm, D_MODEL), lambda i, z, r, s: (i, 0)),
            ],
            out_specs=pl.BlockSpec(memory_space=pl.ANY),
            scratch_shapes=[
                pltpu.VMEM((bm, D_MODEL), F32),
                pltpu.VMEM((2, cb, D_MODEL), F32),
                pltpu.SemaphoreType.DMA((2,)),
                pltpu.SemaphoreType.DMA,
            ],
        ),
        compiler_params=_cparams(("arbitrary",)),
        name="dispatch",
    )(zblk, row_start, seg_flat, route_i, h2)


def _ffn_kernel(be_ref, grp_ref, ia_ref, ib_ref, nv_ref, xa_ref, xb_ref, w1_ref, b1_ref, w2_ref, b2_ref,
                ya_ref, yb_ref, w1_s, w2_s):
    b = pl.program_id(0)

    def expert_block(x_ref, y_ref):
        xb = x_ref[...].astype(BF16)
        glu = _dot(xb, w1_s[:, :D_FF]) + b1_ref[:, :D_FF]
        lin = _dot(xb, w1_s[:, D_FF:]) + b1_ref[:, D_FF:]
        glu = jnp.minimum(glu, SWIGLU_LIMIT)
        lin = jnp.clip(lin, -SWIGLU_LIMIT, SWIGLU_LIMIT)
        act = glu * _sigmoid(SWIGLU_ALPHA * glu) * (lin + 1.0)
        y_ref[...] = _dot(act.astype(BF16), w2_s[...]) + b2_ref[...]

    @pl.when(b < nv_ref[0])
    def _():
        @pl.when((b == 0) | (be_ref[b] != be_ref[jnp.maximum(b - 1, 0)]))
        def _():
            step = 256
            for r0 in range(0, D_MODEL, step):
                w1_s[r0:r0 + step, :] = w1_ref[r0:r0 + step, :].astype(BF16)
            for r0 in range(0, D_FF, step):
                w2_s[r0:r0 + step, :] = w2_ref[r0:r0 + step, :].astype(BF16)

        @pl.when(grp_ref[b] == 0)
        def _():
            expert_block(xa_ref, ya_ref)

        @pl.when(grp_ref[b] == 1)
        def _():
            expert_block(xb_ref, yb_ref)


def _ffn(xs_a, xs_b, steps, layer, w1, b1, w2, b2, bm_a, bm_b):
    step_e, step_g, step_ia, step_ib, n_valid = steps
    nsteps = step_e.shape[0]
    row_a = lambda b, be, g, ia, ib, nv: (ia[b], 0)
    row_b = lambda b, be, g, ia, ib, nv: (ib[b], 0)
    wsel = lambda b, be, g, ia, ib, nv: (layer, be[b], 0, 0)
    return pl.pallas_call(
        _ffn_kernel,
        out_shape=(jax.ShapeDtypeStruct(xs_a.shape, F32), jax.ShapeDtypeStruct(xs_b.shape, F32)),
        grid_spec=pltpu.PrefetchScalarGridSpec(
            num_scalar_prefetch=5,
            grid=(nsteps,),
            in_specs=[
                pl.BlockSpec((bm_a, D_MODEL), row_a),
                pl.BlockSpec((bm_b, D_MODEL), row_b),
                pl.BlockSpec((None, None, D_MODEL, 2 * D_FF), wsel),
                pl.BlockSpec((None, None, 1, 2 * D_FF), wsel),
                pl.BlockSpec((None, None, D_FF, D_MODEL), wsel),
                pl.BlockSpec((None, None, 1, D_MODEL), wsel),
            ],
            out_specs=(pl.BlockSpec((bm_a, D_MODEL), row_a), pl.BlockSpec((bm_b, D_MODEL), row_b)),
            scratch_shapes=[pltpu.VMEM((D_MODEL, 2 * D_FF), BF16), pltpu.VMEM((D_FF, D_MODEL), BF16)],
        ),
        compiler_params=_cparams(("arbitrary",)),
        name="expert_ffn",
    )(step_e, step_g, step_ia, step_ib, n_valid, xs_a, xs_b, w1, b1, w2, b2)


def _combine_kernel(rstart_ref, seg_ref, ri_ref, x_ref, gate_ref, rg_ref, lng_ref, lnb_ref, ys_ref, o_ref,
                    ybuf, yb_s, wt_s, sem, *, tm, cb, ntiles):
    i = pl.program_id(0)
    slot = i % 2

    def run_copies(tile, sl, wait):
        if wait:
            def wait_rows(size):
                pltpu.make_async_copy(ys_ref.at[pl.ds(0, size), :], ybuf.at[sl, pl.ds(0, size), :], sem.at[sl]).wait()
            _wait_tile_rows(seg_ref, tile, cb, wait_rows)
            return

        def piece(loc, glob, size):
            pltpu.make_async_copy(ys_ref.at[pl.ds(glob, size), :], ybuf.at[sl, pl.ds(loc, size), :], sem.at[sl]).start()
        _for_each_run_piece(seg_ref, rstart_ref, tile, tm, piece)

    @pl.when(i == 0)
    def _():
        ybuf[...] = jnp.zeros_like(ybuf)
        run_copies(0, 0, False)

    @pl.when(i + 1 < ntiles)
    def _():
        run_copies(i + 1, 1 - slot, False)

    dest_t = _tr(jnp.concatenate([ri_ref[...].astype(F32), jnp.zeros((LANES - 8, tm), F32)], axis=0))
    g_t = _tr(jnp.concatenate([rg_ref[...], jnp.zeros((LANES - 8, tm), F32)], axis=0))
    lane_step = 256
    nq = max(1, tm // 256)
    tq = tm // nq

    def build(q):
        rq = slice(q * tq, (q + 1) * tq)
        for c0 in range(0, cb, lane_step):
            cols = (lax.broadcasted_iota(I32, (tq, lane_step), 1) + c0).astype(F32)
            w = jnp.where(cols == dest_t[rq, 0:1], g_t[rq, 0:1], 0.0)
            for k in range(1, TOP_K):
                w = w + jnp.where(cols == dest_t[rq, k:k + 1], g_t[rq, k:k + 1], 0.0)
            wt_s[rq, c0:c0 + lane_step] = w.astype(BF16)

    def product(q):
        rq = slice(q * tq, (q + 1) * tq)
        return _dot(wt_s[rq, :], yb_s[...])

    def finish(q, moe):
        rq = slice(q * tq, (q + 1) * tq)
        o_ref[rq, :] = _layer_norm_rows(ALPHA_DN * x_ref[rq, :] + (1.0 + gate_ref[...]) * moe,
                                        lng_ref[...], lnb_ref[...])

    build(0)
    run_copies(i, slot, True)
    row_step = 256
    for r0 in range(0, cb, row_step):
        yb_s[r0:r0 + row_step, :] = ybuf[slot, r0:r0 + row_step, :].astype(BF16)
    moes = {}
    for q in range(nq):
        if q + 1 < nq:
            build(q + 1)
        moes[q] = product(q)
        if q >= 1:
            finish(q - 1, moes.pop(q - 1))
    finish(nq - 1, moes.pop(nq - 1))


def _combine(row_start, seg_flat, route_i, x1, gate, route_g, ln_g, ln_b, ys, seq_len, tm):
    t = x1.shape[0]
    per = seq_len // tm
    ntiles = t // tm
    cb = TOP_K * tm + 8 * N_EXPERTS
    return pl.pallas_call(
        functools.partial(_combine_kernel, tm=tm, cb=cb, ntiles=ntiles),
        out_shape=jax.ShapeDtypeStruct((t, D_MODEL), F32),
        grid_spec=pltpu.PrefetchScalarGridSpec(
            num_scalar_prefetch=2,
            grid=(ntiles,),
            in_specs=[
                pl.BlockSpec((None, 2 * TOP_K, tm), lambda i, r, s: (i, 0, 0)),
                pl.BlockSpec((tm, D_MODEL), lambda i, r, s: (i, 0)),
                pl.BlockSpec((None, 1, D_MODEL), lambda i, r, s: (i // per, 0, 0)),
                pl.BlockSpec((None, 8, tm), lambda i, r, s: (i, 0, 0)),
                pl.BlockSpec((1, D_MODEL), lambda i, r, s: (0, 0)),
                pl.BlockSpec((1, D_MODEL), lambda i, r, s: (0, 0)),
                pl.BlockSpec(memory_space=pl.ANY),
            ],
            out_specs=pl.BlockSpec((tm, D_MODEL), lambda i, r, s: (i, 0)),
            scratch_shapes=[
                pltpu.VMEM((2, cb, D_MODEL), F32),
                pltpu.VMEM((cb, D_MODEL), BF16),
                pltpu.VMEM((tm, cb), BF16),
                pltpu.SemaphoreType.DMA((2,)),
            ],
        ),
        compiler_params=_cparams(("arbitrary",)),
        name="combine",
    )(row_start, seg_flat, route_i, x1, gate, route_g, ln_g, ln_b, ys)


def _max_blocks(t, tm, bm):
    return -(-(t * TOP_K + 7 * N_EXPERTS * (t // tm)) // bm) + N_EXPERTS


def _group_tables(counts, bm):
    cnt = counts[0, :N_EXPERTS].astype(I32)
    nblk = (cnt + bm - 1) // bm
    blk_end = jnp.cumsum(nblk)
    row_start = ((blk_end - nblk) * bm).astype(I32)
    zblk = jnp.where((cnt % bm) != 0, blk_end - 1, -1).astype(I32)
    return nblk, row_start, zblk


def _ffn_steps(nblk_a, nblk_b, nsteps):
    per_e = nblk_a + nblk_b
    ends = jnp.cumsum(per_e)
    n_valid = ends[-1]
    s = jnp.minimum(jnp.arange(nsteps, dtype=I32), n_valid - 1)
    step_e = jnp.minimum(jnp.sum((ends[None, :] <= s[:, None]).astype(I32), axis=1), N_EXPERTS - 1)
    onehot = (step_e[:, None] == jnp.arange(N_EXPERTS, dtype=I32)[None, :]).astype(I32)
    off = s - jnp.sum(onehot * (ends - per_e)[None, :], axis=1)
    in_b = off >= jnp.sum(onehot * nblk_a[None, :], axis=1)
    live = jnp.arange(nsteps, dtype=I32) < n_valid
    step_g = jnp.where(live, in_b.astype(I32), 2)
    step_ia = jnp.maximum(jnp.cumsum((step_g == 0).astype(I32)) - 1, 0)
    step_ib = jnp.maximum(jnp.cumsum((step_g == 1).astype(I32)) - 1, 0)
    return (step_e.astype(I32), step_g.astype(I32), step_ia.astype(I32), step_ib.astype(I32),
            n_valid.reshape(1).astype(I32))


def _group_cfg(batch, seq_len):
    if seq_len >= 512:
        return dict(tm=512, rows=256, bm=512)
    return dict(tm=seq_len, rows=seq_len, bm=128)


def kernel(x_prompt, x_sample, c_prompt, c_sample, state_a_C, state_a_n, state_a_m, cache_b_k, cache_b_v,
           state_c_S, state_c_conv, w_ada, b_ada, ln_g, ln_b, w_in_even, b_in_even, norm_a, sink_b, rel_bias,
           w_out_even, w_in_odd, conv_c, a_log_c, dt_bias_c, norm_c, w_out_odd, w_router, b_router,
           w_e1, b_e1, w_e2, b_e2):
    bp, lp, _ = x_prompt.shape
    bs, ls, _ = x_sample.shape
    groups = [dict(b=bp, l=lp, x=x_prompt.reshape(bp * lp, D_MODEL), **_group_cfg(bp, lp)),
              dict(b=bs, l=ls, x=x_sample.reshape(bs * ls, D_MODEL), **_group_cfg(bs, ls))]
    mod = _ada(jnp.concatenate([c_prompt, c_sample], axis=0), w_ada, b_ada)
    offs = [0, bp]
    states = [dict(), dict()]
    b1_all = b_e1.reshape(DEPTH, N_EXPERTS, 1, 2 * D_FF)
    b2_all = b_e2.reshape(DEPTH, N_EXPERTS, 1, D_MODEL)
    for l in range(DEPTH):
        e = l // 2
        if l % 2 == 0:
            w = w_in_even[e]
            sz = (512, 512, 512, 512, 4, 4, 512, 128, 128)
            o = [sum(sz[:j]) for j in range(len(sz) + 1)]
            pad = jnp.zeros((D_MODEL, LANES - 2 * NH_A), w.dtype)
            w_in = jnp.concatenate([w[:, o[0]:o[4]], w[:, o[6]:o[9]], w[:, o[4]:o[6]], pad], axis=1).astype(BF16)
            bb = b_in_even[e]
            b_in = jnp.concatenate([bb[o[0]:o[4]], bb[o[6]:o[9]], bb[o[4]:o[6]],
                                    jnp.zeros((LANES - 2 * NH_A,), bb.dtype)]).reshape(1, N_EVEN_COLS)
            w_out = w_out_even[e].astype(BF16)
        else:
            w = w_in_odd[e]
            pad = jnp.zeros((D_MODEL, LANES - 2 * NH_C), w.dtype)
            w_in = jnp.concatenate([w, pad], axis=1).astype(BF16)
            w_out = w_out_odd[e].astype(BF16)
        wr_t = w_router[l].T
        br = jnp.broadcast_to(b_router[l][:, None], (N_EXPERTS, LANES))
        moe_in = []
        for gi, gr in enumerate(groups):
            nb, sl = gr['b'], gr['l']
            m = mod[l, offs[gi]:offs[gi] + nb].reshape(nb, 6, 1, D_MODEL)
            m6 = [m[:, j] for j in range(6)]
            st = states[gi]
            if l % 2 == 0:
                proj = _inproj(gr['x'], m6[1], m6[0], w_in, b_in, sl, gr['tm'])
                if gi == 0:
                    kh = jnp.zeros((nb, WINDOW, NKV_B, DH_B), F32)
                    vh = kh
                    c0 = jnp.zeros((nb, NH_A, DK_A, DV_A), F32)
                    n0 = jnp.zeros((nb, NH_A, DK_A), F32)
                    m0 = jnp.zeros((nb, NH_A), F32)
                else:
                    kh, vh, c0, n0, m0 = cache_b_k[e], cache_b_v[e], state_a_C[e], state_a_n[e], state_a_m[e]
                bias_tab = _swa_bias_table(rel_bias, gr['rows'])
                mix, c1, n1, m1, k1, v1 = _even_mixer(proj, kh, vh, c0, n0, m0, bias_tab, norm_a[e], sink_b[e],
                                                      nb, sl, gr['rows'], gi == 1)
                for name, val in (('a_C', c1), ('a_n', n1), ('a_m', m1), ('b_k', k1), ('b_v', v1)):
                    st.setdefault(name, []).append(val)
            else:
                if gi == 0:
                    cv0 = jnp.zeros((nb, CONV_W - 1, QKV_C), F32)
                    s0 = jnp.zeros((nb, NH_C, DK_C, DV_C), F32)
                else:
                    cv0, s0 = state_c_conv[e], state_c_S[e]
                proj, cv1 = _inproj_conv(gr['x'], m6[1], m6[0], w_in, cv0, conv_c[e], sl, gr['tm'])
                mix, s1 = _odd_mixer(proj, s0, a_log_c[e], dt_bias_c[e], norm_c[e], nb, sl, gr['rows'])
                st.setdefault('c_S', []).append(s1)
                st.setdefault('c_conv', []).append(cv1)
            x1, h2, route_i, route_g, seg, counts = _post(
                mix, gr['x'], m6[2], m6[4], m6[3], w_out, ln_g[l, 0].reshape(1, D_MODEL),
                ln_b[l, 0].reshape(1, D_MODEL), wr_t, br, sl, gr['tm'])
            t, tm, bm = nb * sl, gr['tm'], gr['bm']
            ntiles = t // tm
            nblk, row_start, zblk = _group_tables(counts, bm)
            seg_flat = seg[:, :SEG_FIELDS, :N_EXPERTS].reshape(ntiles * SEG_FIELDS * N_EXPERTS)
            xs = _dispatch(h2, route_i, seg_flat, row_start, zblk, _max_blocks(t, tm, bm) * bm, tm, bm)
            moe_in.append(dict(xs=xs, nblk=nblk, row_start=row_start, seg_flat=seg_flat, route_i=route_i,
                               route_g=route_g, x1=x1, gate=m6[5]))
        ga, gb = moe_in
        steps = _ffn_steps(ga['nblk'], gb['nblk'], sum(_max_blocks(g['b'] * g['l'], g['tm'], g['bm']) for g in groups))
        ys = _ffn(ga['xs'], gb['xs'], steps, l, w_e1, b1_all, w_e2, b2_all, groups[0]['bm'], groups[1]['bm'])
        for gi, gr in enumerate(groups):
            mi = moe_in[gi]
            gr['x'] = _combine(mi['row_start'], mi['seg_flat'], mi['route_i'], mi['x1'], mi['gate'], mi['route_g'],
                               ln_g[l, 1].reshape(1, D_MODEL), ln_b[l, 1].reshape(1, D_MODEL), ys[gi],
                               gr['l'], gr['tm'])
    outs = [groups[0]['x'].reshape(bp, lp, D_MODEL), groups[1]['x'].reshape(bs, ls, D_MODEL)]
    for gi in range(2):
        for name in ('a_C', 'a_n', 'a_m', 'b_k', 'b_v', 'c_S', 'c_conv'):
            outs.append(jnp.stack(states[gi][name]))
    return tuple(outs)
```

```python
import functools
import math

import jax
import jax.numpy as jnp
from jax import lax
from jax.experimental import pallas as pl
from jax.experimental.pallas import tpu as pltpu

F32 = jnp.float32
BF16 = jnp.bfloat16
I32 = jnp.int32

D_MODEL = 1024
CHUNK = 64
NH_A, DK_A, DV_A = 4, 128, 128
W_A = NH_A * DV_A
NH_B, NKV_B, DH_B, WINDOW = 8, 2, 64, 128
G_B = NH_B // NKV_B
W_B = NH_B * DH_B
NUM_BUCKETS, MAX_DISTANCE = 32, 256
NH_C, DK_C, DV_C, CONV_W = 8, 128, 128, 4
QKV_C = NH_C * (2 * DK_C + DV_C)
N_EXPERTS, TOP_K, D_FF = 32, 4, 1024
SWIGLU_LIMIT, SWIGLU_ALPHA = 7.0, 1.702
DEPTH = 2
ALPHA_DN = (2 * DEPTH) ** 0.25
LN_EPS = 1e-5
RMS_EPS = 1e-6

LANES = 128
E_QA, E_KA, E_VA, E_OA, E_QB, E_KB, E_VB, E_G = 0, 512, 1024, 1536, 2048, 2560, 2688, 2816
N_EVEN_COLS = 2944
O_QKV, O_Z, O_G = 0, 3072, 4096
N_ODD_COLS = 4224

VMEM_LIMIT = 56 * 1024 * 1024


def _cparams(sem):
    return pltpu.CompilerParams(dimension_semantics=sem, vmem_limit_bytes=VMEM_LIMIT)


def _softplus(x):
    return jnp.maximum(x, 0.0) + jnp.log(1.0 + jnp.exp(-jnp.abs(x)))


def _sigmoid(x):
    return 1.0 / (1.0 + jnp.exp(-x))


_NN = (((1,), (0,)), ((), ()))
_NT = (((1,), (1,)), ((), ()))


def _dot(a, b, dims=_NN):
    return lax.dot_general(a, b, dims, preferred_element_type=F32)


def _dot_exact_lhs(a_bf16, b, dims=_NN):
    b0 = b.astype(BF16)
    r1 = b - b0.astype(F32)
    b1 = r1.astype(BF16)
    b2 = (r1 - b1.astype(F32)).astype(BF16)
    return _dot(a_bf16, b0, dims) + _dot(a_bf16, b1, dims) + _dot(a_bf16, b2, dims)


def _tr(x):
    r = x.shape[0]
    rp = -(-r // LANES) * LANES
    if rp != r:
        x = jnp.concatenate([x, jnp.zeros((rp - r, x.shape[1]), x.dtype)], axis=0)
    return x.T[:, :r]


def _tri(n, strict=False):
    r = lax.broadcasted_iota(I32, (n, n), 0)
    c = lax.broadcasted_iota(I32, (n, n), 1)
    return (r > c) if strict else (r >= c)


def _layer_norm_rows(v, g, b):
    mu = jnp.mean(v, axis=-1, keepdims=True)
    d = v - mu
    var = jnp.mean(d * d, axis=-1, keepdims=True)
    return d * lax.rsqrt(var + LN_EPS) * g + b


def _ada_kernel(c_ref, w_ref, b_ref, o_ref):
    c = c_ref[...]
    a = (c * _sigmoid(c)).astype(BF16)
    o_ref[...] = _dot(a, w_ref[...].astype(BF16)) + b_ref[...]


def _ada(c_all, w_ada, b_ada):
    nb = c_all.shape[0]
    tn = 1536
    return pl.pallas_call(
        _ada_kernel,
        out_shape=jax.ShapeDtypeStruct((DEPTH, nb, 6 * D_MODEL), F32),
        grid=(DEPTH, 6 * D_MODEL // tn),
        in_specs=[
            pl.BlockSpec((nb, D_MODEL), lambda l, j: (0, 0)),
            pl.BlockSpec((None, D_MODEL, tn), lambda l, j: (l, 0, j)),
            pl.BlockSpec((None, 1, tn), lambda l, j: (l, 0, j)),
        ],
        out_specs=pl.BlockSpec((None, nb, tn), lambda l, j: (l, 0, j)),
        compiler_params=_cparams(("arbitrary", "arbitrary")),
        name="ada",
    )(c_all, w_ada, b_ada.reshape(DEPTH, 1, 6 * D_MODEL))


def _inproj_kernel(x_ref, sc_ref, sh_ref, w_ref, b_ref, o_ref, *, n_cols, col_step):
    h = (x_ref[...] * (1.0 + sc_ref[...]) + sh_ref[...]).astype(BF16)
    for c0 in range(0, n_cols, col_step):
        c1 = min(c0 + col_step, n_cols)
        o_ref[:, c0:c1] = _dot(h, w_ref[:, c0:c1]) + b_ref[:, c0:c1]


def _inproj(x2d, scale, shift, w_bf16, bias, seq_len, tm):
    t = x2d.shape[0]
    n = w_bf16.shape[1]
    per = seq_len // tm
    return pl.pallas_call(
        functools.partial(_inproj_kernel, n_cols=n, col_step=1024),
        out_shape=jax.ShapeDtypeStruct((t, n), F32),
        grid=(t // tm,),
        in_specs=[
            pl.BlockSpec((tm, D_MODEL), lambda i: (i, 0)),
            pl.BlockSpec((None, 1, D_MODEL), lambda i: (i // per, 0, 0)),
            pl.BlockSpec((None, 1, D_MODEL), lambda i: (i // per, 0, 0)),
            pl.BlockSpec((D_MODEL, n), lambda i: (0, 0)),
            pl.BlockSpec((1, n), lambda i: (0, 0)),
        ],
        out_specs=pl.BlockSpec((tm, n), lambda i: (i, 0)),
        compiler_params=_cparams(("arbitrary",)),
        name="inproj",
    )(x2d, scale, shift, w_bf16, bias)


CONV_HB = 8


def _inproj_conv_kernel(x_ref, sc_ref, sh_ref, w_ref, conv0_ref, convw_ref, o_ref, convout_ref, xa_s,
                        *, tm, per):
    i = pl.program_id(0)
    HB = CONV_HB

    @pl.when(i % per == 0)
    def _():
        xa_s[0:HB - (CONV_W - 1), :] = jnp.zeros((HB - (CONV_W - 1), QKV_C), F32)
        xa_s[HB - (CONV_W - 1):HB, :] = conv0_ref[...]

    h = (x_ref[...] * (1.0 + sc_ref[...]) + sh_ref[...]).astype(BF16)
    step = 512

    def project(c0):
        return _dot(h, w_ref[:, c0:c0 + step])

    def conv_group(g0, pre):
        xa = jnp.concatenate([xa_s[:, g0:g0 + step], pre], axis=0)
        for c in range(0, step, LANES):
            c0 = g0 + c
            xc = xa[:, c:c + LANES]
            y = None
            for j in range(CONV_W):
                back = xc if j == CONV_W - 1 else pltpu.roll(xc, CONV_W - 1 - j, 0)
                term = back[HB:HB + tm, :] * convw_ref[j:j + 1, c0:c0 + LANES]
                y = term if y is None else y + term
            y = y * _sigmoid(y)
            if c0 < 2 * NH_C * DK_C:
                y = y * lax.rsqrt(jnp.sum(y * y, axis=-1, keepdims=True) + RMS_EPS)
            o_ref[:, c0:c0 + LANES] = y
        convout_ref[:, g0:g0 + step] = pre[tm - (CONV_W - 1):tm, :]
        return pre[tm - HB:tm, :]

    groups = list(range(0, QKV_C, step))
    rest = list(range(O_Z, N_ODD_COLS, 256))
    pres = {0: project(groups[0])}
    tails = []
    for gi, g0 in enumerate(groups):
        if gi + 1 < len(groups):
            pres[gi + 1] = project(groups[gi + 1])
        for c0 in rest[gi::len(groups)]:
            c1 = min(c0 + 256, N_ODD_COLS)
            o_ref[:, c0:c1] = _dot(h, w_ref[:, c0:c1])
        tails.append(conv_group(g0, pres.pop(gi)))
    for g0, tail in zip(groups, tails):
        xa_s[:, g0:g0 + step] = tail


def _inproj_conv(x2d, scale, shift, w_bf16, conv_hist, conv_w, seq_len, tm):
    t = x2d.shape[0]
    per = seq_len // tm
    batch = t // seq_len
    return pl.pallas_call(
        functools.partial(_inproj_conv_kernel, tm=tm, per=per),
        out_shape=(jax.ShapeDtypeStruct((t, N_ODD_COLS), F32),
                   jax.ShapeDtypeStruct((batch, CONV_W - 1, QKV_C), F32)),
        grid=(t // tm,),
        in_specs=[
            pl.BlockSpec((tm, D_MODEL), lambda i: (i, 0)),
            pl.BlockSpec((None, 1, D_MODEL), lambda i: (i // per, 0, 0)),
            pl.BlockSpec((None, 1, D_MODEL), lambda i: (i // per, 0, 0)),
            pl.BlockSpec((D_MODEL, N_ODD_COLS), lambda i: (0, 0)),
            pl.BlockSpec((None, CONV_W - 1, QKV_C), lambda i: (i // per, 0, 0)),
            pl.BlockSpec((CONV_W, QKV_C), lambda i: (0, 0)),
        ],
        out_specs=(pl.BlockSpec((tm, N_ODD_COLS), lambda i: (i, 0)),
                   pl.BlockSpec((None, CONV_W - 1, QKV_C), lambda i: (i // per, 0, 0))),
        scratch_shapes=[pltpu.VMEM((CONV_HB, QKV_C), F32)],
        compiler_params=_cparams(("arbitrary",)),
        name="inproj_conv",
    )(x2d, scale, shift, w_bf16, conv_hist, conv_w)


def _even_kernel(proj_ref, kh0_ref, vh0_ref, c0_ref, n0_ref, m0_ref, bias_ref, norma_ref, sink_ref,
                 mix_ref, cout_ref, nout_ref, mout_ref, kout_ref, vout_ref,
                 c_s, n_s, m_s, kh_s, vh_s, *, rows, hist_valid):
    R = rows
    KW = WINDOW + R
    i = pl.program_id(1)

    @pl.when(i == 0)
    def _():
        c_s[...] = c0_ref[...]
        n_s[...] = n0_ref[...]
        m_s[...] = m0_ref[...]
        kh_s[...] = kh0_ref[...]
        vh_s[...] = vh0_ref[...]

    CS = CHUNK
    g = proj_ref[:, E_G:E_G + LANES]
    lf = -_softplus(-g)
    rr = lax.broadcasted_iota(I32, (R, R), 0)
    cc = lax.broadcasted_iota(I32, (R, R), 1)
    tri_b = jnp.where((rr >= cc) & (rr // CS == cc // CS), 1.0, 0.0).astype(BF16)
    b_all = _dot_exact_lhs(tri_b, lf)
    g_t = _tr(g)
    b_t = _tr(b_all)
    causal = _tri(CS)
    scale_a = DK_A ** -0.5
    chunks = list(range(0, R, CS))
    P = [(c0, h) for c0 in chunks for h in range(NH_A)]
    nP = len(P)
    rsl = [slice(c0, c0 + CS) for c0, _ in P]
    b_col = [b_all[rsl[p], NH_A + h:NH_A + h + 1] for p, (_, h) in enumerate(P)]
    b_row = [b_t[NH_A + h:NH_A + h + 1, rsl[p]] for p, (_, h) in enumerate(P)]
    i_row = [g_t[h:h + 1, rsl[p]] for p, (_, h) in enumerate(P)]
    i_col = [g[rsl[p], h:h + 1] for p, (_, h) in enumerate(P)]
    logw = [jnp.where(causal, b_col[p] - b_row[p] + i_row[p], -jnp.inf) for p in range(nP)]
    lmax = [jnp.max(logw[p], axis=-1, keepdims=True) for p in range(nP)]
    qb = [(proj_ref[rsl[p], E_QA + h * DK_A:E_QA + (h + 1) * DK_A] * scale_a).astype(BF16)
          for p, (_, h) in enumerate(P)]
    kf = [proj_ref[rsl[p], E_KA + h * DK_A:E_KA + (h + 1) * DK_A] for p, (_, h) in enumerate(P)]
    vb = [proj_ref[rsl[p], E_VA + h * DV_A:E_VA + (h + 1) * DV_A].astype(BF16) for p, (_, h) in enumerate(P)]
    qk = [_dot(qb[p], kf[p].astype(BF16), _NT) for p in range(nP)]
    b_last = [b_col[p][CS - 1:CS, :] for p in range(nP)]

    k_win = jnp.concatenate([kh_s[...], proj_ref[:, E_KB:E_KB + NKV_B * DH_B]], axis=0)
    v_win = jnp.concatenate([vh_s[...], proj_ref[:, E_VB:E_VB + NKV_B * DH_B]], axis=0)
    k_win_b = k_win.astype(BF16)
    v_win_b = v_win.astype(BF16)
    if not hist_valid:
        key_pos = lax.broadcasted_iota(I32, (R, KW), 1) + (i * R - WINDOW)
        key_ok = key_pos >= 0
    scale_b = DH_B ** -0.5
    lo = lax.broadcasted_iota(I32, (1, LANES), 1) < DH_B
    k_swp = pltpu.roll(k_win, DH_B, 1).astype(BF16)
    v_swp = pltpu.roll(v_win, DH_B, 1).astype(BF16)
    zk = jnp.zeros_like(k_win_b)
    k_both = [jnp.where(lo, k_win_b, k_swp), jnp.where(lo, k_swp, k_win_b)]
    v_lo = [jnp.where(lo, v_win_b, zk), jnp.where(lo, v_swp, zk)]
    v_hi = [jnp.where(lo, zk, v_swp), jnp.where(lo, zk, v_win_b)]
    npairs = NH_B // 2
    qp = [proj_ref[:, E_QB + j * LANES:E_QB + (j + 1) * LANES].astype(BF16) for j in range(npairs)]
    zq = jnp.zeros_like(qp[0])
    qh = [jnp.where(lo, qp[hd // 2], zq) if hd % 2 == 0 else jnp.where(lo, zq, qp[hd // 2]) for hd in range(NH_B)]
    scores = [_dot(qh[hd], k_both[hd // G_B], _NT) * scale_b + bias_ref[hd] for hd in range(NH_B)]
    if not hist_valid:
        scores = [jnp.where(key_ok, s, -jnp.inf) for s in scores]
    sks = [sink_ref[hd] for hd in range(NH_B)]
    mxs = [jnp.maximum(jnp.max(scores[hd], axis=-1, keepdims=True), sks[hd]) for hd in range(NH_B)]
    ps = [jnp.exp(scores[hd] - mxs[hd]) for hd in range(NH_B)]
    dens = [jnp.sum(ps[hd], axis=-1, keepdims=True) + jnp.exp(sks[hd] - mxs[hd]) for hd in range(NH_B)]
    pbs = [(ps[hd] / dens[hd]).astype(BF16) for hd in range(NH_B)]

    m_prev, m_inter, m_t, m_new = [None] * nP, [None] * nP, [None] * nP, [None] * nP
    for h in range(NH_A):
        run = m_s[h:h + 1, 0:1]
        for ci in range(len(chunks)):
            p = ci * NH_A + h
            m_prev[p] = run
            m_inter[p] = b_col[p] + run
            m_t[p] = jnp.maximum(m_inter[p], lmax[p])
            run = m_t[p][CS - 1:CS, :]
            m_new[p] = run
        m_s[h:h + 1, :] = jnp.broadcast_to(run, (1, LANES))
    w = [jnp.exp(logw[p] - m_t[p]) for p in range(nP)]
    s = [qk[p] * w[p] for p in range(nP)]
    sv = [_dot(s[p].astype(BF16), vb[p]) for p in range(nP)]
    ssum = [jnp.sum(s[p], axis=-1, keepdims=True) for p in range(nP)]
    kfac = [jnp.exp(b_last[p] - b_col[p] + i_col[p] - m_new[p]) for p in range(nP)]
    kw = [kf[p] * kfac[p] for p in range(nP)]
    kwt = [_tr(kw[p]).astype(BF16) for p in range(nP)]
    kv = [_dot(kwt[p], vb[p]) for p in range(nP)]
    ksum = [jnp.sum(kw[p], axis=0, keepdims=True) for p in range(nP)]
    carry = [jnp.exp(b_last[p] + m_prev[p] - m_new[p]) for p in range(nP)]
    dec = [jnp.exp(m_inter[p] - m_t[p]) for p in range(nP)]
    floor_ = [jnp.exp(-m_t[p]) for p in range(nP)]
    c_prev, n_prev = [None] * nP, [None] * nP
    for h in range(NH_A):
        n_run = n_s[h:h + 1, :]
        c_run = c_s[h]
        for ci in range(len(chunks)):
            p = ci * NH_A + h
            c_prev[p], n_prev[p] = c_run, n_run
            c_run = carry[p] * c_run + kv[p]
            n_run = carry[p] * n_run + ksum[p]
        c_s[h] = c_run
        n_s[h:h + 1, :] = n_run
    qc = [_dot(qb[p], c_prev[p].astype(BF16)) for p in range(nP)]
    qn = [jnp.sum(qb[p].astype(F32) * n_prev[p].astype(BF16).astype(F32), axis=-1, keepdims=True) for p in range(nP)]
    num = [dec[p] * qc[p] + sv[p] for p in range(nP)]
    den = [dec[p] * qn[p] + ssum[p] for p in range(nP)]
    hh = [num[p] / jnp.maximum(jnp.abs(den[p]), floor_[p]) for p in range(nP)]
    mu = [jnp.mean(hh[p], axis=-1, keepdims=True) for p in range(nP)]
    dd = [hh[p] - mu[p] for p in range(nP)]
    var = [jnp.mean(dd[p] * dd[p], axis=-1, keepdims=True) for p in range(nP)]
    rs_ = [lax.rsqrt(var[p] + LN_EPS) for p in range(nP)]
    for p, (_, h) in enumerate(P):
        og = proj_ref[rsl[p], E_OA + h * DV_A:E_OA + (h + 1) * DV_A]
        ha = dd[p] * rs_[p] * norma_ref[h:h + 1, :] * _sigmoid(og)
        mix_ref[rsl[p], h * DV_A:(h + 1) * DV_A] = ha.astype(mix_ref.dtype)

    cout_ref[...] = c_s[...]
    nout_ref[...] = n_s[...]
    mout_ref[...] = m_s[...]

    outs = [_dot(pbs[hd], (v_lo if hd % 2 == 0 else v_hi)[hd // G_B]) for hd in range(NH_B)]
    for j in range(npairs):
        mix_ref[:, W_A + j * LANES:W_A + (j + 1) * LANES] = (outs[2 * j] + outs[2 * j + 1]).astype(mix_ref.dtype)

    kh_s[...] = k_win[R:, :]
    vh_s[...] = v_win[R:, :]
    kout_ref[...] = kh_s[...]
    vout_ref[...] = vh_s[...]


def _rel_bucket(rel):
    nb = NUM_BUCKETS // 2
    max_exact = nb // 2
    n = jnp.abs(rel)
    nf = jnp.maximum(n, 1).astype(F32)
    large = max_exact + (jnp.log(nf / max_exact) / math.log(MAX_DISTANCE / max_exact)
                         * (nb - max_exact)).astype(I32)
    large = jnp.minimum(large, nb - 1)
    return jnp.where(rel > 0, nb, 0) + jnp.where(n < max_exact, n, large)


def _swa_bias_table(rel_bias, rows):
    kw = WINDOW + rows
    qi = jnp.arange(rows)[:, None]
    kj = jnp.arange(kw)[None, :]
    bucket = _rel_bucket(kj - WINDOW - qi)
    rb = rel_bias.astype(F32)
    bias = jnp.zeros((NH_B, rows, kw), F32)
    for b in range(NUM_BUCKETS):
        bias = jnp.where((bucket == b)[None], rb[b][:, None, None], bias)
    lo = (qi // CHUNK) * CHUNK
    ok = (kj >= lo) & (kj < lo + WINDOW + CHUNK)
    return jnp.where(ok[None], bias, -jnp.inf)


def _even_mixer(proj, k_hist, v_hist, c0, n0, m0, bias_tab, norm_a, sink, batch, seq_len, rows, hist_valid):
    nsteps = seq_len // rows
    kw = WINDOW + rows
    m0p = jnp.broadcast_to(jnp.pad(m0, ((0, 0), (0, 8 - NH_A)))[:, :, None], (batch, 8, LANES))
    n0p = jnp.pad(n0, ((0, 0), (0, 8 - NH_A), (0, 0)))
    full3 = lambda b, i: (b, 0, 0)
    outs = pl.pallas_call(
        functools.partial(_even_kernel, rows=rows, hist_valid=hist_valid),
        out_shape=(
            jax.ShapeDtypeStruct((batch * seq_len, D_MODEL), BF16),
            jax.ShapeDtypeStruct((batch, NH_A, DK_A, DV_A), F32),
            jax.ShapeDtypeStruct((batch, 8, DK_A), F32),
            jax.ShapeDtypeStruct((batch, 8, LANES), F32),
            jax.ShapeDtypeStruct((batch, WINDOW, NKV_B * DH_B), F32),
            jax.ShapeDtypeStruct((batch, WINDOW, NKV_B * DH_B), F32),
        ),
        grid=(batch, nsteps),
        in_specs=[
            pl.BlockSpec((rows, N_EVEN_COLS), lambda b, i: (b * nsteps + i, 0)),
            pl.BlockSpec((None, WINDOW, NKV_B * DH_B), full3),
            pl.BlockSpec((None, WINDOW, NKV_B * DH_B), full3),
            pl.BlockSpec((None, NH_A, DK_A, DV_A), lambda b, i: (b, 0, 0, 0)),
            pl.BlockSpec((None, 8, DK_A), full3),
            pl.BlockSpec((None, 8, LANES), full3),
            pl.BlockSpec((NH_B, rows, kw), lambda b, i: (0, 0, 0)),
            pl.BlockSpec((NH_A, DV_A), lambda b, i: (0, 0)),
            pl.BlockSpec(memory_space=pltpu.SMEM),
        ],
        out_specs=(
            pl.BlockSpec((rows, D_MODEL), lambda b, i: (b * nsteps + i, 0)),
            pl.BlockSpec((None, NH_A, DK_A, DV_A), lambda b, i: (b, 0, 0, 0)),
            pl.BlockSpec((None, 8, DK_A), full3),
            pl.BlockSpec((None, 8, LANES), full3),
            pl.BlockSpec((None, WINDOW, NKV_B * DH_B), full3),
            pl.BlockSpec((None, WINDOW, NKV_B * DH_B), full3),
        ),
        scratch_shapes=[
            pltpu.VMEM((NH_A, DK_A, DV_A), F32),
            pltpu.VMEM((8, DK_A), F32),
            pltpu.VMEM((8, LANES), F32),
            pltpu.VMEM((WINDOW, NKV_B * DH_B), F32),
            pltpu.VMEM((WINDOW, NKV_B * DH_B), F32),
        ],
        compiler_params=_cparams(("arbitrary", "arbitrary")),
        name="even_mixer",
    )(proj, k_hist.reshape(batch, WINDOW, NKV_B * DH_B), v_hist.reshape(batch, WINDOW, NKV_B * DH_B),
      c0, n0p, m0p, bias_tab, norm_a.reshape(NH_A, DV_A), sink)
    mix, c1, n1, m1, k1, v1 = outs
    return (mix, c1, n1[:, :NH_A, :], m1[:, :NH_A, 0],
            k1.reshape(batch, WINDOW, NKV_B, DH_B), v1.reshape(batch, WINDOW, NKV_B, DH_B))


def _odd_kernel(proj_ref, s0_ref, hp_ref, normc_ref, mix_ref, sout_ref, s_s, *, rows):
    R = rows
    i = pl.program_id(1)

    @pl.when(i == 0)
    def _():
        s_s[...] = s0_ref[...]

    CS = CHUNK
    P2 = 2 * CS
    gcols = proj_ref[:, O_G:O_G + LANES]
    beta_all = _sigmoid(gcols)
    g_all = -jnp.exp(hp_ref[0:1, :]) * _softplus(gcols + hp_ref[1:2, :])
    rr = lax.broadcasted_iota(I32, (R, R), 0)
    cc = lax.broadcasted_iota(I32, (R, R), 1)
    tri_b = jnp.where((rr >= cc) & (rr // CS == cc // CS), 1.0, 0.0).astype(BF16)
    G_all = _dot_exact_lhs(tri_b, g_all)
    G_t = _tr(G_all)
    eG_all = jnp.exp(G_all)
    r2 = lax.broadcasted_iota(I32, (P2, P2), 0)
    c2 = lax.broadcasted_iota(I32, (P2, P2), 1)
    same_head = (r2 // CS) == (c2 // CS)
    incl2 = same_head & (r2 >= c2)
    strict2 = same_head & (r2 > c2)
    eye2 = jnp.where(r2 == c2, 1.0, 0.0)
    lane_lo = lax.broadcasted_iota(I32, (DK_C, P2), 1) < CS
    scale_c = DK_C ** -0.5

    def qkv(rs0, c0):
        return proj_ref[rs0:rs0 + CS, O_QKV + c0:O_QKV + c0 + LANES]

    def stack(a, b):
        return jnp.concatenate([a, b], axis=0)

    def inv_unit_lower_all(a_list):
        ts = [eye2 - a for a in a_list]
        abs_ = [a.astype(BF16) for a in a_list]
        ps = [_dot(ab, ab) for ab in abs_]
        nlev = (CS - 1).bit_length() - 1
        n = len(ts)
        for lvl in range(nlev):
            th = [t.astype(BF16) for t in ts]
            ph = [p.astype(BF16) for p in ps]
            if lvl < nlev - 1:
                lhs = [stack(th[i], ph[i]) for i in range(n)]
                tp = [_dot(lhs[i], ph[i]) for i in range(n)]
                ts = [ts[i] + tp[i][:P2] for i in range(n)]
                ps = [tp[i][P2:] for i in range(n)]
            else:
                tp = [_dot(th[i], ph[i]) for i in range(n)]
                ts = [ts[i] + tp[i] for i in range(n)]
        return ts

    npair = NH_C // 2
    probs = [(c0, pr) for c0 in range(0, R, CS) for pr in range(npair)]
    nP = len(probs)
    rsl = [slice(c0, c0 + CS) for c0, _ in probs]
    hab = [(2 * pr, 2 * pr + 1) for _, pr in probs]

    def per_head(fn):
        return [[fn(p, h) for h in hab[p]] for p in range(nP)]

    def col2(arr, off):
        return [stack(arr[rsl[p], off + hab[p][0]:off + hab[p][0] + 1], arr[rsl[p], off + hab[p][1]:off + hab[p][1] + 1])
                for p in range(nP)]

    kc = per_head(lambda p, h: qkv(probs[p][0], NH_C * DK_C + h * DK_C))
    qc = per_head(lambda p, h: qkv(probs[p][0], h * DK_C))
    vc = per_head(lambda p, h: qkv(probs[p][0], 2 * NH_C * DK_C + h * DV_C))
    k2 = [stack(*kc[p]) for p in range(nP)]
    q2b = [stack(*[x * scale_c for x in qc[p]]).astype(BF16) for p in range(nP)]
    v2 = [stack(*vc[p]) for p in range(nP)]
    k2b = [k2[p].astype(BF16) for p in range(nP)]
    G_col = col2(G_all, NH_C)
    G_row = [jnp.concatenate([G_t[NH_C + hab[p][0]:NH_C + hab[p][0] + 1, rsl[p]],
                              G_t[NH_C + hab[p][1]:NH_C + hab[p][1] + 1, rsl[p]]], axis=1) for p in range(nP)]
    beta = col2(beta_all, 0)
    eg = col2(eG_all, NH_C)
    GL = [[G_all[probs[p][0] + CS - 1:probs[p][0] + CS, NH_C + h:NH_C + h + 1] for h in hab[p]] for p in range(nP)]
    GL2 = [stack(jnp.broadcast_to(GL[p][0], (CS, 1)), jnp.broadcast_to(GL[p][1], (CS, 1))) for p in range(nP)]
    egl = [[jnp.exp(x) for x in GL[p]] for p in range(nP)]
    kk = [_dot(k2b[p], k2b[p], _NT) for p in range(nP)]
    qk = [_dot(q2b[p], k2b[p], _NT) for p in range(nP)]
    dmat = [jnp.exp(jnp.where(incl2, G_col[p] - G_row[p], -jnp.inf)) for p in range(nP)]
    a_mat = [jnp.where(strict2, beta[p] * dmat[p] * kk[p], 0.0) for p in range(nP)]
    attn = [(qk[p] * dmat[p]).astype(BF16) for p in range(nP)]
    rhs = [jnp.concatenate([beta[p] * v2[p], (beta[p] * eg[p]) * k2[p]], axis=1).astype(BF16) for p in range(nP)]
    kdec = [k2[p] * jnp.exp(GL2[p] - G_col[p]) for p in range(nP)]
    kdec_t = [_tr(kdec[p]).astype(BF16) for p in range(nP)]
    zero_t = jnp.zeros((DK_C, P2), BF16)
    kdec_lo = [jnp.where(lane_lo, kdec_t[p], zero_t) for p in range(nP)]
    kdec_hi = [jnp.where(lane_lo, zero_t, kdec_t[p]) for p in range(nP)]
    tinv = [t.astype(BF16) for t in inv_unit_lower_all(a_mat)]
    uw = [_dot(tinv[p], rhs[p]) for p in range(nP)]
    wq_lhs = [[stack(uw[p][j * CS:(j + 1) * CS, DV_C:].astype(BF16), q2b[p][j * CS:(j + 1) * CS]) for j in range(2)]
              for p in range(nP)]

    S_cur = [s_s[h] for h in range(NH_C)]
    for ci, c0 in enumerate(range(0, R, CS)):
        rs = slice(c0, c0 + CS)
        ps_ = list(range(ci * npair, (ci + 1) * npair))
        Sb = [S_cur[h].astype(BF16) for h in range(NH_C)]
        wq = [[_dot(wq_lhs[p][j], Sb[hab[p][j]]) for j in range(2)] for p in ps_]
        db = [(uw[p][:, :DV_C] - stack(wq[i_][0][:CS], wq[i_][1][:CS])).astype(BF16) for i_, p in enumerate(ps_)]
        upd_lo = [_dot(kdec_lo[p], db[i_]) for i_, p in enumerate(ps_)]
        upd_hi = [_dot(kdec_hi[p], db[i_]) for i_, p in enumerate(ps_)]
        for i_, p in enumerate(ps_):
            S_cur[hab[p][0]] = egl[p][0] * S_cur[hab[p][0]] + upd_lo[i_]
            S_cur[hab[p][1]] = egl[p][1] * S_cur[hab[p][1]] + upd_hi[i_]
        ad = [_dot(attn[p], db[i_]) for i_, p in enumerate(ps_)]
        o2 = [eg[p] * stack(wq[i_][0][CS:], wq[i_][1][CS:]) + ad[i_] for i_, p in enumerate(ps_)]
        oh = [o2[i_][j * CS:(j + 1) * CS] for i_ in range(npair) for j in range(2)]
        ms = [jnp.mean(o * o, axis=-1, keepdims=True) for o in oh]
        zs = [proj_ref[rs, O_Z + h * DV_C:O_Z + (h + 1) * DV_C] for h in range(NH_C)]
        zg = [z * _sigmoid(z) for z in zs]
        for h in range(NH_C):
            o = oh[h] * lax.rsqrt(ms[h] + RMS_EPS) * normc_ref[...] * zg[h]
            mix_ref[rs, h * DV_C:(h + 1) * DV_C] = o.astype(mix_ref.dtype)
    for h in range(NH_C):
        s_s[h] = S_cur[h]

    sout_ref[...] = s_s[...]


def _odd_mixer(proj, s0, a_log, dt_bias, norm_c, batch, seq_len, rows):
    nsteps = seq_len // rows
    hp = jnp.zeros((8, LANES), F32)
    hp = hp.at[0, NH_C:2 * NH_C].set(a_log.astype(F32)).at[1, NH_C:2 * NH_C].set(dt_bias.astype(F32))
    outs = pl.pallas_call(
        functools.partial(_odd_kernel, rows=rows),
        out_shape=(
            jax.ShapeDtypeStruct((batch * seq_len, D_MODEL), BF16),
            jax.ShapeDtypeStruct((batch, NH_C, DK_C, DV_C), F32),
        ),
        grid=(batch, nsteps),
        in_specs=[
            pl.BlockSpec((rows, N_ODD_COLS), lambda b, i: (b * nsteps + i, 0)),
            pl.BlockSpec((None, NH_C, DK_C, DV_C), lambda b, i: (b, 0, 0, 0)),
            pl.BlockSpec((8, LANES), lambda b, i: (0, 0)),
            pl.BlockSpec((1, DV_C), lambda b, i: (0, 0)),
        ],
        out_specs=(
            pl.BlockSpec((rows, D_MODEL), lambda b, i: (b * nsteps + i, 0)),
            pl.BlockSpec((None, NH_C, DK_C, DV_C), lambda b, i: (b, 0, 0, 0)),
        ),
        scratch_shapes=[pltpu.VMEM((NH_C, DK_C, DV_C), F32)],
        compiler_params=_cparams(("arbitrary", "arbitrary")),
        name="odd_mixer",
    )(proj, s0, hp, norm_c.reshape(1, DV_C))
    return outs


def _post_kernel(mix_ref, x_ref, gate_ref, sc_ref, sh_ref, wout_ref, lng_ref, lnb_ref, wr_ref, br_ref,
                 x1_ref, h2_ref, ri_ref, rg_ref, seg_ref, cnt_ref, carry_s, *, tm):
    i = pl.program_id(0)

    @pl.when(i == 0)
    def _():
        carry_s[...] = jnp.zeros_like(carry_s)

    ng = max(1, tm // 256)
    tg = tm // ng
    ys = [_dot(mix_ref[q * tg:(q + 1) * tg, :], wout_ref[...]) for q in range(min(2, ng))]
    wr_b = wr_ref[...].astype(BF16)
    lts = []
    for q in range(ng):
        rq = slice(q * tg, (q + 1) * tg)
        x1 = _layer_norm_rows(ALPHA_DN * x_ref[rq, :] + (1.0 + gate_ref[...]) * ys[q], lng_ref[...], lnb_ref[...])
        if q + 2 < ng:
            ys.append(_dot(mix_ref[(q + 2) * tg:(q + 3) * tg, :], wout_ref[...]))
        x1_ref[rq, :] = x1
        h2 = x1 * (1.0 + sc_ref[...]) + sh_ref[...]
        h2b = h2.astype(BF16)
        h2_ref[rq, :] = h2b
        lts.append(_dot(wr_b, h2b, _NT))
    lt = (lts[0] if ng == 1 else jnp.concatenate(lts, axis=1)) + br_ref[:, 0:1]
    e_iota = lax.broadcasted_iota(I32, (N_EXPERTS, tm), 0).astype(F32)
    vals, idxs = [], []
    for _ in range(TOP_K):
        mx = jnp.max(lt, axis=0, keepdims=True)
        idx = jnp.min(jnp.where(lt == mx, e_iota, float(N_EXPERTS)), axis=0, keepdims=True)
        vals.append(mx)
        idxs.append(idx)
        lt = jnp.where(e_iota == idx, -jnp.inf, lt)
    ex = [jnp.exp(v - vals[0]) for v in vals]
    tot = ex[0] + ex[1] + ex[2] + ex[3]
    hot = [jnp.where(e_iota == idx, 1.0, 0.0) for idx in idxs]
    m_all = hot[0] + hot[1] + hot[2] + hot[3]
    m_all_b = m_all.astype(BF16)
    r = lax.broadcasted_iota(I32, (tm, tm), 0)
    c = lax.broadcasted_iota(I32, (tm, tm), 1)
    upper = jnp.where(r < c, 1.0, 0.0).astype(BF16)
    prefix = _dot(m_all_b, upper)
    re = lax.broadcasted_iota(I32, (N_EXPERTS, N_EXPERTS), 0)
    ce = lax.broadcasted_iota(I32, (N_EXPERTS, N_EXPERTS), 1)
    cnt_col = jnp.sum(m_all, axis=1, keepdims=True)
    m8_col = jnp.floor((cnt_col + 7.0) * 0.125)
    lower_e = jnp.where(re > ce, 1.0, 0.0).astype(BF16)
    off8_col = 8.0 * _dot(lower_e, jnp.broadcast_to(m8_col, (N_EXPERTS, LANES)).astype(BF16))[:, 0:1]
    base = off8_col + prefix
    dests = [jnp.sum(hk * base, axis=0, keepdims=True) for hk in hot]
    cnt_row = _dot(jnp.ones((8, tm), BF16), m_all_b, _NT)
    m8_row = jnp.floor((cnt_row + 7.0) * 0.125)
    upper_e = jnp.where(re < ce, 1.0, 0.0).astype(BF16)
    off8_row = 8.0 * _dot(m8_row.astype(BF16), upper_e)
    gc = carry_s[:, 0:N_EXPERTS]
    srow = lax.broadcasted_iota(I32, (8, N_EXPERTS), 0)
    seg = jnp.where(srow == 0, 8.0 * m8_row, jnp.where(srow == 1, off8_row, jnp.where(srow == 2, gc, 0.0)))
    seg_ref[...] = jnp.concatenate([seg, jnp.zeros((8, LANES - N_EXPERTS), F32)], axis=1).astype(I32)
    carry_s[:, 0:N_EXPERTS] = gc + 8.0 * m8_row
    cnt_ref[...] = carry_s[...]
    ri_ref[...] = jnp.concatenate(dests + [jnp.zeros((4, tm), F32)], axis=0).astype(I32)
    rg_ref[...] = jnp.concatenate([e / tot for e in ex] + [jnp.zeros((4, tm), F32)], axis=0)


def _post(mix, x2d, gate, scale, shift, w_out_bf16, ln_g, ln_b, wr_t, br, seq_len, tm):
    t = x2d.shape[0]
    per = seq_len // tm
    vec = lambda i: (i // per, 0, 0)
    const2 = lambda i: (0, 0)
    return pl.pallas_call(
        functools.partial(_post_kernel, tm=tm),
        out_shape=(
            jax.ShapeDtypeStruct((t, D_MODEL), F32),
            jax.ShapeDtypeStruct((t, D_MODEL), BF16),
            jax.ShapeDtypeStruct((t // tm, 8, tm), I32),
            jax.ShapeDtypeStruct((t // tm, 8, tm), F32),
            jax.ShapeDtypeStruct((t // tm, 8, LANES), I32),
            jax.ShapeDtypeStruct((8, LANES), F32),
        ),
        grid=(t // tm,),
        in_specs=[
            pl.BlockSpec((tm, D_MODEL), lambda i: (i, 0)),
            pl.BlockSpec((tm, D_MODEL), lambda i: (i, 0)),
            pl.BlockSpec((None, 1, D_MODEL), vec),
            pl.BlockSpec((None, 1, D_MODEL), vec),
            pl.BlockSpec((None, 1, D_MODEL), vec),
            pl.BlockSpec((D_MODEL, D_MODEL), const2),
            pl.BlockSpec((1, D_MODEL), const2),
            pl.BlockSpec((1, D_MODEL), const2),
            pl.BlockSpec((N_EXPERTS, D_MODEL), const2),
            pl.BlockSpec((N_EXPERTS, LANES), const2),
        ],
        out_specs=(
            pl.BlockSpec((tm, D_MODEL), lambda i: (i, 0)),
            pl.BlockSpec((tm, D_MODEL), lambda i: (i, 0)),
            pl.BlockSpec((None, 8, tm), lambda i: (i, 0, 0)),
            pl.BlockSpec((None, 8, tm), lambda i: (i, 0, 0)),
            pl.BlockSpec((None, 8, LANES), lambda i: (i, 0, 0)),
            pl.BlockSpec((8, LANES), const2),
        ),
        scratch_shapes=[pltpu.VMEM((8, LANES), F32)],
        compiler_params=_cparams(("arbitrary",)),
        name="post_router",
    )(mix, x2d, gate, scale, shift, w_out_bf16, ln_g, ln_b, wr_t, br)


SEG_FIELDS = 3


def _run_sizes(tm):
    top = 1 << (tm - 1).bit_length()
    return [s for s in (512, 256, 128, 64, 32, 16, 8) if s <= max(top, 8)]


def _for_each_run_piece(seg_ref, rstart_ref, tile, tm, fn):
    base = tile * (SEG_FIELDS * N_EXPERTS)

    def body(e, c):
        n8 = seg_ref[base + e]
        src = seg_ref[base + N_EXPERTS + e]
        dst = rstart_ref[e] + seg_ref[base + 2 * N_EXPERTS + e]
        def pieces(sizes, done):
            for size in sizes:
                @pl.when((n8 & size) != 0)
                def _(done=done, size=size):
                    fn(pl.multiple_of(src + done, 8), pl.multiple_of(dst + done, 8), size)
                done = done + (n8 & size)

        split = 128
        big = [s for s in _run_sizes(tm) if s >= split]
        if big:
            @pl.when(n8 >= split)
            def _():
                pieces(big, jnp.int32(0))
        pieces([s for s in _run_sizes(tm) if s < split], n8 & ~jnp.int32(split - 1))
        return c

    lax.fori_loop(0, N_EXPERTS, body, 0)


def _wait_tile_rows(seg_ref, tile, cb, wait_rows):
    base = tile * (SEG_FIELDS * N_EXPERTS)
    last = N_EXPERTS - 1
    total = seg_ref[base + last] + seg_ref[base + N_EXPERTS + last]
    size = 1 << (cb.bit_length() - 1)
    while size >= 8:
        @pl.when((total & size) != 0)
        def _(size=size):
            wait_rows(size)
        size //= 2


def _dispatch_kernel(zblk_ref, rstart_ref, seg_ref, ri_ref, h_ref, xs_ref, zero_s, cbuf, sem, zsem,
                     *, tm, bm, cb, ntiles):
    i = pl.program_id(0)
    slot = i % 2

    def run_copies(tile, sl, wait):
        if wait:
            def wait_rows(size):
                pltpu.make_async_copy(cbuf.at[sl, pl.ds(0, size), :], xs_ref.at[pl.ds(0, size), :], sem.at[sl]).wait()
            _wait_tile_rows(seg_ref, tile, cb, wait_rows)
            return

        def piece(src, dst, size):
            pltpu.make_async_copy(cbuf.at[sl, pl.ds(src, size), :], xs_ref.at[pl.ds(dst, size), :], sem.at[sl]).start()
        _for_each_run_piece(seg_ref, rstart_ref, tile, tm, piece)

    def zero_copy(e):
        return pltpu.make_async_copy(zero_s, xs_ref.at[pl.ds(zblk_ref[e] * bm, bm), :], zsem)

    @pl.when(i == 0)
    def _():
        zero_s[...] = jnp.zeros_like(zero_s)

        def zstart(e, c):
            @pl.when(zblk_ref[e] >= 0)
            def _():
                zero_copy(e).start()
            return c

        def zwait(e, c):
            @pl.when(zblk_ref[e] >= 0)
            def _():
                zero_copy(e).wait()
            return c

        lax.fori_loop(0, N_EXPERTS, zstart, 0)
        lax.fori_loop(0, N_EXPERTS, zwait, 0)

    @pl.when(i >= 2)
    def _():
        run_copies(i - 2, slot, True)

    rows = lax.broadcasted_iota(I32, (cb, tm), 0)
    hit = rows == ri_ref[0:1, :]
    for k in range(1, TOP_K):
        hit = hit | (rows == ri_ref[k:k + 1, :])
    perm = jnp.where(hit, 1.0, 0.0).astype(BF16)
    cbuf[slot] = _dot(perm, h_ref[...])
    run_copies(i, slot, False)

    @pl.when(i == ntiles - 1)
    def _():
        if ntiles > 1:
            run_copies(i - 1, 1 - slot, True)
        run_copies(i, slot, True)


def _dispatch(h2, route_i, seg_flat, row_start, zblk, n_rows, tm, bm):
    t = h2.shape[0]
    ntiles = t // tm
    cb = TOP_K * tm + 8 * N_EXPERTS
    return pl.pallas_call(
        functools.partial(_dispatch_kernel, tm=tm, bm=bm, cb=cb, ntiles=ntiles),
        out_shape=jax.ShapeDtypeStruct((n_rows, D_MODEL), F32),
        grid_spec=pltpu.PrefetchScalarGridSpec(
            num_scalar_prefetch=3,
            grid=(ntiles,),
            in_specs=[
                pl.BlockSpec((None, 2 * TOP_K, tm), lambda i, z, r, s: (i, 0, 0)),
                pl.BlockSpec((tm, D_MODEL), lambda i, z, r, s: (i, 0)),
            ],
            out_specs=pl.BlockSpec(memory_space=pl.ANY),
            scratch_shapes=[
                pltpu.VMEM((bm, D_MODEL), F32),
                pltpu.VMEM((2, cb, D_MODEL), F32),
                pltpu.SemaphoreType.DMA((2,)),
                pltpu.SemaphoreType.DMA,
            ],
        ),
        compiler_params=_cparams(("arbitrary",)),
        name="dispatch",
    )(zblk, row_start, seg_flat, route_i, h2)


def _ffn_kernel(be_ref, grp_ref, ia_ref, ib_ref, nv_ref, xa_ref, xb_ref, w1_ref, b1_ref, w2_ref, b2_ref,
                ya_ref, yb_ref, w1_s, w2_s):
    b = pl.program_id(0)

    def expert_block(x_ref, y_ref):
        xb = x_ref[...].astype(BF16)
        glu = _dot(xb, w1_s[:, :D_FF]) + b1_ref[:, :D_FF]
        lin = _dot(xb, w1_s[:, D_FF:]) + b1_ref[:, D_FF:]
        glu = jnp.minimum(glu, SWIGLU_LIMIT)
        lin = jnp.clip(lin, -SWIGLU_LIMIT, SWIGLU_LIMIT)
        act = glu * _sigmoid(SWIGLU_ALPHA * glu) * (lin + 1.0)
        y_ref[...] = _dot(act.astype(BF16), w2_s[...]) + b2_ref[...]

    @pl.when(b < nv_ref[0])
    def _():
        @pl.when((b == 0) | (be_ref[b] != be_ref[jnp.maximum(b - 1, 0)]))
        def _():
            step = 256
            for r0 in range(0, D_MODEL, step):
                w1_s[r0:r0 + step, :] = w1_ref[r0:r0 + step, :].astype(BF16)
            for r0 in range(0, D_FF, step):
                w2_s[r0:r0 + step, :] = w2_ref[r0:r0 + step, :].astype(BF16)

        @pl.when(grp_ref[b] == 0)
        def _():
            expert_block(xa_ref, ya_ref)

        @pl.when(grp_ref[b] == 1)
        def _():
            expert_block(xb_ref, yb_ref)


def _ffn(xs_a, xs_b, steps, layer, w1, b1, w2, b2, bm_a, bm_b):
    step_e, step_g, step_ia, step_ib, n_valid = steps
    nsteps = step_e.shape[0]
    row_a = lambda b, be, g, ia, ib, nv: (ia[b], 0)
    row_b = lambda b, be, g, ia, ib, nv: (ib[b], 0)
    wsel = lambda b, be, g, ia, ib, nv: (layer, be[b], 0, 0)
    return pl.pallas_call(
        _ffn_kernel,
        out_shape=(jax.ShapeDtypeStruct(xs_a.shape, F32), jax.ShapeDtypeStruct(xs_b.shape, F32)),
        grid_spec=pltpu.PrefetchScalarGridSpec(
            num_scalar_prefetch=5,
            grid=(nsteps,),
            in_specs=[
                pl.BlockSpec((bm_a, D_MODEL), row_a),
                pl.BlockSpec((bm_b, D_MODEL), row_b),
                pl.BlockSpec((None, None, D_MODEL, 2 * D_FF), wsel),
                pl.BlockSpec((None, None, 1, 2 * D_FF), wsel),
                pl.BlockSpec((None, None, D_FF, D_MODEL), wsel),
                pl.BlockSpec((None, None, 1, D_MODEL), wsel),
            ],
            out_specs=(pl.BlockSpec((bm_a, D_MODEL), row_a), pl.BlockSpec((bm_b, D_MODEL), row_b)),
            scratch_shapes=[pltpu.VMEM((D_MODEL, 2 * D_FF), BF16), pltpu.VMEM((D_FF, D_MODEL), BF16)],
        ),
        compiler_params=_cparams(("arbitrary",)),
        name="expert_ffn",
    )(step_e, step_g, step_ia, step_ib, n_valid, xs_a, xs_b, w1, b1, w2, b2)


def _combine_kernel(rstart_ref, seg_ref, ri_ref, x_ref, gate_ref, rg_ref, lng_ref, lnb_ref, ys_ref, o_ref,
                    ybuf, yb_s, wt_s, sem, *, tm, cb, ntiles):
    i = pl.program_id(0)
    slot = i % 2

    def run_copies(tile, sl, wait):
        if wait:
            def wait_rows(size):
                pltpu.make_async_copy(ys_ref.at[pl.ds(0, size), :], ybuf.at[sl, pl.ds(0, size), :], sem.at[sl]).wait()
            _wait_tile_rows(seg_ref, tile, cb, wait_rows)
            return

        def piece(loc, glob, size):
            pltpu.make_async_copy(ys_ref.at[pl.ds(glob, size), :], ybuf.at[sl, pl.ds(loc, size), :], sem.at[sl]).start()
        _for_each_run_piece(seg_ref, rstart_ref, tile, tm, piece)

    @pl.when(i == 0)
    def _():
        ybuf[...] = jnp.zeros_like(ybuf)
        run_copies(0, 0, False)

    @pl.when(i + 1 < ntiles)
    def _():
        run_copies(i + 1, 1 - slot, False)

    dest_t = _tr(jnp.concatenate([ri_ref[...].astype(F32), jnp.zeros((LANES - 8, tm), F32)], axis=0))
    g_t = _tr(jnp.concatenate([rg_ref[...], jnp.zeros((LANES - 8, tm), F32)], axis=0))
    lane_step = 256
    nq = max(1, tm // 256)
    tq = tm // nq

    def build(q):
        rq = slice(q * tq, (q + 1) * tq)
        for c0 in range(0, cb, lane_step):
            cols = (lax.broadcasted_iota(I32, (tq, lane_step), 1) + c0).astype(F32)
            w = jnp.where(cols == dest_t[rq, 0:1], g_t[rq, 0:1], 0.0)
            for k in range(1, TOP_K):
                w = w + jnp.where(cols == dest_t[rq, k:k + 1], g_t[rq, k:k + 1], 0.0)
            wt_s[rq, c0:c0 + lane_step] = w.astype(BF16)

    def product(q):
        rq = slice(q * tq, (q + 1) * tq)
        return _dot(wt_s[rq, :], yb_s[...])

    def finish(q, moe):
        rq = slice(q * tq, (q + 1) * tq)
        o_ref[rq, :] = _layer_norm_rows(ALPHA_DN * x_ref[rq, :] + (1.0 + gate_ref[...]) * moe,
                                        lng_ref[...], lnb_ref[...])

    build(0)
    run_copies(i, slot, True)
    row_step = 256
    for r0 in range(0, cb, row_step):
        yb_s[r0:r0 + row_step, :] = ybuf[slot, r0:r0 + row_step, :].astype(BF16)
    moes = {}
    for q in range(nq):
        if q + 1 < nq:
            build(q + 1)
        moes[q] = product(q)
        if q >= 1:
            finish(q - 1, moes.pop(q - 1))
    finish(nq - 1, moes.pop(nq - 1))


def _combine(row_start, seg_flat, route_i, x1, gate, route_g, ln_g, ln_b, ys, seq_len, tm):
    t = x1.shape[0]
    per = seq_len // tm
    ntiles = t // tm
    cb = TOP_K * tm + 8 * N_EXPERTS
    return pl.pallas_call(
        functools.partial(_combine_kernel, tm=tm, cb=cb, ntiles=ntiles),
        out_shape=jax.ShapeDtypeStruct((t, D_MODEL), F32),
        grid_spec=pltpu.PrefetchScalarGridSpec(
            num_scalar_prefetch=2,
            grid=(ntiles,),
            in_specs=[
                pl.BlockSpec((None, 2 * TOP_K, tm), lambda i, r, s: (i, 0, 0)),
                pl.BlockSpec((tm, D_MODEL), lambda i, r, s: (i, 0)),
                pl.BlockSpec((None, 1, D_MODEL), lambda i, r, s: (i // per, 0, 0)),
                pl.BlockSpec((None, 8, tm), lambda i, r, s: (i, 0, 0)),
                pl.BlockSpec((1, D_MODEL), lambda i, r, s: (0, 0)),
                pl.BlockSpec((1, D_MODEL), lambda i, r, s: (0, 0)),
                pl.BlockSpec(memory_space=pl.ANY),
            ],
            out_specs=pl.BlockSpec((tm, D_MODEL), lambda i, r, s: (i, 0)),
            scratch_shapes=[
                pltpu.VMEM((2, cb, D_MODEL), F32),
                pltpu.VMEM((cb, D_MODEL), BF16),
                pltpu.VMEM((tm, cb), BF16),
                pltpu.SemaphoreType.DMA((2,)),
            ],
        ),
        compiler_params=_cparams(("arbitrary",)),
        name="combine",
    )(row_start, seg_flat, route_i, x1, gate, route_g, ln_g, ln_b, ys)


def _max_blocks(t, tm, bm):
    return -(-(t * TOP_K + 7 * N_EXPERTS * (t // tm)) // bm) + N_EXPERTS


def _group_tables(counts, bm):
    cnt = counts[0, :N_EXPERTS].astype(I32)
    nblk = (cnt + bm - 1) // bm
    blk_end = jnp.cumsum(nblk)
    row_start = ((blk_end - nblk) * bm).astype(I32)
    zblk = jnp.where((cnt % bm) != 0, blk_end - 1, -1).astype(I32)
    return nblk, row_start, zblk


def _ffn_steps(nblk_a, nblk_b, nsteps):
    per_e = nblk_a + nblk_b
    ends = jnp.cumsum(per_e)
    n_valid = ends[-1]
    s = jnp.minimum(jnp.arange(nsteps, dtype=I32), n_valid - 1)
    step_e = jnp.minimum(jnp.sum((ends[None, :] <= s[:, None]).astype(I32), axis=1), N_EXPERTS - 1)
    onehot = (step_e[:, None] == jnp.arange(N_EXPERTS, dtype=I32)[None, :]).astype(I32)
    off = s - jnp.sum(onehot * (ends - per_e)[None, :], axis=1)
    in_b = off >= jnp.sum(onehot * nblk_a[None, :], axis=1)
    live = jnp.arange(nsteps, dtype=I32) < n_valid
    step_g = jnp.where(live, in_b.astype(I32), 2)
    step_ia = jnp.maximum(jnp.cumsum((step_g == 0).astype(I32)) - 1, 0)
    step_ib = jnp.maximum(jnp.cumsum((step_g == 1).astype(I32)) - 1, 0)
    return (step_e.astype(I32), step_g.astype(I32), step_ia.astype(I32), step_ib.astype(I32),
            n_valid.reshape(1).astype(I32))


def _group_cfg(batch, seq_len):
    if seq_len >= 512:
        return dict(tm=512, rows=256, bm=512)
    return dict(tm=seq_len, rows=seq_len, bm=128)


def kernel(x_prompt, x_sample, c_prompt, c_sample, state_a_C, state_a_n, state_a_m, cache_b_k, cache_b_v,
           state_c_S, state_c_conv, w_ada, b_ada, ln_g, ln_b, w_in_even, b_in_even, norm_a, sink_b, rel_bias,
           w_out_even, w_in_odd, conv_c, a_log_c, dt_bias_c, norm_c, w_out_odd, w_router, b_router,
           w_e1, b_e1, w_e2, b_e2):
    bp, lp, _ = x_prompt.shape
    bs, ls, _ = x_sample.shape
    groups = [dict(b=bp, l=lp, x=x_prompt.reshape(bp * lp, D_MODEL), **_group_cfg(bp, lp)),
              dict(b=bs, l=ls, x=x_sample.reshape(bs * ls, D_MODEL), **_group_cfg(bs, ls))]
    mod = _ada(jnp.concatenate([c_prompt, c_sample], axis=0), w_ada, b_ada)
    offs = [0, bp]
    states = [dict(), dict()]
    b1_all = b_e1.reshape(DEPTH, N_EXPERTS, 1, 2 * D_FF)
    b2_all = b_e2.reshape(DEPTH, N_EXPERTS, 1, D_MODEL)
    for l in range(DEPTH):
        e = l // 2
        if l % 2 == 0:
            w = w_in_even[e]
            sz = (512, 512, 512, 512, 4, 4, 512, 128, 128)
            o = [sum(sz[:j]) for j in range(len(sz) + 1)]
            pad = jnp.zeros((D_MODEL, LANES - 2 * NH_A), w.dtype)
            w_in = jnp.concatenate([w[:, o[0]:o[4]], w[:, o[6]:o[9]], w[:, o[4]:o[6]], pad], axis=1).astype(BF16)
            bb = b_in_even[e]
            b_in = jnp.concatenate([bb[o[0]:o[4]], bb[o[6]:o[9]], bb[o[4]:o[6]],
                                    jnp.zeros((LANES - 2 * NH_A,), bb.dtype)]).reshape(1, N_EVEN_COLS)
            w_out = w_out_even[e].astype(BF16)
        else:
            w = w_in_odd[e]
            pad = jnp.zeros((D_MODEL, LANES - 2 * NH_C), w.dtype)
            w_in = jnp.concatenate([w, pad], axis=1).astype(BF16)
            w_out = w_out_odd[e].astype(BF16)
        wr_t = w_router[l].T
        br = jnp.broadcast_to(b_router[l][:, None], (N_EXPERTS, LANES))
        moe_in = []
        for gi, gr in enumerate(groups):
            nb, sl = gr['b'], gr['l']
            m = mod[l, offs[gi]:offs[gi] + nb].reshape(nb, 6, 1, D_MODEL)
            m6 = [m[:, j] for j in range(6)]
            st = states[gi]
            if l % 2 == 0:
                proj = _inproj(gr['x'], m6[1], m6[0], w_in, b_in, sl, gr['tm'])
                if gi == 0:
                    kh = jnp.zeros((nb, WINDOW, NKV_B, DH_B), F32)
                    vh = kh
                    c0 = jnp.zeros((nb, NH_A, DK_A, DV_A), F32)
                    n0 = jnp.zeros((nb, NH_A, DK_A), F32)
                    m0 = jnp.zeros((nb, NH_A), F32)
                else:
                    kh, vh, c0, n0, m0 = cache_b_k[e], cache_b_v[e], state_a_C[e], state_a_n[e], state_a_m[e]
                bias_tab = _swa_bias_table(rel_bias, gr['rows'])
                mix, c1, n1, m1, k1, v1 = _even_mixer(proj, kh, vh, c0, n0, m0, bias_tab, norm_a[e], sink_b[e],
                                                      nb, sl, gr['rows'], gi == 1)
                for name, val in (('a_C', c1), ('a_n', n1), ('a_m', m1), ('b_k', k1), ('b_v', v1)):
                    st.setdefault(name, []).append(val)
            else:
                if gi == 0:
                    cv0 = jnp.zeros((nb, CONV_W - 1, QKV_C), F32)
                    s0 = jnp.zeros((nb, NH_C, DK_C, DV_C), F32)
                else:
                    cv0, s0 = state_c_conv[e], state_c_S[e]
                proj, cv1 = _inproj_conv(gr['x'], m6[1], m6[0], w_in, cv0, conv_c[e], sl, gr['tm'])
                mix, s1 = _odd_mixer(proj, s0, a_log_c[e], dt_bias_c[e], norm_c[e], nb, sl, gr['rows'])
                st.setdefault('c_S', []).append(s1)
                st.setdefault('c_conv', []).append(cv1)
            x1, h2, route_i, route_g, seg, counts = _post(
                mix, gr['x'], m6[2], m6[4], m6[3], w_out, ln_g[l, 0].reshape(1, D_MODEL),
                ln_b[l, 0].reshape(1, D_MODEL), wr_t, br, sl, gr['tm'])
            t, tm, bm = nb * sl, gr['tm'], gr['bm']
            ntiles = t // tm
            nblk, row_start, zblk = _group_tables(counts, bm)
            seg_flat = seg[:, :SEG_FIELDS, :N_EXPERTS].reshape(ntiles * SEG_FIELDS * N_EXPERTS)
            xs = _dispatch(h2, route_i, seg_flat, row_start, zblk, _max_blocks(t, tm, bm) * bm, tm, bm)
            moe_in.append(dict(xs=xs, nblk=nblk, row_start=row_start, seg_flat=seg_flat, route_i=route_i,
                               route_g=route_g, x1=x1, gate=m6[5]))
        ga, gb = moe_in
        steps = _ffn_steps(ga['nblk'], gb['nblk'], sum(_max_blocks(g['b'] * g['l'], g['tm'], g['bm']) for g in groups))
        ys = _ffn(ga['xs'], gb['xs'], steps, l, w_e1, b1_all, w_e2, b2_all, groups[0]['bm'], groups[1]['bm'])
        for gi, gr in enumerate(groups):
            mi = moe_in[gi]
            gr['x'] = _combine(mi['row_start'], mi['seg_flat'], mi['route_i'], mi['x1'], mi['gate'], mi['route_g'],
                               ln_g[l, 1].reshape(1, D_MODEL), ln_b[l, 1].reshape(1, D_MODEL), ys[gi],
                               gr['l'], gr['tm'])
    outs = [groups[0]['x'].reshape(bp, lp, D_MODEL), groups[1]['x'].reshape(bs, ls, D_MODEL)]
    for gi in range(2):
        for name in ('a_C', 'a_n', 'a_m', 'b_k', 'b_v', 'c_S', 'c_conv'):
            outs.append(jnp.stack(states[gi][name]))
    return tuple(outs)
```

```python
import functools
import math

import jax
import jax.numpy as jnp
from jax import lax
from jax.experimental import pallas as pl
from jax.experimental.pallas import tpu as pltpu

F32 = jnp.float32
BF16 = jnp.bfloat16
I32 = jnp.int32

D_MODEL = 1024
CHUNK = 64
NH_A, DK_A, DV_A = 4, 128, 128
W_A = NH_A * DV_A
NH_B, NKV_B, DH_B, WINDOW = 8, 2, 64, 128
G_B = NH_B // NKV_B
W_B = NH_B * DH_B
NUM_BUCKETS, MAX_DISTANCE = 32, 256
NH_C, DK_C, DV_C, CONV_W = 8, 128, 128, 4
QKV_C = NH_C * (2 * DK_C + DV_C)
N_EXPERTS, TOP_K, D_FF = 32, 4, 1024
SWIGLU_LIMIT, SWIGLU_ALPHA = 7.0, 1.702
DEPTH = 2
ALPHA_DN = (2 * DEPTH) ** 0.25
LN_EPS = 1e-5
RMS_EPS = 1e-6

LANES = 128
E_QA, E_KA, E_VA, E_OA, E_QB, E_KB, E_VB, E_G = 0, 512, 1024, 1536, 2048, 2560, 2688, 2816
N_EVEN_COLS = 2944
O_QKV, O_Z, O_G = 0, 3072, 4096
N_ODD_COLS = 4224

VMEM_LIMIT = 56 * 1024 * 1024


def _cparams(sem):
    return pltpu.CompilerParams(dimension_semantics=sem, vmem_limit_bytes=VMEM_LIMIT)


def _softplus(x):
    return jnp.maximum(x, 0.0) + jnp.log(1.0 + jnp.exp(-jnp.abs(x)))


def _sigmoid(x):
    return 1.0 / (1.0 + jnp.exp(-x))


_NN = (((1,), (0,)), ((), ()))
_NT = (((1,), (1,)), ((), ()))


def _dot(a, b, dims=_NN):
    return lax.dot_general(a, b, dims, preferred_element_type=F32)


def _dot_exact_lhs(a_bf16, b, dims=_NN):
    b0 = b.astype(BF16)
    r1 = b - b0.astype(F32)
    b1 = r1.astype(BF16)
    b2 = (r1 - b1.astype(F32)).astype(BF16)
    return _dot(a_bf16, b0, dims) + _dot(a_bf16, b1, dims) + _dot(a_bf16, b2, dims)


def _tr(x):
    r = x.shape[0]
    rp = -(-r // LANES) * LANES
    if rp != r:
        x = jnp.concatenate([x, jnp.zeros((rp - r, x.shape[1]), x.dtype)], axis=0)
    return x.T[:, :r]


def _tri(n, strict=False):
    r = lax.broadcasted_iota(I32, (n, n), 0)
    c = lax.broadcasted_iota(I32, (n, n), 1)
    return (r > c) if strict else (r >= c)


def _layer_norm_rows(v, g, b):
    mu = jnp.mean(v, axis=-1, keepdims=True)
    d = v - mu
    var = jnp.mean(d * d, axis=-1, keepdims=True)
    return d * lax.rsqrt(var + LN_EPS) * g + b


def _ada_kernel(c_ref, w_ref, b_ref, o_ref):
    c = c_ref[...]
    a = (c * _sigmoid(c)).astype(BF16)
    o_ref[...] = _dot(a, w_ref[...].astype(BF16)) + b_ref[...]


def _ada(c_all, w_ada, b_ada):
    nb = c_all.shape[0]
    tn = 1536
    return pl.pallas_call(
        _ada_kernel,
        out_shape=jax.ShapeDtypeStruct((DEPTH, nb, 6 * D_MODEL), F32),
        grid=(DEPTH, 6 * D_MODEL // tn),
        in_specs=[
            pl.BlockSpec((nb, D_MODEL), lambda l, j: (0, 0)),
            pl.BlockSpec((None, D_MODEL, tn), lambda l, j: (l, 0, j)),
            pl.BlockSpec((None, 1, tn), lambda l, j: (l, 0, j)),
        ],
        out_specs=pl.BlockSpec((None, nb, tn), lambda l, j: (l, 0, j)),
        compiler_params=_cparams(("arbitrary", "arbitrary")),
        name="ada",
    )(c_all, w_ada, b_ada.reshape(DEPTH, 1, 6 * D_MODEL))


def _inproj_kernel(x_ref, sc_ref, sh_ref, w_ref, b_ref, o_ref, *, n_cols, col_step):
    h = (x_ref[...] * (1.0 + sc_ref[...]) + sh_ref[...]).astype(BF16)
    for c0 in range(0, n_cols, col_step):
        c1 = min(c0 + col_step, n_cols)
        o_ref[:, c0:c1] = _dot(h, w_ref[:, c0:c1]) + b_ref[:, c0:c1]


def _inproj(x2d, scale, shift, w_bf16, bias, seq_len, tm):
    t = x2d.shape[0]
    n = w_bf16.shape[1]
    per = seq_len // tm
    return pl.pallas_call(
        functools.partial(_inproj_kernel, n_cols=n, col_step=1024),
        out_shape=jax.ShapeDtypeStruct((t, n), F32),
        grid=(t // tm,),
        in_specs=[
            pl.BlockSpec((tm, D_MODEL), lambda i: (i, 0)),
            pl.BlockSpec((None, 1, D_MODEL), lambda i: (i // per, 0, 0)),
            pl.BlockSpec((None, 1, D_MODEL), lambda i: (i // per, 0, 0)),
            pl.BlockSpec((D_MODEL, n), lambda i: (0, 0)),
            pl.BlockSpec((1, n), lambda i: (0, 0)),
        ],
        out_specs=pl.BlockSpec((tm, n), lambda i: (i, 0)),
        compiler_params=_cparams(("arbitrary",)),
        name="inproj",
    )(x2d, scale, shift, w_bf16, bias)


CONV_HB = 8


def _inproj_conv_kernel(x_ref, sc_ref, sh_ref, w_ref, conv0_ref, convw_ref, o_ref, convout_ref, xa_s,
                        *, tm, per):
    i = pl.program_id(0)
    HB = CONV_HB

    @pl.when(i % per == 0)
    def _():
        xa_s[0:HB - (CONV_W - 1), :] = jnp.zeros((HB - (CONV_W - 1), QKV_C), F32)
        xa_s[HB - (CONV_W - 1):HB, :] = conv0_ref[...]

    h = (x_ref[...] * (1.0 + sc_ref[...]) + sh_ref[...]).astype(BF16)
    step = 512

    def project(c0):
        return _dot(h, w_ref[:, c0:c0 + step])

    def conv_group(g0, pre):
        xa = jnp.concatenate([xa_s[:, g0:g0 + step], pre], axis=0)
        for c in range(0, step, LANES):
            c0 = g0 + c
            xc = xa[:, c:c + LANES]
            y = None
            for j in range(CONV_W):
                back = xc if j == CONV_W - 1 else pltpu.roll(xc, CONV_W - 1 - j, 0)
                term = back[HB:HB + tm, :] * convw_ref[j:j + 1, c0:c0 + LANES]
                y = term if y is None else y + term
            y = y * _sigmoid(y)
            if c0 < 2 * NH_C * DK_C:
                y = y * lax.rsqrt(jnp.sum(y * y, axis=-1, keepdims=True) + RMS_EPS)
            o_ref[:, c0:c0 + LANES] = y
        convout_ref[:, g0:g0 + step] = pre[tm - (CONV_W - 1):tm, :]
        return pre[tm - HB:tm, :]

    groups = list(range(0, QKV_C, step))
    rest = list(range(O_Z, N_ODD_COLS, 256))
    pres = {0: project(groups[0])}
    tails = []
    for gi, g0 in enumerate(groups):
        if gi + 1 < len(groups):
            pres[gi + 1] = project(groups[gi + 1])
        for c0 in rest[gi::len(groups)]:
            c1 = min(c0 + 256, N_ODD_COLS)
            o_ref[:, c0:c1] = _dot(h, w_ref[:, c0:c1])
        tails.append(conv_group(g0, pres.pop(gi)))
    for g0, tail in zip(groups, tails):
        xa_s[:, g0:g0 + step] = tail


def _inproj_conv(x2d, scale, shift, w_bf16, conv_hist, conv_w, seq_len, tm):
    t = x2d.shape[0]
    per = seq_len // tm
    batch = t // seq_len
    return pl.pallas_call(
        functools.partial(_inproj_conv_kernel, tm=tm, per=per),
        out_shape=(jax.ShapeDtypeStruct((t, N_ODD_COLS), F32),
                   jax.ShapeDtypeStruct((batch, CONV_W - 1, QKV_C), F32)),
        grid=(t // tm,),
        in_specs=[
            pl.BlockSpec((tm, D_MODEL), lambda i: (i, 0)),
            pl.BlockSpec((None, 1, D_MODEL), lambda i: (i // per, 0, 0)),
            pl.BlockSpec((None, 1, D_MODEL), lambda i: (i // per, 0, 0)),
            pl.BlockSpec((D_MODEL, N_ODD_COLS), lambda i: (0, 0)),
            pl.BlockSpec((None, CONV_W - 1, QKV_C), lambda i: (i // per, 0, 0)),
            pl.BlockSpec((CONV_W, QKV_C), lambda i: (0, 0)),
        ],
        out_specs=(pl.BlockSpec((tm, N_ODD_COLS), lambda i: (i, 0)),
                   pl.BlockSpec((None, CONV_W - 1, QKV_C), lambda i: (i // per, 0, 0))),
        scratch_shapes=[pltpu.VMEM((CONV_HB, QKV_C), F32)],
        compiler_params=_cparams(("arbitrary",)),
        name="inproj_conv",
    )(x2d, scale, shift, w_bf16, conv_hist, conv_w)


def _even_kernel(proj_ref, kh0_ref, vh0_ref, c0_ref, n0_ref, m0_ref, bias_ref, norma_ref, sink_ref,
                 mix_ref, cout_ref, nout_ref, mout_ref, kout_ref, vout_ref,
                 c_s, n_s, m_s, kh_s, vh_s, *, rows, hist_valid):
    R = rows
    KW = WINDOW + R
    i = pl.program_id(1)

    @pl.when(i == 0)
    def _():
        c_s[...] = c0_ref[...]
        n_s[...] = n0_ref[...]
        m_s[...] = m0_ref[...]
        kh_s[...] = kh0_ref[...]
        vh_s[...] = vh0_ref[...]

    CS = CHUNK
    g = proj_ref[:, E_G:E_G + LANES]
    lf = -_softplus(-g)
    rr = lax.broadcasted_iota(I32, (R, R), 0)
    cc = lax.broadcasted_iota(I32, (R, R), 1)
    tri_b = jnp.where((rr >= cc) & (rr // CS == cc // CS), 1.0, 0.0).astype(BF16)
    b_all = _dot_exact_lhs(tri_b, lf)
    g_t = _tr(g)
    b_t = _tr(b_all)
    causal = _tri(CS)
    scale_a = DK_A ** -0.5
    chunks = list(range(0, R, CS))
    P = [(c0, h) for c0 in chunks for h in range(NH_A)]
    nP = len(P)
    rsl = [slice(c0, c0 + CS) for c0, _ in P]
    b_col = [b_all[rsl[p], NH_A + h:NH_A + h + 1] for p, (_, h) in enumerate(P)]
    b_row = [b_t[NH_A + h:NH_A + h + 1, rsl[p]] for p, (_, h) in enumerate(P)]
    i_row = [g_t[h:h + 1, rsl[p]] for p, (_, h) in enumerate(P)]
    i_col = [g[rsl[p], h:h + 1] for p, (_, h) in enumerate(P)]
    logw = [jnp.where(causal, b_col[p] - b_row[p] + i_row[p], -jnp.inf) for p in range(nP)]
    lmax = [jnp.max(logw[p], axis=-1, keepdims=True) for p in range(nP)]
    qb = [(proj_ref[rsl[p], E_QA + h * DK_A:E_QA + (h + 1) * DK_A] * scale_a).astype(BF16)
          for p, (_, h) in enumerate(P)]
    kf = [proj_ref[rsl[p], E_KA + h * DK_A:E_KA + (h + 1) * DK_A] for p, (_, h) in enumerate(P)]
    vb = [proj_ref[rsl[p], E_VA + h * DV_A:E_VA + (h + 1) * DV_A].astype(BF16) for p, (_, h) in enumerate(P)]
    qk = [_dot(qb[p], kf[p].astype(BF16), _NT) for p in range(nP)]
    b_last = [b_col[p][CS - 1:CS, :] for p in range(nP)]

    k_win = jnp.concatenate([kh_s[...], proj_ref[:, E_KB:E_KB + NKV_B * DH_B]], axis=0)
    v_win = jnp.concatenate([vh_s[...], proj_ref[:, E_VB:E_VB + NKV_B * DH_B]], axis=0)
    k_win_b = k_win.astype(BF16)
    v_win_b = v_win.astype(BF16)
    if not hist_valid:
        key_pos = lax.broadcasted_iota(I32, (R, KW), 1) + (i * R - WINDOW)
        key_ok = key_pos >= 0
    scale_b = DH_B ** -0.5
    lo = lax.broadcasted_iota(I32, (1, LANES), 1) < DH_B
    k_swp = pltpu.roll(k_win, DH_B, 1).astype(BF16)
    v_swp = pltpu.roll(v_win, DH_B, 1).astype(BF16)
    zk = jnp.zeros_like(k_win_b)
    k_both = [jnp.where(lo, k_win_b, k_swp), jnp.where(lo, k_swp, k_win_b)]
    v_lo = [jnp.where(lo, v_win_b, zk), jnp.where(lo, v_swp, zk)]
    v_hi = [jnp.where(lo, zk, v_swp), jnp.where(lo, zk, v_win_b)]
    npairs = NH_B // 2
    qp = [proj_ref[:, E_QB + j * LANES:E_QB + (j + 1) * LANES].astype(BF16) for j in range(npairs)]
    zq = jnp.zeros_like(qp[0])
    qh = [jnp.where(lo, qp[hd // 2], zq) if hd % 2 == 0 else jnp.where(lo, zq, qp[hd // 2]) for hd in range(NH_B)]
    scores = [_dot(qh[hd], k_both[hd // G_B], _NT) * scale_b + bias_ref[hd] for hd in range(NH_B)]
    if not hist_valid:
        scores = [jnp.where(key_ok, s, -jnp.inf) for s in scores]
    sks = [sink_ref[hd] for hd in range(NH_B)]
    mxs = [jnp.maximum(jnp.max(scores[hd], axis=-1, keepdims=True), sks[hd]) for hd in range(NH_B)]
    ps = [jnp.exp(scores[hd] - mxs[hd]) for hd in range(NH_B)]
    dens = [jnp.sum(ps[hd], axis=-1, keepdims=True) + jnp.exp(sks[hd] - mxs[hd]) for hd in range(NH_B)]
    pbs = [(ps[hd] / dens[hd]).astype(BF16) for hd in range(NH_B)]

    m_prev, m_inter, m_t, m_new = [None] * nP, [None] * nP, [None] * nP, [None] * nP
    for h in range(NH_A):
        run = m_s[h:h + 1, 0:1]
        for ci in range(len(chunks)):
            p = ci * NH_A + h
            m_prev[p] = run
            m_inter[p] = b_col[p] + run
            m_t[p] = jnp.maximum(m_inter[p], lmax[p])
            run = m_t[p][CS - 1:CS, :]
            m_new[p] = run
        m_s[h:h + 1, :] = jnp.broadcast_to(run, (1, LANES))
    w = [jnp.exp(logw[p] - m_t[p]) for p in range(nP)]
    s = [qk[p] * w[p] for p in range(nP)]
    sv = [_dot(s[p].astype(BF16), vb[p]) for p in range(nP)]
    ssum = [jnp.sum(s[p], axis=-1, keepdims=True) for p in range(nP)]
    kfac = [jnp.exp(b_last[p] - b_col[p] + i_col[p] - m_new[p]) for p in range(nP)]
    kw = [kf[p] * kfac[p] for p in range(nP)]
    kwt = [_tr(kw[p]).astype(BF16) for p in range(nP)]
    kv = [_dot(kwt[p], vb[p]) for p in range(nP)]
    ksum = [jnp.sum(kw[p], axis=0, keepdims=True) for p in range(nP)]
    carry = [jnp.exp(b_last[p] + m_prev[p] - m_new[p]) for p in range(nP)]
    dec = [jnp.exp(m_inter[p] - m_t[p]) for p in range(nP)]
    floor_ = [jnp.exp(-m_t[p]) for p in range(nP)]
    c_prev, n_prev = [None] * nP, [None] * nP
    for h in range(NH_A):
        n_run = n_s[h:h + 1, :]
        c_run = c_s[h]
        for ci in range(len(chunks)):
            p = ci * NH_A + h
            c_prev[p], n_prev[p] = c_run, n_run
            c_run = carry[p] * c_run + kv[p]
            n_run = carry[p] * n_run + ksum[p]
        c_s[h] = c_run
        n_s[h:h + 1, :] = n_run
    qc = [_dot(qb[p], c_prev[p].astype(BF16)) for p in range(nP)]
    qn = [jnp.sum(qb[p].astype(F32) * n_prev[p].astype(BF16).astype(F32), axis=-1, keepdims=True) for p in range(nP)]
    num = [dec[p] * qc[p] + sv[p] for p in range(nP)]
    den = [dec[p] * qn[p] + ssum[p] for p in range(nP)]
    hh = [num[p] / jnp.maximum(jnp.abs(den[p]), floor_[p]) for p in range(nP)]
    mu = [jnp.mean(hh[p], axis=-1, keepdims=True) for p in range(nP)]
    dd = [hh[p] - mu[p] for p in range(nP)]
    var = [jnp.mean(dd[p] * dd[p], axis=-1, keepdims=True) for p in range(nP)]
    rs_ = [lax.rsqrt(var[p] + LN_EPS) for p in range(nP)]
    for p, (_, h) in enumerate(P):
        og = proj_ref[rsl[p], E_OA + h * DV_A:E_OA + (h + 1) * DV_A]
        ha = dd[p] * rs_[p] * norma_ref[h:h + 1, :] * _sigmoid(og)
        mix_ref[rsl[p], h * DV_A:(h + 1) * DV_A] = ha.astype(mix_ref.dtype)

    cout_ref[...] = c_s[...]
    nout_ref[...] = n_s[...]
    mout_ref[...] = m_s[...]

    outs = [_dot(pbs[hd], (v_lo if hd % 2 == 0 else v_hi)[hd // G_B]) for hd in range(NH_B)]
    for j in range(npairs):
        mix_ref[:, W_A + j * LANES:W_A + (j + 1) * LANES] = (outs[2 * j] + outs[2 * j + 1]).astype(mix_ref.dtype)

    kh_s[...] = k_win[R:, :]
    vh_s[...] = v_win[R:, :]
    kout_ref[...] = kh_s[...]
    vout_ref[...] = vh_s[...]


def _rel_bucket(rel):
    nb = NUM_BUCKETS // 2
    max_exact = nb // 2
    n = jnp.abs(rel)
    nf = jnp.maximum(n, 1).astype(F32)
    large = max_exact + (jnp.log(nf / max_exact) / math.log(MAX_DISTANCE / max_exact)
                         * (nb - max_exact)).astype(I32)
    large = jnp.minimum(large, nb - 1)
    return jnp.where(rel > 0, nb, 0) + jnp.where(n < max_exact, n, large)


def _swa_bias_table(rel_bias, rows):
    kw = WINDOW + rows
    qi = jnp.arange(rows)[:, None]
    kj = jnp.arange(kw)[None, :]
    bucket = _rel_bucket(kj - WINDOW - qi)
    rb = rel_bias.astype(F32)
    bias = jnp.zeros((NH_B, rows, kw), F32)
    for b in range(NUM_BUCKETS):
        bias = jnp.where((bucket == b)[None], rb[b][:, None, None], bias)
    lo = (qi // CHUNK) * CHUNK
    ok = (kj >= lo) & (kj < lo + WINDOW + CHUNK)
    return jnp.where(ok[None], bias, -jnp.inf)


def _even_mixer(proj, k_hist, v_hist, c0, n0, m0, bias_tab, norm_a, sink, batch, seq_len, rows, hist_valid):
    nsteps = seq_len // rows
    kw = WINDOW + rows
    m0p = jnp.broadcast_to(jnp.pad(m0, ((0, 0), (0, 8 - NH_A)))[:, :, None], (batch, 8, LANES))
    n0p = jnp.pad(n0, ((0, 0), (0, 8 - NH_A), (0, 0)))
    full3 = lambda b, i: (b, 0, 0)
    outs = pl.pallas_call(
        functools.partial(_even_kernel, rows=rows, hist_valid=hist_valid),
        out_shape=(
            jax.ShapeDtypeStruct((batch * seq_len, D_MODEL), BF16),
            jax.ShapeDtypeStruct((batch, NH_A, DK_A, DV_A), F32),
            jax.ShapeDtypeStruct((batch, 8, DK_A), F32),
            jax.ShapeDtypeStruct((batch, 8, LANES), F32),
            jax.ShapeDtypeStruct((batch, WINDOW, NKV_B * DH_B), F32),
            jax.ShapeDtypeStruct((batch, WINDOW, NKV_B * DH_B), F32),
        ),
        grid=(batch, nsteps),
        in_specs=[
            pl.BlockSpec((rows, N_EVEN_COLS), lambda b, i: (b * nsteps + i, 0)),
            pl.BlockSpec((None, WINDOW, NKV_B * DH_B), full3),
            pl.BlockSpec((None, WINDOW, NKV_B * DH_B), full3),
            pl.BlockSpec((None, NH_A, DK_A, DV_A), lambda b, i: (b, 0, 0, 0)),
            pl.BlockSpec((None, 8, DK_A), full3),
            pl.BlockSpec((None, 8, LANES), full3),
            pl.BlockSpec((NH_B, rows, kw), lambda b, i: (0, 0, 0)),
            pl.BlockSpec((NH_A, DV_A), lambda b, i: (0, 0)),
            pl.BlockSpec(memory_space=pltpu.SMEM),
        ],
        out_specs=(
            pl.BlockSpec((rows, D_MODEL), lambda b, i: (b * nsteps + i, 0)),
            pl.BlockSpec((None, NH_A, DK_A, DV_A), lambda b, i: (b, 0, 0, 0)),
            pl.BlockSpec((None, 8, DK_A), full3),
            pl.BlockSpec((None, 8, LANES), full3),
            pl.BlockSpec((None, WINDOW, NKV_B * DH_B), full3),
            pl.BlockSpec((None, WINDOW, NKV_B * DH_B), full3),
        ),
        scratch_shapes=[
            pltpu.VMEM((NH_A, DK_A, DV_A), F32),
            pltpu.VMEM((8, DK_A), F32),
            pltpu.VMEM((8, LANES), F32),
            pltpu.VMEM((WINDOW, NKV_B * DH_B), F32),
            pltpu.VMEM((WINDOW, NKV_B * DH_B), F32),
        ],
        compiler_params=_cparams(("arbitrary", "arbitrary")),
        name="even_mixer",
    )(proj, k_hist.reshape(batch, WINDOW, NKV_B * DH_B), v_hist.reshape(batch, WINDOW, NKV_B * DH_B),
      c0, n0p, m0p, bias_tab, norm_a.reshape(NH_A, DV_A), sink)
    mix, c1, n1, m1, k1, v1 = outs
    return (mix, c1, n1[:, :NH_A, :], m1[:, :NH_A, 0],
            k1.reshape(batch, WINDOW, NKV_B, DH_B), v1.reshape(batch, WINDOW, NKV_B, DH_B))


def _odd_kernel(proj_ref, s0_ref, hp_ref, normc_ref, mix_ref, sout_ref, s_s, *, rows):
    R = rows
    i = pl.program_id(1)

    @pl.when(i == 0)
    def _():
        s_s[...] = s0_ref[...]

    CS = CHUNK
    P2 = 2 * CS
    gcols = proj_ref[:, O_G:O_G + LANES]
    beta_all = _sigmoid(gcols)
    g_all = -jnp.exp(hp_ref[0:1, :]) * _softplus(gcols + hp_ref[1:2, :])
    rr = lax.broadcasted_iota(I32, (R, R), 0)
    cc = lax.broadcasted_iota(I32, (R, R), 1)
    tri_b = jnp.where((rr >= cc) & (rr // CS == cc // CS), 1.0, 0.0).astype(BF16)
    G_all = _dot_exact_lhs(tri_b, g_all)
    G_t = _tr(G_all)
    eG_all = jnp.exp(G_all)
    r2 = lax.broadcasted_iota(I32, (P2, P2), 0)
    c2 = lax.broadcasted_iota(I32, (P2, P2), 1)
    same_head = (r2 // CS) == (c2 // CS)
    incl2 = same_head & (r2 >= c2)
    strict2 = same_head & (r2 > c2)
    eye2 = jnp.where(r2 == c2, 1.0, 0.0)
    lane_lo = lax.broadcasted_iota(I32, (DK_C, P2), 1) < CS
    scale_c = DK_C ** -0.5

    def qkv(rs0, c0):
        return proj_ref[rs0:rs0 + CS, O_QKV + c0:O_QKV + c0 + LANES]

    def stack(a, b):
        return jnp.concatenate([a, b], axis=0)

    def inv_unit_lower_all(a_list):
        ts = [eye2 - a for a in a_list]
        abs_ = [a.astype(BF16) for a in a_list]
        ps = [_dot(ab, ab) for ab in abs_]
        nlev = (CS - 1).bit_length() - 1
        n = len(ts)
        for lvl in range(nlev):
            th = [t.astype(BF16) for t in ts]
            ph = [p.astype(BF16) for p in ps]
            if lvl < nlev - 1:
                lhs = [stack(th[i], ph[i]) for i in range(n)]
                tp = [_dot(lhs[i], ph[i]) for i in range(n)]
                ts = [ts[i] + tp[i][:P2] for i in range(n)]
                ps = [tp[i][P2:] for i in range(n)]
            else:
                tp = [_dot(th[i], ph[i]) for i in range(n)]
                ts = [ts[i] + tp[i] for i in range(n)]
        return ts

    npair = NH_C // 2
    probs = [(c0, pr) for c0 in range(0, R, CS) for pr in range(npair)]
    nP = len(probs)
    rsl = [slice(c0, c0 + CS) for c0, _ in probs]
    hab = [(2 * pr, 2 * pr + 1) for _, pr in probs]

    def per_head(fn):
        return [[fn(p, h) for h in hab[p]] for p in range(nP)]

    def col2(arr, off):
        return [stack(arr[rsl[p], off + hab[p][0]:off + hab[p][0] + 1], arr[rsl[p], off + hab[p][1]:off + hab[p][1] + 1])
                for p in range(nP)]

    kc = per_head(lambda p, h: qkv(probs[p][0], NH_C * DK_C + h * DK_C))
    qc = per_head(lambda p, h: qkv(probs[p][0], h * DK_C))
    vc = per_head(lambda p, h: qkv(probs[p][0], 2 * NH_C * DK_C + h * DV_C))
    k2 = [stack(*kc[p]) for p in range(nP)]
    q2b = [stack(*[x * scale_c for x in qc[p]]).astype(BF16) for p in range(nP)]
    v2 = [stack(*vc[p]) for p in range(nP)]
    k2b = [k2[p].astype(BF16) for p in range(nP)]
    G_col = col2(G_all, NH_C)
    G_row = [jnp.concatenate([G_t[NH_C + hab[p][0]:NH_C + hab[p][0] + 1, rsl[p]],
                              G_t[NH_C + hab[p][1]:NH_C + hab[p][1] + 1, rsl[p]]], axis=1) for p in range(nP)]
    beta = col2(beta_all, 0)
    eg = col2(eG_all, NH_C)
    GL = [[G_all[probs[p][0] + CS - 1:probs[p][0] + CS, NH_C + h:NH_C + h + 1] for h in hab[p]] for p in range(nP)]
    GL2 = [stack(jnp.broadcast_to(GL[p][0], (CS, 1)), jnp.broadcast_to(GL[p][1], (CS, 1))) for p in range(nP)]
    egl = [[jnp.exp(x) for x in GL[p]] for p in range(nP)]
    kk = [_dot(k2b[p], k2b[p], _NT) for p in range(nP)]
    qk = [_dot(q2b[p], k2b[p], _NT) for p in range(nP)]
    dmat = [jnp.exp(jnp.where(incl2, G_col[p] - G_row[p], -jnp.inf)) for p in range(nP)]
    a_mat = [jnp.where(strict2, beta[p] * dmat[p] * kk[p], 0.0) for p in range(nP)]
    attn = [(qk[p] * dmat[p]).astype(BF16) for p in range(nP)]
    rhs = [jnp.concatenate([beta[p] * v2[p], (beta[p] * eg[p]) * k2[p]], axis=1).astype(BF16) for p in range(nP)]
    kdec = [k2[p] * jnp.exp(GL2[p] - G_col[p]) for p in range(nP)]
    kdec_t = [_tr(kdec[p]).astype(BF16) for p in range(nP)]
    zero_t = jnp.zeros((DK_C, P2), BF16)
    kdec_lo = [jnp.where(lane_lo, kdec_t[p], zero_t) for p in range(nP)]
    kdec_hi = [jnp.where(lane_lo, zero_t, kdec_t[p]) for p in range(nP)]
    tinv = [t.astype(BF16) for t in inv_unit_lower_all(a_mat)]
    uw = [_dot(tinv[p], rhs[p]) for p in range(nP)]
    wq_lhs = [[stack(uw[p][j * CS:(j + 1) * CS, DV_C:].astype(BF16), q2b[p][j * CS:(j + 1) * CS]) for j in range(2)]
              for p in range(nP)]

    S_cur = [s_s[h] for h in range(NH_C)]
    for ci, c0 in enumerate(range(0, R, CS)):
        rs = slice(c0, c0 + CS)
        ps_ = list(range(ci * npair, (ci + 1) * npair))
        Sb = [S_cur[h].astype(BF16) for h in range(NH_C)]
        wq = [[_dot(wq_lhs[p][j], Sb[hab[p][j]]) for j in range(2)] for p in ps_]
        db = [(uw[p][:, :DV_C] - stack(wq[i_][0][:CS], wq[i_][1][:CS])).astype(BF16) for i_, p in enumerate(ps_)]
        upd_lo = [_dot(kdec_lo[p], db[i_]) for i_, p in enumerate(ps_)]
        upd_hi = [_dot(kdec_hi[p], db[i_]) for i_, p in enumerate(ps_)]
        for i_, p in enumerate(ps_):
            S_cur[hab[p][0]] = egl[p][0] * S_cur[hab[p][0]] + upd_lo[i_]
            S_cur[hab[p][1]] = egl[p][1] * S_cur[hab[p][1]] + upd_hi[i_]
        ad = [_dot(attn[p], db[i_]) for i_, p in enumerate(ps_)]
        o2 = [eg[p] * stack(wq[i_][0][CS:], wq[i_][1][CS:]) + ad[i_] for i_, p in enumerate(ps_)]
        oh = [o2[i_][j * CS:(j + 1) * CS] for i_ in range(npair) for j in range(2)]
        ms = [jnp.mean(o * o, axis=-1, keepdims=True) for o in oh]
        zs = [proj_ref[rs, O_Z + h * DV_C:O_Z + (h + 1) * DV_C] for h in range(NH_C)]
        zg = [z * _sigmoid(z) for z in zs]
        for h in range(NH_C):
            o = oh[h] * lax.rsqrt(ms[h] + RMS_EPS) * normc_ref[...] * zg[h]
            mix_ref[rs, h * DV_C:(h + 1) * DV_C] = o.astype(mix_ref.dtype)
    for h in range(NH_C):
        s_s[h] = S_cur[h]

    sout_ref[...] = s_s[...]


def _odd_mixer(proj, s0, a_log, dt_bias, norm_c, batch, seq_len, rows):
    nsteps = seq_len // rows
    hp = jnp.zeros((8, LANES), F32)
    hp = hp.at[0, NH_C:2 * NH_C].set(a_log.astype(F32)).at[1, NH_C:2 * NH_C].set(dt_bias.astype(F32))
    outs = pl.pallas_call(
        functools.partial(_odd_kernel, rows=rows),
        out_shape=(
            jax.ShapeDtypeStruct((batch * seq_len, D_MODEL), BF16),
            jax.ShapeDtypeStruct((batch, NH_C, DK_C, DV_C), F32),
        ),
        grid=(batch, nsteps),
        in_specs=[
            pl.BlockSpec((rows, N_ODD_COLS), lambda b, i: (b * nsteps + i, 0)),
            pl.BlockSpec((None, NH_C, DK_C, DV_C), lambda b, i: (b, 0, 0, 0)),
            pl.BlockSpec((8, LANES), lambda b, i: (0, 0)),
            pl.BlockSpec((1, DV_C), lambda b, i: (0, 0)),
        ],
        out_specs=(
            pl.BlockSpec((rows, D_MODEL), lambda b, i: (b * nsteps + i, 0)),
            pl.BlockSpec((None, NH_C, DK_C, DV_C), lambda b, i: (b, 0, 0, 0)),
        ),
        scratch_shapes=[pltpu.VMEM((NH_C, DK_C, DV_C), F32)],
        compiler_params=_cparams(("arbitrary", "arbitrary")),
        name="odd_mixer",
    )(proj, s0, hp, norm_c.reshape(1, DV_C))
    return outs


def _post_kernel(mix_ref, x_ref, gate_ref, sc_ref, sh_ref, wout_ref, lng_ref, lnb_ref, wr_ref, br_ref,
                 x1_ref, h2_ref, ri_ref, rg_ref, seg_ref, cnt_ref, carry_s, *, tm):
    i = pl.program_id(0)

    @pl.when(i == 0)
    def _():
        carry_s[...] = jnp.zeros_like(carry_s)

    ng = max(1, tm // 256)
    tg = tm // ng
    ys = [_dot(mix_ref[q * tg:(q + 1) * tg, :], wout_ref[...]) for q in range(min(2, ng))]
    wr_b = wr_ref[...].astype(BF16)
    lts = []
    for q in range(ng):
        rq = slice(q * tg, (q + 1) * tg)
        x1 = _layer_norm_rows(ALPHA_DN * x_ref[rq, :] + (1.0 + gate_ref[...]) * ys[q], lng_ref[...], lnb_ref[...])
        if q + 2 < ng:
            ys.append(_dot(mix_ref[(q + 2) * tg:(q + 3) * tg, :], wout_ref[...]))
        x1_ref[rq, :] = x1
        h2 = x1 * (1.0 + sc_ref[...]) + sh_ref[...]
        h2b = h2.astype(BF16)
        h2_ref[rq, :] = h2b
        lts.append(_dot(wr_b, h2b, _NT))
    lt = (lts[0] if ng == 1 else jnp.concatenate(lts, axis=1)) + br_ref[:, 0:1]
    e_iota = lax.broadcasted_iota(I32, (N_EXPERTS, tm), 0).astype(F32)
    vals, idxs = [], []
    for _ in range(TOP_K):
        mx = jnp.max(lt, axis=0, keepdims=True)
        idx = jnp.min(jnp.where(lt == mx, e_iota, float(N_EXPERTS)), axis=0, keepdims=True)
        vals.append(mx)
        idxs.append(idx)
        lt = jnp.where(e_iota == idx, -jnp.inf, lt)
    ex = [jnp.exp(v - vals[0]) for v in vals]
    tot = ex[0] + ex[1] + ex[2] + ex[3]
    hot = [jnp.where(e_iota == idx, 1.0, 0.0) for idx in idxs]
    m_all = hot[0] + hot[1] + hot[2] + hot[3]
    m_all_b = m_all.astype(BF16)
    r = lax.broadcasted_iota(I32, (tm, tm), 0)
    c = lax.broadcasted_iota(I32, (tm, tm), 1)
    upper = jnp.where(r < c, 1.0, 0.0).astype(BF16)
    prefix = _dot(m_all_b, upper)
    re = lax.broadcasted_iota(I32, (N_EXPERTS, N_EXPERTS), 0)
    ce = lax.broadcasted_iota(I32, (N_EXPERTS, N_EXPERTS), 1)
    cnt_col = jnp.sum(m_all, axis=1, keepdims=True)
    m8_col = jnp.floor((cnt_col + 7.0) * 0.125)
    lower_e = jnp.where(re > ce, 1.0, 0.0).astype(BF16)
    off8_col = 8.0 * _dot(lower_e, jnp.broadcast_to(m8_col, (N_EXPERTS, LANES)).astype(BF16))[:, 0:1]
    base = off8_col + prefix
    dests = [jnp.sum(hk * base, axis=0, keepdims=True) for hk in hot]
    cnt_row = _dot(jnp.ones((8, tm), BF16), m_all_b, _NT)
    m8_row = jnp.floor((cnt_row + 7.0) * 0.125)
    upper_e = jnp.where(re < ce, 1.0, 0.0).astype(BF16)
    off8_row = 8.0 * _dot(m8_row.astype(BF16), upper_e)
    gc = carry_s[:, 0:N_EXPERTS]
    srow = lax.broadcasted_iota(I32, (8, N_EXPERTS), 0)
    seg = jnp.where(srow == 0, 8.0 * m8_row, jnp.where(srow == 1, off8_row, jnp.where(srow == 2, gc, 0.0)))
    seg_ref[...] = jnp.concatenate([seg, jnp.zeros((8, LANES - N_EXPERTS), F32)], axis=1).astype(I32)
    carry_s[:, 0:N_EXPERTS] = gc + 8.0 * m8_row
    cnt_ref[...] = carry_s[...]
    ri_ref[...] = jnp.concatenate(dests + [jnp.zeros((4, tm), F32)], axis=0).astype(I32)
    rg_ref[...] = jnp.concatenate([e / tot for e in ex] + [jnp.zeros((4, tm), F32)], axis=0)


def _post(mix, x2d, gate, scale, shift, w_out_bf16, ln_g, ln_b, wr_t, br, seq_len, tm):
    t = x2d.shape[0]
    per = seq_len // tm
    vec = lambda i: (i // per, 0, 0)
    const2 = lambda i: (0, 0)
    return pl.pallas_call(
        functools.partial(_post_kernel, tm=tm),
        out_shape=(
            jax.ShapeDtypeStruct((t, D_MODEL), F32),
            jax.ShapeDtypeStruct((t, D_MODEL), BF16),
            jax.ShapeDtypeStruct((t // tm, 8, tm), I32),
            jax.ShapeDtypeStruct((t // tm, 8, tm), F32),
            jax.ShapeDtypeStruct((t // tm, 8, LANES), I32),
            jax.ShapeDtypeStruct((8, LANES), F32),
        ),
        grid=(t // tm,),
        in_specs=[
            pl.BlockSpec((tm, D_MODEL), lambda i: (i, 0)),
            pl.BlockSpec((tm, D_MODEL), lambda i: (i, 0)),
            pl.BlockSpec((None, 1, D_MODEL), vec),
            pl.BlockSpec((None, 1, D_MODEL), vec),
            pl.BlockSpec((None, 1, D_MODEL), vec),
            pl.BlockSpec((D_MODEL, D_MODEL), const2),
            pl.BlockSpec((1, D_MODEL), const2),
            pl.BlockSpec((1, D_MODEL), const2),
            pl.BlockSpec((N_EXPERTS, D_MODEL), const2),
            pl.BlockSpec((N_EXPERTS, LANES), const2),
        ],
        out_specs=(
            pl.BlockSpec((tm, D_MODEL), lambda i: (i, 0)),
            pl.BlockSpec((tm, D_MODEL), lambda i: (i, 0)),
            pl.BlockSpec((None, 8, tm), lambda i: (i, 0, 0)),
            pl.BlockSpec((None, 8, tm), lambda i: (i, 0, 0)),
            pl.BlockSpec((None, 8, LANES), lambda i: (i, 0, 0)),
            pl.BlockSpec((8, LANES), const2),
        ),
        scratch_shapes=[pltpu.VMEM((8, LANES), F32)],
        compiler_params=_cparams(("arbitrary",)),
        name="post_router",
    )(mix, x2d, gate, scale, shift, w_out_bf16, ln_g, ln_b, wr_t, br)


SEG_FIELDS = 3


def _run_sizes(tm):
    top = 1 << (tm - 1).bit_length()
    return [s for s in (512, 256, 128, 64, 32, 16, 8) if s <= max(top, 8)]


def _for_each_run_piece(seg_ref, rstart_ref, tile, tm, fn):
    base = tile * (SEG_FIELDS * N_EXPERTS)

    def body(e, c):
        n8 = seg_ref[base + e]
        src = seg_ref[base + N_EXPERTS + e]
        dst = rstart_ref[e] + seg_ref[base + 2 * N_EXPERTS + e]
        def pieces(sizes, done):
            for size in sizes:
                @pl.when((n8 & size) != 0)
                def _(done=done, size=size):
                    fn(pl.multiple_of(src + done, 8), pl.multiple_of(dst + done, 8), size)
                done = done + (n8 & size)

        split = 128
        big = [s for s in _run_sizes(tm) if s >= split]
        if big:
            @pl.when(n8 >= split)
            def _():
                pieces(big, jnp.int32(0))
        pieces([s for s in _run_sizes(tm) if s < split], n8 & ~jnp.int32(split - 1))
        return c

    lax.fori_loop(0, N_EXPERTS, body, 0)


def _wait_tile_rows(seg_ref, tile, cb, wait_rows):
    base = tile * (SEG_FIELDS * N_EXPERTS)
    last = N_EXPERTS - 1
    total = seg_ref[base + last] + seg_ref[base + N_EXPERTS + last]
    size = 1 << (cb.bit_length() - 1)
    while size >= 8:
        @pl.when((total & size) != 0)
        def _(size=size):
            wait_rows(size)
        size //= 2


def _dispatch_kernel(zblk_ref, rstart_ref, seg_ref, ri_ref, h_ref, xs_ref, zero_s, cbuf, sem, zsem,
                     *, tm, bm, cb, ntiles):
    i = pl.program_id(0)
    slot = i % 2

    def run_copies(tile, sl, wait):
        if wait:
            def wait_rows(size):
                pltpu.make_async_copy(cbuf.at[sl, pl.ds(0, size), :], xs_ref.at[pl.ds(0, size), :], sem.at[sl]).wait()
            _wait_tile_rows(seg_ref, tile, cb, wait_rows)
            return

        def piece(src, dst, size):
            pltpu.make_async_copy(cbuf.at[sl, pl.ds(src, size), :], xs_ref.at[pl.ds(dst, size), :], sem.at[sl]).start()
        _for_each_run_piece(seg_ref, rstart_ref, tile, tm, piece)

    def zero_copy(e):
        return pltpu.make_async_copy(zero_s, xs_ref.at[pl.ds(zblk_ref[e] * bm, bm), :], zsem)

    @pl.when(i == 0)
    def _():
        zero_s[...] = jnp.zeros_like(zero_s)

        def zstart(e, c):
            @pl.when(zblk_ref[e] >= 0)
            def _():
                zero_copy(e).start()
            return c

        def zwait(e, c):
            @pl.when(zblk_ref[e] >= 0)
            def _():
                zero_copy(e).wait()
            return c

        lax.fori_loop(0, N_EXPERTS, zstart, 0)
        lax.fori_loop(0, N_EXPERTS, zwait, 0)

    @pl.when(i >= 2)
    def _():
        run_copies(i - 2, slot, True)

    rows = lax.broadcasted_iota(I32, (cb, tm), 0)
    hit = rows == ri_ref[0:1, :]
    for k in range(1, TOP_K):
        hit = hit | (rows == ri_ref[k:k + 1, :])
    perm = jnp.where(hit, 1.0, 0.0).astype(BF16)
    cbuf[slot] = _dot(perm, h_ref[...])
    run_copies(i, slot, False)

    @pl.when(i == ntiles - 1)
    def _():
        if ntiles > 1:
            run_copies(i - 1, 1 - slot, True)
        run_copies(i, slot, True)


def _dispatch(h2, route_i, seg_flat, row_start, zblk, n_rows, tm, bm):
    t = h2.shape[0]
    ntiles = t // tm
    cb = TOP_K * tm + 8 * N_EXPERTS
    return pl.pallas_call(
        functools.partial(_dispatch_kernel, tm=tm, bm=bm, cb=cb, ntiles=ntiles),
        out_shape=jax.ShapeDtypeStruct((n_rows, D_MODEL), F32),
        grid_spec=pltpu.PrefetchScalarGridSpec(
            num_scalar_prefetch=3,
            grid=(ntiles,),
            in_specs=[
                pl.BlockSpec((None, 2 * TOP_K, tm), lambda i, z, r, s: (i, 0, 0)),
                pl.BlockSpec((tm, D_MODEL), lambda i, z, r, s: (i, 0)),
            ],
            out_specs=pl.BlockSpec(memory_space=pl.ANY),
            scratch_shapes=[
                pltpu.VMEM((bm, D_MODEL), F32),
                pltpu.VMEM((2, cb, D_MODEL), F32),
                pltpu.SemaphoreType.DMA((2,)),
                pltpu.SemaphoreType.DMA,
            ],
        ),
        compiler_params=_cparams(("arbitrary",)),
        name="dispatch",
    )(zblk, row_start, seg_flat, route_i, h2)


def _ffn_kernel(be_ref, grp_ref, ia_ref, ib_ref, nv_ref, xa_ref, xb_ref, w1_ref, b1_ref, w2_ref, b2_ref,
                ya_ref, yb_ref, w1_s, w2_s):
    b = pl.program_id(0)

    def expert_block(x_ref, y_ref):
        xb = x_ref[...].astype(BF16)
        glu = _dot(xb, w1_s[:, :D_FF]) + b1_ref[:, :D_FF]
        lin = _dot(xb, w1_s[:, D_FF:]) + b1_ref[:, D_FF:]
        glu = jnp.minimum(glu, SWIGLU_LIMIT)
        lin = jnp.clip(lin, -SWIGLU_LIMIT, SWIGLU_LIMIT)
        act = glu * _sigmoid(SWIGLU_ALPHA * glu) * (lin + 1.0)
        y_ref[...] = _dot(act.astype(BF16), w2_s[...]) + b2_ref[...]

    @pl.when(b < nv_ref[0])
    def _():
        @pl.when((b == 0) | (be_ref[b] != be_ref[jnp.maximum(b - 1, 0)]))
        def _():
            step = 256
            for r0 in range(0, D_MODEL, step):
                w1_s[r0:r0 + step, :] = w1_ref[r0:r0 + step, :].astype(BF16)
            for r0 in range(0, D_FF, step):
                w2_s[r0:r0 + step, :] = w2_ref[r0:r0 + step, :].astype(BF16)

        @pl.when(grp_ref[b] == 0)
        def _():
            expert_block(xa_ref, ya_ref)

        @pl.when(grp_ref[b] == 1)
        def _():
            expert_block(xb_ref, yb_ref)


def _ffn(xs_a, xs_b, steps, layer, w1, b1, w2, b2, bm_a, bm_b):
    step_e, step_g, step_ia, step_ib, n_valid = steps
    nsteps = step_e.shape[0]
    row_a = lambda b, be, g, ia, ib, nv: (ia[b], 0)
    row_b = lambda b, be, g, ia, ib, nv: (ib[b], 0)
    wsel = lambda b, be, g, ia, ib, nv: (layer, be[b], 0, 0)
    return pl.pallas_call(
        _ffn_kernel,
        out_shape=(jax.ShapeDtypeStruct(xs_a.shape, F32), jax.ShapeDtypeStruct(xs_b.shape, F32)),
        grid_spec=pltpu.PrefetchScalarGridSpec(
            num_scalar_prefetch=5,
            grid=(nsteps,),
            in_specs=[
                pl.BlockSpec((bm_a, D_MODEL), row_a),
                pl.BlockSpec((bm_b, D_MODEL), row_b),
                pl.BlockSpec((None, None, D_MODEL, 2 * D_FF), wsel),
                pl.BlockSpec((None, None, 1, 2 * D_FF), wsel),
                pl.BlockSpec((None, None, D_FF, D_MODEL), wsel),
                pl.BlockSpec((None, None, 1, D_MODEL), wsel),
            ],
            out_specs=(pl.BlockSpec((bm_a, D_MODEL), row_a), pl.BlockSpec((bm_b, D_MODEL), row_b)),
            scratch_shapes=[pltpu.VMEM((D_MODEL, 2 * D_FF), BF16), pltpu.VMEM((D_FF, D_MODEL), BF16)],
        ),
        compiler_params=_cparams(("arbitrary",)),
        name="expert_ffn",
    )(step_e, step_g, step_ia, step_ib, n_valid, xs_a, xs_b, w1, b1, w2, b2)


def _combine_kernel(rstart_ref, seg_ref, ri_ref, x_ref, gate_ref, rg_ref, lng_ref, lnb_ref, ys_ref, o_ref,
                    ybuf, yb_s, wt_s, sem, *, tm, cb, ntiles):
    i = pl.program_id(0)
    slot = i % 2

    def run_copies(tile, sl, wait):
        if wait:
            def wait_rows(size):
                pltpu.make_async_copy(ys_ref.at[pl.ds(0, size), :], ybuf.at[sl, pl.ds(0, size), :], sem.at[sl]).wait()
            _wait_tile_rows(seg_ref, tile, cb, wait_rows)
            return

        def piece(loc, glob, size):
            pltpu.make_async_copy(ys_ref.at[pl.ds(glob, size), :], ybuf.at[sl, pl.ds(loc, size), :], sem.at[sl]).start()
        _for_each_run_piece(seg_ref, rstart_ref, tile, tm, piece)

    @pl.when(i == 0)
    def _():
        ybuf[...] = jnp.zeros_like(ybuf)
        run_copies(0, 0, False)

    @pl.when(i + 1 < ntiles)
    def _():
        run_copies(i + 1, 1 - slot, False)

    dest_t = _tr(jnp.concatenate([ri_ref[...].astype(F32), jnp.zeros((LANES - 8, tm), F32)], axis=0))
    g_t = _tr(jnp.concatenate([rg_ref[...], jnp.zeros((LANES - 8, tm), F32)], axis=0))
    lane_step = 256
    nq = max(1, tm // 256)
    tq = tm // nq

    def build(q):
        rq = slice(q * tq, (q + 1) * tq)
        for c0 in range(0, cb, lane_step):
            cols = (lax.broadcasted_iota(I32, (tq, lane_step), 1) + c0).astype(F32)
            w = jnp.where(cols == dest_t[rq, 0:1], g_t[rq, 0:1], 0.0)
            for k in range(1, TOP_K):
                w = w + jnp.where(cols == dest_t[rq, k:k + 1], g_t[rq, k:k + 1], 0.0)
            wt_s[rq, c0:c0 + lane_step] = w.astype(BF16)

    def product(q):
        rq = slice(q * tq, (q + 1) * tq)
        return _dot(wt_s[rq, :], yb_s[...])

    def finish(q, moe):
        rq = slice(q * tq, (q + 1) * tq)
        o_ref[rq, :] = _layer_norm_rows(ALPHA_DN * x_ref[rq, :] + (1.0 + gate_ref[...]) * moe,
                                        lng_ref[...], lnb_ref[...])

    build(0)
    run_copies(i, slot, True)
    row_step = 256
    for r0 in range(0, cb, row_step):
        yb_s[r0:r0 + row_step, :] = ybuf[slot, r0:r0 + row_step, :].astype(BF16)
    moes = {}
    for q in range(nq):
        if q + 1 < nq:
            build(q + 1)
        moes[q] = product(q)
        if q >= 1:
            finish(q - 1, moes.pop(q - 1))
    finish(nq - 1, moes.pop(nq - 1))


def _combine(row_start, seg_flat, route_i, x1, gate, route_g, ln_g, ln_b, ys, seq_len, tm):
    t = x1.shape[0]
    per = seq_len // tm
    ntiles = t // tm
    cb = TOP_K * tm + 8 * N_EXPERTS
    return pl.pallas_call(
        functools.partial(_combine_kernel, tm=tm, cb=cb, ntiles=ntiles),
        out_shape=jax.ShapeDtypeStruct((t, D_MODEL), F32),
        grid_spec=pltpu.PrefetchScalarGridSpec(
            num_scalar_prefetch=2,
            grid=(ntiles,),
            in_specs=[
                pl.BlockSpec((None, 2 * TOP_K, tm), lambda i, r, s: (i, 0, 0)),
                pl.BlockSpec((tm, D_MODEL), lambda i, r, s: (i, 0)),
                pl.BlockSpec((None, 1, D_MODEL), lambda i, r, s: (i // per, 0, 0)),
                pl.BlockSpec((None, 8, tm), lambda i, r, s: (i, 0, 0)),
                pl.BlockSpec((1, D_MODEL), lambda i, r, s: (0, 0)),
                pl.BlockSpec((1, D_MODEL), lambda i, r, s: (0, 0)),
                pl.BlockSpec(memory_space=pl.ANY),
            ],
            out_specs=pl.BlockSpec((tm, D_MODEL), lambda i, r, s: (i, 0)),
            scratch_shapes=[
                pltpu.VMEM((2, cb, D_MODEL), F32),
                pltpu.VMEM((cb, D_MODEL), BF16),
                pltpu.VMEM((tm, cb), BF16),
                pltpu.SemaphoreType.DMA((2,)),
            ],
        ),
        compiler_params=_cparams(("arbitrary",)),
        name="combine",
    )(row_start, seg_flat, route_i, x1, gate, route_g, ln_g, ln_b, ys)


def _max_blocks(t, tm, bm):
    return -(-(t * TOP_K + 7 * N_EXPERTS * (t // tm)) // bm) + N_EXPERTS


def _group_tables(counts, bm):
    cnt = counts[0, :N_EXPERTS].astype(I32)
    nblk = (cnt + bm - 1) // bm
    blk_end = jnp.cumsum(nblk)
    row_start = ((blk_end - nblk) * bm).astype(I32)
    zblk = jnp.where((cnt % bm) != 0, blk_end - 1, -1).astype(I32)
    return nblk, row_start, zblk


def _ffn_steps(nblk_a, nblk_b, nsteps):
    per_e = nblk_a + nblk_b
    ends = jnp.cumsum(per_e)
    n_valid = ends[-1]
    s = jnp.minimum(jnp.arange(nsteps, dtype=I32), n_valid - 1)
    step_e = jnp.minimum(jnp.sum((ends[None, :] <= s[:, None]).astype(I32), axis=1), N_EXPERTS - 1)
    onehot = (step_e[:, None] == jnp.arange(N_EXPERTS, dtype=I32)[None, :]).astype(I32)
    off = s - jnp.sum(onehot * (ends - per_e)[None, :], axis=1)
    in_b = off >= jnp.sum(onehot * nblk_a[None, :], axis=1)
    live = jnp.arange(nsteps, dtype=I32) < n_valid
    step_g = jnp.where(live, in_b.astype(I32), 2)
    step_ia = jnp.maximum(jnp.cumsum((step_g == 0).astype(I32)) - 1, 0)
    step_ib = jnp.maximum(jnp.cumsum((step_g == 1).astype(I32)) - 1, 0)
    return (step_e.astype(I32), step_g.astype(I32), step_ia.astype(I32), step_ib.astype(I32),
            n_valid.reshape(1).astype(I32))


def _group_cfg(batch, seq_len):
    if seq_len >= 512:
        return dict(tm=512, tm_in=min(1024, seq_len), rows=256, rows_odd=512, bm=512)
    return dict(tm=seq_len, tm_in=seq_len, rows=seq_len, rows_odd=seq_len, bm=128)


def kernel(x_prompt, x_sample, c_prompt, c_sample, state_a_C, state_a_n, state_a_m, cache_b_k, cache_b_v,
           state_c_S, state_c_conv, w_ada, b_ada, ln_g, ln_b, w_in_even, b_in_even, norm_a, sink_b, rel_bias,
           w_out_even, w_in_odd, conv_c, a_log_c, dt_bias_c, norm_c, w_out_odd, w_router, b_router,
           w_e1, b_e1, w_e2, b_e2):
    bp, lp, _ = x_prompt.shape
    bs, ls, _ = x_sample.shape
    groups = [dict(b=bp, l=lp, x=x_prompt.reshape(bp * lp, D_MODEL), **_group_cfg(bp, lp)),
              dict(b=bs, l=ls, x=x_sample.reshape(bs * ls, D_MODEL), **_group_cfg(bs, ls))]
    mod = _ada(jnp.concatenate([c_prompt, c_sample], axis=0), w_ada, b_ada)
    offs = [0, bp]
    states = [dict(), dict()]
    b1_all = b_e1.reshape(DEPTH, N_EXPERTS, 1, 2 * D_FF)
    b2_all = b_e2.reshape(DEPTH, N_EXPERTS, 1, D_MODEL)
    for l in range(DEPTH):
        e = l // 2
        if l % 2 == 0:
            w = w_in_even[e]
            sz = (512, 512, 512, 512, 4, 4, 512, 128, 128)
            o = [sum(sz[:j]) for j in range(len(sz) + 1)]
            pad = jnp.zeros((D_MODEL, LANES - 2 * NH_A), w.dtype)
            w_in = jnp.concatenate([w[:, o[0]:o[4]], w[:, o[6]:o[9]], w[:, o[4]:o[6]], pad], axis=1).astype(BF16)
            bb = b_in_even[e]
            b_in = jnp.concatenate([bb[o[0]:o[4]], bb[o[6]:o[9]], bb[o[4]:o[6]],
                                    jnp.zeros((LANES - 2 * NH_A,), bb.dtype)]).reshape(1, N_EVEN_COLS)
            w_out = w_out_even[e].astype(BF16)
        else:
            w = w_in_odd[e]
            pad = jnp.zeros((D_MODEL, LANES - 2 * NH_C), w.dtype)
            w_in = jnp.concatenate([w, pad], axis=1).astype(BF16)
            w_out = w_out_odd[e].astype(BF16)
        wr_t = w_router[l].T
        br = jnp.broadcast_to(b_router[l][:, None], (N_EXPERTS, LANES))
        moe_in = []
        for gi, gr in enumerate(groups):
            nb, sl = gr['b'], gr['l']
            m = mod[l, offs[gi]:offs[gi] + nb].reshape(nb, 6, 1, D_MODEL)
            m6 = [m[:, j] for j in range(6)]
            st = states[gi]
            if l % 2 == 0:
                proj = _inproj(gr['x'], m6[1], m6[0], w_in, b_in, sl, gr['tm_in'])
                if gi == 0:
                    kh = jnp.zeros((nb, WINDOW, NKV_B, DH_B), F32)
                    vh = kh
                    c0 = jnp.zeros((nb, NH_A, DK_A, DV_A), F32)
                    n0 = jnp.zeros((nb, NH_A, DK_A), F32)
                    m0 = jnp.zeros((nb, NH_A), F32)
                else:
                    kh, vh, c0, n0, m0 = cache_b_k[e], cache_b_v[e], state_a_C[e], state_a_n[e], state_a_m[e]
                bias_tab = _swa_bias_table(rel_bias, gr['rows'])
                mix, c1, n1, m1, k1, v1 = _even_mixer(proj, kh, vh, c0, n0, m0, bias_tab, norm_a[e], sink_b[e],
                                                      nb, sl, gr['rows'], gi == 1)
                for name, val in (('a_C', c1), ('a_n', n1), ('a_m', m1), ('b_k', k1), ('b_v', v1)):
                    st.setdefault(name, []).append(val)
            else:
                if gi == 0:
                    cv0 = jnp.zeros((nb, CONV_W - 1, QKV_C), F32)
                    s0 = jnp.zeros((nb, NH_C, DK_C, DV_C), F32)
                else:
                    cv0, s0 = state_c_conv[e], state_c_S[e]
                proj, cv1 = _inproj_conv(gr['x'], m6[1], m6[0], w_in, cv0, conv_c[e], sl, gr['tm'])
                mix, s1 = _odd_mixer(proj, s0, a_log_c[e], dt_bias_c[e], norm_c[e], nb, sl, gr['rows_odd'])
                st.setdefault('c_S', []).append(s1)
                st.setdefault('c_conv', []).append(cv1)
            x1, h2, route_i, route_g, seg, counts = _post(
                mix, gr['x'], m6[2], m6[4], m6[3], w_out, ln_g[l, 0].reshape(1, D_MODEL),
                ln_b[l, 0].reshape(1, D_MODEL), wr_t, br, sl, gr['tm'])
            t, tm, bm = nb * sl, gr['tm'], gr['bm']
            ntiles = t // tm
            nblk, row_start, zblk = _group_tables(counts, bm)
            seg_flat = seg[:, :SEG_FIELDS, :N_EXPERTS].reshape(ntiles * SEG_FIELDS * N_EXPERTS)
            xs = _dispatch(h2, route_i, seg_flat, row_start, zblk, _max_blocks(t, tm, bm) * bm, tm, bm)
            moe_in.append(dict(xs=xs, nblk=nblk, row_start=row_start, seg_flat=seg_flat, route_i=route_i,
                               route_g=route_g, x1=x1, gate=m6[5]))
        ga, gb = moe_in
        steps = _ffn_steps(ga['nblk'], gb['nblk'], sum(_max_blocks(g['b'] * g['l'], g['tm'], g['bm']) for g in groups))
        ys = _ffn(ga['xs'], gb['xs'], steps, l, w_e1, b1_all, w_e2, b2_all, groups[0]['bm'], groups[1]['bm'])
        for gi, gr in enumerate(groups):
            mi = moe_in[gi]
            gr['x'] = _combine(mi['row_start'], mi['seg_flat'], mi['route_i'], mi['x1'], mi['gate'], mi['route_g'],
                               ln_g[l, 1].reshape(1, D_MODEL), ln_b[l, 1].reshape(1, D_MODEL), ys[gi],
                               gr['l'], gr['tm'])
    outs = [groups[0]['x'].reshape(bp, lp, D_MODEL), groups[1]['x'].reshape(bs, ls, D_MODEL)]
    for gi in range(2):
        for name in ('a_C', 'a_n', 'a_m', 'b_k', 'b_v', 'c_S', 'c_conv'):
            outs.append(jnp.stack(states[gi][name]))
    return tuple(outs)
```

```python
import functools
import math

import jax
import jax.numpy as jnp
from jax import lax
from jax.experimental import pallas as pl
from jax.experimental.pallas import tpu as pltpu

F32 = jnp.float32
BF16 = jnp.bfloat16
I32 = jnp.int32

D_MODEL = 1024
CHUNK = 64
NH_A, DK_A, DV_A = 4, 128, 128
W_A = NH_A * DV_A
NH_B, NKV_B, DH_B, WINDOW = 8, 2, 64, 128
G_B = NH_B // NKV_B
W_B = NH_B * DH_B
NUM_BUCKETS, MAX_DISTANCE = 32, 256
NH_C, DK_C, DV_C, CONV_W = 8, 128, 128, 4
QKV_C = NH_C * (2 * DK_C + DV_C)
N_EXPERTS, TOP_K, D_FF = 32, 4, 1024
SWIGLU_LIMIT, SWIGLU_ALPHA = 7.0, 1.702
DEPTH = 2
ALPHA_DN = (2 * DEPTH) ** 0.25
LN_EPS = 1e-5
RMS_EPS = 1e-6

LANES = 128
E_QA, E_KA, E_VA, E_OA, E_QB, E_KB, E_VB, E_G = 0, 512, 1024, 1536, 2048, 2560, 2688, 2816
N_EVEN_COLS = 2944
O_QKV, O_Z, O_G = 0, 3072, 4096
N_ODD_COLS = 4224

VMEM_LIMIT = 56 * 1024 * 1024


def _cparams(sem):
    return pltpu.CompilerParams(dimension_semantics=sem, vmem_limit_bytes=VMEM_LIMIT)


def _softplus(x):
    return jnp.maximum(x, 0.0) + jnp.log(1.0 + jnp.exp(-jnp.abs(x)))


def _sigmoid(x):
    return 1.0 / (1.0 + jnp.exp(-x))


_NN = (((1,), (0,)), ((), ()))
_NT = (((1,), (1,)), ((), ()))


def _dot(a, b, dims=_NN):
    return lax.dot_general(a, b, dims, preferred_element_type=F32)


def _dot_exact_lhs(a_bf16, b, dims=_NN):
    b0 = b.astype(BF16)
    r1 = b - b0.astype(F32)
    b1 = r1.astype(BF16)
    b2 = (r1 - b1.astype(F32)).astype(BF16)
    return _dot(a_bf16, b0, dims) + _dot(a_bf16, b1, dims) + _dot(a_bf16, b2, dims)


def _tr(x):
    r = x.shape[0]
    rp = -(-r // LANES) * LANES
    if rp != r:
        x = jnp.concatenate([x, jnp.zeros((rp - r, x.shape[1]), x.dtype)], axis=0)
    return x.T[:, :r]


def _tri(n, strict=False):
    r = lax.broadcasted_iota(I32, (n, n), 0)
    c = lax.broadcasted_iota(I32, (n, n), 1)
    return (r > c) if strict else (r >= c)


def _layer_norm_rows(v, g, b):
    mu = jnp.mean(v, axis=-1, keepdims=True)
    d = v - mu
    var = jnp.mean(d * d, axis=-1, keepdims=True)
    return d * lax.rsqrt(var + LN_EPS) * g + b


def _ada_kernel(c_ref, w_ref, b_ref, o_ref):
    c = c_ref[...]
    a = (c * _sigmoid(c)).astype(BF16)
    o_ref[...] = _dot(a, w_ref[...].astype(BF16)) + b_ref[...]


def _ada(c_all, w_ada, b_ada):
    nb = c_all.shape[0]
    tn = 1536
    return pl.pallas_call(
        _ada_kernel,
        out_shape=jax.ShapeDtypeStruct((DEPTH, nb, 6 * D_MODEL), F32),
        grid=(DEPTH, 6 * D_MODEL // tn),
        in_specs=[
            pl.BlockSpec((nb, D_MODEL), lambda l, j: (0, 0)),
            pl.BlockSpec((None, D_MODEL, tn), lambda l, j: (l, 0, j)),
            pl.BlockSpec((None, 1, tn), lambda l, j: (l, 0, j)),
        ],
        out_specs=pl.BlockSpec((None, nb, tn), lambda l, j: (l, 0, j)),
        compiler_params=_cparams(("arbitrary", "arbitrary")),
        name="ada",
    )(c_all, w_ada, b_ada.reshape(DEPTH, 1, 6 * D_MODEL))


def _inproj_kernel(x_ref, sc_ref, sh_ref, w_ref, b_ref, o_ref, *, n_cols, col_step):
    h = (x_ref[...] * (1.0 + sc_ref[...]) + sh_ref[...]).astype(BF16)
    for c0 in range(0, n_cols, col_step):
        c1 = min(c0 + col_step, n_cols)
        o_ref[:, c0:c1] = _dot(h, w_ref[:, c0:c1]) + b_ref[:, c0:c1]


def _inproj(x2d, scale, shift, w_bf16, bias, seq_len, tm):
    t = x2d.shape[0]
    n = w_bf16.shape[1]
    per = seq_len // tm
    return pl.pallas_call(
        functools.partial(_inproj_kernel, n_cols=n, col_step=1024),
        out_shape=jax.ShapeDtypeStruct((t, n), F32),
        grid=(t // tm,),
        in_specs=[
            pl.BlockSpec((tm, D_MODEL), lambda i: (i, 0)),
            pl.BlockSpec((None, 1, D_MODEL), lambda i: (i // per, 0, 0)),
            pl.BlockSpec((None, 1, D_MODEL), lambda i: (i // per, 0, 0)),
            pl.BlockSpec((D_MODEL, n), lambda i: (0, 0)),
            pl.BlockSpec((1, n), lambda i: (0, 0)),
        ],
        out_specs=pl.BlockSpec((tm, n), lambda i: (i, 0)),
        compiler_params=_cparams(("arbitrary",)),
        name="inproj",
    )(x2d, scale, shift, w_bf16, bias)


CONV_HB = 8


def _inproj_conv_kernel(x_ref, sc_ref, sh_ref, w_ref, conv0_ref, convw_ref, o_ref, convout_ref, xa_s,
                        *, tm, per):
    i = pl.program_id(0)
    HB = CONV_HB

    @pl.when(i % per == 0)
    def _():
        xa_s[0:HB - (CONV_W - 1), :] = jnp.zeros((HB - (CONV_W - 1), QKV_C), F32)
        xa_s[HB - (CONV_W - 1):HB, :] = conv0_ref[...]

    h = (x_ref[...] * (1.0 + sc_ref[...]) + sh_ref[...]).astype(BF16)
    step = 512

    def project(c0):
        return _dot(h, w_ref[:, c0:c0 + step])

    def conv_group(g0, pre):
        xa = jnp.concatenate([xa_s[:, g0:g0 + step], pre], axis=0)
        for c in range(0, step, LANES):
            c0 = g0 + c
            xc = xa[:, c:c + LANES]
            y = None
            for j in range(CONV_W):
                back = xc if j == CONV_W - 1 else pltpu.roll(xc, CONV_W - 1 - j, 0)
                term = back[HB:HB + tm, :] * convw_ref[j:j + 1, c0:c0 + LANES]
                y = term if y is None else y + term
            y = y * _sigmoid(y)
            if c0 < 2 * NH_C * DK_C:
                y = y * lax.rsqrt(jnp.sum(y * y, axis=-1, keepdims=True) + RMS_EPS)
            o_ref[:, c0:c0 + LANES] = y
        convout_ref[:, g0:g0 + step] = pre[tm - (CONV_W - 1):tm, :]
        return pre[tm - HB:tm, :]

    groups = list(range(0, QKV_C, step))
    rest = list(range(O_Z, N_ODD_COLS, 256))
    pres = {0: project(groups[0])}
    tails = []
    for gi, g0 in enumerate(groups):
        if gi + 1 < len(groups):
            pres[gi + 1] = project(groups[gi + 1])
        for c0 in rest[gi::len(groups)]:
            c1 = min(c0 + 256, N_ODD_COLS)
            o_ref[:, c0:c1] = _dot(h, w_ref[:, c0:c1])
        tails.append(conv_group(g0, pres.pop(gi)))
    for g0, tail in zip(groups, tails):
        xa_s[:, g0:g0 + step] = tail


def _inproj_conv(x2d, scale, shift, w_bf16, conv_hist, conv_w, seq_len, tm):
    t = x2d.shape[0]
    per = seq_len // tm
    batch = t // seq_len
    return pl.pallas_call(
        functools.partial(_inproj_conv_kernel, tm=tm, per=per),
        out_shape=(jax.ShapeDtypeStruct((t, N_ODD_COLS), F32),
                   jax.ShapeDtypeStruct((batch, CONV_W - 1, QKV_C), F32)),
        grid=(t // tm,),
        in_specs=[
            pl.BlockSpec((tm, D_MODEL), lambda i: (i, 0)),
            pl.BlockSpec((None, 1, D_MODEL), lambda i: (i // per, 0, 0)),
            pl.BlockSpec((None, 1, D_MODEL), lambda i: (i // per, 0, 0)),
            pl.BlockSpec((D_MODEL, N_ODD_COLS), lambda i: (0, 0)),
            pl.BlockSpec((None, CONV_W - 1, QKV_C), lambda i: (i // per, 0, 0)),
            pl.BlockSpec((CONV_W, QKV_C), lambda i: (0, 0)),
        ],
        out_specs=(pl.BlockSpec((tm, N_ODD_COLS), lambda i: (i, 0)),
                   pl.BlockSpec((None, CONV_W - 1, QKV_C), lambda i: (i // per, 0, 0))),
        scratch_shapes=[pltpu.VMEM((CONV_HB, QKV_C), F32)],
        compiler_params=_cparams(("arbitrary",)),
        name="inproj_conv",
    )(x2d, scale, shift, w_bf16, conv_hist, conv_w)


def _even_kernel(proj_ref, kh0_ref, vh0_ref, c0_ref, n0_ref, m0_ref, bias_ref, norma_ref, sink_ref,
                 mix_ref, cout_ref, nout_ref, mout_ref, kout_ref, vout_ref,
                 c_s, n_s, m_s, kh_s, vh_s, *, rows, hist_valid):
    R = rows
    KW = WINDOW + R
    i = pl.program_id(1)

    @pl.when(i == 0)
    def _():
        c_s[...] = c0_ref[...]
        n_s[...] = n0_ref[...]
        m_s[...] = m0_ref[...]
        kh_s[...] = kh0_ref[...]
        vh_s[...] = vh0_ref[...]

    CS = CHUNK
    g = proj_ref[:, E_G:E_G + LANES]
    lf = -_softplus(-g)
    rr = lax.broadcasted_iota(I32, (R, R), 0)
    cc = lax.broadcasted_iota(I32, (R, R), 1)
    tri_b = jnp.where((rr >= cc) & (rr // CS == cc // CS), 1.0, 0.0).astype(BF16)
    b_all = _dot_exact_lhs(tri_b, lf)
    g_t = _tr(g)
    b_t = _tr(b_all)
    causal = _tri(CS)
    scale_a = DK_A ** -0.5
    chunks = list(range(0, R, CS))
    P = [(c0, h) for c0 in chunks for h in range(NH_A)]
    nP = len(P)
    rsl = [slice(c0, c0 + CS) for c0, _ in P]
    b_col = [b_all[rsl[p], NH_A + h:NH_A + h + 1] for p, (_, h) in enumerate(P)]
    b_row = [b_t[NH_A + h:NH_A + h + 1, rsl[p]] for p, (_, h) in enumerate(P)]
    i_row = [g_t[h:h + 1, rsl[p]] for p, (_, h) in enumerate(P)]
    i_col = [g[rsl[p], h:h + 1] for p, (_, h) in enumerate(P)]
    logw = [jnp.where(causal, b_col[p] - b_row[p] + i_row[p], -jnp.inf) for p in range(nP)]
    lmax = [jnp.max(logw[p], axis=-1, keepdims=True) for p in range(nP)]
    qb = [(proj_ref[rsl[p], E_QA + h * DK_A:E_QA + (h + 1) * DK_A] * scale_a).astype(BF16)
          for p, (_, h) in enumerate(P)]
    kf = [proj_ref[rsl[p], E_KA + h * DK_A:E_KA + (h + 1) * DK_A] for p, (_, h) in enumerate(P)]
    vb = [proj_ref[rsl[p], E_VA + h * DV_A:E_VA + (h + 1) * DV_A].astype(BF16) for p, (_, h) in enumerate(P)]
    qk = [_dot(qb[p], kf[p].astype(BF16), _NT) for p in range(nP)]
    b_last = [b_col[p][CS - 1:CS, :] for p in range(nP)]

    k_win = jnp.concatenate([kh_s[...], proj_ref[:, E_KB:E_KB + NKV_B * DH_B]], axis=0)
    v_win = jnp.concatenate([vh_s[...], proj_ref[:, E_VB:E_VB + NKV_B * DH_B]], axis=0)
    k_win_b = k_win.astype(BF16)
    v_win_b = v_win.astype(BF16)
    if not hist_valid:
        key_pos = lax.broadcasted_iota(I32, (R, KW), 1) + (i * R - WINDOW)
        key_ok = key_pos >= 0
    scale_b = DH_B ** -0.5
    lo = lax.broadcasted_iota(I32, (1, LANES), 1) < DH_B
    k_swp = pltpu.roll(k_win, DH_B, 1).astype(BF16)
    v_swp = pltpu.roll(v_win, DH_B, 1).astype(BF16)
    zk = jnp.zeros_like(k_win_b)
    k_both = [jnp.where(lo, k_win_b, k_swp), jnp.where(lo, k_swp, k_win_b)]
    v_lo = [jnp.where(lo, v_win_b, zk), jnp.where(lo, v_swp, zk)]
    v_hi = [jnp.where(lo, zk, v_swp), jnp.where(lo, zk, v_win_b)]
    npairs = NH_B // 2
    qp = [proj_ref[:, E_QB + j * LANES:E_QB + (j + 1) * LANES].astype(BF16) for j in range(npairs)]
    zq = jnp.zeros_like(qp[0])
    qh = [jnp.where(lo, qp[hd // 2], zq) if hd % 2 == 0 else jnp.where(lo, zq, qp[hd // 2]) for hd in range(NH_B)]
    scores = [_dot(qh[hd], k_both[hd // G_B], _NT) * scale_b + bias_ref[hd] for hd in range(NH_B)]
    if not hist_valid:
        scores = [jnp.where(key_ok, s, -jnp.inf) for s in scores]
    sks = [sink_ref[hd] for hd in range(NH_B)]
    mxs = [jnp.maximum(jnp.max(scores[hd], axis=-1, keepdims=True), sks[hd]) for hd in range(NH_B)]
    ps = [jnp.exp(scores[hd] - mxs[hd]) for hd in range(NH_B)]
    dens = [jnp.sum(ps[hd], axis=-1, keepdims=True) + jnp.exp(sks[hd] - mxs[hd]) for hd in range(NH_B)]
    pbs = [(ps[hd] / dens[hd]).astype(BF16) for hd in range(NH_B)]

    m_prev, m_inter, m_t, m_new = [None] * nP, [None] * nP, [None] * nP, [None] * nP
    for h in range(NH_A):
        run = m_s[h:h + 1, 0:1]
        for ci in range(len(chunks)):
            p = ci * NH_A + h
            m_prev[p] = run
            m_inter[p] = b_col[p] + run
            m_t[p] = jnp.maximum(m_inter[p], lmax[p])
            run = m_t[p][CS - 1:CS, :]
            m_new[p] = run
        m_s[h:h + 1, :] = jnp.broadcast_to(run, (1, LANES))
    w = [jnp.exp(logw[p] - m_t[p]) for p in range(nP)]
    s = [qk[p] * w[p] for p in range(nP)]
    sv = [_dot(s[p].astype(BF16), vb[p]) for p in range(nP)]
    ssum = [jnp.sum(s[p], axis=-1, keepdims=True) for p in range(nP)]
    kfac = [jnp.exp(b_last[p] - b_col[p] + i_col[p] - m_new[p]) for p in range(nP)]
    kw = [kf[p] * kfac[p] for p in range(nP)]
    kwt = [_tr(kw[p]).astype(BF16) for p in range(nP)]
    kv = [_dot(kwt[p], vb[p]) for p in range(nP)]
    ksum = [jnp.sum(kw[p], axis=0, keepdims=True) for p in range(nP)]
    carry = [jnp.exp(b_last[p] + m_prev[p] - m_new[p]) for p in range(nP)]
    dec = [jnp.exp(m_inter[p] - m_t[p]) for p in range(nP)]
    floor_ = [jnp.exp(-m_t[p]) for p in range(nP)]
    c_prev, n_prev = [None] * nP, [None] * nP
    for h in range(NH_A):
        n_run = n_s[h:h + 1, :]
        c_run = c_s[h]
        for ci in range(len(chunks)):
            p = ci * NH_A + h
            c_prev[p], n_prev[p] = c_run, n_run
            c_run = carry[p] * c_run + kv[p]
            n_run = carry[p] * n_run + ksum[p]
        c_s[h] = c_run
        n_s[h:h + 1, :] = n_run
    qc = [_dot(qb[p], c_prev[p].astype(BF16)) for p in range(nP)]
    qn = [jnp.sum(qb[p].astype(F32) * n_prev[p].astype(BF16).astype(F32), axis=-1, keepdims=True) for p in range(nP)]
    num = [dec[p] * qc[p] + sv[p] for p in range(nP)]
    den = [dec[p] * qn[p] + ssum[p] for p in range(nP)]
    hh = [num[p] / jnp.maximum(jnp.abs(den[p]), floor_[p]) for p in range(nP)]
    mu = [jnp.mean(hh[p], axis=-1, keepdims=True) for p in range(nP)]
    dd = [hh[p] - mu[p] for p in range(nP)]
    var = [jnp.mean(dd[p] * dd[p], axis=-1, keepdims=True) for p in range(nP)]
    rs_ = [lax.rsqrt(var[p] + LN_EPS) for p in range(nP)]
    for p, (_, h) in enumerate(P):
        og = proj_ref[rsl[p], E_OA + h * DV_A:E_OA + (h + 1) * DV_A]
        ha = dd[p] * rs_[p] * norma_ref[h:h + 1, :] * _sigmoid(og)
        mix_ref[rsl[p], h * DV_A:(h + 1) * DV_A] = ha.astype(mix_ref.dtype)

    cout_ref[...] = c_s[...]
    nout_ref[...] = n_s[...]
    mout_ref[...] = m_s[...]

    outs = [_dot(pbs[hd], (v_lo if hd % 2 == 0 else v_hi)[hd // G_B]) for hd in range(NH_B)]
    for j in range(npairs):
        mix_ref[:, W_A + j * LANES:W_A + (j + 1) * LANES] = (outs[2 * j] + outs[2 * j + 1]).astype(mix_ref.dtype)

    kh_s[...] = k_win[R:, :]
    vh_s[...] = v_win[R:, :]
    kout_ref[...] = kh_s[...]
    vout_ref[...] = vh_s[...]


def _rel_bucket(rel):
    nb = NUM_BUCKETS // 2
    max_exact = nb // 2
    n = jnp.abs(rel)
    nf = jnp.maximum(n, 1).astype(F32)
    large = max_exact + (jnp.log(nf / max_exact) / math.log(MAX_DISTANCE / max_exact)
                         * (nb - max_exact)).astype(I32)
    large = jnp.minimum(large, nb - 1)
    return jnp.where(rel > 0, nb, 0) + jnp.where(n < max_exact, n, large)


def _swa_bias_table(rel_bias, rows):
    kw = WINDOW + rows
    qi = jnp.arange(rows)[:, None]
    kj = jnp.arange(kw)[None, :]
    bucket = _rel_bucket(kj - WINDOW - qi)
    rb = rel_bias.astype(F32)
    bias = jnp.zeros((NH_B, rows, kw), F32)
    for b in range(NUM_BUCKETS):
        bias = jnp.where((bucket == b)[None], rb[b][:, None, None], bias)
    lo = (qi // CHUNK) * CHUNK
    ok = (kj >= lo) & (kj < lo + WINDOW + CHUNK)
    return jnp.where(ok[None], bias, -jnp.inf)


def _even_mixer(proj, k_hist, v_hist, c0, n0, m0, bias_tab, norm_a, sink, batch, seq_len, rows, hist_valid):
    nsteps = seq_len // rows
    kw = WINDOW + rows
    m0p = jnp.broadcast_to(jnp.pad(m0, ((0, 0), (0, 8 - NH_A)))[:, :, None], (batch, 8, LANES))
    n0p = jnp.pad(n0, ((0, 0), (0, 8 - NH_A), (0, 0)))
    full3 = lambda b, i: (b, 0, 0)
    outs = pl.pallas_call(
        functools.partial(_even_kernel, rows=rows, hist_valid=hist_valid),
        out_shape=(
            jax.ShapeDtypeStruct((batch * seq_len, D_MODEL), BF16),
            jax.ShapeDtypeStruct((batch, NH_A, DK_A, DV_A), F32),
            jax.ShapeDtypeStruct((batch, 8, DK_A), F32),
            jax.ShapeDtypeStruct((batch, 8, LANES), F32),
            jax.ShapeDtypeStruct((batch, WINDOW, NKV_B * DH_B), F32),
            jax.ShapeDtypeStruct((batch, WINDOW, NKV_B * DH_B), F32),
        ),
        grid=(batch, nsteps),
        in_specs=[
            pl.BlockSpec((rows, N_EVEN_COLS), lambda b, i: (b * nsteps + i, 0)),
            pl.BlockSpec((None, WINDOW, NKV_B * DH_B), full3),
            pl.BlockSpec((None, WINDOW, NKV_B * DH_B), full3),
            pl.BlockSpec((None, NH_A, DK_A, DV_A), lambda b, i: (b, 0, 0, 0)),
            pl.BlockSpec((None, 8, DK_A), full3),
            pl.BlockSpec((None, 8, LANES), full3),
            pl.BlockSpec((NH_B, rows, kw), lambda b, i: (0, 0, 0)),
            pl.BlockSpec((NH_A, DV_A), lambda b, i: (0, 0)),
            pl.BlockSpec(memory_space=pltpu.SMEM),
        ],
        out_specs=(
            pl.BlockSpec((rows, D_MODEL), lambda b, i: (b * nsteps + i, 0)),
            pl.BlockSpec((None, NH_A, DK_A, DV_A), lambda b, i: (b, 0, 0, 0)),
            pl.BlockSpec((None, 8, DK_A), full3),
            pl.BlockSpec((None, 8, LANES), full3),
            pl.BlockSpec((None, WINDOW, NKV_B * DH_B), full3),
            pl.BlockSpec((None, WINDOW, NKV_B * DH_B), full3),
        ),
        scratch_shapes=[
            pltpu.VMEM((NH_A, DK_A, DV_A), F32),
            pltpu.VMEM((8, DK_A), F32),
            pltpu.VMEM((8, LANES), F32),
            pltpu.VMEM((WINDOW, NKV_B * DH_B), F32),
            pltpu.VMEM((WINDOW, NKV_B * DH_B), F32),
        ],
        compiler_params=_cparams(("arbitrary", "arbitrary")),
        name="even_mixer",
    )(proj, k_hist.reshape(batch, WINDOW, NKV_B * DH_B), v_hist.reshape(batch, WINDOW, NKV_B * DH_B),
      c0, n0p, m0p, bias_tab, norm_a.reshape(NH_A, DV_A), sink)
    mix, c1, n1, m1, k1, v1 = outs
    return (mix, c1, n1[:, :NH_A, :], m1[:, :NH_A, 0],
            k1.reshape(batch, WINDOW, NKV_B, DH_B), v1.reshape(batch, WINDOW, NKV_B, DH_B))


def _odd_kernel(proj_ref, s0_ref, hp_ref, normc_ref, mix_ref, sout_ref, s_s, *, rows):
    R = rows
    i = pl.program_id(1)

    @pl.when(i == 0)
    def _():
        s_s[...] = s0_ref[...]

    CS = CHUNK
    P2 = 2 * CS
    gcols = proj_ref[:, O_G:O_G + LANES]
    beta_all = _sigmoid(gcols)
    g_all = -jnp.exp(hp_ref[0:1, :]) * _softplus(gcols + hp_ref[1:2, :])
    rr = lax.broadcasted_iota(I32, (R, R), 0)
    cc = lax.broadcasted_iota(I32, (R, R), 1)
    tri_b = jnp.where((rr >= cc) & (rr // CS == cc // CS), 1.0, 0.0).astype(BF16)
    G_all = _dot_exact_lhs(tri_b, g_all)
    G_t = _tr(G_all)
    eG_all = jnp.exp(G_all)
    r2 = lax.broadcasted_iota(I32, (P2, P2), 0)
    c2 = lax.broadcasted_iota(I32, (P2, P2), 1)
    same_head = (r2 // CS) == (c2 // CS)
    incl2 = same_head & (r2 >= c2)
    strict2 = same_head & (r2 > c2)
    eye2 = jnp.where(r2 == c2, 1.0, 0.0)
    lane_lo = lax.broadcasted_iota(I32, (DK_C, P2), 1) < CS
    scale_c = DK_C ** -0.5

    def qkv(rs0, c0):
        return proj_ref[rs0:rs0 + CS, O_QKV + c0:O_QKV + c0 + LANES]

    def stack(a, b):
        return jnp.concatenate([a, b], axis=0)

    def inv_unit_lower_all(a_list):
        ts = [eye2 - a for a in a_list]
        abs_ = [a.astype(BF16) for a in a_list]
        ps = [_dot(ab, ab) for ab in abs_]
        nlev = (CS - 1).bit_length() - 1
        n = len(ts)
        for lvl in range(nlev):
            th = [t.astype(BF16) for t in ts]
            ph = [p.astype(BF16) for p in ps]
            if lvl < nlev - 1:
                lhs = [stack(th[i], ph[i]) for i in range(n)]
                tp = [_dot(lhs[i], ph[i]) for i in range(n)]
                ts = [ts[i] + tp[i][:P2] for i in range(n)]
                ps = [tp[i][P2:] for i in range(n)]
            else:
                tp = [_dot(th[i], ph[i]) for i in range(n)]
                ts = [ts[i] + tp[i] for i in range(n)]
        return ts

    npair = NH_C // 2
    probs = [(c0, pr) for c0 in range(0, R, CS) for pr in range(npair)]
    nP = len(probs)
    rsl = [slice(c0, c0 + CS) for c0, _ in probs]
    hab = [(2 * pr, 2 * pr + 1) for _, pr in probs]

    def per_head(fn):
        return [[fn(p, h) for h in hab[p]] for p in range(nP)]

    def col2(arr, off):
        return [stack(arr[rsl[p], off + hab[p][0]:off + hab[p][0] + 1], arr[rsl[p], off + hab[p][1]:off + hab[p][1] + 1])
                for p in range(nP)]

    kc = per_head(lambda p, h: qkv(probs[p][0], NH_C * DK_C + h * DK_C))
    qc = per_head(lambda p, h: qkv(probs[p][0], h * DK_C))
    vc = per_head(lambda p, h: qkv(probs[p][0], 2 * NH_C * DK_C + h * DV_C))
    k2 = [stack(*kc[p]) for p in range(nP)]
    q2b = [stack(*[x * scale_c for x in qc[p]]).astype(BF16) for p in range(nP)]
    v2 = [stack(*vc[p]) for p in range(nP)]
    k2b = [k2[p].astype(BF16) for p in range(nP)]
    G_col = col2(G_all, NH_C)
    G_row = [jnp.concatenate([G_t[NH_C + hab[p][0]:NH_C + hab[p][0] + 1, rsl[p]],
                              G_t[NH_C + hab[p][1]:NH_C + hab[p][1] + 1, rsl[p]]], axis=1) for p in range(nP)]
    beta = col2(beta_all, 0)
    eg = col2(eG_all, NH_C)
    GL = [[G_all[probs[p][0] + CS - 1:probs[p][0] + CS, NH_C + h:NH_C + h + 1] for h in hab[p]] for p in range(nP)]
    GL2 = [stack(jnp.broadcast_to(GL[p][0], (CS, 1)), jnp.broadcast_to(GL[p][1], (CS, 1))) for p in range(nP)]
    egl = [[jnp.exp(x) for x in GL[p]] for p in range(nP)]
    kk = [_dot(k2b[p], k2b[p], _NT) for p in range(nP)]
    qk = [_dot(q2b[p], k2b[p], _NT) for p in range(nP)]
    dmat = [jnp.exp(jnp.where(incl2, G_col[p] - G_row[p], -jnp.inf)) for p in range(nP)]
    a_mat = [jnp.where(strict2, beta[p] * dmat[p] * kk[p], 0.0) for p in range(nP)]
    attn = [(qk[p] * dmat[p]).astype(BF16) for p in range(nP)]
    rhs = [jnp.concatenate([beta[p] * v2[p], (beta[p] * eg[p]) * k2[p]], axis=1).astype(BF16) for p in range(nP)]
    kdec = [k2[p] * jnp.exp(GL2[p] - G_col[p]) for p in range(nP)]
    kdec_t = [_tr(kdec[p]).astype(BF16) for p in range(nP)]
    zero_t = jnp.zeros((DK_C, P2), BF16)
    kdec_lo = [jnp.where(lane_lo, kdec_t[p], zero_t) for p in range(nP)]
    kdec_hi = [jnp.where(lane_lo, zero_t, kdec_t[p]) for p in range(nP)]
    tinv = [t.astype(BF16) for t in inv_unit_lower_all(a_mat)]
    uw = [_dot(tinv[p], rhs[p]) for p in range(nP)]
    wq_lhs = [[stack(uw[p][j * CS:(j + 1) * CS, DV_C:].astype(BF16), q2b[p][j * CS:(j + 1) * CS]) for j in range(2)]
              for p in range(nP)]

    S_cur = [s_s[h] for h in range(NH_C)]
    for ci, c0 in enumerate(range(0, R, CS)):
        rs = slice(c0, c0 + CS)
        ps_ = list(range(ci * npair, (ci + 1) * npair))
        Sb = [S_cur[h].astype(BF16) for h in range(NH_C)]
        wq = [[_dot(wq_lhs[p][j], Sb[hab[p][j]]) for j in range(2)] for p in ps_]
        db = [(uw[p][:, :DV_C] - stack(wq[i_][0][:CS], wq[i_][1][:CS])).astype(BF16) for i_, p in enumerate(ps_)]
        upd_lo = [_dot(kdec_lo[p], db[i_]) for i_, p in enumerate(ps_)]
        upd_hi = [_dot(kdec_hi[p], db[i_]) for i_, p in enumerate(ps_)]
        for i_, p in enumerate(ps_):
            S_cur[hab[p][0]] = egl[p][0] * S_cur[hab[p][0]] + upd_lo[i_]
            S_cur[hab[p][1]] = egl[p][1] * S_cur[hab[p][1]] + upd_hi[i_]
        ad = [_dot(attn[p], db[i_]) for i_, p in enumerate(ps_)]
        o2 = [eg[p] * stack(wq[i_][0][CS:], wq[i_][1][CS:]) + ad[i_] for i_, p in enumerate(ps_)]
        oh = [o2[i_][j * CS:(j + 1) * CS] for i_ in range(npair) for j in range(2)]
        ms = [jnp.mean(o * o, axis=-1, keepdims=True) for o in oh]
        zs = [proj_ref[rs, O_Z + h * DV_C:O_Z + (h + 1) * DV_C] for h in range(NH_C)]
        zg = [z * _sigmoid(z) for z in zs]
        for h in range(NH_C):
            o = oh[h] * lax.rsqrt(ms[h] + RMS_EPS) * normc_ref[...] * zg[h]
            mix_ref[rs, h * DV_C:(h + 1) * DV_C] = o.astype(mix_ref.dtype)
    for h in range(NH_C):
        s_s[h] = S_cur[h]

    sout_ref[...] = s_s[...]


def _odd_mixer(proj, s0, a_log, dt_bias, norm_c, batch, seq_len, rows):
    nsteps = seq_len // rows
    hp = jnp.zeros((8, LANES), F32)
    hp = hp.at[0, NH_C:2 * NH_C].set(a_log.astype(F32)).at[1, NH_C:2 * NH_C].set(dt_bias.astype(F32))
    outs = pl.pallas_call(
        functools.partial(_odd_kernel, rows=rows),
        out_shape=(
            jax.ShapeDtypeStruct((batch * seq_len, D_MODEL), BF16),
            jax.ShapeDtypeStruct((batch, NH_C, DK_C, DV_C), F32),
        ),
        grid=(batch, nsteps),
        in_specs=[
            pl.BlockSpec((rows, N_ODD_COLS), lambda b, i: (b * nsteps + i, 0)),
            pl.BlockSpec((None, NH_C, DK_C, DV_C), lambda b, i: (b, 0, 0, 0)),
            pl.BlockSpec((8, LANES), lambda b, i: (0, 0)),
            pl.BlockSpec((1, DV_C), lambda b, i: (0, 0)),
        ],
        out_specs=(
            pl.BlockSpec((rows, D_MODEL), lambda b, i: (b * nsteps + i, 0)),
            pl.BlockSpec((None, NH_C, DK_C, DV_C), lambda b, i: (b, 0, 0, 0)),
        ),
        scratch_shapes=[pltpu.VMEM((NH_C, DK_C, DV_C), F32)],
        compiler_params=_cparams(("arbitrary", "arbitrary")),
        name="odd_mixer",
    )(proj, s0, hp, norm_c.reshape(1, DV_C))
    return outs


def _post_kernel(mix_ref, x_ref, gate_ref, sc_ref, sh_ref, wout_ref, lng_ref, lnb_ref, wr_ref, br_ref,
                 x1_ref, h2_ref, ri_ref, rg_ref, seg_ref, cnt_ref, carry_s, *, tm):
    i = pl.program_id(0)

    @pl.when(i == 0)
    def _():
        carry_s[...] = jnp.zeros_like(carry_s)

    ng = max(1, tm // 256)
    tg = tm // ng
    ys = [_dot(mix_ref[q * tg:(q + 1) * tg, :], wout_ref[...]) for q in range(min(2, ng))]
    wr_b = wr_ref[...].astype(BF16)
    lts = []
    for q in range(ng):
        rq = slice(q * tg, (q + 1) * tg)
        x1 = _layer_norm_rows(ALPHA_DN * x_ref[rq, :] + (1.0 + gate_ref[...]) * ys[q], lng_ref[...], lnb_ref[...])
        if q + 2 < ng:
            ys.append(_dot(mix_ref[(q + 2) * tg:(q + 3) * tg, :], wout_ref[...]))
        x1_ref[rq, :] = x1
        h2 = x1 * (1.0 + sc_ref[...]) + sh_ref[...]
        h2b = h2.astype(BF16)
        h2_ref[rq, :] = h2b
        lts.append(_dot(wr_b, h2b, _NT))
    lt = (lts[0] if ng == 1 else jnp.concatenate(lts, axis=1)) + br_ref[:, 0:1]
    e_iota = lax.broadcasted_iota(I32, (N_EXPERTS, tm), 0).astype(F32)
    vals, idxs = [], []
    for _ in range(TOP_K):
        mx = jnp.max(lt, axis=0, keepdims=True)
        idx = jnp.min(jnp.where(lt == mx, e_iota, float(N_EXPERTS)), axis=0, keepdims=True)
        vals.append(mx)
        idxs.append(idx)
        lt = jnp.where(e_iota == idx, -jnp.inf, lt)
    ex = [jnp.exp(v - vals[0]) for v in vals]
    tot = ex[0] + ex[1] + ex[2] + ex[3]
    hot = [jnp.where(e_iota == idx, 1.0, 0.0) for idx in idxs]
    m_all = hot[0] + hot[1] + hot[2] + hot[3]
    m_all_b = m_all.astype(BF16)
    r = lax.broadcasted_iota(I32, (tm, tm), 0)
    c = lax.broadcasted_iota(I32, (tm, tm), 1)
    upper = jnp.where(r < c, 1.0, 0.0).astype(BF16)
    prefix = _dot(m_all_b, upper)
    re = lax.broadcasted_iota(I32, (N_EXPERTS, N_EXPERTS), 0)
    ce = lax.broadcasted_iota(I32, (N_EXPERTS, N_EXPERTS), 1)
    cnt_col = jnp.sum(m_all, axis=1, keepdims=True)
    m8_col = jnp.floor((cnt_col + 7.0) * 0.125)
    lower_e = jnp.where(re > ce, 1.0, 0.0).astype(BF16)
    off8_col = 8.0 * _dot(lower_e, jnp.broadcast_to(m8_col, (N_EXPERTS, LANES)).astype(BF16))[:, 0:1]
    base = off8_col + prefix
    dests = [jnp.sum(hk * base, axis=0, keepdims=True) for hk in hot]
    cnt_row = _dot(jnp.ones((8, tm), BF16), m_all_b, _NT)
    m8_row = jnp.floor((cnt_row + 7.0) * 0.125)
    upper_e = jnp.where(re < ce, 1.0, 0.0).astype(BF16)
    off8_row = 8.0 * _dot(m8_row.astype(BF16), upper_e)
    gc = carry_s[:, 0:N_EXPERTS]
    srow = lax.broadcasted_iota(I32, (8, N_EXPERTS), 0)
    seg = jnp.where(srow == 0, 8.0 * m8_row, jnp.where(srow == 1, off8_row, jnp.where(srow == 2, gc, 0.0)))
    seg_ref[...] = jnp.concatenate([seg, jnp.zeros((8, LANES - N_EXPERTS), F32)], axis=1).astype(I32)
    carry_s[:, 0:N_EXPERTS] = gc + 8.0 * m8_row
    cnt_ref[...] = carry_s[...]
    ri_ref[...] = jnp.concatenate(dests + [jnp.zeros((4, tm), F32)], axis=0).astype(I32)
    rg_ref[...] = jnp.concatenate([e / tot for e in ex] + [jnp.zeros((4, tm), F32)], axis=0)


def _post(mix, x2d, gate, scale, shift, w_out_bf16, ln_g, ln_b, wr_t, br, seq_len, tm):
    t = x2d.shape[0]
    per = seq_len // tm
    vec = lambda i: (i // per, 0, 0)
    const2 = lambda i: (0, 0)
    return pl.pallas_call(
        functools.partial(_post_kernel, tm=tm),
        out_shape=(
            jax.ShapeDtypeStruct((t, D_MODEL), F32),
            jax.ShapeDtypeStruct((t, D_MODEL), BF16),
            jax.ShapeDtypeStruct((t // tm, 8, tm), I32),
            jax.ShapeDtypeStruct((t // tm, 8, tm), F32),
            jax.ShapeDtypeStruct((t // tm, 8, LANES), I32),
            jax.ShapeDtypeStruct((8, LANES), F32),
        ),
        grid=(t // tm,),
        in_specs=[
            pl.BlockSpec((tm, D_MODEL), lambda i: (i, 0)),
            pl.BlockSpec((tm, D_MODEL), lambda i: (i, 0)),
            pl.BlockSpec((None, 1, D_MODEL), vec),
            pl.BlockSpec((None, 1, D_MODEL), vec),
            pl.BlockSpec((None, 1, D_MODEL), vec),
            pl.BlockSpec((D_MODEL, D_MODEL), const2),
            pl.BlockSpec((1, D_MODEL), const2),
            pl.BlockSpec((1, D_MODEL), const2),
            pl.BlockSpec((N_EXPERTS, D_MODEL), const2),
            pl.BlockSpec((N_EXPERTS, LANES), const2),
        ],
        out_specs=(
            pl.BlockSpec((tm, D_MODEL), lambda i: (i, 0)),
            pl.BlockSpec((tm, D_MODEL), lambda i: (i, 0)),
            pl.BlockSpec((None, 8, tm), lambda i: (i, 0, 0)),
            pl.BlockSpec((None, 8, tm), lambda i: (i, 0, 0)),
            pl.BlockSpec((None, 8, LANES), lambda i: (i, 0, 0)),
            pl.BlockSpec((8, LANES), const2),
        ),
        scratch_shapes=[pltpu.VMEM((8, LANES), F32)],
        compiler_params=_cparams(("arbitrary",)),
        name="post_router",
    )(mix, x2d, gate, scale, shift, w_out_bf16, ln_g, ln_b, wr_t, br)


SEG_FIELDS = 3


def _run_sizes(tm):
    top = 1 << (tm - 1).bit_length()
    return [s for s in (512, 256, 128, 64, 32, 16, 8) if s <= max(top, 8)]


def _for_each_run_piece(seg_ref, rstart_ref, tile, tm, fn):
    base = tile * (SEG_FIELDS * N_EXPERTS)

    def body(e, c):
        n8 = seg_ref[base + e]
        src = seg_ref[base + N_EXPERTS + e]
        dst = rstart_ref[e] + seg_ref[base + 2 * N_EXPERTS + e]
        def pieces(sizes, done):
            for size in sizes:
                @pl.when((n8 & size) != 0)
                def _(done=done, size=size):
                    fn(pl.multiple_of(src + done, 8), pl.multiple_of(dst + done, 8), size)
                done = done + (n8 & size)

        split = 128
        big = [s for s in _run_sizes(tm) if s >= split]
        if big:
            @pl.when(n8 >= split)
            def _():
                pieces(big, jnp.int32(0))
        pieces([s for s in _run_sizes(tm) if s < split], n8 & ~jnp.int32(split - 1))
        return c

    lax.fori_loop(0, N_EXPERTS, body, 0)


def _wait_tile_rows(seg_ref, tile, cb, wait_rows):
    base = tile * (SEG_FIELDS * N_EXPERTS)
    last = N_EXPERTS - 1
    total = seg_ref[base + last] + seg_ref[base + N_EXPERTS + last]
    size = 1 << (cb.bit_length() - 1)
    while size >= 8:
        @pl.when((total & size) != 0)
        def _(size=size):
            wait_rows(size)
        size //= 2


def _dispatch_kernel(zblk_ref, rstart_ref, seg_ref, ri_ref, h_ref, xs_ref, zero_s, cbuf, sem, zsem,
                     *, tm, bm, cb, ntiles):
    i = pl.program_id(0)
    slot = i % 2

    def run_copies(tile, sl, wait):
        if wait:
            def wait_rows(size):
                pltpu.make_async_copy(cbuf.at[sl, pl.ds(0, size), :], xs_ref.at[pl.ds(0, size), :], sem.at[sl]).wait()
            _wait_tile_rows(seg_ref, tile, cb, wait_rows)
            return

        def piece(src, dst, size):
            pltpu.make_async_copy(cbuf.at[sl, pl.ds(src, size), :], xs_ref.at[pl.ds(dst, size), :], sem.at[sl]).start()
        _for_each_run_piece(seg_ref, rstart_ref, tile, tm, piece)

    def zero_copy(e):
        return pltpu.make_async_copy(zero_s, xs_ref.at[pl.ds(zblk_ref[e] * bm, bm), :], zsem)

    @pl.when(i == 0)
    def _():
        zero_s[...] = jnp.zeros_like(zero_s)

        def zstart(e, c):
            @pl.when(zblk_ref[e] >= 0)
            def _():
                zero_copy(e).start()
            return c

        def zwait(e, c):
            @pl.when(zblk_ref[e] >= 0)
            def _():
                zero_copy(e).wait()
            return c

        lax.fori_loop(0, N_EXPERTS, zstart, 0)
        lax.fori_loop(0, N_EXPERTS, zwait, 0)

    @pl.when(i >= 2)
    def _():
        run_copies(i - 2, slot, True)

    rows = lax.broadcasted_iota(I32, (cb, tm), 0)
    hit = rows == ri_ref[0:1, :]
    for k in range(1, TOP_K):
        hit = hit | (rows == ri_ref[k:k + 1, :])
    perm = jnp.where(hit, 1.0, 0.0).astype(BF16)
    cbuf[slot] = _dot(perm, h_ref[...])
    run_copies(i, slot, False)

    @pl.when(i == ntiles - 1)
    def _():
        if ntiles > 1:
            run_copies(i - 1, 1 - slot, True)
        run_copies(i, slot, True)


def _dispatch(h2, route_i, seg_flat, row_start, zblk, n_rows, tm, bm):
    t = h2.shape[0]
    ntiles = t // tm
    cb = TOP_K * tm + 8 * N_EXPERTS
    return pl.pallas_call(
        functools.partial(_dispatch_kernel, tm=tm, bm=bm, cb=cb, ntiles=ntiles),
        out_shape=jax.ShapeDtypeStruct((n_rows, D_MODEL), F32),
        grid_spec=pltpu.PrefetchScalarGridSpec(
            num_scalar_prefetch=3,
            grid=(ntiles,),
            in_specs=[
                pl.BlockSpec((None, 2 * TOP_K, tm), lambda i, z, r, s: (i, 0, 0)),
                pl.BlockSpec((tm, D_MODEL), lambda i, z, r, s: (i, 0)),
            ],
            out_specs=pl.BlockSpec(memory_space=pl.ANY),
            scratch_shapes=[
                pltpu.VMEM((bm, D_MODEL), F32),
                pltpu.VMEM((2, cb, D_MODEL), F32),
                pltpu.SemaphoreType.DMA((2,)),
                pltpu.SemaphoreType.DMA,
            ],
        ),
        compiler_params=_cparams(("arbitrary",)),
        name="dispatch",
    )(zblk, row_start, seg_flat, route_i, h2)


def _ffn_kernel(be_ref, grp_ref, ia_ref, ib_ref, nv_ref, xa_ref, xb_ref, w1_ref, b1_ref, w2_ref, b2_ref,
                ya_ref, yb_ref, w1_s, w2_s):
    b = pl.program_id(0)

    def expert_block(x_ref, y_ref):
        xb = x_ref[...].astype(BF16)
        glu = _dot(xb, w1_s[:, :D_FF]) + b1_ref[:, :D_FF]
        lin = _dot(xb, w1_s[:, D_FF:]) + b1_ref[:, D_FF:]
        glu = jnp.minimum(glu, SWIGLU_LIMIT)
        lin = jnp.clip(lin, -SWIGLU_LIMIT, SWIGLU_LIMIT)
        act = glu * _sigmoid(SWIGLU_ALPHA * glu) * (lin + 1.0)
        y_ref[...] = _dot(act.astype(BF16), w2_s[...]) + b2_ref[...]

    @pl.when(b < nv_ref[0])
    def _():
        @pl.when((b == 0) | (be_ref[b] != be_ref[jnp.maximum(b - 1, 0)]))
        def _():
            step = 256
            for r0 in range(0, D_MODEL, step):
                w1_s[r0:r0 + step, :] = w1_ref[r0:r0 + step, :].astype(BF16)
            for r0 in range(0, D_FF, step):
                w2_s[r0:r0 + step, :] = w2_ref[r0:r0 + step, :].astype(BF16)

        @pl.when(grp_ref[b] == 0)
        def _():
            expert_block(xa_ref, ya_ref)

        @pl.when(grp_ref[b] == 1)
        def _():
            expert_block(xb_ref, yb_ref)


def _ffn(xs_a, xs_b, steps, layer, w1, b1, w2, b2, bm_a, bm_b):
    step_e, step_g, step_ia, step_ib, n_valid = steps
    nsteps = step_e.shape[0]
    row_a = lambda b, be, g, ia, ib, nv: (ia[b], 0)
    row_b = lambda b, be, g, ia, ib, nv: (ib[b], 0)
    wsel = lambda b, be, g, ia, ib, nv: (layer, be[b], 0, 0)
    return pl.pallas_call(
        _ffn_kernel,
        out_shape=(jax.ShapeDtypeStruct(xs_a.shape, F32), jax.ShapeDtypeStruct(xs_b.shape, F32)),
        grid_spec=pltpu.PrefetchScalarGridSpec(
            num_scalar_prefetch=5,
            grid=(nsteps,),
            in_specs=[
                pl.BlockSpec((bm_a, D_MODEL), row_a),
                pl.BlockSpec((bm_b, D_MODEL), row_b),
                pl.BlockSpec((None, None, D_MODEL, 2 * D_FF), wsel),
                pl.BlockSpec((None, None, 1, 2 * D_FF), wsel),
                pl.BlockSpec((None, None, D_FF, D_MODEL), wsel),
                pl.BlockSpec((None, None, 1, D_MODEL), wsel),
            ],
            out_specs=(pl.BlockSpec((bm_a, D_MODEL), row_a), pl.BlockSpec((bm_b, D_MODEL), row_b)),
            scratch_shapes=[pltpu.VMEM((D_MODEL, 2 * D_FF), BF16), pltpu.VMEM((D_FF, D_MODEL), BF16)],
        ),
        compiler_params=_cparams(("arbitrary",)),
        name="expert_ffn",
    )(step_e, step_g, step_ia, step_ib, n_valid, xs_a, xs_b, w1, b1, w2, b2)


def _combine_kernel(rstart_ref, seg_ref, ri_ref, x_ref, gate_ref, rg_ref, lng_ref, lnb_ref, ys_ref, o_ref,
                    ybuf, yb_s, wt_s, sem, *, tm, cb, ntiles):
    i = pl.program_id(0)
    slot = i % 2

    def run_copies(tile, sl, wait):
        if wait:
            def wait_rows(size):
                pltpu.make_async_copy(ys_ref.at[pl.ds(0, size), :], ybuf.at[sl, pl.ds(0, size), :], sem.at[sl]).wait()
            _wait_tile_rows(seg_ref, tile, cb, wait_rows)
            return

        def piece(loc, glob, size):
            pltpu.make_async_copy(ys_ref.at[pl.ds(glob, size), :], ybuf.at[sl, pl.ds(loc, size), :], sem.at[sl]).start()
        _for_each_run_piece(seg_ref, rstart_ref, tile, tm, piece)

    @pl.when(i == 0)
    def _():
        ybuf[...] = jnp.zeros_like(ybuf)
        run_copies(0, 0, False)

    @pl.when(i + 1 < ntiles)
    def _():
        run_copies(i + 1, 1 - slot, False)

    dest_t = _tr(jnp.concatenate([ri_ref[...].astype(F32), jnp.zeros((LANES - 8, tm), F32)], axis=0))
    g_t = _tr(jnp.concatenate([rg_ref[...], jnp.zeros((LANES - 8, tm), F32)], axis=0))
    lane_step = 256
    nq = max(1, tm // 256)
    tq = tm // nq

    def build(q):
        rq = slice(q * tq, (q + 1) * tq)
        for c0 in range(0, cb, lane_step):
            cols = (lax.broadcasted_iota(I32, (tq, lane_step), 1) + c0).astype(F32)
            w = jnp.zeros((tq, lane_step), F32)
            for k in range(TOP_K):
                w = jnp.where(cols == dest_t[rq, k:k + 1], g_t[rq, k:k + 1], w)
            wt_s[rq, c0:c0 + lane_step] = w.astype(BF16)

    def product(q):
        rq = slice(q * tq, (q + 1) * tq)
        return _dot(wt_s[rq, :], yb_s[...])

    def finish(q, moe):
        rq = slice(q * tq, (q + 1) * tq)
        o_ref[rq, :] = _layer_norm_rows(ALPHA_DN * x_ref[rq, :] + (1.0 + gate_ref[...]) * moe,
                                        lng_ref[...], lnb_ref[...])

    build(0)
    run_copies(i, slot, True)
    row_step = 256
    for r0 in range(0, cb, row_step):
        yb_s[r0:r0 + row_step, :] = ybuf[slot, r0:r0 + row_step, :].astype(BF16)
    moes = {}
    for q in range(nq):
        if q + 1 < nq:
            build(q + 1)
        moes[q] = product(q)
        if q >= 1:
            finish(q - 1, moes.pop(q - 1))
    finish(nq - 1, moes.pop(nq - 1))


def _combine(row_start, seg_flat, route_i, x1, gate, route_g, ln_g, ln_b, ys, seq_len, tm):
    t = x1.shape[0]
    per = seq_len // tm
    ntiles = t // tm
    cb = TOP_K * tm + 8 * N_EXPERTS
    return pl.pallas_call(
        functools.partial(_combine_kernel, tm=tm, cb=cb, ntiles=ntiles),
        out_shape=jax.ShapeDtypeStruct((t, D_MODEL), F32),
        grid_spec=pltpu.PrefetchScalarGridSpec(
            num_scalar_prefetch=2,
            grid=(ntiles,),
            in_specs=[
                pl.BlockSpec((None, 2 * TOP_K, tm), lambda i, r, s: (i, 0, 0)),
                pl.BlockSpec((tm, D_MODEL), lambda i, r, s: (i, 0)),
                pl.BlockSpec((None, 1, D_MODEL), lambda i, r, s: (i // per, 0, 0)),
                pl.BlockSpec((None, 8, tm), lambda i, r, s: (i, 0, 0)),
                pl.BlockSpec((1, D_MODEL), lambda i, r, s: (0, 0)),
                pl.BlockSpec((1, D_MODEL), lambda i, r, s: (0, 0)),
                pl.BlockSpec(memory_space=pl.ANY),
            ],
            out_specs=pl.BlockSpec((tm, D_MODEL), lambda i, r, s: (i, 0)),
            scratch_shapes=[
                pltpu.VMEM((2, cb, D_MODEL), F32),
                pltpu.VMEM((cb, D_MODEL), BF16),
                pltpu.VMEM((tm, cb), BF16),
                pltpu.SemaphoreType.DMA((2,)),
            ],
        ),
        compiler_params=_cparams(("arbitrary",)),
        name="combine",
    )(row_start, seg_flat, route_i, x1, gate, route_g, ln_g, ln_b, ys)


def _max_blocks(t, tm, bm):
    return -(-(t * TOP_K + 7 * N_EXPERTS * (t // tm)) // bm) + N_EXPERTS


def _group_tables(counts, bm):
    cnt = counts[0, :N_EXPERTS].astype(I32)
    nblk = (cnt + bm - 1) // bm
    blk_end = jnp.cumsum(nblk)
    row_start = ((blk_end - nblk) * bm).astype(I32)
    zblk = jnp.where((cnt % bm) != 0, blk_end - 1, -1).astype(I32)
    return nblk, row_start, zblk


def _ffn_steps(nblk_a, nblk_b, nsteps):
    per_e = nblk_a + nblk_b
    ends = jnp.cumsum(per_e)
    n_valid = ends[-1]
    s = jnp.minimum(jnp.arange(nsteps, dtype=I32), n_valid - 1)
    step_e = jnp.minimum(jnp.sum((ends[None, :] <= s[:, None]).astype(I32), axis=1), N_EXPERTS - 1)
    onehot = (step_e[:, None] == jnp.arange(N_EXPERTS, dtype=I32)[None, :]).astype(I32)
    off = s - jnp.sum(onehot * (ends - per_e)[None, :], axis=1)
    in_b = off >= jnp.sum(onehot * nblk_a[None, :], axis=1)
    live = jnp.arange(nsteps, dtype=I32) < n_valid
    step_g = jnp.where(live, in_b.astype(I32), 2)
    step_ia = jnp.maximum(jnp.cumsum((step_g == 0).astype(I32)) - 1, 0)
    step_ib = jnp.maximum(jnp.cumsum((step_g == 1).astype(I32)) - 1, 0)
    return (step_e.astype(I32), step_g.astype(I32), step_ia.astype(I32), step_ib.astype(I32),
            n_valid.reshape(1).astype(I32))


def _group_cfg(batch, seq_len):
    if seq_len >= 512:
        return dict(tm=512, rows=256, bm=512)
    return dict(tm=seq_len, rows=seq_len, bm=128)


def kernel(x_prompt, x_sample, c_prompt, c_sample, state_a_C, state_a_n, state_a_m, cache_b_k, cache_b_v,
           state_c_S, state_c_conv, w_ada, b_ada, ln_g, ln_b, w_in_even, b_in_even, norm_a, sink_b, rel_bias,
           w_out_even, w_in_odd, conv_c, a_log_c, dt_bias_c, norm_c, w_out_odd, w_router, b_router,
           w_e1, b_e1, w_e2, b_e2):
    bp, lp, _ = x_prompt.shape
    bs, ls, _ = x_sample.shape
    groups = [dict(b=bp, l=lp, x=x_prompt.reshape(bp * lp, D_MODEL), **_group_cfg(bp, lp)),
              dict(b=bs, l=ls, x=x_sample.reshape(bs * ls, D_MODEL), **_group_cfg(bs, ls))]
    mod = _ada(jnp.concatenate([c_prompt, c_sample], axis=0), w_ada, b_ada)
    offs = [0, bp]
    states = [dict(), dict()]
    b1_all = b_e1.reshape(DEPTH, N_EXPERTS, 1, 2 * D_FF)
    b2_all = b_e2.reshape(DEPTH, N_EXPERTS, 1, D_MODEL)
    for l in range(DEPTH):
        e = l // 2
        if l % 2 == 0:
            w = w_in_even[e]
            sz = (512, 512, 512, 512, 4, 4, 512, 128, 128)
            o = [sum(sz[:j]) for j in range(len(sz) + 1)]
            pad = jnp.zeros((D_MODEL, LANES - 2 * NH_A), w.dtype)
            w_in = jnp.concatenate([w[:, o[0]:o[4]], w[:, o[6]:o[9]], w[:, o[4]:o[6]], pad], axis=1).astype(BF16)
            bb = b_in_even[e]
            b_in = jnp.concatenate([bb[o[0]:o[4]], bb[o[6]:o[9]], bb[o[4]:o[6]],
                                    jnp.zeros((LANES - 2 * NH_A,), bb.dtype)]).reshape(1, N_EVEN_COLS)
            w_out = w_out_even[e].astype(BF16)
        else:
            w = w_in_odd[e]
            pad = jnp.zeros((D_MODEL, LANES - 2 * NH_C), w.dtype)
            w_in = jnp.concatenate([w, pad], axis=1).astype(BF16)
            w_out = w_out_odd[e].astype(BF16)
        wr_t = w_router[l].T
        br = jnp.broadcast_to(b_router[l][:, None], (N_EXPERTS, LANES))
        moe_in = []
        for gi, gr in enumerate(groups):
            nb, sl = gr['b'], gr['l']
            m = mod[l, offs[gi]:offs[gi] + nb].reshape(nb, 6, 1, D_MODEL)
            m6 = [m[:, j] for j in range(6)]
            st = states[gi]
            if l % 2 == 0:
                proj = _inproj(gr['x'], m6[1], m6[0], w_in, b_in, sl, gr['tm'])
                if gi == 0:
                    kh = jnp.zeros((nb, WINDOW, NKV_B, DH_B), F32)
                    vh = kh
                    c0 = jnp.zeros((nb, NH_A, DK_A, DV_A), F32)
                    n0 = jnp.zeros((nb, NH_A, DK_A), F32)
                    m0 = jnp.zeros((nb, NH_A), F32)
                else:
                    kh, vh, c0, n0, m0 = cache_b_k[e], cache_b_v[e], state_a_C[e], state_a_n[e], state_a_m[e]
                bias_tab = _swa_bias_table(rel_bias, gr['rows'])
                mix, c1, n1, m1, k1, v1 = _even_mixer(proj, kh, vh, c0, n0, m0, bias_tab, norm_a[e], sink_b[e],
                                                      nb, sl, gr['rows'], gi == 1)
                for name, val in (('a_C', c1), ('a_n', n1), ('a_m', m1), ('b_k', k1), ('b_v', v1)):
                    st.setdefault(name, []).append(val)
            else:
                if gi == 0:
                    cv0 = jnp.zeros((nb, CONV_W - 1, QKV_C), F32)
                    s0 = jnp.zeros((nb, NH_C, DK_C, DV_C), F32)
                else:
                    cv0, s0 = state_c_conv[e], state_c_S[e]
                proj, cv1 = _inproj_conv(gr['x'], m6[1], m6[0], w_in, cv0, conv_c[e], sl, gr['tm'])
                mix, s1 = _odd_mixer(proj, s0, a_log_c[e], dt_bias_c[e], norm_c[e], nb, sl, gr['rows'])
                st.setdefault('c_S', []).append(s1)
                st.setdefault('c_conv', []).append(cv1)
            x1, h2, route_i, route_g, seg, counts = _post(
                mix, gr['x'], m6[2], m6[4], m6[3], w_out, ln_g[l, 0].reshape(1, D_MODEL),
                ln_b[l, 0].reshape(1, D_MODEL), wr_t, br, sl, gr['tm'])
            t, tm, bm = nb * sl, gr['tm'], gr['bm']
            ntiles = t // tm
            nblk, row_start, zblk = _group_tables(counts, bm)
            seg_flat = seg[:, :SEG_FIELDS, :N_EXPERTS].reshape(ntiles * SEG_FIELDS * N_EXPERTS)
            xs = _dispatch(h2, route_i, seg_flat, row_start, zblk, _max_blocks(t, tm, bm) * bm, tm, bm)
            moe_in.append(dict(xs=xs, nblk=nblk, row_start=row_start, seg_flat=seg_flat, route_i=route_i,
                               route_g=route_g, x1=x1, gate=m6[5]))
        ga, gb = moe_in
        steps = _ffn_steps(ga['nblk'], gb['nblk'], sum(_max_blocks(g['b'] * g['l'], g['tm'], g['bm']) for g in groups))
        ys = _ffn(ga['xs'], gb['xs'], steps, l, w_e1, b1_all, w_e2, b2_all, groups[0]['bm'], groups[1]['bm'])
        for gi, gr in enumerate(groups):
            mi = moe_in[gi]
            gr['x'] = _combine(mi['row_start'], mi['seg_flat'], mi['route_i'], mi['x1'], mi['gate'], mi['route_g'],
                               ln_g[l, 1].reshape(1, D_MODEL), ln_b[l, 1].reshape(1, D_MODEL), ys[gi],
                               gr['l'], gr['tm'])
    outs = [groups[0]['x'].reshape(bp, lp, D_MODEL), groups[1]['x'].reshape(bs, ls, D_MODEL)]
    for gi in range(2):
        for name in ('a_C', 'a_n', 'a_m', 'b_k', 'b_v', 'c_S', 'c_conv'):
            outs.append(jnp.stack(states[gi][name]))
    return tuple(outs)
```

```python
import functools
import math

import jax
import jax.numpy as jnp
from jax import lax
from jax.experimental import pallas as pl
from jax.experimental.pallas import tpu as pltpu

F32 = jnp.float32
BF16 = jnp.bfloat16
I32 = jnp.int32

D_MODEL = 1024
CHUNK = 64
NH_A, DK_A, DV_A = 4, 128, 128
W_A = NH_A * DV_A
NH_B, NKV_B, DH_B, WINDOW = 8, 2, 64, 128
G_B = NH_B // NKV_B
W_B = NH_B * DH_B
NUM_BUCKETS, MAX_DISTANCE = 32, 256
NH_C, DK_C, DV_C, CONV_W = 8, 128, 128, 4
QKV_C = NH_C * (2 * DK_C + DV_C)
N_EXPERTS, TOP_K, D_FF = 32, 4, 1024
SWIGLU_LIMIT, SWIGLU_ALPHA = 7.0, 1.702
DEPTH = 2
ALPHA_DN = (2 * DEPTH) ** 0.25
LN_EPS = 1e-5
RMS_EPS = 1e-6

LANES = 128
E_QA, E_KA, E_VA, E_OA, E_QB, E_KB, E_VB, E_G = 0, 512, 1024, 1536, 2048, 2560, 2688, 2816
N_EVEN_COLS = 2944
O_QKV, O_Z, O_G = 0, 3072, 4096
N_ODD_COLS = 4224

VMEM_LIMIT = 56 * 1024 * 1024


def _cparams(sem):
    return pltpu.CompilerParams(dimension_semantics=sem, vmem_limit_bytes=VMEM_LIMIT)


def _softplus(x):
    return jnp.maximum(x, 0.0) + jnp.log(1.0 + jnp.exp(-jnp.abs(x)))


def _sigmoid(x):
    return 1.0 / (1.0 + jnp.exp(-x))


_NN = (((1,), (0,)), ((), ()))
_NT = (((1,), (1,)), ((), ()))


def _dot(a, b, dims=_NN):
    return lax.dot_general(a, b, dims, preferred_element_type=F32)


def _dot_exact_lhs(a_bf16, b, dims=_NN):
    b0 = b.astype(BF16)
    r1 = b - b0.astype(F32)
    b1 = r1.astype(BF16)
    b2 = (r1 - b1.astype(F32)).astype(BF16)
    return _dot(a_bf16, b0, dims) + _dot(a_bf16, b1, dims) + _dot(a_bf16, b2, dims)


def _tr(x):
    r = x.shape[0]
    rp = -(-r // LANES) * LANES
    if rp != r:
        x = jnp.concatenate([x, jnp.zeros((rp - r, x.shape[1]), x.dtype)], axis=0)
    return x.T[:, :r]


def _tri(n, strict=False):
    r = lax.broadcasted_iota(I32, (n, n), 0)
    c = lax.broadcasted_iota(I32, (n, n), 1)
    return (r > c) if strict else (r >= c)


def _layer_norm_rows(v, g, b):
    mu = jnp.mean(v, axis=-1, keepdims=True)
    d = v - mu
    var = jnp.mean(d * d, axis=-1, keepdims=True)
    return d * lax.rsqrt(var + LN_EPS) * g + b


def _ada_kernel(c_ref, w_ref, b_ref, o_ref):
    c = c_ref[...]
    a = (c * _sigmoid(c)).astype(BF16)
    o_ref[...] = _dot(a, w_ref[...].astype(BF16)) + b_ref[...]


def _ada(c_all, w_ada, b_ada):
    nb = c_all.shape[0]
    tn = 1536
    return pl.pallas_call(
        _ada_kernel,
        out_shape=jax.ShapeDtypeStruct((DEPTH, nb, 6 * D_MODEL), F32),
        grid=(DEPTH, 6 * D_MODEL // tn),
        in_specs=[
            pl.BlockSpec((nb, D_MODEL), lambda l, j: (0, 0)),
            pl.BlockSpec((None, D_MODEL, tn), lambda l, j: (l, 0, j)),
            pl.BlockSpec((None, 1, tn), lambda l, j: (l, 0, j)),
        ],
        out_specs=pl.BlockSpec((None, nb, tn), lambda l, j: (l, 0, j)),
        compiler_params=_cparams(("arbitrary", "arbitrary")),
        name="ada",
    )(c_all, w_ada, b_ada.reshape(DEPTH, 1, 6 * D_MODEL))


def _inproj_kernel(x_ref, sc_ref, sh_ref, w_ref, b_ref, o_ref, *, n_cols, col_step):
    h = (x_ref[...] * (1.0 + sc_ref[...]) + sh_ref[...]).astype(BF16)
    for c0 in range(0, n_cols, col_step):
        c1 = min(c0 + col_step, n_cols)
        o_ref[:, c0:c1] = _dot(h, w_ref[:, c0:c1]) + b_ref[:, c0:c1]


def _inproj(x2d, scale, shift, w_bf16, bias, seq_len, tm):
    t = x2d.shape[0]
    n = w_bf16.shape[1]
    per = seq_len // tm
    return pl.pallas_call(
        functools.partial(_inproj_kernel, n_cols=n, col_step=1024),
        out_shape=jax.ShapeDtypeStruct((t, n), F32),
        grid=(t // tm,),
        in_specs=[
            pl.BlockSpec((tm, D_MODEL), lambda i: (i, 0)),
            pl.BlockSpec((None, 1, D_MODEL), lambda i: (i // per, 0, 0)),
            pl.BlockSpec((None, 1, D_MODEL), lambda i: (i // per, 0, 0)),
            pl.BlockSpec((D_MODEL, n), lambda i: (0, 0)),
            pl.BlockSpec((1, n), lambda i: (0, 0)),
        ],
        out_specs=pl.BlockSpec((tm, n), lambda i: (i, 0)),
        compiler_params=_cparams(("arbitrary",)),
        name="inproj",
    )(x2d, scale, shift, w_bf16, bias)


CONV_HB = 8


def _inproj_conv_kernel(x_ref, sc_ref, sh_ref, w_ref, conv0_ref, convw_ref, o_ref, convout_ref, xa_s,
                        *, tm, per):
    i = pl.program_id(0)
    HB = CONV_HB

    @pl.when(i % per == 0)
    def _():
        xa_s[0:HB - (CONV_W - 1), :] = jnp.zeros((HB - (CONV_W - 1), QKV_C), F32)
        xa_s[HB - (CONV_W - 1):HB, :] = conv0_ref[...]

    h = (x_ref[...] * (1.0 + sc_ref[...]) + sh_ref[...]).astype(BF16)
    step = 512

    def project(c0):
        return _dot(h, w_ref[:, c0:c0 + step])

    def conv_group(g0, pre):
        xa = jnp.concatenate([xa_s[:, g0:g0 + step], pre], axis=0)
        for c in range(0, step, LANES):
            c0 = g0 + c
            xc = xa[:, c:c + LANES]
            y = None
            for j in range(CONV_W):
                back = xc if j == CONV_W - 1 else pltpu.roll(xc, CONV_W - 1 - j, 0)
                term = back[HB:HB + tm, :] * convw_ref[j:j + 1, c0:c0 + LANES]
                y = term if y is None else y + term
            y = y * _sigmoid(y)
            if c0 < 2 * NH_C * DK_C:
                y = y * lax.rsqrt(jnp.sum(y * y, axis=-1, keepdims=True) + RMS_EPS)
            o_ref[:, c0:c0 + LANES] = y
        convout_ref[:, g0:g0 + step] = pre[tm - (CONV_W - 1):tm, :]
        return pre[tm - HB:tm, :]

    groups = list(range(0, QKV_C, step))
    rest = list(range(O_Z, N_ODD_COLS, 256))
    pres = {0: project(groups[0])}
    tails = []
    for gi, g0 in enumerate(groups):
        if gi + 1 < len(groups):
            pres[gi + 1] = project(groups[gi + 1])
        for c0 in rest[gi::len(groups)]:
            c1 = min(c0 + 256, N_ODD_COLS)
            o_ref[:, c0:c1] = _dot(h, w_ref[:, c0:c1])
        tails.append(conv_group(g0, pres.pop(gi)))
    for g0, tail in zip(groups, tails):
        xa_s[:, g0:g0 + step] = tail


def _inproj_conv(x2d, scale, shift, w_bf16, conv_hist, conv_w, seq_len, tm):
    t = x2d.shape[0]
    per = seq_len // tm
    batch = t // seq_len
    return pl.pallas_call(
        functools.partial(_inproj_conv_kernel, tm=tm, per=per),
        out_shape=(jax.ShapeDtypeStruct((t, N_ODD_COLS), F32),
                   jax.ShapeDtypeStruct((batch, CONV_W - 1, QKV_C), F32)),
        grid=(t // tm,),
        in_specs=[
            pl.BlockSpec((tm, D_MODEL), lambda i: (i, 0)),
            pl.BlockSpec((None, 1, D_MODEL), lambda i: (i // per, 0, 0)),
            pl.BlockSpec((None, 1, D_MODEL), lambda i: (i // per, 0, 0)),
            pl.BlockSpec((D_MODEL, N_ODD_COLS), lambda i: (0, 0)),
            pl.BlockSpec((None, CONV_W - 1, QKV_C), lambda i: (i // per, 0, 0)),
            pl.BlockSpec((CONV_W, QKV_C), lambda i: (0, 0)),
        ],
        out_specs=(pl.BlockSpec((tm, N_ODD_COLS), lambda i: (i, 0)),
                   pl.BlockSpec((None, CONV_W - 1, QKV_C), lambda i: (i // per, 0, 0))),
        scratch_shapes=[pltpu.VMEM((CONV_HB, QKV_C), F32)],
        compiler_params=_cparams(("arbitrary",)),
        name="inproj_conv",
    )(x2d, scale, shift, w_bf16, conv_hist, conv_w)


def _even_kernel(proj_ref, kh0_ref, vh0_ref, c0_ref, n0_ref, m0_ref, bias_ref, norma_ref, sink_ref,
                 mix_ref, cout_ref, nout_ref, mout_ref, kout_ref, vout_ref,
                 c_s, n_s, m_s, kh_s, vh_s, *, rows, hist_valid):
    R = rows
    KW = WINDOW + R
    i = pl.program_id(1)

    @pl.when(i == 0)
    def _():
        c_s[...] = c0_ref[...]
        n_s[...] = n0_ref[...]
        m_s[...] = m0_ref[...]
        kh_s[...] = kh0_ref[...]
        vh_s[...] = vh0_ref[...]

    CS = CHUNK
    g = proj_ref[:, E_G:E_G + LANES]
    lf = -_softplus(-g)
    rr = lax.broadcasted_iota(I32, (R, R), 0)
    cc = lax.broadcasted_iota(I32, (R, R), 1)
    tri_b = jnp.where((rr >= cc) & (rr // CS == cc // CS), 1.0, 0.0).astype(BF16)
    b_all = _dot_exact_lhs(tri_b, lf)
    g_t = _tr(g)
    b_t = _tr(b_all)
    causal = _tri(CS)
    scale_a = DK_A ** -0.5
    chunks = list(range(0, R, CS))
    P = [(c0, h) for c0 in chunks for h in range(NH_A)]
    nP = len(P)
    rsl = [slice(c0, c0 + CS) for c0, _ in P]
    b_col = [b_all[rsl[p], NH_A + h:NH_A + h + 1] for p, (_, h) in enumerate(P)]
    b_row = [b_t[NH_A + h:NH_A + h + 1, rsl[p]] for p, (_, h) in enumerate(P)]
    i_row = [g_t[h:h + 1, rsl[p]] for p, (_, h) in enumerate(P)]
    i_col = [g[rsl[p], h:h + 1] for p, (_, h) in enumerate(P)]
    logw = [jnp.where(causal, b_col[p] - b_row[p] + i_row[p], -jnp.inf) for p in range(nP)]
    lmax = [jnp.max(logw[p], axis=-1, keepdims=True) for p in range(nP)]
    qb = [(proj_ref[rsl[p], E_QA + h * DK_A:E_QA + (h + 1) * DK_A] * scale_a).astype(BF16)
          for p, (_, h) in enumerate(P)]
    kf = [proj_ref[rsl[p], E_KA + h * DK_A:E_KA + (h + 1) * DK_A] for p, (_, h) in enumerate(P)]
    vb = [proj_ref[rsl[p], E_VA + h * DV_A:E_VA + (h + 1) * DV_A].astype(BF16) for p, (_, h) in enumerate(P)]
    qk = [_dot(qb[p], kf[p].astype(BF16), _NT) for p in range(nP)]
    b_last = [b_col[p][CS - 1:CS, :] for p in range(nP)]

    k_win = jnp.concatenate([kh_s[...], proj_ref[:, E_KB:E_KB + NKV_B * DH_B]], axis=0)
    v_win = jnp.concatenate([vh_s[...], proj_ref[:, E_VB:E_VB + NKV_B * DH_B]], axis=0)
    k_win_b = k_win.astype(BF16)
    v_win_b = v_win.astype(BF16)
    if not hist_valid:
        key_pos = lax.broadcasted_iota(I32, (R, KW), 1) + (i * R - WINDOW)
        key_ok = key_pos >= 0
    scale_b = DH_B ** -0.5
    lo = lax.broadcasted_iota(I32, (1, LANES), 1) < DH_B
    k_swp = pltpu.roll(k_win, DH_B, 1).astype(BF16)
    v_swp = pltpu.roll(v_win, DH_B, 1).astype(BF16)
    zk = jnp.zeros_like(k_win_b)
    k_both = [jnp.where(lo, k_win_b, k_swp), jnp.where(lo, k_swp, k_win_b)]
    v_lo = [jnp.where(lo, v_win_b, zk), jnp.where(lo, v_swp, zk)]
    v_hi = [jnp.where(lo, zk, v_swp), jnp.where(lo, zk, v_win_b)]
    npairs = NH_B // 2
    qp = [proj_ref[:, E_QB + j * LANES:E_QB + (j + 1) * LANES].astype(BF16) for j in range(npairs)]
    zq = jnp.zeros_like(qp[0])
    qh = [jnp.where(lo, qp[hd // 2], zq) if hd % 2 == 0 else jnp.where(lo, zq, qp[hd // 2]) for hd in range(NH_B)]
    scores = [_dot(qh[hd], k_both[hd // G_B], _NT) * scale_b + bias_ref[hd] for hd in range(NH_B)]
    if not hist_valid:
        scores = [jnp.where(key_ok, s, -jnp.inf) for s in scores]
    sks = [sink_ref[hd] for hd in range(NH_B)]
    mxs = [jnp.maximum(jnp.max(scores[hd], axis=-1, keepdims=True), sks[hd]) for hd in range(NH_B)]
    ps = [jnp.exp(scores[hd] - mxs[hd]) for hd in range(NH_B)]
    dens = [jnp.sum(ps[hd], axis=-1, keepdims=True) + jnp.exp(sks[hd] - mxs[hd]) for hd in range(NH_B)]
    pbs = [(ps[hd] / dens[hd]).astype(BF16) for hd in range(NH_B)]

    m_prev, m_inter, m_t, m_new = [None] * nP, [None] * nP, [None] * nP, [None] * nP
    for h in range(NH_A):
        run = m_s[h:h + 1, 0:1]
        for ci in range(len(chunks)):
            p = ci * NH_A + h
            m_prev[p] = run
            m_inter[p] = b_col[p] + run
            m_t[p] = jnp.maximum(m_inter[p], lmax[p])
            run = m_t[p][CS - 1:CS, :]
            m_new[p] = run
        m_s[h:h + 1, :] = jnp.broadcast_to(run, (1, LANES))
    w = [jnp.exp(logw[p] - m_t[p]) for p in range(nP)]
    s = [qk[p] * w[p] for p in range(nP)]
    sv = [_dot(s[p].astype(BF16), vb[p]) for p in range(nP)]
    ssum = [jnp.sum(s[p], axis=-1, keepdims=True) for p in range(nP)]
    kfac = [jnp.exp(b_last[p] - b_col[p] + i_col[p] - m_new[p]) for p in range(nP)]
    kw = [kf[p] * kfac[p] for p in range(nP)]
    kwt = [_tr(kw[p]).astype(BF16) for p in range(nP)]
    kv = [_dot(kwt[p], vb[p]) for p in range(nP)]
    ksum = [jnp.sum(kw[p], axis=0, keepdims=True) for p in range(nP)]
    carry = [jnp.exp(b_last[p] + m_prev[p] - m_new[p]) for p in range(nP)]
    dec = [jnp.exp(m_inter[p] - m_t[p]) for p in range(nP)]
    floor_ = [jnp.exp(-m_t[p]) for p in range(nP)]
    c_prev, n_prev = [None] * nP, [None] * nP
    for h in range(NH_A):
        n_run = n_s[h:h + 1, :]
        c_run = c_s[h]
        for ci in range(len(chunks)):
            p = ci * NH_A + h
            c_prev[p], n_prev[p] = c_run, n_run
            c_run = carry[p] * c_run + kv[p]
            n_run = carry[p] * n_run + ksum[p]
        c_s[h] = c_run
        n_s[h:h + 1, :] = n_run
    qc = [_dot(qb[p], c_prev[p].astype(BF16)) for p in range(nP)]
    qn = [jnp.sum(qb[p].astype(F32) * n_prev[p].astype(BF16).astype(F32), axis=-1, keepdims=True) for p in range(nP)]
    num = [dec[p] * qc[p] + sv[p] for p in range(nP)]
    den = [dec[p] * qn[p] + ssum[p] for p in range(nP)]
    hh = [num[p] / jnp.maximum(jnp.abs(den[p]), floor_[p]) for p in range(nP)]
    mu = [jnp.mean(hh[p], axis=-1, keepdims=True) for p in range(nP)]
    dd = [hh[p] - mu[p] for p in range(nP)]
    var = [jnp.mean(dd[p] * dd[p], axis=-1, keepdims=True) for p in range(nP)]
    rs_ = [lax.rsqrt(var[p] + LN_EPS) for p in range(nP)]
    for p, (_, h) in enumerate(P):
        og = proj_ref[rsl[p], E_OA + h * DV_A:E_OA + (h + 1) * DV_A]
        ha = dd[p] * rs_[p] * norma_ref[h:h + 1, :] * _sigmoid(og)
        mix_ref[rsl[p], h * DV_A:(h + 1) * DV_A] = ha.astype(mix_ref.dtype)

    cout_ref[...] = c_s[...]
    nout_ref[...] = n_s[...]
    mout_ref[...] = m_s[...]

    outs = [_dot(pbs[hd], (v_lo if hd % 2 == 0 else v_hi)[hd // G_B]) for hd in range(NH_B)]
    for j in range(npairs):
        mix_ref[:, W_A + j * LANES:W_A + (j + 1) * LANES] = (outs[2 * j] + outs[2 * j + 1]).astype(mix_ref.dtype)

    kh_s[...] = k_win[R:, :]
    vh_s[...] = v_win[R:, :]
    kout_ref[...] = kh_s[...]
    vout_ref[...] = vh_s[...]


def _rel_bucket(rel):
    nb = NUM_BUCKETS // 2
    max_exact = nb // 2
    n = jnp.abs(rel)
    nf = jnp.maximum(n, 1).astype(F32)
    large = max_exact + (jnp.log(nf / max_exact) / math.log(MAX_DISTANCE / max_exact)
                         * (nb - max_exact)).astype(I32)
    large = jnp.minimum(large, nb - 1)
    return jnp.where(rel > 0, nb, 0) + jnp.where(n < max_exact, n, large)


def _swa_bias_table(rel_bias, rows):
    kw = WINDOW + rows
    qi = jnp.arange(rows)[:, None]
    kj = jnp.arange(kw)[None, :]
    bucket = _rel_bucket(kj - WINDOW - qi)
    rb = rel_bias.astype(F32)
    bias = jnp.zeros((NH_B, rows, kw), F32)
    for b in range(NUM_BUCKETS):
        bias = jnp.where((bucket == b)[None], rb[b][:, None, None], bias)
    lo = (qi // CHUNK) * CHUNK
    ok = (kj >= lo) & (kj < lo + WINDOW + CHUNK)
    return jnp.where(ok[None], bias, -jnp.inf)


def _even_mixer(proj, k_hist, v_hist, c0, n0, m0, bias_tab, norm_a, sink, batch, seq_len, rows, hist_valid):
    nsteps = seq_len // rows
    kw = WINDOW + rows
    m0p = jnp.broadcast_to(jnp.pad(m0, ((0, 0), (0, 8 - NH_A)))[:, :, None], (batch, 8, LANES))
    n0p = jnp.pad(n0, ((0, 0), (0, 8 - NH_A), (0, 0)))
    full3 = lambda b, i: (b, 0, 0)
    outs = pl.pallas_call(
        functools.partial(_even_kernel, rows=rows, hist_valid=hist_valid),
        out_shape=(
            jax.ShapeDtypeStruct((batch * seq_len, D_MODEL), BF16),
            jax.ShapeDtypeStruct((batch, NH_A, DK_A, DV_A), F32),
            jax.ShapeDtypeStruct((batch, 8, DK_A), F32),
            jax.ShapeDtypeStruct((batch, 8, LANES), F32),
            jax.ShapeDtypeStruct((batch, WINDOW, NKV_B * DH_B), F32),
            jax.ShapeDtypeStruct((batch, WINDOW, NKV_B * DH_B), F32),
        ),
        grid=(batch, nsteps),
        in_specs=[
            pl.BlockSpec((rows, N_EVEN_COLS), lambda b, i: (b * nsteps + i, 0)),
            pl.BlockSpec((None, WINDOW, NKV_B * DH_B), full3),
            pl.BlockSpec((None, WINDOW, NKV_B * DH_B), full3),
            pl.BlockSpec((None, NH_A, DK_A, DV_A), lambda b, i: (b, 0, 0, 0)),
            pl.BlockSpec((None, 8, DK_A), full3),
            pl.BlockSpec((None, 8, LANES), full3),
            pl.BlockSpec((NH_B, rows, kw), lambda b, i: (0, 0, 0)),
            pl.BlockSpec((NH_A, DV_A), lambda b, i: (0, 0)),
            pl.BlockSpec(memory_space=pltpu.SMEM),
        ],
        out_specs=(
            pl.BlockSpec((rows, D_MODEL), lambda b, i: (b * nsteps + i, 0)),
            pl.BlockSpec((None, NH_A, DK_A, DV_A), lambda b, i: (b, 0, 0, 0)),
            pl.BlockSpec((None, 8, DK_A), full3),
            pl.BlockSpec((None, 8, LANES), full3),
            pl.BlockSpec((None, WINDOW, NKV_B * DH_B), full3),
            pl.BlockSpec((None, WINDOW, NKV_B * DH_B), full3),
        ),
        scratch_shapes=[
            pltpu.VMEM((NH_A, DK_A, DV_A), F32),
            pltpu.VMEM((8, DK_A), F32),
            pltpu.VMEM((8, LANES), F32),
            pltpu.VMEM((WINDOW, NKV_B * DH_B), F32),
            pltpu.VMEM((WINDOW, NKV_B * DH_B), F32),
        ],
        compiler_params=_cparams(("arbitrary", "arbitrary")),
        name="even_mixer",
    )(proj, k_hist.reshape(batch, WINDOW, NKV_B * DH_B), v_hist.reshape(batch, WINDOW, NKV_B * DH_B),
      c0, n0p, m0p, bias_tab, norm_a.reshape(NH_A, DV_A), sink)
    mix, c1, n1, m1, k1, v1 = outs
    return (mix, c1, n1[:, :NH_A, :], m1[:, :NH_A, 0],
            k1.reshape(batch, WINDOW, NKV_B, DH_B), v1.reshape(batch, WINDOW, NKV_B, DH_B))


def _odd_kernel(proj_ref, s0_ref, hp_ref, normc_ref, mix_ref, sout_ref, s_s, *, rows):
    R = rows
    i = pl.program_id(1)

    @pl.when(i == 0)
    def _():
        s_s[...] = s0_ref[...]

    CS = CHUNK
    P2 = 2 * CS
    gcols = proj_ref[:, O_G:O_G + LANES]
    beta_all = _sigmoid(gcols)
    g_all = -jnp.exp(hp_ref[0:1, :]) * _softplus(gcols + hp_ref[1:2, :])
    rr = lax.broadcasted_iota(I32, (R, R), 0)
    cc = lax.broadcasted_iota(I32, (R, R), 1)
    tri_b = jnp.where((rr >= cc) & (rr // CS == cc // CS), 1.0, 0.0).astype(BF16)
    G_all = _dot_exact_lhs(tri_b, g_all)
    G_t = _tr(G_all)
    eG_all = jnp.exp(G_all)
    r2 = lax.broadcasted_iota(I32, (P2, P2), 0)
    c2 = lax.broadcasted_iota(I32, (P2, P2), 1)
    same_head = (r2 // CS) == (c2 // CS)
    incl2 = same_head & (r2 >= c2)
    strict2 = same_head & (r2 > c2)
    eye2 = jnp.where(r2 == c2, 1.0, 0.0)
    lane_lo = lax.broadcasted_iota(I32, (DK_C, P2), 1) < CS
    scale_c = DK_C ** -0.5

    def qkv(rs0, c0):
        return proj_ref[rs0:rs0 + CS, O_QKV + c0:O_QKV + c0 + LANES]

    def stack(a, b):
        return jnp.concatenate([a, b], axis=0)

    def inv_unit_lower_all(a_list):
        ts = [eye2 - a for a in a_list]
        abs_ = [a.astype(BF16) for a in a_list]
        ps = [_dot(ab, ab) for ab in abs_]
        nlev = (CS - 1).bit_length() - 1
        n = len(ts)
        for lvl in range(nlev):
            th = [t.astype(BF16) for t in ts]
            ph = [p.astype(BF16) for p in ps]
            if lvl < nlev - 1:
                lhs = [stack(th[i], ph[i]) for i in range(n)]
                tp = [_dot(lhs[i], ph[i]) for i in range(n)]
                ts = [ts[i] + tp[i][:P2] for i in range(n)]
                ps = [tp[i][P2:] for i in range(n)]
            else:
                tp = [_dot(th[i], ph[i]) for i in range(n)]
                ts = [ts[i] + tp[i] for i in range(n)]
        return ts

    npair = NH_C // 2
    probs = [(c0, pr) for c0 in range(0, R, CS) for pr in range(npair)]
    nP = len(probs)
    rsl = [slice(c0, c0 + CS) for c0, _ in probs]
    hab = [(2 * pr, 2 * pr + 1) for _, pr in probs]

    def per_head(fn):
        return [[fn(p, h) for h in hab[p]] for p in range(nP)]

    def col2(arr, off):
        return [stack(arr[rsl[p], off + hab[p][0]:off + hab[p][0] + 1], arr[rsl[p], off + hab[p][1]:off + hab[p][1] + 1])
                for p in range(nP)]

    kc = per_head(lambda p, h: qkv(probs[p][0], NH_C * DK_C + h * DK_C))
    qc = per_head(lambda p, h: qkv(probs[p][0], h * DK_C))
    vc = per_head(lambda p, h: qkv(probs[p][0], 2 * NH_C * DK_C + h * DV_C))
    k2 = [stack(*kc[p]) for p in range(nP)]
    q2b = [stack(*[x * scale_c for x in qc[p]]).astype(BF16) for p in range(nP)]
    v2 = [stack(*vc[p]) for p in range(nP)]
    k2b = [k2[p].astype(BF16) for p in range(nP)]
    G_col = col2(G_all, NH_C)
    G_row = [jnp.concatenate([G_t[NH_C + hab[p][0]:NH_C + hab[p][0] + 1, rsl[p]],
                              G_t[NH_C + hab[p][1]:NH_C + hab[p][1] + 1, rsl[p]]], axis=1) for p in range(nP)]
    beta = col2(beta_all, 0)
    eg = col2(eG_all, NH_C)
    GL = [[G_all[probs[p][0] + CS - 1:probs[p][0] + CS, NH_C + h:NH_C + h + 1] for h in hab[p]] for p in range(nP)]
    GL2 = [stack(jnp.broadcast_to(GL[p][0], (CS, 1)), jnp.broadcast_to(GL[p][1], (CS, 1))) for p in range(nP)]
    egl = [[jnp.exp(x) for x in GL[p]] for p in range(nP)]
    kk = [_dot(k2b[p], k2b[p], _NT) for p in range(nP)]
    qk = [_dot(q2b[p], k2b[p], _NT) for p in range(nP)]
    dmat = [jnp.exp(jnp.where(incl2, G_col[p] - G_row[p], -jnp.inf)) for p in range(nP)]
    a_mat = [jnp.where(strict2, beta[p] * dmat[p] * kk[p], 0.0) for p in range(nP)]
    attn = [(qk[p] * dmat[p]).astype(BF16) for p in range(nP)]
    rhs = [jnp.concatenate([beta[p] * v2[p], (beta[p] * eg[p]) * k2[p]], axis=1).astype(BF16) for p in range(nP)]
    kdec = [k2[p] * jnp.exp(GL2[p] - G_col[p]) for p in range(nP)]
    kdec_t = [_tr(kdec[p]).astype(BF16) for p in range(nP)]
    zero_t = jnp.zeros((DK_C, P2), BF16)
    kdec_lo = [jnp.where(lane_lo, kdec_t[p], zero_t) for p in range(nP)]
    kdec_hi = [jnp.where(lane_lo, zero_t, kdec_t[p]) for p in range(nP)]
    tinv = [t.astype(BF16) for t in inv_unit_lower_all(a_mat)]
    uw = [_dot(tinv[p], rhs[p]) for p in range(nP)]
    wq_lhs = [[stack(uw[p][j * CS:(j + 1) * CS, DV_C:].astype(BF16), q2b[p][j * CS:(j + 1) * CS]) for j in range(2)]
              for p in range(nP)]

    S_cur = [s_s[h] for h in range(NH_C)]
    for ci, c0 in enumerate(range(0, R, CS)):
        rs = slice(c0, c0 + CS)
        ps_ = list(range(ci * npair, (ci + 1) * npair))
        Sb = [S_cur[h].astype(BF16) for h in range(NH_C)]
        wq = [[_dot(wq_lhs[p][j], Sb[hab[p][j]]) for j in range(2)] for p in ps_]
        db = [(uw[p][:, :DV_C] - stack(wq[i_][0][:CS], wq[i_][1][:CS])).astype(BF16) for i_, p in enumerate(ps_)]
        upd_lo = [_dot(kdec_lo[p], db[i_]) for i_, p in enumerate(ps_)]
        upd_hi = [_dot(kdec_hi[p], db[i_]) for i_, p in enumerate(ps_)]
        for i_, p in enumerate(ps_):
            S_cur[hab[p][0]] = egl[p][0] * S_cur[hab[p][0]] + upd_lo[i_]
            S_cur[hab[p][1]] = egl[p][1] * S_cur[hab[p][1]] + upd_hi[i_]
        ad = [_dot(attn[p], db[i_]) for i_, p in enumerate(ps_)]
        o2 = [eg[p] * stack(wq[i_][0][CS:], wq[i_][1][CS:]) + ad[i_] for i_, p in enumerate(ps_)]
        oh = [o2[i_][j * CS:(j + 1) * CS] for i_ in range(npair) for j in range(2)]
        ms = [jnp.mean(o * o, axis=-1, keepdims=True) for o in oh]
        zs = [proj_ref[rs, O_Z + h * DV_C:O_Z + (h + 1) * DV_C] for h in range(NH_C)]
        zg = [z * _sigmoid(z) for z in zs]
        for h in range(NH_C):
            o = oh[h] * lax.rsqrt(ms[h] + RMS_EPS) * normc_ref[...] * zg[h]
            mix_ref[rs, h * DV_C:(h + 1) * DV_C] = o.astype(mix_ref.dtype)
    for h in range(NH_C):
        s_s[h] = S_cur[h]

    sout_ref[...] = s_s[...]


def _odd_mixer(proj, s0, a_log, dt_bias, norm_c, batch, seq_len, rows):
    nsteps = seq_len // rows
    hp = jnp.zeros((8, LANES), F32)
    hp = hp.at[0, NH_C:2 * NH_C].set(a_log.astype(F32)).at[1, NH_C:2 * NH_C].set(dt_bias.astype(F32))
    outs = pl.pallas_call(
        functools.partial(_odd_kernel, rows=rows),
        out_shape=(
            jax.ShapeDtypeStruct((batch * seq_len, D_MODEL), BF16),
            jax.ShapeDtypeStruct((batch, NH_C, DK_C, DV_C), F32),
        ),
        grid=(batch, nsteps),
        in_specs=[
            pl.BlockSpec((rows, N_ODD_COLS), lambda b, i: (b * nsteps + i, 0)),
            pl.BlockSpec((None, NH_C, DK_C, DV_C), lambda b, i: (b, 0, 0, 0)),
            pl.BlockSpec((8, LANES), lambda b, i: (0, 0)),
            pl.BlockSpec((1, DV_C), lambda b, i: (0, 0)),
        ],
        out_specs=(
            pl.BlockSpec((rows, D_MODEL), lambda b, i: (b * nsteps + i, 0)),
            pl.BlockSpec((None, NH_C, DK_C, DV_C), lambda b, i: (b, 0, 0, 0)),
        ),
        scratch_shapes=[pltpu.VMEM((NH_C, DK_C, DV_C), F32)],
        compiler_params=_cparams(("arbitrary", "arbitrary")),
        name="odd_mixer",
    )(proj, s0, hp, norm_c.reshape(1, DV_C))
    return outs


def _post_kernel(mix_ref, x_ref, gate_ref, sc_ref, sh_ref, wout_ref, lng_ref, lnb_ref, wr_ref, br_ref,
                 x1_ref, h2_ref, ri_ref, rg_ref, seg_ref, cnt_ref, carry_s, *, tm):
    i = pl.program_id(0)

    @pl.when(i == 0)
    def _():
        carry_s[...] = jnp.zeros_like(carry_s)

    ng = max(1, tm // 256)
    tg = tm // ng
    ys = [_dot(mix_ref[q * tg:(q + 1) * tg, :], wout_ref[...]) for q in range(min(2, ng))]
    wr_b = wr_ref[...].astype(BF16)
    lts = []
    for q in range(ng):
        rq = slice(q * tg, (q + 1) * tg)
        x1 = _layer_norm_rows(ALPHA_DN * x_ref[rq, :] + (1.0 + gate_ref[...]) * ys[q], lng_ref[...], lnb_ref[...])
        if q + 2 < ng:
            ys.append(_dot(mix_ref[(q + 2) * tg:(q + 3) * tg, :], wout_ref[...]))
        x1_ref[rq, :] = x1
        h2 = x1 * (1.0 + sc_ref[...]) + sh_ref[...]
        h2b = h2.astype(BF16)
        h2_ref[rq, :] = h2b
        lts.append(_dot(wr_b, h2b, _NT))
    lt = (lts[0] if ng == 1 else jnp.concatenate(lts, axis=1)) + br_ref[:, 0:1]
    e_iota = lax.broadcasted_iota(I32, (N_EXPERTS, tm), 0).astype(F32)
    vals, idxs = [], []
    for _ in range(TOP_K):
        mx = jnp.max(lt, axis=0, keepdims=True)
        idx = jnp.min(jnp.where(lt == mx, e_iota, float(N_EXPERTS)), axis=0, keepdims=True)
        vals.append(mx)
        idxs.append(idx)
        lt = jnp.where(e_iota == idx, -jnp.inf, lt)
    ex = [jnp.exp(v - vals[0]) for v in vals]
    tot = ex[0] + ex[1] + ex[2] + ex[3]
    hot = [jnp.where(e_iota == idx, 1.0, 0.0) for idx in idxs]
    m_all = hot[0] + hot[1] + hot[2] + hot[3]
    m_all_b = m_all.astype(BF16)
    r = lax.broadcasted_iota(I32, (tm, tm), 0)
    c = lax.broadcasted_iota(I32, (tm, tm), 1)
    upper = jnp.where(r < c, 1.0, 0.0).astype(BF16)
    prefix = _dot(m_all_b, upper)
    re = lax.broadcasted_iota(I32, (N_EXPERTS, N_EXPERTS), 0)
    ce = lax.broadcasted_iota(I32, (N_EXPERTS, N_EXPERTS), 1)
    cnt_col = jnp.sum(m_all, axis=1, keepdims=True)
    m8_col = jnp.floor((cnt_col + 7.0) * 0.125)
    lower_e = jnp.where(re > ce, 1.0, 0.0).astype(BF16)
    off8_col = 8.0 * _dot(lower_e, jnp.broadcast_to(m8_col, (N_EXPERTS, LANES)).astype(BF16))[:, 0:1]
    base = off8_col + prefix
    dests = [jnp.sum(hk * base, axis=0, keepdims=True) for hk in hot]
    cnt_row = _dot(jnp.ones((8, tm), BF16), m_all_b, _NT)
    m8_row = jnp.floor((cnt_row + 7.0) * 0.125)
    upper_e = jnp.where(re < ce, 1.0, 0.0).astype(BF16)
    off8_row = 8.0 * _dot(m8_row.astype(BF16), upper_e)
    gc = carry_s[:, 0:N_EXPERTS]
    srow = lax.broadcasted_iota(I32, (8, N_EXPERTS), 0)
    seg = jnp.where(srow == 0, 8.0 * m8_row, jnp.where(srow == 1, off8_row, jnp.where(srow == 2, gc, 0.0)))
    seg_ref[...] = jnp.concatenate([seg, jnp.zeros((8, LANES - N_EXPERTS), F32)], axis=1).astype(I32)
    carry_s[:, 0:N_EXPERTS] = gc + 8.0 * m8_row
    cnt_ref[...] = carry_s[...]
    ri_ref[...] = jnp.concatenate(dests + [jnp.zeros((4, tm), F32)], axis=0).astype(I32)
    rg_ref[...] = jnp.concatenate([e / tot for e in ex] + [jnp.zeros((4, tm), F32)], axis=0)


def _post(mix, x2d, gate, scale, shift, w_out_bf16, ln_g, ln_b, wr_t, br, seq_len, tm):
    t = x2d.shape[0]
    per = seq_len // tm
    vec = lambda i: (i // per, 0, 0)
    const2 = lambda i: (0, 0)
    return pl.pallas_call(
        functools.partial(_post_kernel, tm=tm),
        out_shape=(
            jax.ShapeDtypeStruct((t, D_MODEL), F32),
            jax.ShapeDtypeStruct((t, D_MODEL), BF16),
            jax.ShapeDtypeStruct((t // tm, 8, tm), I32),
            jax.ShapeDtypeStruct((t // tm, 8, tm), F32),
            jax.ShapeDtypeStruct((t // tm, 8, LANES), I32),
            jax.ShapeDtypeStruct((8, LANES), F32),
        ),
        grid=(t // tm,),
        in_specs=[
            pl.BlockSpec((tm, D_MODEL), lambda i: (i, 0)),
            pl.BlockSpec((tm, D_MODEL), lambda i: (i, 0)),
            pl.BlockSpec((None, 1, D_MODEL), vec),
            pl.BlockSpec((None, 1, D_MODEL), vec),
            pl.BlockSpec((None, 1, D_MODEL), vec),
            pl.BlockSpec((D_MODEL, D_MODEL), const2),
            pl.BlockSpec((1, D_MODEL), const2),
            pl.BlockSpec((1, D_MODEL), const2),
            pl.BlockSpec((N_EXPERTS, D_MODEL), const2),
            pl.BlockSpec((N_EXPERTS, LANES), const2),
        ],
        out_specs=(
            pl.BlockSpec((tm, D_MODEL), lambda i: (i, 0)),
            pl.BlockSpec((tm, D_MODEL), lambda i: (i, 0)),
            pl.BlockSpec((None, 8, tm), lambda i: (i, 0, 0)),
            pl.BlockSpec((None, 8, tm), lambda i: (i, 0, 0)),
            pl.BlockSpec((None, 8, LANES), lambda i: (i, 0, 0)),
            pl.BlockSpec((8, LANES), const2),
        ),
        scratch_shapes=[pltpu.VMEM((8, LANES), F32)],
        compiler_params=_cparams(("arbitrary",)),
        name="post_router",
    )(mix, x2d, gate, scale, shift, w_out_bf16, ln_g, ln_b, wr_t, br)


SEG_FIELDS = 3


def _run_sizes(tm):
    top = 1 << (tm - 1).bit_length()
    return [s for s in (512, 256, 128, 64, 32, 16, 8) if s <= max(top, 8)]


def _for_each_run_piece(seg_ref, rstart_ref, tile, tm, fn):
    base = tile * (SEG_FIELDS * N_EXPERTS)

    def body(e, c):
        n8 = seg_ref[base + e]
        src = seg_ref[base + N_EXPERTS + e]
        dst = rstart_ref[e] + seg_ref[base + 2 * N_EXPERTS + e]
        def pieces(sizes, done):
            for size in sizes:
                @pl.when((n8 & size) != 0)
                def _(done=done, size=size):
                    fn(pl.multiple_of(src + done, 8), pl.multiple_of(dst + done, 8), size)
                done = done + (n8 & size)

        split = 128
        big = [s for s in _run_sizes(tm) if s >= split]
        if big:
            @pl.when(n8 >= split)
            def _():
                pieces(big, jnp.int32(0))
        pieces([s for s in _run_sizes(tm) if s < split], n8 & ~jnp.int32(split - 1))
        return c

    lax.fori_loop(0, N_EXPERTS, body, 0)


def _wait_tile_rows(seg_ref, tile, cb, wait_rows):
    base = tile * (SEG_FIELDS * N_EXPERTS)
    last = N_EXPERTS - 1
    total = seg_ref[base + last] + seg_ref[base + N_EXPERTS + last]
    size = 1 << (cb.bit_length() - 1)
    while size >= 8:
        @pl.when((total & size) != 0)
        def _(size=size):
            wait_rows(size)
        size //= 2


def _dispatch_kernel(zblk_ref, rstart_ref, seg_ref, ri_ref, h_ref, xs_ref, zero_s, cbuf, sem, zsem,
                     *, tm, bm, cb, ntiles):
    i = pl.program_id(0)
    slot = i % 2

    def run_copies(tile, sl, wait):
        if wait:
            def wait_rows(size):
                pltpu.make_async_copy(cbuf.at[sl, pl.ds(0, size), :], xs_ref.at[pl.ds(0, size), :], sem.at[sl]).wait()
            _wait_tile_rows(seg_ref, tile, cb, wait_rows)
            return

        def piece(src, dst, size):
            pltpu.make_async_copy(cbuf.at[sl, pl.ds(src, size), :], xs_ref.at[pl.ds(dst, size), :], sem.at[sl]).start()
        _for_each_run_piece(seg_ref, rstart_ref, tile, tm, piece)

    def zero_copy(e):
        return pltpu.make_async_copy(zero_s, xs_ref.at[pl.ds(zblk_ref[e] * bm, bm), :], zsem)

    @pl.when(i == 0)
    def _():
        zero_s[...] = jnp.zeros_like(zero_s)

        def zstart(e, c):
            @pl.when(zblk_ref[e] >= 0)
            def _():
                zero_copy(e).start()
            return c

        def zwait(e, c):
            @pl.when(zblk_ref[e] >= 0)
            def _():
                zero_copy(e).wait()
            return c

        lax.fori_loop(0, N_EXPERTS, zstart, 0)
        lax.fori_loop(0, N_EXPERTS, zwait, 0)

    @pl.when(i >= 2)
    def _():
        run_copies(i - 2, slot, True)

    rows = lax.broadcasted_iota(I32, (cb, tm), 0)
    perm = jnp.zeros((cb, tm), F32)
    for k in range(TOP_K):
        perm = jnp.where(rows == ri_ref[k:k + 1, :], 1.0, perm)
    cbuf[slot] = _dot(perm.astype(BF16), h_ref[...])
    run_copies(i, slot, False)

    @pl.when(i == ntiles - 1)
    def _():
        if ntiles > 1:
            run_copies(i - 1, 1 - slot, True)
        run_copies(i, slot, True)


def _dispatch(h2, route_i, seg_flat, row_start, zblk, n_rows, tm, bm):
    t = h2.shape[0]
    ntiles = t // tm
    cb = TOP_K * tm + 8 * N_EXPERTS
    return pl.pallas_call(
        functools.partial(_dispatch_kernel, tm=tm, bm=bm, cb=cb, ntiles=ntiles),
        out_shape=jax.ShapeDtypeStruct((n_rows, D_MODEL), F32),
        grid_spec=pltpu.PrefetchScalarGridSpec(
            num_scalar_prefetch=3,
            grid=(ntiles,),
            in_specs=[
                pl.BlockSpec((None, 2 * TOP_K, tm), lambda i, z, r, s: (i, 0, 0)),
                pl.BlockSpec((tm, D_MODEL), lambda i, z, r, s: (i, 0)),
            ],
            out_specs=pl.BlockSpec(memory_space=pl.ANY),
            scratch_shapes=[
                pltpu.VMEM((bm, D_MODEL), F32),
                pltpu.VMEM((2, cb, D_MODEL), F32),
                pltpu.SemaphoreType.DMA((2,)),
                pltpu.SemaphoreType.DMA,
            ],
        ),
        compiler_params=_cparams(("arbitrary",)),
        name="dispatch",
    )(zblk, row_start, seg_flat, route_i, h2)


def _ffn_kernel(be_ref, grp_ref, ia_ref, ib_ref, nv_ref, xa_ref, xb_ref, w1_ref, b1_ref, w2_ref, b2_ref,
                ya_ref, yb_ref, w1_s, w2_s):
    b = pl.program_id(0)

    def expert_block(x_ref, y_ref):
        xb = x_ref[...].astype(BF16)
        glu = _dot(xb, w1_s[:, :D_FF]) + b1_ref[:, :D_FF]
        lin = _dot(xb, w1_s[:, D_FF:]) + b1_ref[:, D_FF:]
        glu = jnp.minimum(glu, SWIGLU_LIMIT)
        lin = jnp.clip(lin, -SWIGLU_LIMIT, SWIGLU_LIMIT)
        act = glu * _sigmoid(SWIGLU_ALPHA * glu) * (lin + 1.0)
        y_ref[...] = _dot(act.astype(BF16), w2_s[...]) + b2_ref[...]

    @pl.when(b < nv_ref[0])
    def _():
        @pl.when((b == 0) | (be_ref[b] != be_ref[jnp.maximum(b - 1, 0)]))
        def _():
            step = 256
            for r0 in range(0, D_MODEL, step):
                w1_s[r0:r0 + step, :] = w1_ref[r0:r0 + step, :].astype(BF16)
            for r0 in range(0, D_FF, step):
                w2_s[r0:r0 + step, :] = w2_ref[r0:r0 + step, :].astype(BF16)

        @pl.when(grp_ref[b] == 0)
        def _():
            expert_block(xa_ref, ya_ref)

        @pl.when(grp_ref[b] == 1)
        def _():
            expert_block(xb_ref, yb_ref)


def _ffn(xs_a, xs_b, steps, layer, w1, b1, w2, b2, bm_a, bm_b):
    step_e, step_g, step_ia, step_ib, n_valid = steps
    nsteps = step_e.shape[0]
    row_a = lambda b, be, g, ia, ib, nv: (ia[b], 0)
    row_b = lambda b, be, g, ia, ib, nv: (ib[b], 0)
    wsel = lambda b, be, g, ia, ib, nv: (layer, be[b], 0, 0)
    return pl.pallas_call(
        _ffn_kernel,
        out_shape=(jax.ShapeDtypeStruct(xs_a.shape, F32), jax.ShapeDtypeStruct(xs_b.shape, F32)),
        grid_spec=pltpu.PrefetchScalarGridSpec(
            num_scalar_prefetch=5,
            grid=(nsteps,),
            in_specs=[
                pl.BlockSpec((bm_a, D_MODEL), row_a),
                pl.BlockSpec((bm_b, D_MODEL), row_b),
                pl.BlockSpec((None, None, D_MODEL, 2 * D_FF), wsel),
                pl.BlockSpec((None, None, 1, 2 * D_FF), wsel),
                pl.BlockSpec((None, None, D_FF, D_MODEL), wsel),
                pl.BlockSpec((None, None, 1, D_MODEL), wsel),
            ],
            out_specs=(pl.BlockSpec((bm_a, D_MODEL), row_a), pl.BlockSpec((bm_b, D_MODEL), row_b)),
            scratch_shapes=[pltpu.VMEM((D_MODEL, 2 * D_FF), BF16), pltpu.VMEM((D_FF, D_MODEL), BF16)],
        ),
        compiler_params=_cparams(("arbitrary",)),
        name="expert_ffn",
    )(step_e, step_g, step_ia, step_ib, n_valid, xs_a, xs_b, w1, b1, w2, b2)


def _combine_kernel(rstart_ref, seg_ref, ri_ref, x_ref, gate_ref, rg_ref, lng_ref, lnb_ref, ys_ref, o_ref,
                    ybuf, yb_s, wt_s, sem, *, tm, cb, ntiles):
    i = pl.program_id(0)
    slot = i % 2

    def run_copies(tile, sl, wait):
        if wait:
            def wait_rows(size):
                pltpu.make_async_copy(ys_ref.at[pl.ds(0, size), :], ybuf.at[sl, pl.ds(0, size), :], sem.at[sl]).wait()
            _wait_tile_rows(seg_ref, tile, cb, wait_rows)
            return

        def piece(loc, glob, size):
            pltpu.make_async_copy(ys_ref.at[pl.ds(glob, size), :], ybuf.at[sl, pl.ds(loc, size), :], sem.at[sl]).start()
        _for_each_run_piece(seg_ref, rstart_ref, tile, tm, piece)

    @pl.when(i == 0)
    def _():
        ybuf[...] = jnp.zeros_like(ybuf)
        run_copies(0, 0, False)

    @pl.when(i + 1 < ntiles)
    def _():
        run_copies(i + 1, 1 - slot, False)

    dest_t = _tr(jnp.concatenate([ri_ref[...].astype(F32), jnp.zeros((LANES - 8, tm), F32)], axis=0))
    g_t = _tr(jnp.concatenate([rg_ref[...], jnp.zeros((LANES - 8, tm), F32)], axis=0))
    lane_step = 256
    nq = max(1, tm // 256)
    tq = tm // nq

    def build(q):
        rq = slice(q * tq, (q + 1) * tq)
        for c0 in range(0, cb, lane_step):
            cols = (lax.broadcasted_iota(I32, (tq, lane_step), 1) + c0).astype(F32)
            w = jnp.zeros((tq, lane_step), F32)
            for k in range(TOP_K):
                w = jnp.where(cols == dest_t[rq, k:k + 1], g_t[rq, k:k + 1], w)
            wt_s[rq, c0:c0 + lane_step] = w.astype(BF16)

    def product(q):
        rq = slice(q * tq, (q + 1) * tq)
        return _dot(wt_s[rq, :], yb_s[...])

    def finish(q, moe):
        rq = slice(q * tq, (q + 1) * tq)
        o_ref[rq, :] = _layer_norm_rows(ALPHA_DN * x_ref[rq, :] + (1.0 + gate_ref[...]) * moe,
                                        lng_ref[...], lnb_ref[...])

    build(0)
    run_copies(i, slot, True)
    row_step = 256
    for r0 in range(0, cb, row_step):
        yb_s[r0:r0 + row_step, :] = ybuf[slot, r0:r0 + row_step, :].astype(BF16)
    moes = {}
    for q in range(nq):
        if q + 1 < nq:
            build(q + 1)
        moes[q] = product(q)
        if q >= 1:
            finish(q - 1, moes.pop(q - 1))
    finish(nq - 1, moes.pop(nq - 1))


def _combine(row_start, seg_flat, route_i, x1, gate, route_g, ln_g, ln_b, ys, seq_len, tm):
    t = x1.shape[0]
    per = seq_len // tm
    ntiles = t // tm
    cb = TOP_K * tm + 8 * N_EXPERTS
    return pl.pallas_call(
        functools.partial(_combine_kernel, tm=tm, cb=cb, ntiles=ntiles),
        out_shape=jax.ShapeDtypeStruct((t, D_MODEL), F32),
        grid_spec=pltpu.PrefetchScalarGridSpec(
            num_scalar_prefetch=2,
            grid=(ntiles,),
            in_specs=[
                pl.BlockSpec((None, 2 * TOP_K, tm), lambda i, r, s: (i, 0, 0)),
                pl.BlockSpec((tm, D_MODEL), lambda i, r, s: (i, 0)),
                pl.BlockSpec((None, 1, D_MODEL), lambda i, r, s: (i // per, 0, 0)),
                pl.BlockSpec((None, 8, tm), lambda i, r, s: (i, 0, 0)),
                pl.BlockSpec((1, D_MODEL), lambda i, r, s: (0, 0)),
                pl.BlockSpec((1, D_MODEL), lambda i, r, s: (0, 0)),
                pl.BlockSpec(memory_space=pl.ANY),
            ],
            out_specs=pl.BlockSpec((tm, D_MODEL), lambda i, r, s: (i, 0)),
            scratch_shapes=[
                pltpu.VMEM((2, cb, D_MODEL), F32),
                pltpu.VMEM((cb, D_MODEL), BF16),
                pltpu.VMEM((tm, cb), BF16),
                pltpu.SemaphoreType.DMA((2,)),
            ],
        ),
        compiler_params=_cparams(("arbitrary",)),
        name="combine",
    )(row_start, seg_flat, route_i, x1, gate, route_g, ln_g, ln_b, ys)


def _max_blocks(t, tm, bm):
    return -(-(t * TOP_K + 7 * N_EXPERTS * (t // tm)) // bm) + N_EXPERTS


def _group_tables(counts, bm):
    cnt = counts[0, :N_EXPERTS].astype(I32)
    nblk = (cnt + bm - 1) // bm
    blk_end = jnp.cumsum(nblk)
    row_start = ((blk_end - nblk) * bm).astype(I32)
    zblk = jnp.where((cnt % bm) != 0, blk_end - 1, -1).astype(I32)
    return nblk, row_start, zblk


def _ffn_steps(nblk_a, nblk_b, nsteps):
    per_e = nblk_a + nblk_b
    ends = jnp.cumsum(per_e)
    n_valid = ends[-1]
    s = jnp.minimum(jnp.arange(nsteps, dtype=I32), n_valid - 1)
    step_e = jnp.minimum(jnp.sum((ends[None, :] <= s[:, None]).astype(I32), axis=1), N_EXPERTS - 1)
    onehot = (step_e[:, None] == jnp.arange(N_EXPERTS, dtype=I32)[None, :]).astype(I32)
    off = s - jnp.sum(onehot * (ends - per_e)[None, :], axis=1)
    in_b = off >= jnp.sum(onehot * nblk_a[None, :], axis=1)
    live = jnp.arange(nsteps, dtype=I32) < n_valid
    step_g = jnp.where(live, in_b.astype(I32), 2)
    step_ia = jnp.maximum(jnp.cumsum((step_g == 0).astype(I32)) - 1, 0)
    step_ib = jnp.maximum(jnp.cumsum((step_g == 1).astype(I32)) - 1, 0)
    return (step_e.astype(I32), step_g.astype(I32), step_ia.astype(I32), step_ib.astype(I32),
            n_valid.reshape(1).astype(I32))


def _group_cfg(batch, seq_len):
    if seq_len >= 512:
        return dict(tm=512, rows=256, bm=512)
    return dict(tm=seq_len, rows=seq_len, bm=128)


def kernel(x_prompt, x_sample, c_prompt, c_sample, state_a_C, state_a_n, state_a_m, cache_b_k, cache_b_v,
           state_c_S, state_c_conv, w_ada, b_ada, ln_g, ln_b, w_in_even, b_in_even, norm_a, sink_b, rel_bias,
           w_out_even, w_in_odd, conv_c, a_log_c, dt_bias_c, norm_c, w_out_odd, w_router, b_router,
           w_e1, b_e1, w_e2, b_e2):
    bp, lp, _ = x_prompt.shape
    bs, ls, _ = x_sample.shape
    groups = [dict(b=bp, l=lp, x=x_prompt.reshape(bp * lp, D_MODEL), **_group_cfg(bp, lp)),
              dict(b=bs, l=ls, x=x_sample.reshape(bs * ls, D_MODEL), **_group_cfg(bs, ls))]
    mod = _ada(jnp.concatenate([c_prompt, c_sample], axis=0), w_ada, b_ada)
    offs = [0, bp]
    states = [dict(), dict()]
    b1_all = b_e1.reshape(DEPTH, N_EXPERTS, 1, 2 * D_FF)
    b2_all = b_e2.reshape(DEPTH, N_EXPERTS, 1, D_MODEL)
    for l in range(DEPTH):
        e = l // 2
        if l % 2 == 0:
            w = w_in_even[e]
            sz = (512, 512, 512, 512, 4, 4, 512, 128, 128)
            o = [sum(sz[:j]) for j in range(len(sz) + 1)]
            pad = jnp.zeros((D_MODEL, LANES - 2 * NH_A), w.dtype)
            w_in = jnp.concatenate([w[:, o[0]:o[4]], w[:, o[6]:o[9]], w[:, o[4]:o[6]], pad], axis=1).astype(BF16)
            bb = b_in_even[e]
            b_in = jnp.concatenate([bb[o[0]:o[4]], bb[o[6]:o[9]], bb[o[4]:o[6]],
                                    jnp.zeros((LANES - 2 * NH_A,), bb.dtype)]).reshape(1, N_EVEN_COLS)
            w_out = w_out_even[e].astype(BF16)
        else:
            w = w_in_odd[e]
            pad = jnp.zeros((D_MODEL, LANES - 2 * NH_C), w.dtype)
            w_in = jnp.concatenate([w, pad], axis=1).astype(BF16)
            w_out = w_out_odd[e].astype(BF16)
        wr_t = w_router[l].T
        br = jnp.broadcast_to(b_router[l][:, None], (N_EXPERTS, LANES))
        moe_in = []
        for gi, gr in enumerate(groups):
            nb, sl = gr['b'], gr['l']
            m = mod[l, offs[gi]:offs[gi] + nb].reshape(nb, 6, 1, D_MODEL)
            m6 = [m[:, j] for j in range(6)]
            st = states[gi]
            if l % 2 == 0:
                proj = _inproj(gr['x'], m6[1], m6[0], w_in, b_in, sl, gr['tm'])
                if gi == 0:
                    kh = jnp.zeros((nb, WINDOW, NKV_B, DH_B), F32)
                    vh = kh
                    c0 = jnp.zeros((nb, NH_A, DK_A, DV_A), F32)
                    n0 = jnp.zeros((nb, NH_A, DK_A), F32)
                    m0 = jnp.zeros((nb, NH_A), F32)
                else:
                    kh, vh, c0, n0, m0 = cache_b_k[e], cache_b_v[e], state_a_C[e], state_a_n[e], state_a_m[e]
                bias_tab = _swa_bias_table(rel_bias, gr['rows'])
                mix, c1, n1, m1, k1, v1 = _even_mixer(proj, kh, vh, c0, n0, m0, bias_tab, norm_a[e], sink_b[e],
                                                      nb, sl, gr['rows'], gi == 1)
                for name, val in (('a_C', c1), ('a_n', n1), ('a_m', m1), ('b_k', k1), ('b_v', v1)):
                    st.setdefault(name, []).append(val)
            else:
                if gi == 0:
                    cv0 = jnp.zeros((nb, CONV_W - 1, QKV_C), F32)
                    s0 = jnp.zeros((nb, NH_C, DK_C, DV_C), F32)
                else:
                    cv0, s0 = state_c_conv[e], state_c_S[e]
                proj, cv1 = _inproj_conv(gr['x'], m6[1], m6[0], w_in, cv0, conv_c[e], sl, gr['tm'])
                mix, s1 = _odd_mixer(proj, s0, a_log_c[e], dt_bias_c[e], norm_c[e], nb, sl, gr['rows'])
                st.setdefault('c_S', []).append(s1)
                st.setdefault('c_conv', []).append(cv1)
            x1, h2, route_i, route_g, seg, counts = _post(
                mix, gr['x'], m6[2], m6[4], m6[3], w_out, ln_g[l, 0].reshape(1, D_MODEL),
                ln_b[l, 0].reshape(1, D_MODEL), wr_t, br, sl, gr['tm'])
            t, tm, bm = nb * sl, gr['tm'], gr['bm']
            ntiles = t // tm
            nblk, row_start, zblk = _group_tables(counts, bm)
            seg_flat = seg[:, :SEG_FIELDS, :N_EXPERTS].reshape(ntiles * SEG_FIELDS * N_EXPERTS)
            xs = _dispatch(h2, route_i, seg_flat, row_start, zblk, _max_blocks(t, tm, bm) * bm, tm, bm)
            moe_in.append(dict(xs=xs, nblk=nblk, row_start=row_start, seg_flat=seg_flat, route_i=route_i,
                               route_g=route_g, x1=x1, gate=m6[5]))
        ga, gb = moe_in
        steps = _ffn_steps(ga['nblk'], gb['nblk'], sum(_max_blocks(g['b'] * g['l'], g['tm'], g['bm']) for g in groups))
        ys = _ffn(ga['xs'], gb['xs'], steps, l, w_e1, b1_all, w_e2, b2_all, groups[0]['bm'], groups[1]['bm'])
        for gi, gr in enumerate(groups):
            mi = moe_in[gi]
            gr['x'] = _combine(mi['row_start'], mi['seg_flat'], mi['route_i'], mi['x1'], mi['gate'], mi['route_g'],
                               ln_g[l, 1].reshape(1, D_MODEL), ln_b[l, 1].reshape(1, D_MODEL), ys[gi],
                               gr['l'], gr['tm'])
    outs = [groups[0]['x'].reshape(bp, lp, D_MODEL), groups[1]['x'].reshape(bs, ls, D_MODEL)]
    for gi in range(2):
        for name in ('a_C', 'a_n', 'a_m', 'b_k', 'b_v', 'c_S', 'c_conv'):
            outs.append(jnp.stack(states[gi][name]))
    return tuple(outs)
```
